```python
import numpy as np
import jax
import jax.numpy as jnp
from jax import lax

D_MODEL = 2048
BATCH = 4
SEQ = 2048
DEPTH = 2

MEM_LEN = 256
POS_OFFSET_MAX = 4096
ROPE_THETA = 500000.0
RMS_EPS = 1e-6
NEG_INF = -1e30
FORCE_SCORE = 1e4
Q_BLOCK = 128

MLA_HEADS = 8
MLA_NOPE = 128
MLA_ROPE = 64
MLA_V = 128
MLA_Q_RANK = D_MODEL // 4
MLA_KV_RANK = D_MODEL // 8

NSA_HEADS = 8
NSA_KV_GROUPS = 2
NSA_HPG = NSA_HEADS // NSA_KV_GROUPS
NSA_DK = 128
NSA_DV = 128
NSA_ROT = NSA_DK // 4
CMP_LEN = 32
CMP_STRIDE = 16
CMP_HID = 4 * NSA_DK
SEL_BLOCK = 64
SEL_TOPN = 8
WINDOW = 512
N_BRANCH = 3

D_MIX = MLA_HEADS * MLA_V + NSA_HEADS * NSA_DV
KV_WK = NSA_KV_GROUPS * NSA_DK
KV_WV = NSA_KV_GROUPS * NSA_DV
IN_SIZES = (MLA_Q_RANK, MLA_KV_RANK, MLA_ROPE, NSA_HEADS * NSA_DK, KV_WK, KV_WV, KV_WK, KV_WV, KV_WK, KV_WV, NSA_HEADS * N_BRANCH)
D_IN = MLA_Q_RANK + MLA_KV_RANK + MLA_ROPE + NSA_HEADS * NSA_DK + 3 * KV_WK + 3 * KV_WV + NSA_HEADS * N_BRANCH

XATTN_HEADS = 4
XATTN_HD = 128

N_GROUPS = 8
EXP_PER_GROUP = 8
N_EXPERTS = N_GROUPS * EXP_PER_GROUP
TOP_K = 2
D_EXPERT = D_MODEL // 4
MOE_BLOCK = 128

kernel_name = 'hybrid_mla_nsa_hier_moe_block'


def rms_norm(x, g):
    xf = x.astype(jnp.float32)
    xf = xf * lax.rsqrt(jnp.mean(xf * xf, axis=-1, keepdims=True) + RMS_EPS)
    return (xf * g.astype(jnp.float32)).astype(x.dtype)


def rope_angles(positions, rot_dim):
    inv_freq = ROPE_THETA ** (-jnp.arange(0, rot_dim, 2, dtype=jnp.float32) / rot_dim)
    ang = positions.astype(jnp.float32)[..., None] * inv_freq
    return jnp.cos(ang), jnp.sin(ang)


def apply_rope(x, cos, sin):
    half = cos.shape[-1]
    xf = x[..., :2 * half].astype(jnp.float32)
    x1, x2 = xf[..., :half], xf[..., half:]
    rot = jnp.concatenate([x1 * cos - x2 * sin, x2 * cos + x1 * sin], axis=-1).astype(x.dtype)
    return jnp.concatenate([rot, x[..., 2 * half:]], axis=-1)


def masked_softmax(s, mask):
    s = jnp.where(mask, s.astype(jnp.float32), NEG_INF)
    return jnp.where(mask, jax.nn.softmax(s, axis=-1), 0.0)


def mla(c_q, c_kv, k_r, cos, sin, g_cq, w_uq, g_ckv, w_ukv):
    b, s, _ = c_q.shape
    q = (rms_norm(c_q, g_cq) @ w_uq).reshape(b, s, MLA_HEADS, MLA_NOPE + MLA_ROPE)
    q = jnp.concatenate([q[..., :MLA_NOPE], apply_rope(q[..., MLA_NOPE:], cos[:, :, None], sin[:, :, None])], axis=-1)
    kv = (rms_norm(c_kv, g_ckv) @ w_ukv).reshape(b, s, MLA_HEADS, MLA_NOPE + MLA_V)
    k_nope, v = kv[..., :MLA_NOPE], kv[..., MLA_NOPE:]
    k_rope = apply_rope(k_r, cos, sin)
    k = jnp.concatenate([k_nope, jnp.broadcast_to(k_rope[:, :, None, :], (b, s, MLA_HEADS, MLA_ROPE))], axis=-1)
    scale = (MLA_NOPE + MLA_ROPE) ** -0.5
    nqb = s // Q_BLOCK
    qb = q.reshape(b, nqb, Q_BLOCK, MLA_HEADS, MLA_NOPE + MLA_ROPE).transpose(1, 0, 2, 3, 4)
    kpos = jnp.arange(s)

    def q_block(args):
        i, qi = args
        sc = jnp.einsum('bqhd,bkhd->bhqk', qi, k) * scale
        qpos = i * Q_BLOCK + jnp.arange(Q_BLOCK)
        p = masked_softmax(sc, kpos[None, :] <= qpos[:, None]).astype(v.dtype)
        return jnp.einsum('bhqk,bkhd->bqhd', p, v)

    o = lax.map(q_block, (jnp.arange(nqb), qb))
    return o.transpose(1, 0, 2, 3, 4).reshape(b, s, MLA_HEADS * MLA_V)


def nsa(q, k_c, v_c, k_s, v_s, k_w, v_w, gate_logits, cos, sin, pe_k, pe_v, w_ck1, w_ck2, w_cv1, w_cv2):
    b, s, _ = q.shape
    G, HPG = NSA_KV_GROUPS, NSA_HPG
    q = apply_rope(q.reshape(b, s, G, HPG, NSA_DK), cos[:, :, None, None], sin[:, :, None, None])

    def keys(t):
        return apply_rope(t.reshape(b, s, G, NSA_DK), cos[:, :, None], sin[:, :, None])

    def vals(t):
        return t.reshape(b, s, G, NSA_DV)

    k_c, k_s, k_w = keys(k_c), keys(k_s), keys(k_w)
    v_c, v_s, v_w = vals(v_c), vals(v_s), vals(v_w)
    scale = NSA_DK ** -0.5
    pos = np.arange(s)

    nc = (s - CMP_LEN) // CMP_STRIDE + 1
    cidx = np.arange(nc)[:, None] * CMP_STRIDE + np.arange(CMP_LEN)[None, :]

    def compress(t, pe, w1, w2):
        blk = t[:, cidx] + pe[:, None, :]
        blk = blk.transpose(0, 1, 3, 2, 4).reshape(b, nc, G, CMP_LEN * t.shape[-1])
        return jax.nn.gelu(blk @ w1) @ w2

    kc = compress(k_c, pe_k, w_ck1, w_ck2)
    vc = compress(v_c, pe_v, w_cv1, w_cv2)
    s_c = jnp.einsum('bsghd,bcgd->bghsc', q, kc) * scale
    cmp_end = np.arange(nc) * CMP_STRIDE + CMP_LEN - 1
    p_c = masked_softmax(s_c, cmp_end[None, :] <= pos[:, None])
    o_c = jnp.einsum('bghsc,bcgd->bsghd', p_c.astype(vc.dtype), vc)

    ns = s // SEL_BLOCK
    sel_start = np.arange(ns) * SEL_BLOCK
    c_start = np.arange(nc)[:, None] * CMP_STRIDE
    overlap = ((c_start <= sel_start[None, :] + SEL_BLOCK - 1) & (c_start + CMP_LEN - 1 >= sel_start[None, :])).astype(np.float32)
    imp = jnp.einsum('bghsc,cn->bgsn', p_c, overlap)
    forced = (sel_start[None, :] == 0) | (sel_start[None, :] == (pos[:, None] // SEL_BLOCK) * SEL_BLOCK)
    visible = sel_start[None, :] <= pos[:, None]
    imp = jnp.where(forced, FORCE_SCORE, jnp.where(visible, imp, -FORCE_SCORE))
    n_top = min(SEL_TOPN, ns)
    _, sel_idx = lax.top_k(imp, n_top)

    ks_blk = k_s.reshape(b, ns, SEL_BLOCK, G, NSA_DK).transpose(0, 3, 1, 2, 4)
    vs_blk = v_s.reshape(b, ns, SEL_BLOCK, G, NSA_DV).transpose(0, 3, 1, 2, 4)
    nqb = s // Q_BLOCK
    q_b = q.reshape(b, nqb, Q_BLOCK, G, HPG, NSA_DK).transpose(1, 0, 3, 2, 4, 5)
    idx_b = sel_idx.reshape(b, G, nqb, Q_BLOCK, n_top).transpose(2, 0, 1, 3, 4)
    gather = jax.vmap(jax.vmap(lambda blocks, ids: blocks[ids]))

    def sel_block(args):
        i, qi, ii = args
        kg = gather(ks_blk, ii)
        vg = gather(vs_blk, ii).reshape(b, G, Q_BLOCK, n_top * SEL_BLOCK, NSA_DV)
        sc = jnp.einsum('bgqhd,bgqnkd->bgqhnk', qi, kg).reshape(b, G, Q_BLOCK, HPG, n_top * SEL_BLOCK) * scale
        qpos = i * Q_BLOCK + jnp.arange(Q_BLOCK)
        kpos = ii[..., None] * SEL_BLOCK + jnp.arange(SEL_BLOCK)
        mask = (kpos <= qpos[None, None, :, None, None]).reshape(b, G, Q_BLOCK, 1, n_top * SEL_BLOCK)
        p = masked_softmax(sc, mask).astype(vg.dtype)
        return jnp.einsum('bgqhk,bgqkd->bqghd', p, vg)

    o_s = lax.map(sel_block, (jnp.arange(nqb), q_b, idx_b))
    o_s = o_s.transpose(1, 0, 2, 3, 4, 5).reshape(b, s, G, HPG, NSA_DV)

    nwb = WINDOW // Q_BLOCK
    bidx = np.arange(nqb)[:, None] + np.arange(nwb + 1)[None, :]

    def band(t):
        tp = jnp.pad(t, ((0, 0), (WINDOW, 0), (0, 0), (0, 0))).reshape(b, nqb + nwb, Q_BLOCK, G, t.shape[-1])
        return tp[:, bidx].reshape(b, nqb, (nwb + 1) * Q_BLOCK, G, t.shape[-1])

    kw, vw = band(k_w), band(v_w)
    qw = q.reshape(b, nqb, Q_BLOCK, G, HPG, NSA_DK)
    s_w = jnp.einsum('bnqghd,bnkgd->bnghqk', qw, kw) * scale
    qpos = np.arange(nqb)[:, None] * Q_BLOCK + np.arange(Q_BLOCK)[None, :]
    kpos = np.arange(nqb)[:, None] * Q_BLOCK - WINDOW + np.arange((nwb + 1) * Q_BLOCK)[None, :]
    dist = qpos[:, :, None] - kpos[:, None, :]
    mask_w = (dist >= 0) & (dist < WINDOW) & (kpos[:, None, :] >= 0)
    p_w = masked_softmax(s_w, mask_w[None, :, None, None]).astype(vw.dtype)
    o_w = jnp.einsum('bnghqk,bnkgd->bnqghd', p_w, vw).reshape(b, s, G, HPG, NSA_DV)

    g = jax.nn.sigmoid(gate_logits.astype(jnp.float32)).reshape(b, s, G, HPG, N_BRANCH).astype(o_c.dtype)
    o = g[..., 0:1] * o_c + g[..., 1:2] * o_s + g[..., 2:3] * o_w
    return o.reshape(b, s, NSA_HEADS * NSA_DV)


def hybrid_mixer(h, cos_a, sin_a, cos_n, sin_n, w_in, g_cq, w_uq, g_ckv, w_ukv, pe_k, pe_v, w_ck1, w_ck2, w_cv1, w_cv2, g_ya, g_yb, w_out):
    proj = h @ w_in
    cuts = np.cumsum(IN_SIZES)[:-1].tolist()
    c_q, c_kv, k_r, q_n, k_c, v_c, k_s, v_s, k_w, v_w, gate_logits = jnp.split(proj, cuts, axis=-1)
    y_a = mla(c_q, c_kv, k_r, cos_a, sin_a, g_cq, w_uq, g_ckv, w_ukv)
    y_b = nsa(q_n, k_c, v_c, k_s, v_s, k_w, v_w, gate_logits, cos_n, sin_n, pe_k, pe_v, w_ck1, w_ck2, w_cv1, w_cv2)
    y = jnp.concatenate([rms_norm(y_a, g_ya), rms_norm(y_b, g_yb)], axis=-1)
    return y @ w_out


def mem_cross_attn(h, mem_n, w_xq, w_xkv, w_xo):
    b, s, _ = h.shape
    m = mem_n.shape[1]
    q = (h @ w_xq).reshape(b, s, XATTN_HEADS, XATTN_HD)
    kv = (mem_n @ w_xkv).reshape(b, m, 2, XATTN_HEADS, XATTN_HD)
    k, v = kv[:, :, 0], kv[:, :, 1]
    sc = jnp.einsum('bshd,bmhd->bhsm', q, k) * (XATTN_HD ** -0.5)
    p = jax.nn.softmax(sc.astype(jnp.float32), axis=-1).astype(v.dtype)
    o = jnp.einsum('bhsm,bmhd->bshd', p, v).reshape(b, s, XATTN_HEADS * XATTN_HD)
    return o @ w_xo


def hier_moe(h, w_grp, b_grp, w_exp, b_exp, w_e1, w_e3, w_e2):
    b, s, d = h.shape
    n_tok = b * s
    xt = h.reshape(n_tok, d)
    p_grp = jax.nn.softmax((xt @ w_grp + b_grp).astype(jnp.float32), axis=-1)
    grp = jnp.argmax(p_grp, axis=-1).astype(jnp.int32)
    p_top_grp = jnp.take_along_axis(p_grp, grp[:, None], axis=-1)
    e_logits = (xt @ w_exp + b_exp).astype(jnp.float32).reshape(n_tok, N_GROUPS, EXP_PER_GROUP)
    in_grp = jnp.take_along_axis(e_logits, grp[:, None, None], axis=1)[:, 0]
    top_val, top_idx = lax.top_k(in_grp, TOP_K)
    gate = (p_top_grp * jax.nn.softmax(top_val, axis=-1)).reshape(-1)
    expert = (grp[:, None] * EXP_PER_GROUP + top_idx).reshape(-1).astype(jnp.int32)
    tok = jnp.repeat(jnp.arange(n_tok, dtype=jnp.int32), TOP_K)
    onehot = jax.nn.one_hot(expert, N_EXPERTS, dtype=jnp.int32)
    rank = jnp.take_along_axis(jnp.cumsum(onehot, axis=0), expert[:, None], axis=1)[:, 0] - 1
    counts = jnp.sum(onehot, axis=0)
    padded = (counts + MOE_BLOCK - 1) // MOE_BLOCK * MOE_BLOCK
    pad_end = jnp.cumsum(padded)
    pad_start = pad_end - padded
    dest = pad_start[expert] + rank
    n_assign = n_tok * TOP_K
    n_blocks = (n_assign + N_EXPERTS * (MOE_BLOCK - 1) + MOE_BLOCK - 1) // MOE_BLOCK
    n_slots = n_blocks * MOE_BLOCK
    slot_tok = jnp.zeros((n_slots,), jnp.int32).at[dest].set(tok)
    slot_gate = jnp.zeros((n_slots,), jnp.float32).at[dest].set(gate)
    blk_start = jnp.arange(n_blocks, dtype=jnp.int32) * MOE_BLOCK
    blk_expert = jnp.clip(jnp.searchsorted(pad_end, blk_start, side='right'), 0, N_EXPERTS - 1)

    def expert_block(args):
        e_id, toks = args
        xb = xt[toks]
        hb = jax.nn.silu(xb @ w_e1[e_id]) * (xb @ w_e3[e_id])
        return hb @ w_e2[e_id]

    yb = lax.map(expert_block, (blk_expert, slot_tok.reshape(n_blocks, MOE_BLOCK)))
    yb = yb.reshape(n_slots, d) * slot_gate[:, None].astype(h.dtype)
    y = jnp.zeros((n_tok, d), h.dtype).at[slot_tok].add(yb)
    return y.reshape(b, s, d)


def setup_inputs(seed: int = 0) -> dict:
    key = jax.random.key(seed)
    ks = jax.random.split(key, 40)
    f32 = jnp.float32
    L = DEPTH

    def nrm(k, shape, fan_in):
        return jax.random.normal(k, shape, f32) * (fan_in ** -0.5)

    def gain(k, shape):
        return 1.0 + 0.02 * jax.random.normal(k, shape, f32)

    x = jax.random.normal(ks[0], (BATCH, SEQ, D_MODEL), f32)
    mem = jax.random.normal(ks[1], (BATCH, MEM_LEN, D_MODEL), f32)
    positions = jax.random.randint(ks[2], (BATCH, 1), 0, POS_OFFSET_MAX, dtype=jnp.int32) + jnp.arange(SEQ, dtype=jnp.int32)[None, :]
    return {
        'x': x,
        'mem': mem,
        'positions': positions,
        'g_mem': gain(ks[3], (D_MODEL,)),
        'g_final': gain(ks[4], (D_MODEL,)),
        'g_mix': gain(ks[5], (L, D_MODEL)),
        'w_in': nrm(ks[6], (L, D_MODEL, D_IN), D_MODEL),
        'g_cq': gain(ks[7], (L, MLA_Q_RANK)),
        'w_uq': nrm(ks[8], (L, MLA_Q_RANK, MLA_HEADS * (MLA_NOPE + MLA_ROPE)), MLA_Q_RANK),
        'g_ckv': gain(ks[9], (L, MLA_KV_RANK)),
        'w_ukv': nrm(ks[10], (L, MLA_KV_RANK, MLA_HEADS * (MLA_NOPE + MLA_V)), MLA_KV_RANK),
        'pe_k': 0.1 * jax.random.normal(ks[11], (L, CMP_LEN, NSA_DK), f32),
        'pe_v': 0.1 * jax.random.normal(ks[12], (L, CMP_LEN, NSA_DV), f32),
        'w_ck1': nrm(ks[13], (L, CMP_LEN * NSA_DK, CMP_HID), CMP_LEN * NSA_DK),
        'w_ck2': nrm(ks[14], (L, CMP_HID, NSA_DK), CMP_HID),
        'w_cv1': nrm(ks[15], (L, CMP_LEN * NSA_DV, CMP_HID), CMP_LEN * NSA_DV),
        'w_cv2': nrm(ks[16], (L, CMP_HID, NSA_DV), CMP_HID),
        'g_ya': gain(ks[17], (L, MLA_HEADS * MLA_V)),
        'g_yb': gain(ks[18], (L, NSA_HEADS * NSA_DV)),
        'w_out': nrm(ks[19], (L, D_MIX, D_MODEL), D_MIX),
        'g_x': gain(ks[20], (L, D_MODEL)),
        'w_xq': nrm(ks[21], (L, D_MODEL, XATTN_HEADS * XATTN_HD), D_MODEL),
        'w_xkv': nrm(ks[22], (L, D_MODEL, 2 * XATTN_HEADS * XATTN_HD), D_MODEL),
        'w_xo': nrm(ks[23], (L, XATTN_HEADS * XATTN_HD, D_MODEL), XATTN_HEADS * XATTN_HD),
        'g_ffn': gain(ks[24], (L, D_MODEL)),
        'w_grp': nrm(ks[25], (L, D_MODEL, N_GROUPS), D_MODEL),
        'b_grp': 0.01 * jax.random.normal(ks[26], (L, N_GROUPS), f32),
        'w_exp': nrm(ks[27], (L, D_MODEL, N_EXPERTS), D_MODEL),
        'b_exp': 0.01 * jax.random.normal(ks[28], (L, N_EXPERTS), f32),
        'w_e1': nrm(ks[29], (L, N_EXPERTS, D_MODEL, D_EXPERT), D_MODEL),
        'w_e3': nrm(ks[30], (L, N_EXPERTS, D_MODEL, D_EXPERT), D_MODEL),
        'w_e2': nrm(ks[31], (L, N_EXPERTS, D_EXPERT, D_MODEL), D_EXPERT),
    }


def reference(x, mem, positions, g_mem, g_final, g_mix, w_in, g_cq, w_uq, g_ckv, w_ukv, pe_k, pe_v, w_ck1, w_ck2, w_cv1, w_cv2, g_ya, g_yb, w_out, g_x, w_xq, w_xkv, w_xo, g_ffn, w_grp, b_grp, w_exp, b_exp, w_e1, w_e3, w_e2):
    cos_a, sin_a = rope_angles(positions, MLA_ROPE)
    cos_n, sin_n = rope_angles(positions, NSA_ROT)
    mem_n = rms_norm(mem, g_mem)
    h = x
    for l in range(DEPTH):
        h = h + hybrid_mixer(rms_norm(h, g_mix[l]), cos_a, sin_a, cos_n, sin_n, w_in[l], g_cq[l], w_uq[l], g_ckv[l], w_ukv[l], pe_k[l], pe_v[l], w_ck1[l], w_ck2[l], w_cv1[l], w_cv2[l], g_ya[l], g_yb[l], w_out[l])
        h = h + mem_cross_attn(rms_norm(h, g_x[l]), mem_n, w_xq[l], w_xkv[l], w_xo[l])
        h = h + hier_moe(rms_norm(h, g_ffn[l]), w_grp[l], b_grp[l], w_exp[l], b_exp[l], w_e1[l], w_e3[l], w_e2[l])
    return rms_norm(h, g_final)
```

```python
import functools

import jax
import jax.numpy as jnp
import numpy as np
from jax import lax
from jax.experimental import pallas as pl
from jax.experimental.pallas import tpu as pltpu

F32 = jnp.float32
BF16 = jnp.bfloat16

D_MODEL = 2048
RMS_EPS = 1e-6
ROPE_THETA = 500000.0
NEG = -1e30
FORCE_SCORE = 1e4

MLA_HEADS = 8
MLA_NOPE = 128
MLA_ROPE = 64
MLA_V = 128
MLA_Q_RANK = D_MODEL // 4
MLA_KV_RANK = D_MODEL // 8
MLA_QK_PAD = 256

NSA_HEADS = 8
NSA_G = 2
NSA_HPG = 4
NSA_D = 128
NSA_ROT = 32
CMP_LEN = 32
CMP_STRIDE = 16
CMP_HID = 512
SEL_BLOCK = 64
SEL_TOPN = 8
WINDOW = 512
QB = 128

XH = 4
XD = 128
MEM_LEN = 256

N_GROUPS = 8
EXP_PER_GROUP = 8
N_EXPERTS = 64
TOP_K = 2
D_EXPERT = D_MODEL // 4
MOE_BLOCK = 128

VMEM_LIMIT_BYTES = 56 * 1024 * 1024

C_CQ = 0
C_CKV = 512
C_KR = 768
C_QN = 896
C_KK = 1920
C_VV = 2688
C_GT = 3456
N_IN = 3584

SCALE_A = (MLA_NOPE + MLA_ROPE) ** -0.5
SCALE_N = NSA_D ** -0.5
SCALE_X = XD ** -0.5


def _params(sem):
    return pltpu.CompilerParams(dimension_semantics=sem, vmem_limit_bytes=VMEM_LIMIT_BYTES)


def _mm(a, b):
    return jnp.dot(a, b, preferred_element_type=F32)


def _mm_nt(a, b):
    return lax.dot_general(a, b, (((1,), (1,)), ((), ())), preferred_element_type=F32)


def _rms(x, g):
    return x * lax.rsqrt(jnp.mean(x * x, axis=-1, keepdims=True) + RMS_EPS) * g


def _rope(x, c, sa, sb, half):
    return x * c + pltpu.roll(x, half, 1) * sa + pltpu.roll(x, 128 - half, 1) * sb


def _const_spec(shape):
    nd = len(shape)
    return pl.BlockSpec(shape, lambda *_: (0,) * nd, pipeline_mode=pl.Buffered(1))


def _in_proj_body(h_ref, gmix_ref, w1_ref, gcq_ref, gckv_ref, wq_ref, wkn_ref, wv_ref, tab_ref,
                  qa_ref, ka_ref, va_ref, qn_ref, kc_ref, vc_ref, ks_ref, vs_ref, kw_ref, vw_ref, gt_ref):
    xn = _rms(h_ref[...], gmix_ref[...]).astype(BF16)
    tab = tab_ref[...]
    ca, saa, sba, cn, san, sbn = [tab[:, i * 128:(i + 1) * 128] for i in range(6)]

    xq = _rms(_mm(xn, w1_ref[:, C_CQ:C_CKV]), gcq_ref[...]).astype(BF16)
    ckr = _mm(xn, w1_ref[:, C_CKV:C_QN])
    xkv = _rms(ckr[:, :MLA_KV_RANK], gckv_ref[...]).astype(BF16)
    kr = _rope(ckr[:, MLA_KV_RANK:], ca, saa, sba, MLA_ROPE // 2).astype(BF16)
    q = _mm(xq, wq_ref[...])
    kn = _mm(xkv, wkn_ref[...])
    for h in range(MLA_HEADS):
        lo = h * MLA_QK_PAD
        qa_ref[:, lo:lo + 128] = (q[:, lo:lo + 128] * SCALE_A).astype(BF16)
        qa_ref[:, lo + 128:lo + 256] = (
            _rope(q[:, lo + 128:lo + 256], ca, saa, sba, MLA_ROPE // 2) * SCALE_A).astype(BF16)
        ka_ref[:, lo:lo + 128] = kn[:, h * 128:(h + 1) * 128].astype(BF16)
        ka_ref[:, lo + 128:lo + 256] = kr
    va_ref[...] = _mm(xkv, wv_ref[...]).astype(BF16)

    qn = _mm(xn, w1_ref[:, C_QN:C_KK])
    for h in range(NSA_HEADS):
        sl = slice(h * 128, (h + 1) * 128)
        qn_ref[:, sl] = (_rope(qn[:, sl], cn, san, sbn, NSA_ROT // 2) * SCALE_N).astype(BF16)
    kk = _mm(xn, w1_ref[:, C_KK:C_VV])
    vv = _mm(xn, w1_ref[:, C_VV:C_GT])
    for g in range(NSA_G):
        sl = slice(g * 128, (g + 1) * 128)
        kc_ref[g] = _rope(kk[:, sl], cn, san, sbn, NSA_ROT // 2)
        ks_ref[:, sl] = _rope(kk[:, 256 + g * 128:256 + (g + 1) * 128], cn, san, sbn, NSA_ROT // 2).astype(BF16)
        kw_ref[:, sl] = _rope(kk[:, 512 + g * 128:512 + (g + 1) * 128], cn, san, sbn, NSA_ROT // 2).astype(BF16)
        vc_ref[g] = vv[:, sl]
    vs_ref[...] = vv[:, 256:512].astype(BF16)
    vw_ref[...] = vv[:, 512:768].astype(BF16)
    gl = _mm(xn, w1_ref[:, C_GT:N_IN])
    gt_ref[0] = gl
    gt_ref[1] = pltpu.roll(gl, 128 - NSA_HPG * 3, 1)


def _in_proj(h, gmix, w1, gcq, gckv, wq, wkn, wv, tab, tm=256):
    t = h.shape[0]
    row = lambda n: pl.BlockSpec((tm, n), lambda i: (i, 0))
    grp = pl.BlockSpec((NSA_G, tm, 128), lambda i: (0, i, 0))
    out_shape = [
        jax.ShapeDtypeStruct((t, MLA_HEADS * MLA_QK_PAD), BF16),
        jax.ShapeDtypeStruct((t, MLA_HEADS * MLA_QK_PAD), BF16),
        jax.ShapeDtypeStruct((t, MLA_HEADS * MLA_V), BF16),
        jax.ShapeDtypeStruct((t, NSA_HEADS * NSA_D), BF16),
        jax.ShapeDtypeStruct((NSA_G, t, NSA_D), F32),
        jax.ShapeDtypeStruct((NSA_G, t, NSA_D), F32),
        jax.ShapeDtypeStruct((t, NSA_G * NSA_D), BF16),
        jax.ShapeDtypeStruct((t, NSA_G * NSA_D), BF16),
        jax.ShapeDtypeStruct((t, NSA_G * NSA_D), BF16),
        jax.ShapeDtypeStruct((t, NSA_G * NSA_D), BF16),
        jax.ShapeDtypeStruct((NSA_G, t, 128), F32),
    ]
    out_specs = [row(2048), row(2048), row(1024), row(1024), grp, grp, row(256), row(256), row(256), row(256), grp]
    in_specs = [row(D_MODEL), _const_spec((1, D_MODEL)), _const_spec((D_MODEL, N_IN)),
                _const_spec((1, MLA_Q_RANK)), _const_spec((1, MLA_KV_RANK)),
                _const_spec((MLA_Q_RANK, MLA_HEADS * MLA_QK_PAD)), _const_spec((MLA_KV_RANK, MLA_HEADS * 128)),
                _const_spec((MLA_KV_RANK, MLA_HEADS * 128)), row(6 * 128)]
    return pl.pallas_call(
        _in_proj_body, out_shape=out_shape, grid=(t // tm,), in_specs=in_specs, out_specs=out_specs,
        compiler_params=_params(("parallel",)), name="in_proj",
    )(h, gmix, w1, gcq, gckv, wq, wkn, wv, tab)


def _compress_body(k_ref, v_ref, pek_ref, pev_ref, wk1_ref, wk2_ref, wv1_ref, wv2_ref, ko_ref, vo_ref):
    half = CMP_STRIDE * NSA_D

    def phi(x, pe, w1_ref, w2_ref):
        a = _mm((x + pe[:, :half]).astype(BF16), w1_ref[:half, :])
        b = _mm((x + pe[:, half:]).astype(BF16), w1_ref[half:, :])
        pre = a + pltpu.roll(b, b.shape[0] - 1, 0)
        return _mm(jax.nn.gelu(pre).astype(BF16), w2_ref[...])

    ko_ref[...] = phi(k_ref[...], pek_ref[...], wk1_ref, wk2_ref).astype(BF16)
    vo_ref[...] = phi(v_ref[...], pev_ref[...], wv1_ref, wv2_ref).astype(BF16)


def _compress(kc, vc, pek, pev, wk1, wk2, wv1, wv2):
    g, b, ncp, w = kc.shape
    in_blk = pl.BlockSpec((None, None, ncp, w), lambda bi, gi: (gi, bi, 0, 0))
    out_blk = pl.BlockSpec((None, None, ncp, NSA_D), lambda bi, gi: (bi, gi, 0, 0))
    out = jax.ShapeDtypeStruct((b, g, ncp, NSA_D), BF16)
    return pl.pallas_call(
        _compress_body, out_shape=[out, out], grid=(b, g),
        in_specs=[in_blk, in_blk, _const_spec(pek.shape), _const_spec(pev.shape), _const_spec(wk1.shape),
                  _const_spec(wk2.shape), _const_spec(wv1.shape), _const_spec(wv2.shape)],
        out_specs=[out_blk, out_blk], compiler_params=_params(("parallel", "parallel")), name="compress",
    )(kc, vc, pek, pev, wk1, wk2, wv1, wv2)


def _mla_body(q_ref, k_ref, v_ref, o_ref, *, tq, tk):
    i = pl.program_id(2)
    q = q_ref[...]
    qpos = i * tq + lax.broadcasted_iota(jnp.int32, (tq, tk), 0)
    kidx = lax.broadcasted_iota(jnp.int32, (tq, tk), 1)

    def step(j, carry):
        m, l, acc = carry
        off = pl.multiple_of(j * tk, tk)
        s = _mm_nt(q, k_ref[pl.ds(off, tk), :])
        s = jnp.where(kidx + off <= qpos, s, NEG)
        m_new = jnp.maximum(m, jnp.max(s, axis=1, keepdims=True))
        alpha = jnp.exp(m - m_new)
        p = jnp.exp(s - m_new)
        l = alpha * l + jnp.sum(p, axis=1, keepdims=True)
        acc = alpha * acc + _mm(p.astype(BF16), v_ref[pl.ds(off, tk), :])
        return m_new, l, acc

    init = (jnp.full((tq, 1), NEG, F32), jnp.zeros((tq, 1), F32), jnp.zeros((tq, MLA_V), F32))
    n_tiles = (i * tq + tq + tk - 1) // tk
    _, l, acc = lax.fori_loop(0, n_tiles, step, init)
    o_ref[...] = acc / l


def _mla_attn(q, k, v, tq=256, tk=256):
    b, s, _ = q.shape
    return pl.pallas_call(
        functools.partial(_mla_body, tq=tq, tk=tk),
        out_shape=jax.ShapeDtypeStruct((b, s, MLA_HEADS * MLA_V), F32),
        grid=(b, MLA_HEADS, s // tq),
        in_specs=[pl.BlockSpec((None, tq, MLA_QK_PAD), lambda bi, hi, i: (bi, i, hi)),
                  pl.BlockSpec((None, s, MLA_QK_PAD), lambda bi, hi, i: (bi, 0, hi)),
                  pl.BlockSpec((None, s, MLA_V), lambda bi, hi, i: (bi, 0, hi))],
        out_specs=pl.BlockSpec((None, tq, MLA_V), lambda bi, hi, i: (bi, i, hi)),
        compiler_params=_params(("parallel", "parallel", "arbitrary")), name="mla_attn",
    )(q, k, v)


SEL_TK = 256
WIN_LEN = WINDOW + QB


def _nsa_body(q_ref, kc_ref, vc_ref, ks_ref, vs_ref, kw_ref, vw_ref, gt_ref, ovt_ref, exp_ref, o_ref, mask_ref):
    i = pl.program_id(2)
    ns = ovt_ref.shape[0]
    ncp = ovt_ref.shape[1]
    rows = NSA_HPG * QB
    q = q_ref[...]
    qst = jnp.concatenate([q[:, h * 128:(h + 1) * 128] for h in range(NSA_HPG)], axis=0)

    def tile4(x):
        return jnp.concatenate([x] * NSA_HPG, axis=0)

    pos_r = i * QB + (lax.broadcasted_iota(jnp.int32, (rows, ncp), 0) & (QB - 1))
    cend = lax.broadcasted_iota(jnp.int32, (rows, ncp), 1) * CMP_STRIDE + (CMP_LEN - 1)
    vis = cend <= pos_r
    s = jnp.where(vis, _mm_nt(qst, kc_ref[...]), NEG)
    e = jnp.where(vis, jnp.exp(s - jnp.max(s, axis=1, keepdims=True)), 0.0)
    l = jnp.sum(e, axis=1, keepdims=True)
    p_c = e / jnp.where(l > 0.0, l, 1.0)
    o_c = _mm(p_c.astype(BF16), vc_ref[...])

    psum = p_c[0:QB] + p_c[QB:2 * QB] + p_c[2 * QB:3 * QB] + p_c[3 * QB:4 * QB]
    p_hi = psum.astype(BF16)
    p_lo = (psum - p_hi.astype(F32)).astype(BF16)
    imp = _mm_nt(ovt_ref[...], p_hi) + _mm_nt(ovt_ref[...], p_lo)
    n_i = lax.broadcasted_iota(jnp.int32, (ns, QB), 0)
    pos_l = i * QB + lax.broadcasted_iota(jnp.int32, (ns, QB), 1)
    forced = (n_i == 0) | (n_i == pos_l // SEL_BLOCK)
    val = jnp.where(forced, FORCE_SCORE, jnp.where(n_i * SEL_BLOCK <= pos_l, imp, -FORCE_SCORE))
    rank = jnp.zeros((ns, QB), F32)
    for m_ in range(ns):
        vm = val[m_:m_ + 1, :]
        rank = rank + jnp.where((vm > val) | ((vm == val) & (n_i > m_)), 1.0, 0.0)
    sel_t = jnp.where(rank < float(min(SEL_TOPN, ns)), 1.0, 0.0).astype(BF16)
    eye = jnp.where(lax.broadcasted_iota(jnp.int32, (QB, QB), 0) == lax.broadcasted_iota(jnp.int32, (QB, QB), 1),
                    1.0, 0.0).astype(BF16)
    sel = _mm_nt(eye, sel_t).astype(BF16)
    mask_ref[...] = _mm(sel, exp_ref[...])

    qrow = i * QB + lax.broadcasted_iota(jnp.int32, (QB, SEL_TK), 0)
    kcol = lax.broadcasted_iota(jnp.int32, (QB, SEL_TK), 1)

    def sel_step(j, carry):
        m, l, acc = carry
        off = pl.multiple_of(j * SEL_TK, SEL_TK)
        ok = (mask_ref[:, pl.ds(off, SEL_TK)] > 0.5) & (kcol + off <= qrow)
        s = _mm_nt(qst, ks_ref[pl.ds(off, SEL_TK), :]) + tile4(jnp.where(ok, 0.0, NEG))
        m_new = jnp.maximum(m, jnp.max(s, axis=1, keepdims=True))
        alpha = jnp.exp(m - m_new)
        p = jnp.exp(s - m_new)
        l = alpha * l + jnp.sum(p, axis=1, keepdims=True)
        acc = alpha * acc + _mm(p.astype(BF16), vs_ref[pl.ds(off, SEL_TK), :])
        return m_new, l, acc

    init = (jnp.full((rows, 1), NEG, F32), jnp.zeros((rows, 1), F32), jnp.zeros((rows, NSA_D), F32))
    n_tiles = (i * QB + QB + SEL_TK - 1) // SEL_TK
    _, l, acc = lax.fori_loop(0, n_tiles, sel_step, init)
    o_s = acc / l

    start = pl.multiple_of(jnp.maximum(i * QB - WINDOW, 0), QB)
    dist = (i * QB + lax.broadcasted_iota(jnp.int32, (QB, WIN_LEN), 0)) - (
        start + lax.broadcasted_iota(jnp.int32, (QB, WIN_LEN), 1))
    ok = (dist >= 0) & (dist < WINDOW)
    s = _mm_nt(qst, kw_ref[pl.ds(start, WIN_LEN), :]) + tile4(jnp.where(ok, 0.0, NEG))
    p = jnp.exp(s - jnp.max(s, axis=1, keepdims=True))
    o_w = _mm(p.astype(BF16), vw_ref[pl.ds(start, WIN_LEN), :]) / jnp.sum(p, axis=1, keepdims=True)

    g = jax.nn.sigmoid(gt_ref[...])
    for h in range(NSA_HPG):
        sl = slice(h * QB, (h + 1) * QB)
        o_ref[:, h * 128:(h + 1) * 128] = (g[:, 3 * h:3 * h + 1] * o_c[sl] + g[:, 3 * h + 1:3 * h + 2] * o_s[sl]
                                           + g[:, 3 * h + 2:3 * h + 3] * o_w[sl])


def _nsa_attn(q, kcmp, vcmp, ks, vs, kw, vw, gt, ovt, expand):
    b, s, _ = q.shape
    ncp = kcmp.shape[2]
    gw = NSA_HPG * NSA_D
    slab = pl.BlockSpec((None, s, NSA_D), lambda bi, gi, i: (bi, 0, gi))
    cmp = pl.BlockSpec((None, None, ncp, NSA_D), lambda bi, gi, i: (bi, gi, 0, 0))
    return pl.pallas_call(
        _nsa_body, out_shape=jax.ShapeDtypeStruct((b, s, NSA_HEADS * NSA_D), F32),
        grid=(b, NSA_G, s // QB),
        in_specs=[pl.BlockSpec((None, QB, gw), lambda bi, gi, i: (bi, i, gi)), cmp, cmp, slab, slab, slab, slab,
                  pl.BlockSpec((None, None, QB, 128), lambda bi, gi, i: (gi, bi, i, 0)),
                  _const_spec(ovt.shape), _const_spec(expand.shape)],
        out_specs=pl.BlockSpec((None, QB, gw), lambda bi, gi, i: (bi, i, gi)),
        scratch_shapes=[pltpu.VMEM((QB, s), F32)],
        compiler_params=_params(("parallel", "parallel", "arbitrary")), name="nsa_attn",
    )(q, kcmp, vcmp, ks, vs, kw, vw, gt, ovt, expand)


def _post_body(ya_ref, yb_ref, h_ref, gya_ref, gyb_ref, wo_ref, gx_ref, wxq_ref, mk_ref, mv_ref, wxo_ref,
               gf_ref, wrh_ref, wrl_ref, br_ref, h2_ref, xn_ref, eid_ref, gate_ref):
    half = wo_ref.shape[0] // 2
    h1 = (h_ref[...] + _mm(_rms(ya_ref[...], gya_ref[...]).astype(BF16), wo_ref[:half, :])
          + _mm(_rms(yb_ref[...], gyb_ref[...]).astype(BF16), wo_ref[half:, :]))

    q = (_mm(_rms(h1, gx_ref[...]).astype(BF16), wxq_ref[...]) * SCALE_X).astype(BF16)
    outs = []
    for hd in range(XH):
        sl = slice(hd * XD, (hd + 1) * XD)
        s = _mm_nt(q[:, sl], mk_ref[:, sl])
        p = jnp.exp(s - jnp.max(s, axis=1, keepdims=True))
        outs.append(_mm(p.astype(BF16), mv_ref[:, sl]) / jnp.sum(p, axis=1, keepdims=True))
    h2 = h1 + _mm(jnp.concatenate(outs, axis=1).astype(BF16), wxo_ref[...])
    h2_ref[...] = h2

    xf = _rms(h2, gf_ref[...])
    xn_ref[...] = xf
    x_hi = xf.astype(BF16)
    x_lo = (xf - x_hi.astype(F32)).astype(BF16)
    logits = _mm(x_hi, wrh_ref[...]) + _mm(x_lo, wrh_ref[...]) + _mm(x_hi, wrl_ref[...]) + br_ref[...]
    lane = lax.broadcasted_iota(jnp.int32, logits.shape, 1)
    lane_f = lane.astype(F32)

    def argmax_first(x):
        v = jnp.max(x, axis=1, keepdims=True)
        return v, jnp.min(jnp.where(x == v, lane_f, 1e9), axis=1, keepdims=True)

    is_grp = lane < N_GROUPS
    lg = jnp.where(is_grp, logits, NEG)
    mx, grp = argmax_first(lg)
    p_top = 1.0 / jnp.sum(jnp.where(is_grp, jnp.exp(lg - mx), 0.0), axis=1, keepdims=True)
    in_grp = (lane >= N_GROUPS) & (lane < N_GROUPS + N_EXPERTS) & (
        ((lane - N_GROUPS) // EXP_PER_GROUP).astype(F32) == grp)
    le = jnp.where(in_grp, logits, NEG)
    v1, i1 = argmax_first(le)
    v2, i2 = argmax_first(jnp.where(lane_f == i1, NEG, le))
    t = jnp.exp(v2 - v1)
    g1 = p_top / (1.0 + t)
    g2 = p_top * t / (1.0 + t)
    eid_ref[...] = jnp.where(lane == 0, i1 - N_GROUPS, jnp.where(lane == 1, i2 - N_GROUPS, 0.0)).astype(jnp.int32)
    gate_ref[...] = jnp.where(lane == 0, g1, jnp.where(lane == 1, g2, 0.0))


def _post_mixer(ya, yb, h, gya, gyb, wo, gx, wxq, memkv, layer, wxo, gf, wrh, wrl, br, tm=256):
    b, s, _ = ya.shape
    row = lambda n: pl.BlockSpec((None, tm, n), lambda bi, i: (bi, i, 0))
    hw = XH * XD
    in_specs = [row(1024), row(1024), row(D_MODEL), _const_spec(gya.shape), _const_spec(gyb.shape),
                _const_spec(wo.shape), _const_spec(gx.shape), _const_spec(wxq.shape),
                pl.BlockSpec((None, MEM_LEN, hw), lambda bi, i: (bi, 0, 2 * layer)),
                pl.BlockSpec((None, MEM_LEN, hw), lambda bi, i: (bi, 0, 2 * layer + 1)),
                _const_spec(wxo.shape), _const_spec(gf.shape), _const_spec(wrh.shape), _const_spec(wrl.shape),
                _const_spec(br.shape)]
    out_shape = [jax.ShapeDtypeStruct((b, s, D_MODEL), F32), jax.ShapeDtypeStruct((b, s, D_MODEL), F32),
                 jax.ShapeDtypeStruct((b, s, 128), jnp.int32), jax.ShapeDtypeStruct((b, s, 128), F32)]
    return pl.pallas_call(
        _post_body, out_shape=out_shape, grid=(b, s // tm), in_specs=in_specs,
        out_specs=[row(D_MODEL), row(D_MODEL), row(128), row(128)],
        compiler_params=_params(("parallel", "parallel")), name="post_mixer",
    )(ya, yb, h, gya, gyb, wo, gx, wxq, memkv, memkv, wxo, gf, wrh, wrl, br)


def _memkv_body(m_ref, g_ref, w_ref, o_ref):
    o_ref[...] = _mm(_rms(m_ref[...], g_ref[...]).astype(BF16), w_ref[...]).astype(BF16)


def _memkv(mem, g, w, tm=256):
    t = mem.shape[0]
    n = w.shape[1]
    return pl.pallas_call(
        _memkv_body, out_shape=jax.ShapeDtypeStruct((t, n), BF16), grid=(t // tm,),
        in_specs=[pl.BlockSpec((tm, D_MODEL), lambda i: (i, 0)), _const_spec(g.shape), _const_spec(w.shape)],
        out_specs=pl.BlockSpec((tm, n), lambda i: (i, 0)),
        compiler_params=_params(("parallel",)), name="memkv",
    )(mem, g, w)


GATHER_ROWS = 128


def _row_copy(src_ref, dst_ref, sem, src_row, dst_row):
    return pltpu.make_async_copy(src_ref.at[pl.ds(src_row, 1), :], dst_ref.at[pl.ds(dst_row, 1), :], sem)


def _gather_body(idx_ref, src_ref, o_ref, sem):
    base = pl.program_id(0) * GATHER_ROWS

    def issue(r, c):
        _row_copy(src_ref, o_ref, sem, idx_ref[base + r], r).start()
        return c

    def drain(r, c):
        _row_copy(src_ref, o_ref, sem, 0, r).wait()
        return c

    lax.fori_loop(0, GATHER_ROWS, issue, 0)
    lax.fori_loop(0, GATHER_ROWS, drain, 0)


def _gather_rows(idx, src):
    n = idx.shape[0]
    d = src.shape[1]
    return pl.pallas_call(
        _gather_body, out_shape=jax.ShapeDtypeStruct((n, d), src.dtype),
        grid_spec=pltpu.PrefetchScalarGridSpec(
            num_scalar_prefetch=1, grid=(n // GATHER_ROWS,),
            in_specs=[pl.BlockSpec(memory_space=pl.ANY)],
            out_specs=pl.BlockSpec((GATHER_ROWS, d), lambda i, idx_ref: (i, 0)),
            scratch_shapes=[pltpu.SemaphoreType.DMA(())]),
        compiler_params=_params(("arbitrary",)), name="gather",
    )(idx, src)


def _experts_body(be_ref, bf_ref, nu_ref, x_ref, w1_ref, w3_ref, w2_ref, o_ref, w1b, w3b, w2b):
    b = pl.program_id(0)

    @pl.when(b < nu_ref[0])
    def _():
        @pl.when(bf_ref[b] == 1)
        def _():
            w1b[...] = w1_ref[...].astype(BF16)
            w3b[...] = w3_ref[...].astype(BF16)
            w2b[...] = w2_ref[...].astype(BF16)

        x = x_ref[...].astype(BF16)
        hb = (jax.nn.silu(_mm(x, w1b[...])) * _mm(x, w3b[...])).astype(BF16)
        o_ref[...] = _mm(hb, w2b[...])

    @pl.when(b >= nu_ref[0])
    def _():
        o_ref[...] = jnp.zeros_like(o_ref)


def _experts(blk_expert, blk_first, n_used, xs, w_e1, w_e3, w_e2, layer):
    n_slots, d = xs.shape
    n_blocks = n_slots // MOE_BLOCK
    wspec = lambda r, c: pl.BlockSpec((None, None, r, c), lambda b, be, bf, nu: (layer, be[b], 0, 0))
    return pl.pallas_call(
        _experts_body, out_shape=jax.ShapeDtypeStruct((n_slots, d), F32),
        grid_spec=pltpu.PrefetchScalarGridSpec(
            num_scalar_prefetch=3, grid=(n_blocks,),
            in_specs=[pl.BlockSpec((MOE_BLOCK, d), lambda b, be, bf, nu: (b, 0)),
                      wspec(d, D_EXPERT), wspec(d, D_EXPERT), wspec(D_EXPERT, d)],
            out_specs=pl.BlockSpec((MOE_BLOCK, d), lambda b, be, bf, nu: (b, 0)),
            scratch_shapes=[pltpu.VMEM((d, D_EXPERT), BF16), pltpu.VMEM((d, D_EXPERT), BF16),
                            pltpu.VMEM((D_EXPERT, d), BF16)]),
        compiler_params=_params(("arbitrary",)), name="experts",
    )(blk_expert, blk_first, n_used, xs, w_e1, w_e3, w_e2)


COMBINE_ROWS = 128


def _combine_body(dest_ref, yb_ref, h_ref, gate_ref, gfin_ref, o_ref, a_ref, b_ref, sem, *, final_norm):
    base = pl.program_id(0) * COMBINE_ROWS

    def issue(r, c):
        _row_copy(yb_ref, a_ref, sem, dest_ref[2 * (base + r)], r).start()
        _row_copy(yb_ref, b_ref, sem, dest_ref[2 * (base + r) + 1], r).start()
        return c

    def drain(r, c):
        _row_copy(yb_ref, a_ref, sem, 0, r).wait()
        _row_copy(yb_ref, b_ref, sem, 0, r).wait()
        return c

    lax.fori_loop(0, COMBINE_ROWS, issue, 0)
    lax.fori_loop(0, COMBINE_ROWS, drain, 0)
    g = gate_ref[...]
    y = h_ref[...] + g[:, 0:1] * a_ref[...] + g[:, 1:2] * b_ref[...]
    if final_norm:
        y = _rms(y, gfin_ref[...])
    o_ref[...] = y


def _combine(dest, yb, h, gate, gfin, final_norm):
    t, d = h.shape
    return pl.pallas_call(
        functools.partial(_combine_body, final_norm=final_norm),
        out_shape=jax.ShapeDtypeStruct((t, d), F32),
        grid_spec=pltpu.PrefetchScalarGridSpec(
            num_scalar_prefetch=1, grid=(t // COMBINE_ROWS,),
            in_specs=[pl.BlockSpec(memory_space=pl.ANY),
                      pl.BlockSpec((COMBINE_ROWS, d), lambda i, dr: (i, 0)),
                      pl.BlockSpec((COMBINE_ROWS, 128), lambda i, dr: (i, 0)),
                      pl.BlockSpec((1, d), lambda i, dr: (0, 0))],
            out_specs=pl.BlockSpec((COMBINE_ROWS, d), lambda i, dr: (i, 0)),
            scratch_shapes=[pltpu.VMEM((COMBINE_ROWS, d), F32), pltpu.VMEM((COMBINE_ROWS, d), F32),
                            pltpu.SemaphoreType.DMA(())]),
        compiler_params=_params(("arbitrary",)), name="combine",
    )(dest, yb, h, gate, gfin)


def _dispatch(eid):
    n_tok = eid.shape[0]
    n_assign = n_tok * TOP_K
    expert = eid.reshape(-1)
    tok = jnp.repeat(jnp.arange(n_tok, dtype=jnp.int32), TOP_K)
    onehot = jax.nn.one_hot(expert, N_EXPERTS, dtype=jnp.int32)
    rank = jnp.take_along_axis(jnp.cumsum(onehot, axis=0), expert[:, None], axis=1)[:, 0] - 1
    counts = jnp.sum(onehot, axis=0)
    padded = (counts + MOE_BLOCK - 1) // MOE_BLOCK * MOE_BLOCK
    pad_end = jnp.cumsum(padded)
    dest = ((pad_end - padded)[expert] + rank).astype(jnp.int32)
    n_blocks = (n_assign + N_EXPERTS * (MOE_BLOCK - 1) + MOE_BLOCK - 1) // MOE_BLOCK
    slot_tok = jnp.zeros((n_blocks * MOE_BLOCK,), jnp.int32).at[dest].set(tok)
    blk_start = jnp.arange(n_blocks, dtype=jnp.int32) * MOE_BLOCK
    blk_expert = jnp.clip(jnp.searchsorted(pad_end, blk_start, side='right'), 0, N_EXPERTS - 1).astype(jnp.int32)
    blk_first = jnp.concatenate([jnp.ones((1,), jnp.int32), (blk_expert[1:] != blk_expert[:-1]).astype(jnp.int32)])
    n_used = (pad_end[-1:] // MOE_BLOCK).astype(jnp.int32)
    return dest, slot_tok, blk_expert, blk_first, n_used


def _rope_tables(positions):
    pos = positions.reshape(-1).astype(F32)[:, None]

    def tables(rot, passthrough):
        half = rot // 2
        ang = pos * (ROPE_THETA ** (-jnp.arange(0, rot, 2, dtype=F32) / rot))
        cos, sin = jnp.cos(ang), jnp.sin(ang)
        z = lambda n: jnp.zeros((pos.shape[0], n), F32)
        tail = jnp.full((pos.shape[0], 128 - rot), passthrough, F32)
        return [jnp.concatenate([cos, cos, tail], 1), jnp.concatenate([z(half), sin, z(128 - rot)], 1),
                jnp.concatenate([-sin, z(128 - half)], 1)]

    return jnp.concatenate(tables(MLA_ROPE, 0.0) + tables(NSA_ROT, 1.0), axis=1)


def _selection_constants(s):
    ncp = s // CMP_STRIDE
    ns = s // SEL_BLOCK
    c_start = np.arange(ncp)[None, :] * CMP_STRIDE
    sel_start = np.arange(ns)[:, None] * SEL_BLOCK
    ovt = (c_start <= sel_start + SEL_BLOCK - 1) & (c_start + CMP_LEN - 1 >= sel_start)
    ovt[:, ncp - 1] = False
    expand = np.arange(s)[None, :] // SEL_BLOCK == np.arange(ns)[:, None]
    return jnp.asarray(ovt, BF16), jnp.asarray(expand, BF16)


def _arrange_w_in(w):
    cuts = np.cumsum([0, 512, 256, 64, 1024, 256, 256, 256, 256, 256, 256, 24])
    c_q, c_kv, k_r, q_n, k_c, v_c, k_s, v_s, k_w, v_w, gates = [w[:, cuts[i]:cuts[i + 1]] for i in range(11)]
    z = lambda n: jnp.zeros((w.shape[0], n), w.dtype)
    return jnp.concatenate([c_q, c_kv, k_r, z(64), q_n, k_c, k_s, k_w, v_c, v_s, v_w, gates, z(104)], axis=1).astype(BF16)


def _arrange_w_uq(w):
    w = w.reshape(MLA_Q_RANK, MLA_HEADS, MLA_NOPE + MLA_ROPE)
    w = jnp.pad(w, ((0, 0), (0, 0), (0, MLA_QK_PAD - MLA_NOPE - MLA_ROPE)))
    return w.reshape(MLA_Q_RANK, MLA_HEADS * MLA_QK_PAD).astype(BF16)


def _split_w_ukv(w):
    w = w.reshape(MLA_KV_RANK, MLA_HEADS, MLA_NOPE + MLA_V)
    return (w[:, :, :MLA_NOPE].reshape(MLA_KV_RANK, -1).astype(BF16),
            w[:, :, MLA_NOPE:].reshape(MLA_KV_RANK, -1).astype(BF16))


def _router_weights(w_grp, b_grp, w_exp, b_exp):
    pad = 128 - N_GROUPS - N_EXPERTS
    w = jnp.concatenate([w_grp, w_exp, jnp.zeros((w_grp.shape[0], pad), F32)], axis=1)
    hi = w.astype(BF16)
    lo = (w - hi.astype(F32)).astype(BF16)
    b = jnp.concatenate([b_grp, b_exp, jnp.zeros((pad,), F32)])[None, :]
    return hi, lo, b


def kernel(x, mem, positions, g_mem, g_final, g_mix, w_in, g_cq, w_uq, g_ckv, w_ukv, pe_k, pe_v, w_ck1, w_ck2,
           w_cv1, w_cv2, g_ya, g_yb, w_out, g_x, w_xq, w_xkv, w_xo, g_ffn, w_grp, b_grp, w_exp, b_exp,
           w_e1, w_e3, w_e2):
    b, s, d = x.shape
    depth = w_in.shape[0]
    t = b * s
    ncp = s // CMP_STRIDE
    tab = _rope_tables(positions)
    ovt, expand = _selection_constants(s)
    r2 = lambda v: v.reshape(1, -1)

    w_kv_all = jnp.concatenate([w_xkv[l] for l in range(depth)], axis=1).astype(BF16)
    memkv = _memkv(mem.reshape(-1, d), r2(g_mem), w_kv_all).reshape(b, mem.shape[1], -1)

    h = x.reshape(t, d)
    for l in range(depth):
        wkn, wv = _split_w_ukv(w_ukv[l])
        qa, ka, va, qn, kc, vc, ks, vs, kw, vw, gt = _in_proj(
            h, r2(g_mix[l]), _arrange_w_in(w_in[l]), r2(g_cq[l]), r2(g_ckv[l]), _arrange_w_uq(w_uq[l]), wkn, wv, tab)
        kcmp, vcmp = _compress(
            kc.reshape(NSA_G, b, ncp, CMP_STRIDE * NSA_D), vc.reshape(NSA_G, b, ncp, CMP_STRIDE * NSA_D),
            pe_k[l].reshape(1, -1), pe_v[l].reshape(1, -1), w_ck1[l].astype(BF16), w_ck2[l].astype(BF16),
            w_cv1[l].astype(BF16), w_cv2[l].astype(BF16))
        r3 = lambda a: a.reshape(b, s, -1)
        ya = _mla_attn(r3(qa), r3(ka), r3(va))
        yb = _nsa_attn(r3(qn), kcmp, vcmp, r3(ks), r3(vs), r3(kw), r3(vw), gt.reshape(NSA_G, b, s, 128), ovt, expand)
        wrh, wrl, br = _router_weights(w_grp[l], b_grp[l], w_exp[l], b_exp[l])
        h2, xn, eid, gate = _post_mixer(
            ya, yb, r3(h), r2(g_ya[l]), r2(g_yb[l]), w_out[l].astype(BF16), r2(g_x[l]), w_xq[l].astype(BF16),
            memkv, l, w_xo[l].astype(BF16), r2(g_ffn[l]), wrh, wrl, br)
        eid = eid.reshape(t, 128)
        dest, slot_tok, blk_expert, blk_first, n_used = _dispatch(eid[:, :TOP_K])
        xs = _gather_rows(slot_tok, xn.reshape(t, d))
        y_slots = _experts(blk_expert, blk_first, n_used, xs, w_e1, w_e3, w_e2, l)
        h = _combine(dest, y_slots, h2.reshape(t, d), gate.reshape(t, 128), r2(g_final), final_norm=(l == depth - 1))
    return h.reshape(b, s, d)
```

```python
import functools

import jax
import jax.numpy as jnp
import numpy as np
from jax import lax
from jax.experimental import pallas as pl
from jax.experimental.pallas import tpu as pltpu

F32 = jnp.float32
BF16 = jnp.bfloat16

D_MODEL = 2048
RMS_EPS = 1e-6
ROPE_THETA = 500000.0
NEG = -1e30
FORCE_SCORE = 1e4

MLA_HEADS = 8
MLA_NOPE = 128
MLA_ROPE = 64
MLA_V = 128
MLA_Q_RANK = D_MODEL // 4
MLA_KV_RANK = D_MODEL // 8
MLA_QK_PAD = 256

NSA_HEADS = 8
NSA_G = 2
NSA_HPG = 4
NSA_D = 128
NSA_ROT = 32
CMP_LEN = 32
CMP_STRIDE = 16
CMP_HID = 512
SEL_BLOCK = 64
SEL_TOPN = 8
WINDOW = 512
QB = 128

XH = 4
XD = 128
MEM_LEN = 256

N_GROUPS = 8
EXP_PER_GROUP = 8
N_EXPERTS = 64
TOP_K = 2
D_EXPERT = D_MODEL // 4
MOE_BLOCK = 128

VMEM_LIMIT_BYTES = 56 * 1024 * 1024

C_CQ = 0
C_CKV = 512
C_KR = 768
C_QN = 896
C_KK = 1920
C_VV = 2688
C_GT = 3456
N_IN = 3584

SCALE_A = (MLA_NOPE + MLA_ROPE) ** -0.5
SCALE_N = NSA_D ** -0.5
SCALE_X = XD ** -0.5


def _params(sem):
    return pltpu.CompilerParams(dimension_semantics=sem, vmem_limit_bytes=VMEM_LIMIT_BYTES)


def _mm(a, b):
    return jnp.dot(a, b, preferred_element_type=F32)


def _mm_nt(a, b):
    return lax.dot_general(a, b, (((1,), (1,)), ((), ())), preferred_element_type=F32)


def _rms(x, g):
    return x * lax.rsqrt(jnp.mean(x * x, axis=-1, keepdims=True) + RMS_EPS) * g


def _rope(x, c, sa, sb, half):
    return x * c + pltpu.roll(x, half, 1) * sa + pltpu.roll(x, 128 - half, 1) * sb


def _const_spec(shape):
    nd = len(shape)
    return pl.BlockSpec(shape, lambda *_: (0,) * nd, pipeline_mode=pl.Buffered(1))


def _in_proj_body(h_ref, gmix_ref, w1_ref, gcq_ref, gckv_ref, wq_ref, wkn_ref, wv_ref, tab_ref,
                  qa_ref, ka_ref, va_ref, qn_ref, kc_ref, vc_ref, ks_ref, vs_ref, kw_ref, vw_ref, gt_ref):
    xn = _rms(h_ref[...], gmix_ref[...]).astype(BF16)
    tab = tab_ref[...]
    ca, saa, sba, cn, san, sbn = [tab[:, i * 128:(i + 1) * 128] for i in range(6)]

    xq = _rms(_mm(xn, w1_ref[:, C_CQ:C_CKV]), gcq_ref[...]).astype(BF16)
    ckr = _mm(xn, w1_ref[:, C_CKV:C_QN])
    xkv = _rms(ckr[:, :MLA_KV_RANK], gckv_ref[...]).astype(BF16)
    kr = _rope(ckr[:, MLA_KV_RANK:], ca, saa, sba, MLA_ROPE // 2).astype(BF16)
    q = _mm(xq, wq_ref[...])
    kn = _mm(xkv, wkn_ref[...])
    for h in range(MLA_HEADS):
        lo = h * MLA_QK_PAD
        qa_ref[:, lo:lo + 128] = (q[:, lo:lo + 128] * SCALE_A).astype(BF16)
        qa_ref[:, lo + 128:lo + 256] = (
            _rope(q[:, lo + 128:lo + 256], ca, saa, sba, MLA_ROPE // 2) * SCALE_A).astype(BF16)
        ka_ref[:, lo:lo + 128] = kn[:, h * 128:(h + 1) * 128].astype(BF16)
        ka_ref[:, lo + 128:lo + 256] = kr
    va_ref[...] = _mm(xkv, wv_ref[...]).astype(BF16)

    qn = _mm(xn, w1_ref[:, C_QN:C_KK])
    for h in range(NSA_HEADS):
        sl = slice(h * 128, (h + 1) * 128)
        qn_ref[:, sl] = (_rope(qn[:, sl], cn, san, sbn, NSA_ROT // 2) * SCALE_N).astype(BF16)
    kk = _mm(xn, w1_ref[:, C_KK:C_VV])
    vv = _mm(xn, w1_ref[:, C_VV:C_GT])
    for g in range(NSA_G):
        sl = slice(g * 128, (g + 1) * 128)
        kc_ref[g] = _rope(kk[:, sl], cn, san, sbn, NSA_ROT // 2)
        ks_ref[:, sl] = _rope(kk[:, 256 + g * 128:256 + (g + 1) * 128], cn, san, sbn, NSA_ROT // 2).astype(BF16)
        kw_ref[:, sl] = _rope(kk[:, 512 + g * 128:512 + (g + 1) * 128], cn, san, sbn, NSA_ROT // 2).astype(BF16)
        vc_ref[g] = vv[:, sl]
    vs_ref[...] = vv[:, 256:512].astype(BF16)
    vw_ref[...] = vv[:, 512:768].astype(BF16)
    gl = _mm(xn, w1_ref[:, C_GT:N_IN])
    gt_ref[0] = gl
    gt_ref[1] = pltpu.roll(gl, 128 - NSA_HPG * 3, 1)


def _in_proj(h, gmix, w1, gcq, gckv, wq, wkn, wv, tab, tm=256):
    t = h.shape[0]
    row = lambda n: pl.BlockSpec((tm, n), lambda i: (i, 0))
    grp = pl.BlockSpec((NSA_G, tm, 128), lambda i: (0, i, 0))
    out_shape = [
        jax.ShapeDtypeStruct((t, MLA_HEADS * MLA_QK_PAD), BF16),
        jax.ShapeDtypeStruct((t, MLA_HEADS * MLA_QK_PAD), BF16),
        jax.ShapeDtypeStruct((t, MLA_HEADS * MLA_V), BF16),
        jax.ShapeDtypeStruct((t, NSA_HEADS * NSA_D), BF16),
        jax.ShapeDtypeStruct((NSA_G, t, NSA_D), F32),
        jax.ShapeDtypeStruct((NSA_G, t, NSA_D), F32),
        jax.ShapeDtypeStruct((t, NSA_G * NSA_D), BF16),
        jax.ShapeDtypeStruct((t, NSA_G * NSA_D), BF16),
        jax.ShapeDtypeStruct((t, NSA_G * NSA_D), BF16),
        jax.ShapeDtypeStruct((t, NSA_G * NSA_D), BF16),
        jax.ShapeDtypeStruct((NSA_G, t, 128), F32),
    ]
    out_specs = [row(2048), row(2048), row(1024), row(1024), grp, grp, row(256), row(256), row(256), row(256), grp]
    in_specs = [row(D_MODEL), _const_spec((1, D_MODEL)), _const_spec((D_MODEL, N_IN)),
                _const_spec((1, MLA_Q_RANK)), _const_spec((1, MLA_KV_RANK)),
                _const_spec((MLA_Q_RANK, MLA_HEADS * MLA_QK_PAD)), _const_spec((MLA_KV_RANK, MLA_HEADS * 128)),
                _const_spec((MLA_KV_RANK, MLA_HEADS * 128)), row(6 * 128)]
    return pl.pallas_call(
        _in_proj_body, out_shape=out_shape, grid=(t // tm,), in_specs=in_specs, out_specs=out_specs,
        compiler_params=_params(("parallel",)), name="in_proj",
    )(h, gmix, w1, gcq, gckv, wq, wkn, wv, tab)


def _compress_body(k_ref, v_ref, pek_ref, pev_ref, wk1_ref, wk2_ref, wv1_ref, wv2_ref, ko_ref, vo_ref):
    half = CMP_STRIDE * NSA_D

    def phi(x, pe, w1_ref, w2_ref):
        a = _mm((x + pe[:, :half]).astype(BF16), w1_ref[:half, :])
        b = _mm((x + pe[:, half:]).astype(BF16), w1_ref[half:, :])
        pre = a + pltpu.roll(b, b.shape[0] - 1, 0)
        return _mm(jax.nn.gelu(pre).astype(BF16), w2_ref[...])

    ko_ref[...] = phi(k_ref[...], pek_ref[...], wk1_ref, wk2_ref).astype(BF16)
    vo_ref[...] = phi(v_ref[...], pev_ref[...], wv1_ref, wv2_ref).astype(BF16)


def _compress(kc, vc, pek, pev, wk1, wk2, wv1, wv2):
    g, b, ncp, w = kc.shape
    in_blk = pl.BlockSpec((None, None, ncp, w), lambda bi, gi: (gi, bi, 0, 0))
    out_blk = pl.BlockSpec((None, None, ncp, NSA_D), lambda bi, gi: (bi, gi, 0, 0))
    out = jax.ShapeDtypeStruct((b, g, ncp, NSA_D), BF16)
    return pl.pallas_call(
        _compress_body, out_shape=[out, out], grid=(b, g),
        in_specs=[in_blk, in_blk, _const_spec(pek.shape), _const_spec(pev.shape), _const_spec(wk1.shape),
                  _const_spec(wk2.shape), _const_spec(wv1.shape), _const_spec(wv2.shape)],
        out_specs=[out_blk, out_blk], compiler_params=_params(("parallel", "parallel")), name="compress",
    )(kc, vc, pek, pev, wk1, wk2, wv1, wv2)


def _mla_body(q_ref, k_ref, v_ref, o_ref, *, tq, n_q):
    i = pl.program_id(2)
    q = q_ref[...]
    tri = lax.broadcasted_iota(jnp.int32, (tq, tq), 1) <= lax.broadcasted_iota(jnp.int32, (tq, tq), 0)
    for ci in range(n_q):
        @pl.when(i == ci)
        def _(ci=ci):
            lo = ci * tq
            sd = jnp.where(tri, _mm_nt(q, k_ref[lo:lo + tq, :]), NEG)
            m = jnp.max(sd, axis=1, keepdims=True)
            if ci > 0:
                sf = _mm_nt(q, k_ref[0:lo, :])
                m = jnp.maximum(m, jnp.max(sf, axis=1, keepdims=True))
                pf = jnp.exp(sf - m)
            pd = jnp.exp(sd - m)
            l = jnp.sum(pd, axis=1, keepdims=True)
            acc = _mm(pd.astype(BF16), v_ref[lo:lo + tq, :])
            if ci > 0:
                l = l + jnp.sum(pf, axis=1, keepdims=True)
                acc = acc + _mm(pf.astype(BF16), v_ref[0:lo, :])
            o_ref[...] = acc / l


def _mla_attn(q, k, v, tq=512):
    b, s, _ = q.shape
    return pl.pallas_call(
        functools.partial(_mla_body, tq=tq, n_q=s // tq),
        out_shape=jax.ShapeDtypeStruct((b, s, MLA_HEADS * MLA_V), F32),
        grid=(b, MLA_HEADS, s // tq),
        in_specs=[pl.BlockSpec((None, tq, MLA_QK_PAD), lambda bi, hi, i: (bi, i, hi)),
                  pl.BlockSpec((None, s, MLA_QK_PAD), lambda bi, hi, i: (bi, 0, hi)),
                  pl.BlockSpec((None, s, MLA_V), lambda bi, hi, i: (bi, 0, hi))],
        out_specs=pl.BlockSpec((None, tq, MLA_V), lambda bi, hi, i: (bi, i, hi)),
        compiler_params=_params(("parallel", "parallel", "arbitrary")), name="mla_attn",
    )(q, k, v)


SEL_BUCKET = 512
WIN_LEN = WINDOW + QB


def _nsa_body(q_ref, kc_ref, vc_ref, ks_ref, vs_ref, kw_ref, vw_ref, gt_ref, ovt_ref, exp_ref, o_ref, os_ref, *,
              n_buckets):
    i = pl.program_id(2)
    ns = ovt_ref.shape[0]
    ncp = ovt_ref.shape[1]
    rows = NSA_HPG * QB
    q = q_ref[...]
    qst = jnp.concatenate([q[:, h * 128:(h + 1) * 128] for h in range(NSA_HPG)], axis=0)

    def tile4(x):
        return jnp.concatenate([x] * NSA_HPG, axis=0)

    pos_r = i * QB + (lax.broadcasted_iota(jnp.int32, (rows, ncp), 0) & (QB - 1))
    cend = lax.broadcasted_iota(jnp.int32, (rows, ncp), 1) * CMP_STRIDE + (CMP_LEN - 1)
    vis = cend <= pos_r
    s = jnp.where(vis, _mm_nt(qst, kc_ref[...]), NEG)
    e = jnp.where(vis, jnp.exp(s - jnp.max(s, axis=1, keepdims=True)), 0.0)
    l = jnp.sum(e, axis=1, keepdims=True)
    p_c = e / jnp.where(l > 0.0, l, 1.0)
    o_c = _mm(p_c.astype(BF16), vc_ref[...])

    psum = p_c[0:QB] + p_c[QB:2 * QB] + p_c[2 * QB:3 * QB] + p_c[3 * QB:4 * QB]
    p_hi = psum.astype(BF16)
    p_lo = (psum - p_hi.astype(F32)).astype(BF16)
    imp = _mm_nt(ovt_ref[...], p_hi) + _mm_nt(ovt_ref[...], p_lo)
    n_i = lax.broadcasted_iota(jnp.int32, (ns, QB), 0)
    pos_l = i * QB + lax.broadcasted_iota(jnp.int32, (ns, QB), 1)
    forced = (n_i == 0) | (n_i == pos_l // SEL_BLOCK)
    val = jnp.where(forced, FORCE_SCORE, jnp.where(n_i * SEL_BLOCK <= pos_l, imp, -FORCE_SCORE))
    rank = jnp.zeros((ns, QB), F32)
    for m_ in range(ns):
        vm = val[m_:m_ + 1, :]
        rank = rank + jnp.where((vm > val) | ((vm == val) & (n_i > m_)), 1.0, 0.0)
    sel_t = jnp.where(rank < float(min(SEL_TOPN, ns)), 1.0, 0.0).astype(BF16)
    eye = jnp.where(lax.broadcasted_iota(jnp.int32, (QB, QB), 0) == lax.broadcasted_iota(jnp.int32, (QB, QB), 1),
                    1.0, 0.0).astype(BF16)
    sel = _mm_nt(eye, sel_t).astype(BF16)

    for cb in range(n_buckets):
        @pl.when(i // (SEL_BUCKET // QB) == cb)
        def _(cb=cb):
            n_keys = (cb + 1) * SEL_BUCKET
            member = _mm(sel, exp_ref[:, 0:n_keys])
            qrow = i * QB + lax.broadcasted_iota(jnp.int32, (QB, n_keys), 0)
            ok = (member > 0.5) & (lax.broadcasted_iota(jnp.int32, (QB, n_keys), 1) <= qrow)
            sc = _mm_nt(qst, ks_ref[0:n_keys, :]) + tile4(jnp.where(ok, 0.0, NEG))
            p = jnp.exp(sc - jnp.max(sc, axis=1, keepdims=True))
            os_ref[...] = _mm(p.astype(BF16), vs_ref[0:n_keys, :]) / jnp.sum(p, axis=1, keepdims=True)

    o_s = os_ref[...]

    start = pl.multiple_of(jnp.maximum(i * QB - WINDOW, 0), QB)
    dist = (i * QB + lax.broadcasted_iota(jnp.int32, (QB, WIN_LEN), 0)) - (
        start + lax.broadcasted_iota(jnp.int32, (QB, WIN_LEN), 1))
    ok = (dist >= 0) & (dist < WINDOW)
    s = _mm_nt(qst, kw_ref[pl.ds(start, WIN_LEN), :]) + tile4(jnp.where(ok, 0.0, NEG))
    p = jnp.exp(s - jnp.max(s, axis=1, keepdims=True))
    o_w = _mm(p.astype(BF16), vw_ref[pl.ds(start, WIN_LEN), :]) / jnp.sum(p, axis=1, keepdims=True)

    g = jax.nn.sigmoid(gt_ref[...])
    for h in range(NSA_HPG):
        sl = slice(h * QB, (h + 1) * QB)
        o_ref[:, h * 128:(h + 1) * 128] = (g[:, 3 * h:3 * h + 1] * o_c[sl] + g[:, 3 * h + 1:3 * h + 2] * o_s[sl]
                                           + g[:, 3 * h + 2:3 * h + 3] * o_w[sl])


def _nsa_attn(q, kcmp, vcmp, ks, vs, kw, vw, gt, ovt, expand):
    b, s, _ = q.shape
    ncp = kcmp.shape[2]
    gw = NSA_HPG * NSA_D
    slab = pl.BlockSpec((None, s, NSA_D), lambda bi, gi, i: (bi, 0, gi))
    cmp = pl.BlockSpec((None, None, ncp, NSA_D), lambda bi, gi, i: (bi, gi, 0, 0))
    return pl.pallas_call(
        functools.partial(_nsa_body, n_buckets=s // SEL_BUCKET),
        out_shape=jax.ShapeDtypeStruct((b, s, NSA_HEADS * NSA_D), F32),
        grid=(b, NSA_G, s // QB),
        in_specs=[pl.BlockSpec((None, QB, gw), lambda bi, gi, i: (bi, i, gi)), cmp, cmp, slab, slab, slab, slab,
                  pl.BlockSpec((None, None, QB, 128), lambda bi, gi, i: (gi, bi, i, 0)),
                  _const_spec(ovt.shape), _const_spec(expand.shape)],
        out_specs=pl.BlockSpec((None, QB, gw), lambda bi, gi, i: (bi, i, gi)),
        scratch_shapes=[pltpu.VMEM((NSA_HPG * QB, NSA_D), F32)],
        compiler_params=_params(("parallel", "parallel", "arbitrary")), name="nsa_attn",
    )(q, kcmp, vcmp, ks, vs, kw, vw, gt, ovt, expand)


def _post_body(ya_ref, yb_ref, h_ref, gya_ref, gyb_ref, wo_ref, gx_ref, wxq_ref, mk_ref, mv_ref, wxo_ref,
               gf_ref, wrh_ref, wrl_ref, br_ref, h2_ref, xn_ref, eid_ref, gate_ref):
    half = wo_ref.shape[0] // 2
    h1 = (h_ref[...] + _mm(_rms(ya_ref[...], gya_ref[...]).astype(BF16), wo_ref[:half, :])
          + _mm(_rms(yb_ref[...], gyb_ref[...]).astype(BF16), wo_ref[half:, :]))

    q = (_mm(_rms(h1, gx_ref[...]).astype(BF16), wxq_ref[...]) * SCALE_X).astype(BF16)
    outs = []
    for hd in range(XH):
        sl = slice(hd * XD, (hd + 1) * XD)
        s = _mm_nt(q[:, sl], mk_ref[:, sl])
        p = jnp.exp(s - jnp.max(s, axis=1, keepdims=True))
        outs.append(_mm(p.astype(BF16), mv_ref[:, sl]) / jnp.sum(p, axis=1, keepdims=True))
    h2 = h1 + _mm(jnp.concatenate(outs, axis=1).astype(BF16), wxo_ref[...])
    h2_ref[...] = h2

    xf = _rms(h2, gf_ref[...])
    xn_ref[...] = xf
    x_hi = xf.astype(BF16)
    x_lo = (xf - x_hi.astype(F32)).astype(BF16)
    logits = _mm(x_hi, wrh_ref[...]) + _mm(x_lo, wrh_ref[...]) + _mm(x_hi, wrl_ref[...]) + br_ref[...]
    lane = lax.broadcasted_iota(jnp.int32, logits.shape, 1)
    lane_f = lane.astype(F32)

    def argmax_first(x):
        v = jnp.max(x, axis=1, keepdims=True)
        return v, jnp.min(jnp.where(x == v, lane_f, 1e9), axis=1, keepdims=True)

    is_grp = lane < N_GROUPS
    lg = jnp.where(is_grp, logits, NEG)
    mx, grp = argmax_first(lg)
    p_top = 1.0 / jnp.sum(jnp.where(is_grp, jnp.exp(lg - mx), 0.0), axis=1, keepdims=True)
    in_grp = (lane >= N_GROUPS) & (lane < N_GROUPS + N_EXPERTS) & (
        ((lane - N_GROUPS) // EXP_PER_GROUP).astype(F32) == grp)
    le = jnp.where(in_grp, logits, NEG)
    v1, i1 = argmax_first(le)
    v2, i2 = argmax_first(jnp.where(lane_f == i1, NEG, le))
    t = jnp.exp(v2 - v1)
    g1 = p_top / (1.0 + t)
    g2 = p_top * t / (1.0 + t)
    eid_ref[...] = jnp.where(lane == 0, i1 - N_GROUPS, jnp.where(lane == 1, i2 - N_GROUPS, 0.0)).astype(jnp.int32)
    gate_ref[...] = jnp.where(lane == 0, g1, jnp.where(lane == 1, g2, 0.0))


def _post_mixer(ya, yb, h, gya, gyb, wo, gx, wxq, memkv, layer, wxo, gf, wrh, wrl, br, tm=256):
    b, s, _ = ya.shape
    row = lambda n: pl.BlockSpec((None, tm, n), lambda bi, i: (bi, i, 0))
    hw = XH * XD
    in_specs = [row(1024), row(1024), row(D_MODEL), _const_spec(gya.shape), _const_spec(gyb.shape),
                _const_spec(wo.shape), _const_spec(gx.shape), _const_spec(wxq.shape),
                pl.BlockSpec((None, MEM_LEN, hw), lambda bi, i: (bi, 0, 2 * layer)),
                pl.BlockSpec((None, MEM_LEN, hw), lambda bi, i: (bi, 0, 2 * layer + 1)),
                _const_spec(wxo.shape), _const_spec(gf.shape), _const_spec(wrh.shape), _const_spec(wrl.shape),
                _const_spec(br.shape)]
    out_shape = [jax.ShapeDtypeStruct((b, s, D_MODEL), F32), jax.ShapeDtypeStruct((b, s, D_MODEL), F32),
                 jax.ShapeDtypeStruct((b, s, 128), jnp.int32), jax.ShapeDtypeStruct((b, s, 128), F32)]
    return pl.pallas_call(
        _post_body, out_shape=out_shape, grid=(b, s // tm), in_specs=in_specs,
        out_specs=[row(D_MODEL), row(D_MODEL), row(128), row(128)],
        compiler_params=_params(("parallel", "parallel")), name="post_mixer",
    )(ya, yb, h, gya, gyb, wo, gx, wxq, memkv, memkv, wxo, gf, wrh, wrl, br)


def _memkv_body(m_ref, g_ref, w_ref, o_ref):
    o_ref[...] = _mm(_rms(m_ref[...], g_ref[...]).astype(BF16), w_ref[...]).astype(BF16)


def _memkv(mem, g, w, tm=256):
    t = mem.shape[0]
    n = w.shape[1]
    return pl.pallas_call(
        _memkv_body, out_shape=jax.ShapeDtypeStruct((t, n), BF16), grid=(t // tm,),
        in_specs=[pl.BlockSpec((tm, D_MODEL), lambda i: (i, 0)), _const_spec(g.shape), _const_spec(w.shape)],
        out_specs=pl.BlockSpec((tm, n), lambda i: (i, 0)),
        compiler_params=_params(("parallel",)), name="memkv",
    )(mem, g, w)


def _experts_body(src_ref, dst_ref, be_ref, bf_ref, nu_ref, xn_ref, w1_ref, w3_ref, w2_ref, y_ref,
                  xbuf, ybuf, w1b, w3b, w2b, sem_in, sem_out, *, n_blocks):
    b = pl.program_id(0)
    slot = b % 2

    def in_copy(blk, sl, r):
        return pltpu.make_async_copy(xn_ref.at[pl.ds(src_ref[blk * MOE_BLOCK + r], 1), :],
                                     xbuf.at[sl, pl.ds(r, 1), :], sem_in.at[sl])

    def out_copy(blk, sl, r):
        return pltpu.make_async_copy(ybuf.at[sl, pl.ds(r, 1), :],
                                     y_ref.at[pl.ds(dst_ref[blk * MOE_BLOCK + r], 1), :], sem_out.at[sl])

    @pl.when(b == 0)
    def _():
        ybuf[...] = jnp.zeros_like(ybuf)
        for r in range(MOE_BLOCK):
            in_copy(0, 0, r).start()
        for sl in range(2):
            pad_rows = y_ref.at[pl.ds(y_ref.shape[0] - (2 - sl) * MOE_BLOCK, MOE_BLOCK), :]
            pltpu.make_async_copy(ybuf.at[sl], pad_rows, sem_out.at[sl]).start()
            pltpu.make_async_copy(ybuf.at[sl], pad_rows, sem_out.at[sl]).wait()

    @pl.when(b + 1 < n_blocks)
    def _():
        for r in range(MOE_BLOCK):
            in_copy(b + 1, 1 - slot, r).start()

    for r in range(MOE_BLOCK):
        in_copy(b, slot, r).wait()

    @pl.when(b >= 2)
    def _():
        for r in range(MOE_BLOCK):
            out_copy(b - 2, slot, r).wait()

    @pl.when(b < nu_ref[0])
    def _():
        @pl.when(bf_ref[b] == 1)
        def _():
            w1b[...] = w1_ref[...].astype(BF16)
            w3b[...] = w3_ref[...].astype(BF16)
            w2b[...] = w2_ref[...].astype(BF16)

        x = xbuf[slot].astype(BF16)
        hb = (jax.nn.silu(_mm(x, w1b[...])) * _mm(x, w3b[...])).astype(BF16)
        ybuf[slot] = _mm(hb, w2b[...])

    for r in range(MOE_BLOCK):
        out_copy(b, slot, r).start()

    @pl.when(b == n_blocks - 1)
    def _():
        for r in range(MOE_BLOCK):
            out_copy(b - 1, 1 - slot, r).wait()
        for r in range(MOE_BLOCK):
            out_copy(b, slot, r).wait()


def _experts(slot_src, slot_dst, blk_expert, blk_first, n_used, xn, w_e1, w_e3, w_e2, layer):
    t, d = xn.shape
    n_slots = slot_src.shape[0]
    n_blocks = n_slots // MOE_BLOCK
    assert n_blocks >= 2
    wspec = lambda r, c: pl.BlockSpec((None, None, r, c), lambda b, ss, sd, be, bf, nu: (layer, be[b], 0, 0))
    return pl.pallas_call(
        functools.partial(_experts_body, n_blocks=n_blocks),
        out_shape=jax.ShapeDtypeStruct((TOP_K * t + 2 * MOE_BLOCK, d), F32),
        grid_spec=pltpu.PrefetchScalarGridSpec(
            num_scalar_prefetch=5, grid=(n_blocks,),
            in_specs=[pl.BlockSpec(memory_space=pl.ANY), wspec(d, D_EXPERT), wspec(d, D_EXPERT), wspec(D_EXPERT, d)],
            out_specs=pl.BlockSpec(memory_space=pl.ANY),
            scratch_shapes=[pltpu.VMEM((2, MOE_BLOCK, d), F32), pltpu.VMEM((2, MOE_BLOCK, d), F32),
                            pltpu.VMEM((d, D_EXPERT), BF16), pltpu.VMEM((d, D_EXPERT), BF16),
                            pltpu.VMEM((D_EXPERT, d), BF16),
                            pltpu.SemaphoreType.DMA((2,)), pltpu.SemaphoreType.DMA((2,))]),
        compiler_params=_params(("arbitrary",)), name="experts",
    )(slot_src, slot_dst, blk_expert, blk_first, n_used, xn, w_e1, w_e3, w_e2)


def _combine_body(y1_ref, y2_ref, h_ref, gate_ref, gfin_ref, o_ref, *, final_norm):
    g = gate_ref[...]
    y = h_ref[...] + g[:, 0:1] * y1_ref[...] + g[:, 1:2] * y2_ref[...]
    if final_norm:
        y = _rms(y, gfin_ref[...])
    o_ref[...] = y


def _combine(y12, h, gate, gfin, final_norm, tm=256):
    t, d = h.shape
    nt = t // tm
    return pl.pallas_call(
        functools.partial(_combine_body, final_norm=final_norm),
        out_shape=jax.ShapeDtypeStruct((t, d), F32), grid=(nt,),
        in_specs=[pl.BlockSpec((tm, d), lambda i: (i, 0)), pl.BlockSpec((tm, d), lambda i: (i + nt, 0)),
                  pl.BlockSpec((tm, d), lambda i: (i, 0)), pl.BlockSpec((tm, 128), lambda i: (i, 0)),
                  _const_spec((1, d))],
        out_specs=pl.BlockSpec((tm, d), lambda i: (i, 0)),
        compiler_params=_params(("parallel",)), name="combine",
    )(y12, y12, h, gate, gfin)


def _dispatch(eid):
    n_tok = eid.shape[0]
    n_assign = n_tok * TOP_K
    expert = eid.reshape(-1)
    onehot = jax.nn.one_hot(expert, N_EXPERTS, dtype=jnp.int32)
    rank = jnp.take_along_axis(jnp.cumsum(onehot, axis=0), expert[:, None], axis=1)[:, 0] - 1
    counts = jnp.sum(onehot, axis=0)
    padded = (counts + MOE_BLOCK - 1) // MOE_BLOCK * MOE_BLOCK
    pad_end = jnp.cumsum(padded)
    dest = ((pad_end - padded)[expert] + rank).astype(jnp.int32)
    n_blocks = (n_assign + N_EXPERTS * (MOE_BLOCK - 1) + MOE_BLOCK - 1) // MOE_BLOCK
    n_slots = n_blocks * MOE_BLOCK
    slot_assign = jnp.zeros((n_slots,), jnp.int32).at[dest].set(jnp.arange(1, n_assign + 1, dtype=jnp.int32))
    a = slot_assign - 1
    live = slot_assign > 0
    slot_src = jnp.where(live, a // TOP_K, 0)
    spare = n_assign + jnp.arange(n_slots, dtype=jnp.int32) % (2 * MOE_BLOCK)
    slot_dst = jnp.where(live, (a % TOP_K) * n_tok + a // TOP_K, spare)
    blk_start = jnp.arange(n_blocks, dtype=jnp.int32) * MOE_BLOCK
    blk_expert = jnp.clip(jnp.searchsorted(pad_end, blk_start, side='right'), 0, N_EXPERTS - 1).astype(jnp.int32)
    blk_first = jnp.concatenate([jnp.ones((1,), jnp.int32), (blk_expert[1:] != blk_expert[:-1]).astype(jnp.int32)])
    n_used = (pad_end[-1:] // MOE_BLOCK).astype(jnp.int32)
    return slot_src, slot_dst, blk_expert, blk_first, n_used


def _rope_tables(positions):
    pos = positions.reshape(-1).astype(F32)[:, None]

    def tables(rot, passthrough):
        half = rot // 2
        ang = pos * (ROPE_THETA ** (-jnp.arange(0, rot, 2, dtype=F32) / rot))
        cos, sin = jnp.cos(ang), jnp.sin(ang)
        z = lambda n: jnp.zeros((pos.shape[0], n), F32)
        tail = jnp.full((pos.shape[0], 128 - rot), passthrough, F32)
        return [jnp.concatenate([cos, cos, tail], 1), jnp.concatenate([z(half), sin, z(128 - rot)], 1),
                jnp.concatenate([-sin, z(128 - half)], 1)]

    return jnp.concatenate(tables(MLA_ROPE, 0.0) + tables(NSA_ROT, 1.0), axis=1)


def _selection_constants(s):
    ncp = s // CMP_STRIDE
    ns = s // SEL_BLOCK
    c_start = np.arange(ncp)[None, :] * CMP_STRIDE
    sel_start = np.arange(ns)[:, None] * SEL_BLOCK
    ovt = (c_start <= sel_start + SEL_BLOCK - 1) & (c_start + CMP_LEN - 1 >= sel_start)
    ovt[:, ncp - 1] = False
    expand = np.arange(s)[None, :] // SEL_BLOCK == np.arange(ns)[:, None]
    return jnp.asarray(ovt, BF16), jnp.asarray(expand, BF16)


def _arrange_w_in(w):
    cuts = np.cumsum([0, 512, 256, 64, 1024, 256, 256, 256, 256, 256, 256, 24])
    c_q, c_kv, k_r, q_n, k_c, v_c, k_s, v_s, k_w, v_w, gates = [w[:, cuts[i]:cuts[i + 1]] for i in range(11)]
    z = lambda n: jnp.zeros((w.shape[0], n), w.dtype)
    return jnp.concatenate([c_q, c_kv, k_r, z(64), q_n, k_c, k_s, k_w, v_c, v_s, v_w, gates, z(104)], axis=1).astype(BF16)


def _arrange_w_uq(w):
    w = w.reshape(MLA_Q_RANK, MLA_HEADS, MLA_NOPE + MLA_ROPE)
    w = jnp.pad(w, ((0, 0), (0, 0), (0, MLA_QK_PAD - MLA_NOPE - MLA_ROPE)))
    return w.reshape(MLA_Q_RANK, MLA_HEADS * MLA_QK_PAD).astype(BF16)


def _split_w_ukv(w):
    w = w.reshape(MLA_KV_RANK, MLA_HEADS, MLA_NOPE + MLA_V)
    return (w[:, :, :MLA_NOPE].reshape(MLA_KV_RANK, -1).astype(BF16),
            w[:, :, MLA_NOPE:].reshape(MLA_KV_RANK, -1).astype(BF16))


def _router_weights(w_grp, b_grp, w_exp, b_exp):
    pad = 128 - N_GROUPS - N_EXPERTS
    w = jnp.concatenate([w_grp, w_exp, jnp.zeros((w_grp.shape[0], pad), F32)], axis=1)
    hi = w.astype(BF16)
    lo = (w - hi.astype(F32)).astype(BF16)
    b = jnp.concatenate([b_grp, b_exp, jnp.zeros((pad,), F32)])[None, :]
    return hi, lo, b


def kernel(x, mem, positions, g_mem, g_final, g_mix, w_in, g_cq, w_uq, g_ckv, w_ukv, pe_k, pe_v, w_ck1, w_ck2,
           w_cv1, w_cv2, g_ya, g_yb, w_out, g_x, w_xq, w_xkv, w_xo, g_ffn, w_grp, b_grp, w_exp, b_exp,
           w_e1, w_e3, w_e2):
    b, s, d = x.shape
    depth = w_in.shape[0]
    t = b * s
    ncp = s // CMP_STRIDE
    tab = _rope_tables(positions)
    ovt, expand = _selection_constants(s)
    r2 = lambda v: v.reshape(1, -1)

    w_kv_all = jnp.concatenate([w_xkv[l] for l in range(depth)], axis=1).astype(BF16)
    memkv = _memkv(mem.reshape(-1, d), r2(g_mem), w_kv_all).reshape(b, mem.shape[1], -1)

    h = x.reshape(t, d)
    for l in range(depth):
        wkn, wv = _split_w_ukv(w_ukv[l])
        qa, ka, va, qn, kc, vc, ks, vs, kw, vw, gt = _in_proj(
            h, r2(g_mix[l]), _arrange_w_in(w_in[l]), r2(g_cq[l]), r2(g_ckv[l]), _arrange_w_uq(w_uq[l]), wkn, wv, tab)
        kcmp, vcmp = _compress(
            kc.reshape(NSA_G, b, ncp, CMP_STRIDE * NSA_D), vc.reshape(NSA_G, b, ncp, CMP_STRIDE * NSA_D),
            pe_k[l].reshape(1, -1), pe_v[l].reshape(1, -1), w_ck1[l].astype(BF16), w_ck2[l].astype(BF16),
            w_cv1[l].astype(BF16), w_cv2[l].astype(BF16))
        r3 = lambda a: a.reshape(b, s, -1)
        ya = _mla_attn(r3(qa), r3(ka), r3(va))
        yb = _nsa_attn(r3(qn), kcmp, vcmp, r3(ks), r3(vs), r3(kw), r3(vw), gt.reshape(NSA_G, b, s, 128), ovt, expand)
        wrh, wrl, br = _router_weights(w_grp[l], b_grp[l], w_exp[l], b_exp[l])
        h2, xn, eid, gate = _post_mixer(
            ya, yb, r3(h), r2(g_ya[l]), r2(g_yb[l]), w_out[l].astype(BF16), r2(g_x[l]), w_xq[l].astype(BF16),
            memkv, l, w_xo[l].astype(BF16), r2(g_ffn[l]), wrh, wrl, br)
        slot_src, slot_dst, blk_expert, blk_first, n_used = _dispatch(eid.reshape(t, 128)[:, :TOP_K])
        y12 = _experts(slot_src, slot_dst, blk_expert, blk_first, n_used, xn.reshape(t, d), w_e1, w_e3, w_e2, l)
        h = _combine(y12, h2.reshape(t, d), gate.reshape(t, 128), r2(g_final), final_norm=(l == depth - 1))
    return h.reshape(b, s, d)
```

```python
import functools

import jax
import jax.numpy as jnp
import numpy as np
from jax import lax
from jax.experimental import pallas as pl
from jax.experimental.pallas import tpu as pltpu

F32 = jnp.float32
BF16 = jnp.bfloat16

D_MODEL = 2048
RMS_EPS = 1e-6
ROPE_THETA = 500000.0
NEG = -1e30
FORCE_SCORE = 1e4

MLA_HEADS = 8
MLA_NOPE = 128
MLA_ROPE = 64
MLA_V = 128
MLA_Q_RANK = D_MODEL // 4
MLA_KV_RANK = D_MODEL // 8
MLA_QK_PAD = 256

NSA_HEADS = 8
NSA_G = 2
NSA_HPG = 4
NSA_D = 128
NSA_ROT = 32
CMP_LEN = 32
CMP_STRIDE = 16
CMP_HID = 512
SEL_BLOCK = 64
SEL_TOPN = 8
WINDOW = 512
QB = 128

XH = 4
XD = 128
MEM_LEN = 256

N_GROUPS = 8
EXP_PER_GROUP = 8
N_EXPERTS = 64
TOP_K = 2
D_EXPERT = D_MODEL // 4
MOE_BLOCK = 128

VMEM_LIMIT_BYTES = 56 * 1024 * 1024

C_CQ = 0
C_CKV = 512
C_KR = 768
C_QN = 896
C_KK = 1920
C_VV = 2688
C_GT = 3456
N_IN = 3584

SCALE_A = (MLA_NOPE + MLA_ROPE) ** -0.5
SCALE_N = NSA_D ** -0.5
SCALE_X = XD ** -0.5


def _params(sem):
    return pltpu.CompilerParams(dimension_semantics=sem, vmem_limit_bytes=VMEM_LIMIT_BYTES)


def _mm(a, b):
    return jnp.dot(a, b, preferred_element_type=F32)


def _mm_nt(a, b):
    return lax.dot_general(a, b, (((1,), (1,)), ((), ())), preferred_element_type=F32)


def _rms(x, g):
    return x * lax.rsqrt(jnp.mean(x * x, axis=-1, keepdims=True) + RMS_EPS) * g


def _rope(x, c, sa, sb, half):
    return x * c + pltpu.roll(x, half, 1) * sa + pltpu.roll(x, 128 - half, 1) * sb


SLAB_ROWS = D_MODEL // 128
SLAB_PITCH_VMEM = 24


def _load_slabs(ref, n_tok, pitch):
    return jnp.concatenate([ref[pl.ds(c, n_tok, stride=pitch), :] for c in range(SLAB_ROWS)], axis=1)


def _store_slabs(ref, x, pitch):
    for c in range(SLAB_ROWS):
        ref[pl.ds(c, x.shape[0], stride=pitch), :] = x[:, c * 128:(c + 1) * 128]


def _const_spec(shape):
    nd = len(shape)
    return pl.BlockSpec(shape, lambda *_: (0,) * nd, pipeline_mode=pl.Buffered(1))


def _in_proj_body(h_ref, gmix_ref, w1_ref, gcq_ref, gckv_ref, wq_ref, wkn_ref, wv_ref, tab_ref,
                  qa_ref, ka_ref, va_ref, qn_ref, kc_ref, vc_ref, ks_ref, vs_ref, kw_ref, vw_ref, gt_ref):
    xn = _rms(h_ref[...], gmix_ref[...]).astype(BF16)
    tab = tab_ref[...]
    ca, saa, sba, cn, san, sbn = [tab[:, i * 128:(i + 1) * 128] for i in range(6)]

    xq = _rms(_mm(xn, w1_ref[:, C_CQ:C_CKV]), gcq_ref[...]).astype(BF16)
    ckr = _mm(xn, w1_ref[:, C_CKV:C_QN])
    xkv = _rms(ckr[:, :MLA_KV_RANK], gckv_ref[...]).astype(BF16)
    kr = _rope(ckr[:, MLA_KV_RANK:], ca, saa, sba, MLA_ROPE // 2).astype(BF16)
    q = _mm(xq, wq_ref[...])
    kn = _mm(xkv, wkn_ref[...])
    for h in range(MLA_HEADS):
        lo = h * MLA_QK_PAD
        qa_ref[:, lo:lo + 128] = (q[:, lo:lo + 128] * SCALE_A).astype(BF16)
        qa_ref[:, lo + 128:lo + 256] = (
            _rope(q[:, lo + 128:lo + 256], ca, saa, sba, MLA_ROPE // 2) * SCALE_A).astype(BF16)
        ka_ref[:, lo:lo + 128] = kn[:, h * 128:(h + 1) * 128].astype(BF16)
        ka_ref[:, lo + 128:lo + 256] = kr
    va_ref[...] = _mm(xkv, wv_ref[...]).astype(BF16)

    qn = _mm(xn, w1_ref[:, C_QN:C_KK])
    for h in range(NSA_HEADS):
        sl = slice(h * 128, (h + 1) * 128)
        qn_ref[:, sl] = (_rope(qn[:, sl], cn, san, sbn, NSA_ROT // 2) * SCALE_N).astype(BF16)
    kk = _mm(xn, w1_ref[:, C_KK:C_VV])
    vv = _mm(xn, w1_ref[:, C_VV:C_GT])
    for g in range(NSA_G):
        sl = slice(g * 128, (g + 1) * 128)
        kc_ref[g] = _rope(kk[:, sl], cn, san, sbn, NSA_ROT // 2)
        ks_ref[:, sl] = _rope(kk[:, 256 + g * 128:256 + (g + 1) * 128], cn, san, sbn, NSA_ROT // 2).astype(BF16)
        kw_ref[:, sl] = _rope(kk[:, 512 + g * 128:512 + (g + 1) * 128], cn, san, sbn, NSA_ROT // 2).astype(BF16)
        vc_ref[g] = vv[:, sl]
    vs_ref[...] = vv[:, 256:512].astype(BF16)
    vw_ref[...] = vv[:, 512:768].astype(BF16)
    gl = _mm(xn, w1_ref[:, C_GT:N_IN])
    gt_ref[0] = gl
    gt_ref[1] = pltpu.roll(gl, 128 - NSA_HPG * 3, 1)


def _in_proj(h, gmix, w1, gcq, gckv, wq, wkn, wv, tab, tm=256):
    t = h.shape[0]
    row = lambda n: pl.BlockSpec((tm, n), lambda i: (i, 0))
    grp = pl.BlockSpec((NSA_G, tm, 128), lambda i: (0, i, 0))
    out_shape = [
        jax.ShapeDtypeStruct((t, MLA_HEADS * MLA_QK_PAD), BF16),
        jax.ShapeDtypeStruct((t, MLA_HEADS * MLA_QK_PAD), BF16),
        jax.ShapeDtypeStruct((t, MLA_HEADS * MLA_V), BF16),
        jax.ShapeDtypeStruct((t, NSA_HEADS * NSA_D), BF16),
        jax.ShapeDtypeStruct((NSA_G, t, NSA_D), F32),
        jax.ShapeDtypeStruct((NSA_G, t, NSA_D), F32),
        jax.ShapeDtypeStruct((t, NSA_G * NSA_D), BF16),
        jax.ShapeDtypeStruct((t, NSA_G * NSA_D), BF16),
        jax.ShapeDtypeStruct((t, NSA_G * NSA_D), BF16),
        jax.ShapeDtypeStruct((t, NSA_G * NSA_D), BF16),
        jax.ShapeDtypeStruct((NSA_G, t, 128), F32),
    ]
    out_specs = [row(2048), row(2048), row(1024), row(1024), grp, grp, row(256), row(256), row(256), row(256), grp]
    in_specs = [row(D_MODEL), _const_spec((1, D_MODEL)), _const_spec((D_MODEL, N_IN)),
                _const_spec((1, MLA_Q_RANK)), _const_spec((1, MLA_KV_RANK)),
                _const_spec((MLA_Q_RANK, MLA_HEADS * MLA_QK_PAD)), _const_spec((MLA_KV_RANK, MLA_HEADS * 128)),
                _const_spec((MLA_KV_RANK, MLA_HEADS * 128)), row(6 * 128)]
    return pl.pallas_call(
        _in_proj_body, out_shape=out_shape, grid=(t // tm,), in_specs=in_specs, out_specs=out_specs,
        compiler_params=_params(("parallel",)), name="in_proj",
    )(h, gmix, w1, gcq, gckv, wq, wkn, wv, tab)


def _compress_body(k_ref, v_ref, pek_ref, pev_ref, wk1_ref, wk2_ref, wv1_ref, wv2_ref, ko_ref, vo_ref):
    half = CMP_STRIDE * NSA_D

    def phi(x, pe, w1_ref, w2_ref):
        a = _mm((x + pe[:, :half]).astype(BF16), w1_ref[:half, :])
        b = _mm((x + pe[:, half:]).astype(BF16), w1_ref[half:, :])
        pre = a + pltpu.roll(b, b.shape[0] - 1, 0)
        return _mm(jax.nn.gelu(pre).astype(BF16), w2_ref[...])

    ko_ref[...] = phi(k_ref[...], pek_ref[...], wk1_ref, wk2_ref).astype(BF16)
    vo_ref[...] = phi(v_ref[...], pev_ref[...], wv1_ref, wv2_ref).astype(BF16)


def _compress(kc, vc, pek, pev, wk1, wk2, wv1, wv2):
    g, b, ncp, w = kc.shape
    in_blk = pl.BlockSpec((None, None, ncp, w), lambda bi, gi: (gi, bi, 0, 0))
    out_blk = pl.BlockSpec((None, None, ncp, NSA_D), lambda bi, gi: (bi, gi, 0, 0))
    out = jax.ShapeDtypeStruct((b, g, ncp, NSA_D), BF16)
    return pl.pallas_call(
        _compress_body, out_shape=[out, out], grid=(b, g),
        in_specs=[in_blk, in_blk, _const_spec(pek.shape), _const_spec(pev.shape), _const_spec(wk1.shape),
                  _const_spec(wk2.shape), _const_spec(wv1.shape), _const_spec(wv2.shape)],
        out_specs=[out_blk, out_blk], compiler_params=_params(("parallel", "parallel")), name="compress",
    )(kc, vc, pek, pev, wk1, wk2, wv1, wv2)


def _mla_body(q_ref, k_ref, v_ref, o_ref, *, tq, n_q):
    i = pl.program_id(2)
    q = q_ref[...]
    tri = lax.broadcasted_iota(jnp.int32, (tq, tq), 1) <= lax.broadcasted_iota(jnp.int32, (tq, tq), 0)
    for ci in range(n_q):
        @pl.when(i == ci)
        def _(ci=ci):
            lo = ci * tq
            sd = jnp.where(tri, _mm_nt(q, k_ref[lo:lo + tq, :]), NEG)
            m = jnp.max(sd, axis=1, keepdims=True)
            if ci > 0:
                sf = _mm_nt(q, k_ref[0:lo, :])
                m = jnp.maximum(m, jnp.max(sf, axis=1, keepdims=True))
                pf = jnp.exp(sf - m)
            pd = jnp.exp(sd - m)
            l = jnp.sum(pd, axis=1, keepdims=True)
            acc = _mm(pd.astype(BF16), v_ref[lo:lo + tq, :])
            if ci > 0:
                l = l + jnp.sum(pf, axis=1, keepdims=True)
                acc = acc + _mm(pf.astype(BF16), v_ref[0:lo, :])
            o_ref[...] = acc / l


def _mla_attn(q, k, v, tq=512):
    b, s, _ = q.shape
    return pl.pallas_call(
        functools.partial(_mla_body, tq=tq, n_q=s // tq),
        out_shape=jax.ShapeDtypeStruct((b, s, MLA_HEADS * MLA_V), F32),
        grid=(b, MLA_HEADS, s // tq),
        in_specs=[pl.BlockSpec((None, tq, MLA_QK_PAD), lambda bi, hi, i: (bi, i, hi)),
                  pl.BlockSpec((None, s, MLA_QK_PAD), lambda bi, hi, i: (bi, 0, hi)),
                  pl.BlockSpec((None, s, MLA_V), lambda bi, hi, i: (bi, 0, hi))],
        out_specs=pl.BlockSpec((None, tq, MLA_V), lambda bi, hi, i: (bi, i, hi)),
        compiler_params=_params(("parallel", "parallel", "arbitrary")), name="mla_attn",
    )(q, k, v)


SEL_BUCKET = 512
WIN_LEN = WINDOW + QB


def _nsa_body(q_ref, kc_ref, vc_ref, ks_ref, vs_ref, kw_ref, vw_ref, gt_ref, ovt_ref, exp_ref, o_ref, os_ref, *,
              n_buckets):
    i = pl.program_id(2)
    ns = ovt_ref.shape[0]
    ncp = ovt_ref.shape[1]
    rows = NSA_HPG * QB
    q = q_ref[...]
    qst = jnp.concatenate([q[:, h * 128:(h + 1) * 128] for h in range(NSA_HPG)], axis=0)

    def tile4(x):
        return jnp.concatenate([x] * NSA_HPG, axis=0)

    pos_r = i * QB + (lax.broadcasted_iota(jnp.int32, (rows, ncp), 0) & (QB - 1))
    cend = lax.broadcasted_iota(jnp.int32, (rows, ncp), 1) * CMP_STRIDE + (CMP_LEN - 1)
    vis = cend <= pos_r
    s = jnp.where(vis, _mm_nt(qst, kc_ref[...]), NEG)
    e = jnp.where(vis, jnp.exp(s - jnp.max(s, axis=1, keepdims=True)), 0.0)
    l = jnp.sum(e, axis=1, keepdims=True)
    p_c = e / jnp.where(l > 0.0, l, 1.0)
    o_c = _mm(p_c.astype(BF16), vc_ref[...])

    psum = p_c[0:QB] + p_c[QB:2 * QB] + p_c[2 * QB:3 * QB] + p_c[3 * QB:4 * QB]
    p_hi = psum.astype(BF16)
    p_lo = (psum - p_hi.astype(F32)).astype(BF16)
    imp = _mm_nt(ovt_ref[...], p_hi) + _mm_nt(ovt_ref[...], p_lo)
    n_i = lax.broadcasted_iota(jnp.int32, (ns, QB), 0)
    pos_l = i * QB + lax.broadcasted_iota(jnp.int32, (ns, QB), 1)
    forced = (n_i == 0) | (n_i == pos_l // SEL_BLOCK)
    val = jnp.where(forced, FORCE_SCORE, jnp.where(n_i * SEL_BLOCK <= pos_l, imp, -FORCE_SCORE))
    rank = jnp.zeros((ns, QB), F32)
    for m_ in range(ns):
        vm = val[m_:m_ + 1, :]
        rank = rank + jnp.where((vm > val) | ((vm == val) & (n_i > m_)), 1.0, 0.0)
    sel_t = jnp.where(rank < float(min(SEL_TOPN, ns)), 1.0, 0.0).astype(BF16)
    eye = jnp.where(lax.broadcasted_iota(jnp.int32, (QB, QB), 0) == lax.broadcasted_iota(jnp.int32, (QB, QB), 1),
                    1.0, 0.0).astype(BF16)
    sel = _mm_nt(eye, sel_t).astype(BF16)

    for cb in range(n_buckets):
        @pl.when(i // (SEL_BUCKET // QB) == cb)
        def _(cb=cb):
            n_keys = (cb + 1) * SEL_BUCKET
            member = _mm(sel, exp_ref[:, 0:n_keys])
            qrow = i * QB + lax.broadcasted_iota(jnp.int32, (QB, n_keys), 0)
            ok = (member > 0.5) & (lax.broadcasted_iota(jnp.int32, (QB, n_keys), 1) <= qrow)
            sc = _mm_nt(qst, ks_ref[0:n_keys, :]) + tile4(jnp.where(ok, 0.0, NEG))
            p = jnp.exp(sc - jnp.max(sc, axis=1, keepdims=True))
            os_ref[...] = _mm(p.astype(BF16), vs_ref[0:n_keys, :]) / jnp.sum(p, axis=1, keepdims=True)

    o_s = os_ref[...]

    start = pl.multiple_of(jnp.maximum(i * QB - WINDOW, 0), QB)
    dist = (i * QB + lax.broadcasted_iota(jnp.int32, (QB, WIN_LEN), 0)) - (
        start + lax.broadcasted_iota(jnp.int32, (QB, WIN_LEN), 1))
    ok = (dist >= 0) & (dist < WINDOW)
    s = _mm_nt(qst, kw_ref[pl.ds(start, WIN_LEN), :]) + tile4(jnp.where(ok, 0.0, NEG))
    p = jnp.exp(s - jnp.max(s, axis=1, keepdims=True))
    o_w = _mm(p.astype(BF16), vw_ref[pl.ds(start, WIN_LEN), :]) / jnp.sum(p, axis=1, keepdims=True)

    g = jax.nn.sigmoid(gt_ref[...])
    for h in range(NSA_HPG):
        sl = slice(h * QB, (h + 1) * QB)
        o_ref[:, h * 128:(h + 1) * 128] = (g[:, 3 * h:3 * h + 1] * o_c[sl] + g[:, 3 * h + 1:3 * h + 2] * o_s[sl]
                                           + g[:, 3 * h + 2:3 * h + 3] * o_w[sl])


def _nsa_attn(q, kcmp, vcmp, ks, vs, kw, vw, gt, ovt, expand):
    b, s, _ = q.shape
    ncp = kcmp.shape[2]
    gw = NSA_HPG * NSA_D
    slab = pl.BlockSpec((None, s, NSA_D), lambda bi, gi, i: (bi, 0, gi))
    cmp = pl.BlockSpec((None, None, ncp, NSA_D), lambda bi, gi, i: (bi, gi, 0, 0))
    return pl.pallas_call(
        functools.partial(_nsa_body, n_buckets=s // SEL_BUCKET),
        out_shape=jax.ShapeDtypeStruct((b, s, NSA_HEADS * NSA_D), F32),
        grid=(b, NSA_G, s // QB),
        in_specs=[pl.BlockSpec((None, QB, gw), lambda bi, gi, i: (bi, i, gi)), cmp, cmp, slab, slab, slab, slab,
                  pl.BlockSpec((None, None, QB, 128), lambda bi, gi, i: (gi, bi, i, 0)),
                  _const_spec(ovt.shape), _const_spec(expand.shape)],
        out_specs=pl.BlockSpec((None, QB, gw), lambda bi, gi, i: (bi, i, gi)),
        scratch_shapes=[pltpu.VMEM((NSA_HPG * QB, NSA_D), F32)],
        compiler_params=_params(("parallel", "parallel", "arbitrary")), name="nsa_attn",
    )(q, kcmp, vcmp, ks, vs, kw, vw, gt, ovt, expand)


def _post_body(ya_ref, yb_ref, h_ref, gya_ref, gyb_ref, wo_ref, gx_ref, wxq_ref, mk_ref, mv_ref, wxo_ref,
               gf_ref, wrh_ref, wrl_ref, br_ref, h2_ref, xn_ref, eid_ref, gate_ref):
    half = wo_ref.shape[0] // 2
    h1 = (h_ref[...] + _mm(_rms(ya_ref[...], gya_ref[...]).astype(BF16), wo_ref[:half, :])
          + _mm(_rms(yb_ref[...], gyb_ref[...]).astype(BF16), wo_ref[half:, :]))

    q = (_mm(_rms(h1, gx_ref[...]).astype(BF16), wxq_ref[...]) * SCALE_X).astype(BF16)
    outs = []
    for hd in range(XH):
        sl = slice(hd * XD, (hd + 1) * XD)
        s = _mm_nt(q[:, sl], mk_ref[:, sl])
        p = jnp.exp(s - jnp.max(s, axis=1, keepdims=True))
        outs.append(_mm(p.astype(BF16), mv_ref[:, sl]) / jnp.sum(p, axis=1, keepdims=True))
    h2 = h1 + _mm(jnp.concatenate(outs, axis=1).astype(BF16), wxo_ref[...])
    h2_ref[...] = h2

    xf = _rms(h2, gf_ref[...])
    _store_slabs(xn_ref, xf, SLAB_ROWS)
    x_hi = xf.astype(BF16)
    x_lo = (xf - x_hi.astype(F32)).astype(BF16)
    logits = _mm(x_hi, wrh_ref[...]) + _mm(x_lo, wrh_ref[...]) + _mm(x_hi, wrl_ref[...]) + br_ref[...]
    lane = lax.broadcasted_iota(jnp.int32, logits.shape, 1)
    lane_f = lane.astype(F32)

    def argmax_first(x):
        v = jnp.max(x, axis=1, keepdims=True)
        return v, jnp.min(jnp.where(x == v, lane_f, 1e9), axis=1, keepdims=True)

    is_grp = lane < N_GROUPS
    lg = jnp.where(is_grp, logits, NEG)
    mx, grp = argmax_first(lg)
    p_top = 1.0 / jnp.sum(jnp.where(is_grp, jnp.exp(lg - mx), 0.0), axis=1, keepdims=True)
    in_grp = (lane >= N_GROUPS) & (lane < N_GROUPS + N_EXPERTS) & (
        ((lane - N_GROUPS) // EXP_PER_GROUP).astype(F32) == grp)
    le = jnp.where(in_grp, logits, NEG)
    v1, i1 = argmax_first(le)
    v2, i2 = argmax_first(jnp.where(lane_f == i1, NEG, le))
    t = jnp.exp(v2 - v1)
    g1 = p_top / (1.0 + t)
    g2 = p_top * t / (1.0 + t)
    eid_ref[...] = jnp.where(lane == 0, i1 - N_GROUPS, jnp.where(lane == 1, i2 - N_GROUPS, 0.0)).astype(jnp.int32)
    gate_ref[...] = jnp.where(lane == 0, g1, jnp.where(lane == 1, g2, 0.0))


def _post_mixer(ya, yb, h, gya, gyb, wo, gx, wxq, memkv, layer, wxo, gf, wrh, wrl, br, tm=256):
    b, s, _ = ya.shape
    row = lambda n: pl.BlockSpec((None, tm, n), lambda bi, i: (bi, i, 0))
    hw = XH * XD
    in_specs = [row(1024), row(1024), row(D_MODEL), _const_spec(gya.shape), _const_spec(gyb.shape),
                _const_spec(wo.shape), _const_spec(gx.shape), _const_spec(wxq.shape),
                pl.BlockSpec((None, MEM_LEN, hw), lambda bi, i: (bi, 0, 2 * layer)),
                pl.BlockSpec((None, MEM_LEN, hw), lambda bi, i: (bi, 0, 2 * layer + 1)),
                _const_spec(wxo.shape), _const_spec(gf.shape), _const_spec(wrh.shape), _const_spec(wrl.shape),
                _const_spec(br.shape)]
    out_shape = [jax.ShapeDtypeStruct((b, s, D_MODEL), F32), jax.ShapeDtypeStruct((b * s * SLAB_ROWS, 128), F32),
                 jax.ShapeDtypeStruct((b, s, 128), jnp.int32), jax.ShapeDtypeStruct((b, s, 128), F32)]
    slabs = pl.BlockSpec((tm * SLAB_ROWS, 128), lambda bi, i: (bi * (s // tm) + i, 0))
    return pl.pallas_call(
        _post_body, out_shape=out_shape, grid=(b, s // tm), in_specs=in_specs,
        out_specs=[row(D_MODEL), slabs, row(128), row(128)],
        compiler_params=_params(("parallel", "parallel")), name="post_mixer",
    )(ya, yb, h, gya, gyb, wo, gx, wxq, memkv, memkv, wxo, gf, wrh, wrl, br)


def _memkv_body(m_ref, g_ref, w_ref, o_ref):
    o_ref[...] = _mm(_rms(m_ref[...], g_ref[...]).astype(BF16), w_ref[...]).astype(BF16)


def _memkv(mem, g, w, tm=256):
    t = mem.shape[0]
    n = w.shape[1]
    return pl.pallas_call(
        _memkv_body, out_shape=jax.ShapeDtypeStruct((t, n), BF16), grid=(t // tm,),
        in_specs=[pl.BlockSpec((tm, D_MODEL), lambda i: (i, 0)), _const_spec(g.shape), _const_spec(w.shape)],
        out_specs=pl.BlockSpec((tm, n), lambda i: (i, 0)),
        compiler_params=_params(("parallel",)), name="memkv",
    )(mem, g, w)


def _experts_body(src_ref, dst_ref, be_ref, bf_ref, nu_ref, xn_ref, w1_ref, w3_ref, w2_ref, y_ref,
                  xbuf, ybuf, w1b, w3b, w2b, sem_in, sem_out, *, n_blocks):
    b = pl.program_id(0)
    slot = b % 2

    def in_copy(blk, sl, r):
        row0 = pl.multiple_of(src_ref[blk * MOE_BLOCK + r] * SLAB_ROWS, SLAB_ROWS)
        return pltpu.make_async_copy(xn_ref.at[pl.ds(row0, SLAB_ROWS), :],
                                     xbuf.at[sl, pl.ds(r * SLAB_PITCH_VMEM, SLAB_ROWS), :], sem_in.at[sl])

    def out_copy(blk, sl, r):
        row0 = pl.multiple_of(dst_ref[blk * MOE_BLOCK + r] * SLAB_ROWS, SLAB_ROWS)
        return pltpu.make_async_copy(ybuf.at[sl, pl.ds(r * SLAB_PITCH_VMEM, SLAB_ROWS), :],
                                     y_ref.at[pl.ds(row0, SLAB_ROWS), :], sem_out.at[sl])

    @pl.when(b == 0)
    def _():
        ybuf[...] = jnp.zeros_like(ybuf)
        for r in range(MOE_BLOCK):
            in_copy(0, 0, r).start()
        spare = MOE_BLOCK * SLAB_ROWS
        for sl in range(2):
            zeros = ybuf.at[sl, pl.ds(0, spare), :]
            pad_rows = y_ref.at[pl.ds(y_ref.shape[0] - (2 - sl) * spare, spare), :]
            pltpu.make_async_copy(zeros, pad_rows, sem_out.at[sl]).start()
            pltpu.make_async_copy(zeros, pad_rows, sem_out.at[sl]).wait()

    @pl.when(b + 1 < n_blocks)
    def _():
        for r in range(MOE_BLOCK):
            in_copy(b + 1, 1 - slot, r).start()

    for r in range(MOE_BLOCK):
        in_copy(b, slot, r).wait()

    @pl.when(b >= 2)
    def _():
        for r in range(MOE_BLOCK):
            out_copy(b - 2, slot, r).wait()

    @pl.when(b < nu_ref[0])
    def _():
        @pl.when(bf_ref[b] == 1)
        def _():
            w1b[...] = w1_ref[...].astype(BF16)
            w3b[...] = w3_ref[...].astype(BF16)
            w2b[...] = w2_ref[...].astype(BF16)

        x = _load_slabs(xbuf.at[slot], MOE_BLOCK, SLAB_PITCH_VMEM).astype(BF16)
        hb = (jax.nn.silu(_mm(x, w1b[...])) * _mm(x, w3b[...])).astype(BF16)
        _store_slabs(ybuf.at[slot], _mm(hb, w2b[...]), SLAB_PITCH_VMEM)

    for r in range(MOE_BLOCK):
        out_copy(b, slot, r).start()

    @pl.when(b == n_blocks - 1)
    def _():
        for r in range(MOE_BLOCK):
            out_copy(b - 1, 1 - slot, r).wait()
        for r in range(MOE_BLOCK):
            out_copy(b, slot, r).wait()


def _experts(slot_src, slot_dst, blk_expert, blk_first, n_used, xn, w_e1, w_e3, w_e2, layer):
    t, d = xn.shape[0] // SLAB_ROWS, D_MODEL
    n_slots = slot_src.shape[0]
    n_blocks = n_slots // MOE_BLOCK
    assert n_blocks >= 2
    wspec = lambda r, c: pl.BlockSpec((None, None, r, c), lambda b, ss, sd, be, bf, nu: (layer, be[b], 0, 0))
    return pl.pallas_call(
        functools.partial(_experts_body, n_blocks=n_blocks),
        out_shape=jax.ShapeDtypeStruct(((TOP_K * t + 2 * MOE_BLOCK) * SLAB_ROWS, 128), F32),
        grid_spec=pltpu.PrefetchScalarGridSpec(
            num_scalar_prefetch=5, grid=(n_blocks,),
            in_specs=[pl.BlockSpec(memory_space=pl.ANY), wspec(d, D_EXPERT), wspec(d, D_EXPERT), wspec(D_EXPERT, d)],
            out_specs=pl.BlockSpec(memory_space=pl.ANY),
            scratch_shapes=[pltpu.VMEM((2, MOE_BLOCK * SLAB_PITCH_VMEM, 128), F32),
                            pltpu.VMEM((2, MOE_BLOCK * SLAB_PITCH_VMEM, 128), F32),
                            pltpu.VMEM((d, D_EXPERT), BF16), pltpu.VMEM((d, D_EXPERT), BF16),
                            pltpu.VMEM((D_EXPERT, d), BF16),
                            pltpu.SemaphoreType.DMA((2,)), pltpu.SemaphoreType.DMA((2,))]),
        compiler_params=_params(("arbitrary",)), name="experts",
    )(slot_src, slot_dst, blk_expert, blk_first, n_used, xn, w_e1, w_e3, w_e2)


def _combine_body(y1_ref, y2_ref, h_ref, gate_ref, gfin_ref, o_ref, *, final_norm):
    g = gate_ref[...]
    n_tok = h_ref.shape[0]
    y = (h_ref[...] + g[:, 0:1] * _load_slabs(y1_ref, n_tok, SLAB_ROWS)
         + g[:, 1:2] * _load_slabs(y2_ref, n_tok, SLAB_ROWS))
    if final_norm:
        y = _rms(y, gfin_ref[...])
    o_ref[...] = y


def _combine(y12, h, gate, gfin, final_norm, tm=256):
    t, d = h.shape
    nt = t // tm
    slabs = lambda off: pl.BlockSpec((tm * SLAB_ROWS, 128), lambda i: (i + off, 0))
    return pl.pallas_call(
        functools.partial(_combine_body, final_norm=final_norm),
        out_shape=jax.ShapeDtypeStruct((t, d), F32), grid=(nt,),
        in_specs=[slabs(0), slabs(nt),
                  pl.BlockSpec((tm, d), lambda i: (i, 0)), pl.BlockSpec((tm, 128), lambda i: (i, 0)),
                  _const_spec((1, d))],
        out_specs=pl.BlockSpec((tm, d), lambda i: (i, 0)),
        compiler_params=_params(("parallel",)), name="combine",
    )(y12, y12, h, gate, gfin)


def _dispatch(eid):
    n_tok = eid.shape[0]
    n_assign = n_tok * TOP_K
    expert = eid.reshape(-1)
    onehot = jax.nn.one_hot(expert, N_EXPERTS, dtype=jnp.int32)
    rank = jnp.take_along_axis(jnp.cumsum(onehot, axis=0), expert[:, None], axis=1)[:, 0] - 1
    counts = jnp.sum(onehot, axis=0)
    padded = (counts + MOE_BLOCK - 1) // MOE_BLOCK * MOE_BLOCK
    pad_end = jnp.cumsum(padded)
    dest = ((pad_end - padded)[expert] + rank).astype(jnp.int32)
    n_blocks = (n_assign + N_EXPERTS * (MOE_BLOCK - 1) + MOE_BLOCK - 1) // MOE_BLOCK
    n_slots = n_blocks * MOE_BLOCK
    slot_assign = jnp.zeros((n_slots,), jnp.int32).at[dest].set(jnp.arange(1, n_assign + 1, dtype=jnp.int32))
    a = slot_assign - 1
    live = slot_assign > 0
    slot_src = jnp.where(live, a // TOP_K, 0)
    spare = n_assign + jnp.arange(n_slots, dtype=jnp.int32) % (2 * MOE_BLOCK)
    slot_dst = jnp.where(live, (a % TOP_K) * n_tok + a // TOP_K, spare)
    blk_start = jnp.arange(n_blocks, dtype=jnp.int32) * MOE_BLOCK
    blk_expert = jnp.clip(jnp.searchsorted(pad_end, blk_start, side='right'), 0, N_EXPERTS - 1).astype(jnp.int32)
    blk_first = jnp.concatenate([jnp.ones((1,), jnp.int32), (blk_expert[1:] != blk_expert[:-1]).astype(jnp.int32)])
    n_used = (pad_end[-1:] // MOE_BLOCK).astype(jnp.int32)
    return slot_src, slot_dst, blk_expert, blk_first, n_used


def _rope_tables(positions):
    pos = positions.reshape(-1).astype(F32)[:, None]

    def tables(rot, passthrough):
        half = rot // 2
        ang = pos * (ROPE_THETA ** (-jnp.arange(0, rot, 2, dtype=F32) / rot))
        cos, sin = jnp.cos(ang), jnp.sin(ang)
        z = lambda n: jnp.zeros((pos.shape[0], n), F32)
        tail = jnp.full((pos.shape[0], 128 - rot), passthrough, F32)
        return [jnp.concatenate([cos, cos, tail], 1), jnp.concatenate([z(half), sin, z(128 - rot)], 1),
                jnp.concatenate([-sin, z(128 - half)], 1)]

    return jnp.concatenate(tables(MLA_ROPE, 0.0) + tables(NSA_ROT, 1.0), axis=1)


def _selection_constants(s):
    ncp = s // CMP_STRIDE
    ns = s // SEL_BLOCK
    c_start = np.arange(ncp)[None, :] * CMP_STRIDE
    sel_start = np.arange(ns)[:, None] * SEL_BLOCK
    ovt = (c_start <= sel_start + SEL_BLOCK - 1) & (c_start + CMP_LEN - 1 >= sel_start)
    ovt[:, ncp - 1] = False
    expand = np.arange(s)[None, :] // SEL_BLOCK == np.arange(ns)[:, None]
    return jnp.asarray(ovt, BF16), jnp.asarray(expand, BF16)


def _arrange_w_in(w):
    cuts = np.cumsum([0, 512, 256, 64, 1024, 256, 256, 256, 256, 256, 256, 24])
    c_q, c_kv, k_r, q_n, k_c, v_c, k_s, v_s, k_w, v_w, gates = [w[:, cuts[i]:cuts[i + 1]] for i in range(11)]
    z = lambda n: jnp.zeros((w.shape[0], n), w.dtype)
    return jnp.concatenate([c_q, c_kv, k_r, z(64), q_n, k_c, k_s, k_w, v_c, v_s, v_w, gates, z(104)], axis=1).astype(BF16)


def _arrange_w_uq(w):
    w = w.reshape(MLA_Q_RANK, MLA_HEADS, MLA_NOPE + MLA_ROPE)
    w = jnp.pad(w, ((0, 0), (0, 0), (0, MLA_QK_PAD - MLA_NOPE - MLA_ROPE)))
    return w.reshape(MLA_Q_RANK, MLA_HEADS * MLA_QK_PAD).astype(BF16)


def _split_w_ukv(w):
    w = w.reshape(MLA_KV_RANK, MLA_HEADS, MLA_NOPE + MLA_V)
    return (w[:, :, :MLA_NOPE].reshape(MLA_KV_RANK, -1).astype(BF16),
            w[:, :, MLA_NOPE:].reshape(MLA_KV_RANK, -1).astype(BF16))


def _router_weights(w_grp, b_grp, w_exp, b_exp):
    pad = 128 - N_GROUPS - N_EXPERTS
    w = jnp.concatenate([w_grp, w_exp, jnp.zeros((w_grp.shape[0], pad), F32)], axis=1)
    hi = w.astype(BF16)
    lo = (w - hi.astype(F32)).astype(BF16)
    b = jnp.concatenate([b_grp, b_exp, jnp.zeros((pad,), F32)])[None, :]
    return hi, lo, b


def kernel(x, mem, positions, g_mem, g_final, g_mix, w_in, g_cq, w_uq, g_ckv, w_ukv, pe_k, pe_v, w_ck1, w_ck2,
           w_cv1, w_cv2, g_ya, g_yb, w_out, g_x, w_xq, w_xkv, w_xo, g_ffn, w_grp, b_grp, w_exp, b_exp,
           w_e1, w_e3, w_e2):
    b, s, d = x.shape
    depth = w_in.shape[0]
    t = b * s
    ncp = s // CMP_STRIDE
    tab = _rope_tables(positions)
    ovt, expand = _selection_constants(s)
    r2 = lambda v: v.reshape(1, -1)

    w_kv_all = jnp.concatenate([w_xkv[l] for l in range(depth)], axis=1).astype(BF16)
    memkv = _memkv(mem.reshape(-1, d), r2(g_mem), w_kv_all).reshape(b, mem.shape[1], -1)

    h = x.reshape(t, d)
    for l in range(depth):
        wkn, wv = _split_w_ukv(w_ukv[l])
        qa, ka, va, qn, kc, vc, ks, vs, kw, vw, gt = _in_proj(
            h, r2(g_mix[l]), _arrange_w_in(w_in[l]), r2(g_cq[l]), r2(g_ckv[l]), _arrange_w_uq(w_uq[l]), wkn, wv, tab)
        kcmp, vcmp = _compress(
            kc.reshape(NSA_G, b, ncp, CMP_STRIDE * NSA_D), vc.reshape(NSA_G, b, ncp, CMP_STRIDE * NSA_D),
            pe_k[l].reshape(1, -1), pe_v[l].reshape(1, -1), w_ck1[l].astype(BF16), w_ck2[l].astype(BF16),
            w_cv1[l].astype(BF16), w_cv2[l].astype(BF16))
        r3 = lambda a: a.reshape(b, s, -1)
        ya = _mla_attn(r3(qa), r3(ka), r3(va))
        yb = _nsa_attn(r3(qn), kcmp, vcmp, r3(ks), r3(vs), r3(kw), r3(vw), gt.reshape(NSA_G, b, s, 128), ovt, expand)
        wrh, wrl, br = _router_weights(w_grp[l], b_grp[l], w_exp[l], b_exp[l])
        h2, xn, eid, gate = _post_mixer(
            ya, yb, r3(h), r2(g_ya[l]), r2(g_yb[l]), w_out[l].astype(BF16), r2(g_x[l]), w_xq[l].astype(BF16),
            memkv, l, w_xo[l].astype(BF16), r2(g_ffn[l]), wrh, wrl, br)
        slot_src, slot_dst, blk_expert, blk_first, n_used = _dispatch(eid.reshape(t, 128)[:, :TOP_K])
        y12 = _experts(slot_src, slot_dst, blk_expert, blk_first, n_used, xn, w_e1, w_e3, w_e2, l)
        h = _combine(y12, h2.reshape(t, d), gate.reshape(t, 128), r2(g_final), final_norm=(l == depth - 1))
    return h.reshape(b, s, d)
```

```python
import functools

import jax
import jax.numpy as jnp
import numpy as np
from jax import lax
from jax.experimental import pallas as pl
from jax.experimental.pallas import tpu as pltpu

F32 = jnp.float32
BF16 = jnp.bfloat16

D_MODEL = 2048
RMS_EPS = 1e-6
ROPE_THETA = 500000.0
NEG = -1e30
FORCE_SCORE = 1e4

MLA_HEADS = 8
MLA_NOPE = 128
MLA_ROPE = 64
MLA_V = 128
MLA_Q_RANK = D_MODEL // 4
MLA_KV_RANK = D_MODEL // 8
MLA_QK_PAD = 256

NSA_HEADS = 8
NSA_G = 2
NSA_HPG = 4
NSA_D = 128
NSA_ROT = 32
CMP_LEN = 32
CMP_STRIDE = 16
CMP_HID = 512
SEL_BLOCK = 64
SEL_TOPN = 8
WINDOW = 512
QB = 128

XH = 4
XD = 128
MEM_LEN = 256

N_GROUPS = 8
EXP_PER_GROUP = 8
N_EXPERTS = 64
TOP_K = 2
D_EXPERT = D_MODEL // 4
MOE_BLOCK = 128

VMEM_LIMIT_BYTES = 56 * 1024 * 1024

C_CQ = 0
C_CKV = 512
C_KR = 768
C_QN = 896
C_KK = 1920
C_VV = 2688
C_GT = 3456
N_IN = 3584

SCALE_A = (MLA_NOPE + MLA_ROPE) ** -0.5
SCALE_N = NSA_D ** -0.5
SCALE_X = XD ** -0.5


def _params(sem):
    return pltpu.CompilerParams(dimension_semantics=sem, vmem_limit_bytes=VMEM_LIMIT_BYTES)


def _mm(a, b):
    return jnp.dot(a, b, preferred_element_type=F32)


def _mm_nt(a, b):
    return lax.dot_general(a, b, (((1,), (1,)), ((), ())), preferred_element_type=F32)


def _rms(x, g):
    return x * lax.rsqrt(jnp.mean(x * x, axis=-1, keepdims=True) + RMS_EPS) * g


def _rope(x, c, sa, sb, half):
    return x * c + pltpu.roll(x, half, 1) * sa + pltpu.roll(x, 128 - half, 1) * sb


SLAB_ROWS = D_MODEL // 128
SLAB_PITCH_VMEM = 24


def _load_slabs(ref, n_tok, pitch):
    return jnp.concatenate([ref[pl.ds(c, n_tok, stride=pitch), :] for c in range(SLAB_ROWS)], axis=1)


def _store_slabs(ref, x, pitch):
    for c in range(SLAB_ROWS):
        ref[pl.ds(c, x.shape[0], stride=pitch), :] = x[:, c * 128:(c + 1) * 128]


def _const_spec(shape):
    nd = len(shape)
    return pl.BlockSpec(shape, lambda *_: (0,) * nd, pipeline_mode=pl.Buffered(1))


def _in_proj_body(h_ref, gmix_ref, w1_ref, gcq_ref, gckv_ref, wq_ref, wkn_ref, wv_ref, tab_ref,
                  qa_ref, ka_ref, va_ref, qn_ref, kc_ref, vc_ref, ks_ref, vs_ref, kw_ref, vw_ref, gt_ref):
    xn = _rms(h_ref[...], gmix_ref[...]).astype(BF16)
    tab = tab_ref[...]
    ca, saa, sba, cn, san, sbn = [tab[:, i * 128:(i + 1) * 128] for i in range(6)]

    xq = _rms(_mm(xn, w1_ref[:, C_CQ:C_CKV]), gcq_ref[...]).astype(BF16)
    ckr = _mm(xn, w1_ref[:, C_CKV:C_QN])
    xkv = _rms(ckr[:, :MLA_KV_RANK], gckv_ref[...]).astype(BF16)
    kr = _rope(ckr[:, MLA_KV_RANK:], ca, saa, sba, MLA_ROPE // 2).astype(BF16)
    q = _mm(xq, wq_ref[...])
    kn = _mm(xkv, wkn_ref[...])
    for h in range(MLA_HEADS):
        lo = h * MLA_QK_PAD
        qa_ref[:, lo:lo + 128] = (q[:, lo:lo + 128] * SCALE_A).astype(BF16)
        qa_ref[:, lo + 128:lo + 256] = (
            _rope(q[:, lo + 128:lo + 256], ca, saa, sba, MLA_ROPE // 2) * SCALE_A).astype(BF16)
        ka_ref[:, lo:lo + 128] = kn[:, h * 128:(h + 1) * 128].astype(BF16)
        ka_ref[:, lo + 128:lo + 256] = kr
    va_ref[...] = _mm(xkv, wv_ref[...]).astype(BF16)

    qn = _mm(xn, w1_ref[:, C_QN:C_KK])
    for h in range(NSA_HEADS):
        sl = slice(h * 128, (h + 1) * 128)
        qn_ref[:, sl] = (_rope(qn[:, sl], cn, san, sbn, NSA_ROT // 2) * SCALE_N).astype(BF16)
    kk = _mm(xn, w1_ref[:, C_KK:C_VV])
    vv = _mm(xn, w1_ref[:, C_VV:C_GT])
    for g in range(NSA_G):
        sl = slice(g * 128, (g + 1) * 128)
        kc_ref[g] = _rope(kk[:, sl], cn, san, sbn, NSA_ROT // 2)
        ks_ref[:, sl] = _rope(kk[:, 256 + g * 128:256 + (g + 1) * 128], cn, san, sbn, NSA_ROT // 2).astype(BF16)
        kw_ref[:, sl] = _rope(kk[:, 512 + g * 128:512 + (g + 1) * 128], cn, san, sbn, NSA_ROT // 2).astype(BF16)
        vc_ref[g] = vv[:, sl]
    vs_ref[...] = vv[:, 256:512].astype(BF16)
    vw_ref[...] = vv[:, 512:768].astype(BF16)
    gl = _mm(xn, w1_ref[:, C_GT:N_IN])
    gt_ref[0] = gl
    gt_ref[1] = pltpu.roll(gl, 128 - NSA_HPG * 3, 1)


def _in_proj(h, gmix, w1, gcq, gckv, wq, wkn, wv, tab, tm=256):
    t = h.shape[0]
    row = lambda n: pl.BlockSpec((tm, n), lambda i: (i, 0))
    grp = pl.BlockSpec((NSA_G, tm, 128), lambda i: (0, i, 0))
    out_shape = [
        jax.ShapeDtypeStruct((t, MLA_HEADS * MLA_QK_PAD), BF16),
        jax.ShapeDtypeStruct((t, MLA_HEADS * MLA_QK_PAD), BF16),
        jax.ShapeDtypeStruct((t, MLA_HEADS * MLA_V), BF16),
        jax.ShapeDtypeStruct((t, NSA_HEADS * NSA_D), BF16),
        jax.ShapeDtypeStruct((NSA_G, t, NSA_D), F32),
        jax.ShapeDtypeStruct((NSA_G, t, NSA_D), F32),
        jax.ShapeDtypeStruct((t, NSA_G * NSA_D), BF16),
        jax.ShapeDtypeStruct((t, NSA_G * NSA_D), BF16),
        jax.ShapeDtypeStruct((t, NSA_G * NSA_D), BF16),
        jax.ShapeDtypeStruct((t, NSA_G * NSA_D), BF16),
        jax.ShapeDtypeStruct((NSA_G, t, 128), F32),
    ]
    out_specs = [row(2048), row(2048), row(1024), row(1024), grp, grp, row(256), row(256), row(256), row(256), grp]
    in_specs = [row(D_MODEL), _const_spec((1, D_MODEL)), _const_spec((D_MODEL, N_IN)),
                _const_spec((1, MLA_Q_RANK)), _const_spec((1, MLA_KV_RANK)),
                _const_spec((MLA_Q_RANK, MLA_HEADS * MLA_QK_PAD)), _const_spec((MLA_KV_RANK, MLA_HEADS * 128)),
                _const_spec((MLA_KV_RANK, MLA_HEADS * 128)), row(6 * 128)]
    return pl.pallas_call(
        _in_proj_body, out_shape=out_shape, grid=(t // tm,), in_specs=in_specs, out_specs=out_specs,
        compiler_params=_params(("parallel",)), name="in_proj",
    )(h, gmix, w1, gcq, gckv, wq, wkn, wv, tab)


def _compress_body(k_ref, v_ref, pek_ref, pev_ref, wk1_ref, wk2_ref, wv1_ref, wv2_ref, ko_ref, vo_ref):
    half = CMP_STRIDE * NSA_D

    def phi(x, pe, w1_ref, w2_ref):
        a = _mm((x + pe[:, :half]).astype(BF16), w1_ref[:half, :])
        b = _mm((x + pe[:, half:]).astype(BF16), w1_ref[half:, :])
        pre = a + pltpu.roll(b, b.shape[0] - 1, 0)
        return _mm(jax.nn.gelu(pre).astype(BF16), w2_ref[...])

    ko_ref[...] = phi(k_ref[...], pek_ref[...], wk1_ref, wk2_ref).astype(BF16)
    vo_ref[...] = phi(v_ref[...], pev_ref[...], wv1_ref, wv2_ref).astype(BF16)


def _compress(kc, vc, pek, pev, wk1, wk2, wv1, wv2):
    g, b, ncp, w = kc.shape
    in_blk = pl.BlockSpec((None, None, ncp, w), lambda bi, gi: (gi, bi, 0, 0))
    out_blk = pl.BlockSpec((None, None, ncp, NSA_D), lambda bi, gi: (bi, gi, 0, 0))
    out = jax.ShapeDtypeStruct((b, g, ncp, NSA_D), BF16)
    return pl.pallas_call(
        _compress_body, out_shape=[out, out], grid=(b, g),
        in_specs=[in_blk, in_blk, _const_spec(pek.shape), _const_spec(pev.shape), _const_spec(wk1.shape),
                  _const_spec(wk2.shape), _const_spec(wv1.shape), _const_spec(wv2.shape)],
        out_specs=[out_blk, out_blk], compiler_params=_params(("parallel", "parallel")), name="compress",
    )(kc, vc, pek, pev, wk1, wk2, wv1, wv2)


def _mla_body(q_ref, k_ref, v_ref, o_ref, *, tq, n_q):
    i = pl.program_id(2)
    q = q_ref[...]
    tri = lax.broadcasted_iota(jnp.int32, (tq, tq), 1) <= lax.broadcasted_iota(jnp.int32, (tq, tq), 0)
    for ci in range(n_q):
        @pl.when(i == ci)
        def _(ci=ci):
            lo = ci * tq
            sd = jnp.where(tri, _mm_nt(q, k_ref[lo:lo + tq, :]), NEG)
            m = jnp.max(sd, axis=1, keepdims=True)
            if ci > 0:
                sf = _mm_nt(q, k_ref[0:lo, :])
                m = jnp.maximum(m, jnp.max(sf, axis=1, keepdims=True))
                pf = jnp.exp(sf - m)
            pd = jnp.exp(sd - m)
            l = jnp.sum(pd, axis=1, keepdims=True)
            acc = _mm(pd.astype(BF16), v_ref[lo:lo + tq, :])
            if ci > 0:
                l = l + jnp.sum(pf, axis=1, keepdims=True)
                acc = acc + _mm(pf.astype(BF16), v_ref[0:lo, :])
            o_ref[...] = acc / l


def _mla_attn(q, k, v, tq=512):
    b, s, _ = q.shape
    return pl.pallas_call(
        functools.partial(_mla_body, tq=tq, n_q=s // tq),
        out_shape=jax.ShapeDtypeStruct((b, s, MLA_HEADS * MLA_V), F32),
        grid=(b, MLA_HEADS, s // tq),
        in_specs=[pl.BlockSpec((None, tq, MLA_QK_PAD), lambda bi, hi, i: (bi, i, hi)),
                  pl.BlockSpec((None, s, MLA_QK_PAD), lambda bi, hi, i: (bi, 0, hi)),
                  pl.BlockSpec((None, s, MLA_V), lambda bi, hi, i: (bi, 0, hi))],
        out_specs=pl.BlockSpec((None, tq, MLA_V), lambda bi, hi, i: (bi, i, hi)),
        compiler_params=_params(("parallel", "parallel", "arbitrary")), name="mla_attn",
    )(q, k, v)


SEL_BUCKET = 512
WIN_LEN = WINDOW + QB


def _nsa_body(q_ref, kc_ref, vc_ref, ks_ref, vs_ref, kw_ref, vw_ref, gt_ref, ovt_ref, exp_ref, o_ref, os_ref, *,
              n_buckets):
    i = pl.program_id(2)
    ns = ovt_ref.shape[0]
    ncp = ovt_ref.shape[1]
    rows = NSA_HPG * QB
    q = q_ref[...]
    qst = jnp.concatenate([q[:, h * 128:(h + 1) * 128] for h in range(NSA_HPG)], axis=0)

    def tile4(x):
        return jnp.concatenate([x] * NSA_HPG, axis=0)

    pos_r = i * QB + (lax.broadcasted_iota(jnp.int32, (rows, ncp), 0) & (QB - 1))
    cend = lax.broadcasted_iota(jnp.int32, (rows, ncp), 1) * CMP_STRIDE + (CMP_LEN - 1)
    vis = cend <= pos_r
    s = jnp.where(vis, _mm_nt(qst, kc_ref[...]), NEG)
    e = jnp.where(vis, jnp.exp(s - jnp.max(s, axis=1, keepdims=True)), 0.0)
    l = jnp.sum(e, axis=1, keepdims=True)
    p_c = e / jnp.where(l > 0.0, l, 1.0)
    o_c = _mm(p_c.astype(BF16), vc_ref[...])

    psum = p_c[0:QB] + p_c[QB:2 * QB] + p_c[2 * QB:3 * QB] + p_c[3 * QB:4 * QB]
    p_hi = psum.astype(BF16)
    p_lo = (psum - p_hi.astype(F32)).astype(BF16)
    imp = _mm_nt(ovt_ref[...], p_hi) + _mm_nt(ovt_ref[...], p_lo)
    n_i = lax.broadcasted_iota(jnp.int32, (ns, QB), 0)
    pos_l = i * QB + lax.broadcasted_iota(jnp.int32, (ns, QB), 1)
    forced = (n_i == 0) | (n_i == pos_l // SEL_BLOCK)
    val = jnp.where(forced, FORCE_SCORE, jnp.where(n_i * SEL_BLOCK <= pos_l, imp, -FORCE_SCORE))
    rank = jnp.zeros((ns, QB), F32)
    for m_ in range(ns):
        vm = val[m_:m_ + 1, :]
        rank = rank + jnp.where((vm > val) | ((vm == val) & (n_i > m_)), 1.0, 0.0)
    sel_t = jnp.where(rank < float(min(SEL_TOPN, ns)), 1.0, 0.0).astype(BF16)
    eye = jnp.where(lax.broadcasted_iota(jnp.int32, (QB, QB), 0) == lax.broadcasted_iota(jnp.int32, (QB, QB), 1),
                    1.0, 0.0).astype(BF16)
    sel = _mm_nt(eye, sel_t).astype(BF16)

    for cb in range(n_buckets):
        @pl.when(i // (SEL_BUCKET // QB) == cb)
        def _(cb=cb):
            n_keys = (cb + 1) * SEL_BUCKET
            member = _mm(sel, exp_ref[:, 0:n_keys])
            qrow = i * QB + lax.broadcasted_iota(jnp.int32, (QB, n_keys), 0)
            ok = (member > 0.5) & (lax.broadcasted_iota(jnp.int32, (QB, n_keys), 1) <= qrow)
            sc = _mm_nt(qst, ks_ref[0:n_keys, :]) + tile4(jnp.where(ok, 0.0, NEG))
            p = jnp.exp(sc - jnp.max(sc, axis=1, keepdims=True))
            os_ref[...] = _mm(p.astype(BF16), vs_ref[0:n_keys, :]) / jnp.sum(p, axis=1, keepdims=True)

    o_s = os_ref[...]

    start = pl.multiple_of(jnp.maximum(i * QB - WINDOW, 0), QB)
    dist = (i * QB + lax.broadcasted_iota(jnp.int32, (QB, WIN_LEN), 0)) - (
        start + lax.broadcasted_iota(jnp.int32, (QB, WIN_LEN), 1))
    ok = (dist >= 0) & (dist < WINDOW)
    s = _mm_nt(qst, kw_ref[pl.ds(start, WIN_LEN), :]) + tile4(jnp.where(ok, 0.0, NEG))
    p = jnp.exp(s - jnp.max(s, axis=1, keepdims=True))
    o_w = _mm(p.astype(BF16), vw_ref[pl.ds(start, WIN_LEN), :]) / jnp.sum(p, axis=1, keepdims=True)

    g = jax.nn.sigmoid(gt_ref[...])
    for h in range(NSA_HPG):
        sl = slice(h * QB, (h + 1) * QB)
        o_ref[:, h * 128:(h + 1) * 128] = (g[:, 3 * h:3 * h + 1] * o_c[sl] + g[:, 3 * h + 1:3 * h + 2] * o_s[sl]
                                           + g[:, 3 * h + 2:3 * h + 3] * o_w[sl])


def _nsa_attn(q, kcmp, vcmp, ks, vs, kw, vw, gt, ovt, expand):
    b, s, _ = q.shape
    ncp = kcmp.shape[2]
    gw = NSA_HPG * NSA_D
    slab = pl.BlockSpec((None, s, NSA_D), lambda bi, gi, i: (bi, 0, gi))
    cmp = pl.BlockSpec((None, None, ncp, NSA_D), lambda bi, gi, i: (bi, gi, 0, 0))
    return pl.pallas_call(
        functools.partial(_nsa_body, n_buckets=s // SEL_BUCKET),
        out_shape=jax.ShapeDtypeStruct((b, s, NSA_HEADS * NSA_D), F32),
        grid=(b, NSA_G, s // QB),
        in_specs=[pl.BlockSpec((None, QB, gw), lambda bi, gi, i: (bi, i, gi)), cmp, cmp, slab, slab, slab, slab,
                  pl.BlockSpec((None, None, QB, 128), lambda bi, gi, i: (gi, bi, i, 0)),
                  _const_spec(ovt.shape), _const_spec(expand.shape)],
        out_specs=pl.BlockSpec((None, QB, gw), lambda bi, gi, i: (bi, i, gi)),
        scratch_shapes=[pltpu.VMEM((NSA_HPG * QB, NSA_D), F32)],
        compiler_params=_params(("parallel", "parallel", "arbitrary")), name="nsa_attn",
    )(q, kcmp, vcmp, ks, vs, kw, vw, gt, ovt, expand)


def _post_body(ya_ref, yb_ref, h_ref, gya_ref, gyb_ref, wo_ref, gx_ref, wxq_ref, mk_ref, mv_ref, wxo_ref,
               gf_ref, wrh_ref, wrl_ref, br_ref, h2_ref, xn_ref, eid_ref, gate_ref):
    half = wo_ref.shape[0] // 2
    h1 = (h_ref[...] + _mm(_rms(ya_ref[...], gya_ref[...]).astype(BF16), wo_ref[:half, :])
          + _mm(_rms(yb_ref[...], gyb_ref[...]).astype(BF16), wo_ref[half:, :]))

    q = (_mm(_rms(h1, gx_ref[...]).astype(BF16), wxq_ref[...]) * SCALE_X).astype(BF16)
    outs = []
    for hd in range(XH):
        sl = slice(hd * XD, (hd + 1) * XD)
        s = _mm_nt(q[:, sl], mk_ref[:, sl])
        p = jnp.exp(s - jnp.max(s, axis=1, keepdims=True))
        outs.append(_mm(p.astype(BF16), mv_ref[:, sl]) / jnp.sum(p, axis=1, keepdims=True))
    h2 = h1 + _mm(jnp.concatenate(outs, axis=1).astype(BF16), wxo_ref[...])
    h2_ref[...] = h2

    xf = _rms(h2, gf_ref[...])
    _store_slabs(xn_ref, xf, SLAB_ROWS)
    x_hi = xf.astype(BF16)
    x_lo = (xf - x_hi.astype(F32)).astype(BF16)
    logits = _mm(x_hi, wrh_ref[...]) + _mm(x_lo, wrh_ref[...]) + _mm(x_hi, wrl_ref[...]) + br_ref[...]
    lane = lax.broadcasted_iota(jnp.int32, logits.shape, 1)
    lane_f = lane.astype(F32)

    def argmax_first(x):
        v = jnp.max(x, axis=1, keepdims=True)
        return v, jnp.min(jnp.where(x == v, lane_f, 1e9), axis=1, keepdims=True)

    is_grp = lane < N_GROUPS
    lg = jnp.where(is_grp, logits, NEG)
    mx, grp = argmax_first(lg)
    p_top = 1.0 / jnp.sum(jnp.where(is_grp, jnp.exp(lg - mx), 0.0), axis=1, keepdims=True)
    in_grp = (lane >= N_GROUPS) & (lane < N_GROUPS + N_EXPERTS) & (
        ((lane - N_GROUPS) // EXP_PER_GROUP).astype(F32) == grp)
    le = jnp.where(in_grp, logits, NEG)
    v1, i1 = argmax_first(le)
    v2, i2 = argmax_first(jnp.where(lane_f == i1, NEG, le))
    t = jnp.exp(v2 - v1)
    g1 = p_top / (1.0 + t)
    g2 = p_top * t / (1.0 + t)
    eid_ref[...] = jnp.where(lane == 0, i1 - N_GROUPS, jnp.where(lane == 1, i2 - N_GROUPS, 0.0)).astype(jnp.int32)
    gate_ref[...] = jnp.where(lane == 0, g1, jnp.where(lane == 1, g2, 0.0))


def _post_mixer(ya, yb, h, gya, gyb, wo, gx, wxq, memkv, layer, wxo, gf, wrh, wrl, br, tm=256):
    b, s, _ = ya.shape
    row = lambda n: pl.BlockSpec((None, tm, n), lambda bi, i: (bi, i, 0))
    hw = XH * XD
    in_specs = [row(1024), row(1024), row(D_MODEL), _const_spec(gya.shape), _const_spec(gyb.shape),
                _const_spec(wo.shape), _const_spec(gx.shape), _const_spec(wxq.shape),
                pl.BlockSpec((None, MEM_LEN, hw), lambda bi, i: (bi, 0, 2 * layer)),
                pl.BlockSpec((None, MEM_LEN, hw), lambda bi, i: (bi, 0, 2 * layer + 1)),
                _const_spec(wxo.shape), _const_spec(gf.shape), _const_spec(wrh.shape), _const_spec(wrl.shape),
                _const_spec(br.shape)]
    out_shape = [jax.ShapeDtypeStruct((b, s, D_MODEL), F32), jax.ShapeDtypeStruct((b * s * SLAB_ROWS, 128), F32),
                 jax.ShapeDtypeStruct((b, s, 128), jnp.int32), jax.ShapeDtypeStruct((b, s, 128), F32)]
    slabs = pl.BlockSpec((tm * SLAB_ROWS, 128), lambda bi, i: (bi * (s // tm) + i, 0))
    return pl.pallas_call(
        _post_body, out_shape=out_shape, grid=(b, s // tm), in_specs=in_specs,
        out_specs=[row(D_MODEL), slabs, row(128), row(128)],
        compiler_params=_params(("parallel", "parallel")), name="post_mixer",
    )(ya, yb, h, gya, gyb, wo, gx, wxq, memkv, memkv, wxo, gf, wrh, wrl, br)


def _memkv_body(m_ref, g_ref, w_ref, o_ref):
    o_ref[...] = _mm(_rms(m_ref[...], g_ref[...]).astype(BF16), w_ref[...]).astype(BF16)


def _memkv(mem, g, w, tm=256):
    t = mem.shape[0]
    n = w.shape[1]
    return pl.pallas_call(
        _memkv_body, out_shape=jax.ShapeDtypeStruct((t, n), BF16), grid=(t // tm,),
        in_specs=[pl.BlockSpec((tm, D_MODEL), lambda i: (i, 0)), _const_spec(g.shape), _const_spec(w.shape)],
        out_specs=pl.BlockSpec((tm, n), lambda i: (i, 0)),
        compiler_params=_params(("parallel",)), name="memkv",
    )(mem, g, w)


def _experts_body(src_ref, dst_ref, be_ref, bf_ref, ws_ref, nx_ref, nu_ref, xn_ref, w1_ref, w3_ref, w2_ref, y_ref,
                  xbuf, ybuf, w1f, w3f, w2f, w1b, w3b, w2b, sem_in, sem_out, sem_w, *, n_blocks, layer):
    b = pl.program_id(0)
    slot = b % 2
    n_used = nu_ref[0]

    def w_copies(e, ws):
        return [pltpu.make_async_copy(w_ref.at[layer, e], wf.at[ws], sem_w.at[ws])
                for w_ref, wf in ((w1_ref, w1f), (w3_ref, w3f), (w2_ref, w2f))]

    def in_copy(blk, sl, r):
        row0 = pl.multiple_of(src_ref[blk * MOE_BLOCK + r] * SLAB_ROWS, SLAB_ROWS)
        return pltpu.make_async_copy(xn_ref.at[pl.ds(row0, SLAB_ROWS), :],
                                     xbuf.at[sl, pl.ds(r * SLAB_PITCH_VMEM, SLAB_ROWS), :], sem_in.at[sl])

    def out_copy(blk, sl, r):
        row0 = pl.multiple_of(dst_ref[blk * MOE_BLOCK + r] * SLAB_ROWS, SLAB_ROWS)
        return pltpu.make_async_copy(ybuf.at[sl, pl.ds(r * SLAB_PITCH_VMEM, SLAB_ROWS), :],
                                     y_ref.at[pl.ds(row0, SLAB_ROWS), :], sem_out.at[sl])

    @pl.when(b == 0)
    def _():
        for c in w_copies(be_ref[0], 0):
            c.start(priority=1)
        ybuf[...] = jnp.zeros_like(ybuf)
        for r in range(MOE_BLOCK):
            in_copy(0, 0, r).start()
        spare = MOE_BLOCK * SLAB_ROWS
        for sl in range(2):
            zeros = ybuf.at[sl, pl.ds(0, spare), :]
            pad_rows = y_ref.at[pl.ds(y_ref.shape[0] - (2 - sl) * spare, spare), :]
            pltpu.make_async_copy(zeros, pad_rows, sem_out.at[sl]).start()
            pltpu.make_async_copy(zeros, pad_rows, sem_out.at[sl]).wait()

    @pl.when(b + 1 < n_used)
    def _():
        for r in range(MOE_BLOCK):
            in_copy(b + 1, 1 - slot, r).start()

    @pl.when(b < n_used)
    def _():
        @pl.when(bf_ref[b] == 1)
        def _():
            ws = ws_ref[b]
            for c in w_copies(be_ref[b], ws):
                c.wait()

            @pl.when(nx_ref[b] >= 0)
            def _():
                for c in w_copies(nx_ref[b], 1 - ws):
                    c.start(priority=1)

            w1b[...] = w1f[ws].astype(BF16)
            w3b[...] = w3f[ws].astype(BF16)
            w2b[...] = w2f[ws].astype(BF16)

        for r in range(MOE_BLOCK):
            in_copy(b, slot, r).wait()

        @pl.when(b >= 2)
        def _():
            for r in range(MOE_BLOCK):
                out_copy(b - 2, slot, r).wait()

        x = _load_slabs(xbuf.at[slot], MOE_BLOCK, SLAB_PITCH_VMEM).astype(BF16)
        hb = (jax.nn.silu(_mm(x, w1b[...])) * _mm(x, w3b[...])).astype(BF16)
        _store_slabs(ybuf.at[slot], _mm(hb, w2b[...]), SLAB_PITCH_VMEM)
        for r in range(MOE_BLOCK):
            out_copy(b, slot, r).start()

    @pl.when(b == n_blocks - 1)
    def _():
        for r in range(MOE_BLOCK):
            out_copy(0, (n_used - 1) % 2, r).wait()

        @pl.when(n_used >= 2)
        def _():
            for r in range(MOE_BLOCK):
                out_copy(0, n_used % 2, r).wait()


def _experts(slot_src, slot_dst, blk_expert, blk_first, blk_wslot, nxt_expert, n_used, xn, w_e1, w_e3, w_e2, layer):
    t, d = xn.shape[0] // SLAB_ROWS, D_MODEL
    n_slots = slot_src.shape[0]
    n_blocks = n_slots // MOE_BLOCK
    hbm = pl.BlockSpec(memory_space=pl.ANY)
    return pl.pallas_call(
        functools.partial(_experts_body, n_blocks=n_blocks, layer=layer),
        out_shape=jax.ShapeDtypeStruct(((TOP_K * t + 2 * MOE_BLOCK) * SLAB_ROWS, 128), F32),
        grid_spec=pltpu.PrefetchScalarGridSpec(
            num_scalar_prefetch=7, grid=(n_blocks,),
            in_specs=[hbm, hbm, hbm, hbm], out_specs=hbm,
            scratch_shapes=[pltpu.VMEM((2, MOE_BLOCK * SLAB_PITCH_VMEM, 128), F32),
                            pltpu.VMEM((2, MOE_BLOCK * SLAB_PITCH_VMEM, 128), F32),
                            pltpu.VMEM((2, d, D_EXPERT), F32), pltpu.VMEM((2, d, D_EXPERT), F32),
                            pltpu.VMEM((2, D_EXPERT, d), F32),
                            pltpu.VMEM((d, D_EXPERT), BF16), pltpu.VMEM((d, D_EXPERT), BF16),
                            pltpu.VMEM((D_EXPERT, d), BF16),
                            pltpu.SemaphoreType.DMA((2,)), pltpu.SemaphoreType.DMA((2,)),
                            pltpu.SemaphoreType.DMA((2,))]),
        compiler_params=_params(("arbitrary",)), name="experts",
    )(slot_src, slot_dst, blk_expert, blk_first, blk_wslot, nxt_expert, n_used, xn, w_e1, w_e3, w_e2)


def _combine_body(y1_ref, y2_ref, h_ref, gate_ref, gfin_ref, o_ref, *, final_norm):
    g = gate_ref[...]
    n_tok = h_ref.shape[0]
    y = (h_ref[...] + g[:, 0:1] * _load_slabs(y1_ref, n_tok, SLAB_ROWS)
         + g[:, 1:2] * _load_slabs(y2_ref, n_tok, SLAB_ROWS))
    if final_norm:
        y = _rms(y, gfin_ref[...])
    o_ref[...] = y


def _combine(y12, h, gate, gfin, final_norm, tm=256):
    t, d = h.shape
    nt = t // tm
    slabs = lambda off: pl.BlockSpec((tm * SLAB_ROWS, 128), lambda i: (i + off, 0))
    return pl.pallas_call(
        functools.partial(_combine_body, final_norm=final_norm),
        out_shape=jax.ShapeDtypeStruct((t, d), F32), grid=(nt,),
        in_specs=[slabs(0), slabs(nt),
                  pl.BlockSpec((tm, d), lambda i: (i, 0)), pl.BlockSpec((tm, 128), lambda i: (i, 0)),
                  _const_spec((1, d))],
        out_specs=pl.BlockSpec((tm, d), lambda i: (i, 0)),
        compiler_params=_params(("parallel",)), name="combine",
    )(y12, y12, h, gate, gfin)


def _dispatch(eid):
    n_tok = eid.shape[0]
    n_assign = n_tok * TOP_K
    expert = eid.reshape(-1)
    onehot = jax.nn.one_hot(expert, N_EXPERTS, dtype=jnp.int32)
    rank = jnp.take_along_axis(jnp.cumsum(onehot, axis=0), expert[:, None], axis=1)[:, 0] - 1
    counts = jnp.sum(onehot, axis=0)
    padded = (counts + MOE_BLOCK - 1) // MOE_BLOCK * MOE_BLOCK
    pad_end = jnp.cumsum(padded)
    dest = ((pad_end - padded)[expert] + rank).astype(jnp.int32)
    n_blocks = (n_assign + N_EXPERTS * (MOE_BLOCK - 1) + MOE_BLOCK - 1) // MOE_BLOCK
    n_slots = n_blocks * MOE_BLOCK
    slot_assign = jnp.zeros((n_slots,), jnp.int32).at[dest].set(jnp.arange(1, n_assign + 1, dtype=jnp.int32))
    a = slot_assign - 1
    live = slot_assign > 0
    slot_src = jnp.where(live, a // TOP_K, 0)
    spare = n_assign + jnp.arange(n_slots, dtype=jnp.int32) % (2 * MOE_BLOCK)
    slot_dst = jnp.where(live, (a % TOP_K) * n_tok + a // TOP_K, spare)
    blk_start = jnp.arange(n_blocks, dtype=jnp.int32) * MOE_BLOCK
    blk_expert = jnp.clip(jnp.searchsorted(pad_end, blk_start, side='right'), 0, N_EXPERTS - 1).astype(jnp.int32)
    blk_first = jnp.concatenate([jnp.ones((1,), jnp.int32), (blk_expert[1:] != blk_expert[:-1]).astype(jnp.int32)])
    n_used = (pad_end[-1:] // MOE_BLOCK).astype(jnp.int32)
    ids = jnp.arange(N_EXPERTS, dtype=jnp.int32)
    has_rows = counts > 0
    wslot = ((jnp.cumsum(has_rows.astype(jnp.int32)) - 1) % 2).astype(jnp.int32)
    later = jnp.where(has_rows[None, :] & (ids[None, :] > ids[:, None]), ids[None, :], N_EXPERTS)
    nxt = jnp.min(later, axis=1)
    nxt = jnp.where(nxt < N_EXPERTS, nxt, -1).astype(jnp.int32)
    return slot_src, slot_dst, blk_expert, blk_first, wslot[blk_expert], nxt[blk_expert], n_used


def _rope_tables(positions):
    pos = positions.reshape(-1).astype(F32)[:, None]

    def tables(rot, passthrough):
        half = rot // 2
        ang = pos * (ROPE_THETA ** (-jnp.arange(0, rot, 2, dtype=F32) / rot))
        cos, sin = jnp.cos(ang), jnp.sin(ang)
        z = lambda n: jnp.zeros((pos.shape[0], n), F32)
        tail = jnp.full((pos.shape[0], 128 - rot), passthrough, F32)
        return [jnp.concatenate([cos, cos, tail], 1), jnp.concatenate([z(half), sin, z(128 - rot)], 1),
                jnp.concatenate([-sin, z(128 - half)], 1)]

    return jnp.concatenate(tables(MLA_ROPE, 0.0) + tables(NSA_ROT, 1.0), axis=1)


def _selection_constants(s):
    ncp = s // CMP_STRIDE
    ns = s // SEL_BLOCK
    c_start = np.arange(ncp)[None, :] * CMP_STRIDE
    sel_start = np.arange(ns)[:, None] * SEL_BLOCK
    ovt = (c_start <= sel_start + SEL_BLOCK - 1) & (c_start + CMP_LEN - 1 >= sel_start)
    ovt[:, ncp - 1] = False
    expand = np.arange(s)[None, :] // SEL_BLOCK == np.arange(ns)[:, None]
    return jnp.asarray(ovt, BF16), jnp.asarray(expand, BF16)


def _arrange_w_in(w):
    cuts = np.cumsum([0, 512, 256, 64, 1024, 256, 256, 256, 256, 256, 256, 24])
    c_q, c_kv, k_r, q_n, k_c, v_c, k_s, v_s, k_w, v_w, gates = [w[:, cuts[i]:cuts[i + 1]] for i in range(11)]
    z = lambda n: jnp.zeros((w.shape[0], n), w.dtype)
    return jnp.concatenate([c_q, c_kv, k_r, z(64), q_n, k_c, k_s, k_w, v_c, v_s, v_w, gates, z(104)], axis=1).astype(BF16)


def _arrange_w_uq(w):
    w = w.reshape(MLA_Q_RANK, MLA_HEADS, MLA_NOPE + MLA_ROPE)
    w = jnp.pad(w, ((0, 0), (0, 0), (0, MLA_QK_PAD - MLA_NOPE - MLA_ROPE)))
    return w.reshape(MLA_Q_RANK, MLA_HEADS * MLA_QK_PAD).astype(BF16)


def _split_w_ukv(w):
    w = w.reshape(MLA_KV_RANK, MLA_HEADS, MLA_NOPE + MLA_V)
    return (w[:, :, :MLA_NOPE].reshape(MLA_KV_RANK, -1).astype(BF16),
            w[:, :, MLA_NOPE:].reshape(MLA_KV_RANK, -1).astype(BF16))


def _router_weights(w_grp, b_grp, w_exp, b_exp):
    pad = 128 - N_GROUPS - N_EXPERTS
    w = jnp.concatenate([w_grp, w_exp, jnp.zeros((w_grp.shape[0], pad), F32)], axis=1)
    hi = w.astype(BF16)
    lo = (w - hi.astype(F32)).astype(BF16)
    b = jnp.concatenate([b_grp, b_exp, jnp.zeros((pad,), F32)])[None, :]
    return hi, lo, b


def kernel(x, mem, positions, g_mem, g_final, g_mix, w_in, g_cq, w_uq, g_ckv, w_ukv, pe_k, pe_v, w_ck1, w_ck2,
           w_cv1, w_cv2, g_ya, g_yb, w_out, g_x, w_xq, w_xkv, w_xo, g_ffn, w_grp, b_grp, w_exp, b_exp,
           w_e1, w_e3, w_e2):
    b, s, d = x.shape
    depth = w_in.shape[0]
    t = b * s
    ncp = s // CMP_STRIDE
    tab = _rope_tables(positions)
    ovt, expand = _selection_constants(s)
    r2 = lambda v: v.reshape(1, -1)

    w_kv_all = jnp.concatenate([w_xkv[l] for l in range(depth)], axis=1).astype(BF16)
    memkv = _memkv(mem.reshape(-1, d), r2(g_mem), w_kv_all).reshape(b, mem.shape[1], -1)

    h = x.reshape(t, d)
    for l in range(depth):
        wkn, wv = _split_w_ukv(w_ukv[l])
        qa, ka, va, qn, kc, vc, ks, vs, kw, vw, gt = _in_proj(
            h, r2(g_mix[l]), _arrange_w_in(w_in[l]), r2(g_cq[l]), r2(g_ckv[l]), _arrange_w_uq(w_uq[l]), wkn, wv, tab)
        kcmp, vcmp = _compress(
            kc.reshape(NSA_G, b, ncp, CMP_STRIDE * NSA_D), vc.reshape(NSA_G, b, ncp, CMP_STRIDE * NSA_D),
            pe_k[l].reshape(1, -1), pe_v[l].reshape(1, -1), w_ck1[l].astype(BF16), w_ck2[l].astype(BF16),
            w_cv1[l].astype(BF16), w_cv2[l].astype(BF16))
        r3 = lambda a: a.reshape(b, s, -1)
        ya = _mla_attn(r3(qa), r3(ka), r3(va))
        yb = _nsa_attn(r3(qn), kcmp, vcmp, r3(ks), r3(vs), r3(kw), r3(vw), gt.reshape(NSA_G, b, s, 128), ovt, expand)
        wrh, wrl, br = _router_weights(w_grp[l], b_grp[l], w_exp[l], b_exp[l])
        h2, xn, eid, gate = _post_mixer(
            ya, yb, r3(h), r2(g_ya[l]), r2(g_yb[l]), w_out[l].astype(BF16), r2(g_x[l]), w_xq[l].astype(BF16),
            memkv, l, w_xo[l].astype(BF16), r2(g_ffn[l]), wrh, wrl, br)
        y12 = _experts(*_dispatch(eid.reshape(t, 128)[:, :TOP_K]), xn, w_e1, w_e3, w_e2, l)
        h = _combine(y12, h2.reshape(t, d), gate.reshape(t, 128), r2(g_final), final_norm=(l == depth - 1))
    return h.reshape(b, s, d)
```

```python
import functools

import jax
import jax.numpy as jnp
import numpy as np
from jax import lax
from jax.experimental import pallas as pl
from jax.experimental.pallas import tpu as pltpu

F32 = jnp.float32
BF16 = jnp.bfloat16

D_MODEL = 2048
RMS_EPS = 1e-6
ROPE_THETA = 500000.0
NEG = -1e30
FORCE_SCORE = 1e4

MLA_HEADS = 8
MLA_NOPE = 128
MLA_ROPE = 64
MLA_V = 128
MLA_Q_RANK = D_MODEL // 4
MLA_KV_RANK = D_MODEL // 8
MLA_QK_PAD = 256

NSA_HEADS = 8
NSA_G = 2
NSA_HPG = 4
NSA_D = 128
NSA_ROT = 32
CMP_LEN = 32
CMP_STRIDE = 16
CMP_HID = 512
SEL_BLOCK = 64
SEL_TOPN = 8
WINDOW = 512
QB = 128

XH = 4
XD = 128
MEM_LEN = 256

N_GROUPS = 8
EXP_PER_GROUP = 8
N_EXPERTS = 64
TOP_K = 2
D_EXPERT = D_MODEL // 4
MOE_BLOCK = 128

VMEM_LIMIT_BYTES = 56 * 1024 * 1024

C_CQ = 0
C_CKV = 512
C_KR = 768
C_QN = 896
C_KK = 1920
C_VC = 2688
C_GT = 2944
N_IN = 3072

LOG2E = 1.4426950408889634
SCALE_A = (MLA_NOPE + MLA_ROPE) ** -0.5 * LOG2E
SCALE_N = NSA_D ** -0.5 * LOG2E
SCALE_X = XD ** -0.5 * LOG2E


def _params(sem):
    return pltpu.CompilerParams(dimension_semantics=sem, vmem_limit_bytes=VMEM_LIMIT_BYTES)


def _mm(a, b):
    return jnp.dot(a, b, preferred_element_type=F32)


def _mm_nt(a, b):
    return lax.dot_general(a, b, (((1,), (1,)), ((), ())), preferred_element_type=F32)


def _rms(x, g):
    return x * lax.rsqrt(jnp.mean(x * x, axis=-1, keepdims=True) + RMS_EPS) * g


def _rope(x, c, sa, sb, half):
    return x * c + pltpu.roll(x, half, 1) * sa + pltpu.roll(x, 128 - half, 1) * sb


SLAB_ROWS = D_MODEL // 128
SLAB_PITCH_VMEM = 24


def _load_slabs(ref, n_tok, pitch):
    return jnp.concatenate([ref[pl.ds(c, n_tok, stride=pitch), :] for c in range(SLAB_ROWS)], axis=1)


def _store_slabs(ref, x, pitch):
    for c in range(SLAB_ROWS):
        ref[pl.ds(c, x.shape[0], stride=pitch), :] = x[:, c * 128:(c + 1) * 128]


def _const_spec(shape):
    nd = len(shape)
    return pl.BlockSpec(shape, lambda *_: (0,) * nd, pipeline_mode=pl.Buffered(1))


def _in_proj_body(h_ref, gmix_ref, w1_ref, wvt_ref, gcq_ref, gckv_ref, wq_ref, wkn_ref, wv_ref, tab_ref,
                  qa_ref, ka_ref, va_ref, qn_ref, kc_ref, vc_ref, ks_ref, kw_ref, vt_ref, gt_ref):
    xn = _rms(h_ref[...], gmix_ref[...]).astype(BF16)
    tab = tab_ref[...]
    ca, saa, sba, cn, san, sbn = [tab[:, i * 128:(i + 1) * 128] for i in range(6)]

    xq = _rms(_mm(xn, w1_ref[:, C_CQ:C_CKV]), gcq_ref[...]).astype(BF16)
    ckr = _mm(xn, w1_ref[:, C_CKV:C_QN])
    xkv = _rms(ckr[:, :MLA_KV_RANK], gckv_ref[...]).astype(BF16)
    kr = _rope(ckr[:, MLA_KV_RANK:], ca, saa, sba, MLA_ROPE // 2).astype(BF16)
    q = _mm(xq, wq_ref[...])
    kn = _mm(xkv, wkn_ref[...])
    for h in range(MLA_HEADS):
        lo = h * MLA_QK_PAD
        qa_ref[:, lo:lo + 128] = (q[:, lo:lo + 128] * SCALE_A).astype(BF16)
        qa_ref[:, lo + 128:lo + 256] = (
            _rope(q[:, lo + 128:lo + 256], ca, saa, sba, MLA_ROPE // 2) * SCALE_A).astype(BF16)
        ka_ref[:, lo:lo + 128] = kn[:, h * 128:(h + 1) * 128].astype(BF16)
        ka_ref[:, lo + 128:lo + 256] = kr
    va_ref[...] = _mm_nt(wv_ref[...], xkv).astype(BF16)

    qn = _mm(xn, w1_ref[:, C_QN:C_KK])
    for h in range(NSA_HEADS):
        sl = slice(h * 128, (h + 1) * 128)
        qn_ref[:, sl] = (_rope(qn[:, sl], cn, san, sbn, NSA_ROT // 2) * SCALE_N).astype(BF16)
    kk = _mm(xn, w1_ref[:, C_KK:C_VC])
    vc = _mm(xn, w1_ref[:, C_VC:C_GT])
    for g in range(NSA_G):
        sl = slice(g * 128, (g + 1) * 128)
        kc_ref[g] = _rope(kk[:, sl], cn, san, sbn, NSA_ROT // 2)
        ks_ref[:, sl] = _rope(kk[:, 256 + g * 128:256 + (g + 1) * 128], cn, san, sbn, NSA_ROT // 2).astype(BF16)
        kw_ref[:, sl] = _rope(kk[:, 512 + g * 128:512 + (g + 1) * 128], cn, san, sbn, NSA_ROT // 2).astype(BF16)
        vc_ref[g] = vc[:, sl]
    vt_ref[...] = _mm_nt(wvt_ref[...], xn).astype(BF16)
    gl = _mm(xn, w1_ref[:, C_GT:N_IN])
    gt_ref[0] = gl
    gt_ref[1] = pltpu.roll(gl, 128 - NSA_HPG * 3, 1)


def _in_proj(h, gmix, w1, wvt, gcq, gckv, wq, wkn, wv, tab, tm=256):
    t = h.shape[0]
    row = lambda n: pl.BlockSpec((tm, n), lambda i: (i, 0))
    col = lambda n: pl.BlockSpec((n, tm), lambda i: (0, i))
    grp = pl.BlockSpec((NSA_G, tm, 128), lambda i: (0, i, 0))
    out_shape = [
        jax.ShapeDtypeStruct((t, MLA_HEADS * MLA_QK_PAD), BF16),
        jax.ShapeDtypeStruct((t, MLA_HEADS * MLA_QK_PAD), BF16),
        jax.ShapeDtypeStruct((MLA_HEADS * MLA_V, t), BF16),
        jax.ShapeDtypeStruct((t, NSA_HEADS * NSA_D), BF16),
        jax.ShapeDtypeStruct((NSA_G, t, NSA_D), F32),
        jax.ShapeDtypeStruct((NSA_G, t, NSA_D), F32),
        jax.ShapeDtypeStruct((t, NSA_G * NSA_D), BF16),
        jax.ShapeDtypeStruct((t, NSA_G * NSA_D), BF16),
        jax.ShapeDtypeStruct((2 * NSA_G * NSA_D, t), BF16),
        jax.ShapeDtypeStruct((NSA_G, t, 128), F32),
    ]
    out_specs = [row(2048), row(2048), col(1024), row(1024), grp, grp, row(256), row(256), col(512), grp]
    in_specs = [row(D_MODEL), _const_spec((1, D_MODEL)), _const_spec((D_MODEL, N_IN)),
                _const_spec((2 * NSA_G * NSA_D, D_MODEL)),
                _const_spec((1, MLA_Q_RANK)), _const_spec((1, MLA_KV_RANK)),
                _const_spec((MLA_Q_RANK, MLA_HEADS * MLA_QK_PAD)), _const_spec((MLA_KV_RANK, MLA_HEADS * 128)),
                _const_spec((MLA_HEADS * 128, MLA_KV_RANK)), row(6 * 128)]
    return pl.pallas_call(
        _in_proj_body, out_shape=out_shape, grid=(t // tm,), in_specs=in_specs, out_specs=out_specs,
        compiler_params=_params(("parallel",)), name="in_proj",
    )(h, gmix, w1, wvt, gcq, gckv, wq, wkn, wv, tab)


def _compress_body(k_ref, v_ref, pek_ref, pev_ref, wk1_ref, wk2_ref, wv1_ref, wv2_ref, ko_ref, vo_ref):
    half = CMP_STRIDE * NSA_D

    def phi(x, pe, w1_ref, w2_ref):
        a = _mm((x + pe[:, :half]).astype(BF16), w1_ref[:half, :])
        b = _mm((x + pe[:, half:]).astype(BF16), w1_ref[half:, :])
        pre = a + pltpu.roll(b, b.shape[0] - 1, 0)
        return _mm(jax.nn.gelu(pre).astype(BF16), w2_ref[...])

    ko_ref[...] = phi(k_ref[...], pek_ref[...], wk1_ref, wk2_ref).astype(BF16)
    vo_ref[...] = phi(v_ref[...], pev_ref[...], wv1_ref, wv2_ref).T.astype(BF16)


def _compress(kc, vc, pek, pev, wk1, wk2, wv1, wv2):
    g, b, ncp, w = kc.shape
    in_blk = pl.BlockSpec((None, None, ncp, w), lambda bi, gi: (gi, bi, 0, 0))
    return pl.pallas_call(
        _compress_body,
        out_shape=[jax.ShapeDtypeStruct((b, g, ncp, NSA_D), BF16), jax.ShapeDtypeStruct((b, g, NSA_D, ncp), BF16)],
        grid=(b, g),
        in_specs=[in_blk, in_blk, _const_spec(pek.shape), _const_spec(pev.shape), _const_spec(wk1.shape),
                  _const_spec(wk2.shape), _const_spec(wv1.shape), _const_spec(wv2.shape)],
        out_specs=[pl.BlockSpec((None, None, ncp, NSA_D), lambda bi, gi: (bi, gi, 0, 0)),
                   pl.BlockSpec((None, None, NSA_D, ncp), lambda bi, gi: (bi, gi, 0, 0))],
        compiler_params=_params(("parallel", "parallel")), name="compress",
    )(kc, vc, pek, pev, wk1, wk2, wv1, wv2)


def _mla_body(q_ref, k_ref, vt_ref, o_ref, *, tq, n_q):
    i = pl.program_id(2)
    q = q_ref[...]
    tri = lax.broadcasted_iota(jnp.int32, (tq, tq), 0) <= lax.broadcasted_iota(jnp.int32, (tq, tq), 1)
    for ci in range(n_q):
        @pl.when(i == ci)
        def _(ci=ci):
            lo = ci * tq
            sd = jnp.where(tri, _mm_nt(k_ref[lo:lo + tq, :], q), NEG)
            m = jnp.max(sd, axis=0, keepdims=True)
            if ci > 0:
                sf = _mm_nt(k_ref[0:lo, :], q)
                m = jnp.maximum(m, jnp.max(sf, axis=0, keepdims=True))
                pf = jnp.exp2(sf - m)
            pd = jnp.exp2(sd - m)
            l = jnp.sum(pd, axis=0, keepdims=True)
            acc = _mm(vt_ref[:, lo:lo + tq], pd.astype(BF16))
            if ci > 0:
                l = l + jnp.sum(pf, axis=0, keepdims=True)
                acc = acc + _mm(vt_ref[:, 0:lo], pf.astype(BF16))
            o_ref[...] = (acc / l).T


def _mla_attn(q, k, vt, tq=512):
    b, s, _ = q.shape
    return pl.pallas_call(
        functools.partial(_mla_body, tq=tq, n_q=s // tq),
        out_shape=jax.ShapeDtypeStruct((b, s, MLA_HEADS * MLA_V), F32),
        grid=(b, MLA_HEADS, s // tq),
        in_specs=[pl.BlockSpec((None, tq, MLA_QK_PAD), lambda bi, hi, i: (bi, i, hi)),
                  pl.BlockSpec((None, s, MLA_QK_PAD), lambda bi, hi, i: (bi, 0, hi)),
                  pl.BlockSpec((MLA_V, s), lambda bi, hi, i: (hi, bi))],
        out_specs=pl.BlockSpec((None, tq, MLA_V), lambda bi, hi, i: (bi, i, hi)),
        compiler_params=_params(("parallel", "parallel", "arbitrary")), name="mla_attn",
    )(q, k, vt)


SEL_BUCKET = 512
WIN_LEN = WINDOW + QB


def _nsa_body(q_ref, kc_ref, vct_ref, ks_ref, vst_ref, kw_ref, vwt_ref, gt_ref, ovt_ref, expt_ref, o_ref, os_ref, *,
              n_buckets):
    i = pl.program_id(2)
    ns = ovt_ref.shape[0]
    ncp = ovt_ref.shape[1]
    cols = NSA_HPG * QB
    q = q_ref[...]
    qst = jnp.concatenate([q[:, h * 128:(h + 1) * 128] for h in range(NSA_HPG)], axis=0)

    def tile4(x):
        return jnp.concatenate([x] * NSA_HPG, axis=1)

    pos_c = i * QB + (lax.broadcasted_iota(jnp.int32, (ncp, cols), 1) & (QB - 1))
    cend = lax.broadcasted_iota(jnp.int32, (ncp, cols), 0) * CMP_STRIDE + (CMP_LEN - 1)
    vis = cend <= pos_c
    s = jnp.where(vis, _mm_nt(kc_ref[...], qst), NEG)
    e = jnp.where(vis, jnp.exp2(s - jnp.max(s, axis=0, keepdims=True)), 0.0)
    l = jnp.sum(e, axis=0, keepdims=True)
    p_c = e / jnp.where(l > 0.0, l, 1.0)
    o_c = _mm(vct_ref[...], p_c.astype(BF16))

    psum = p_c[:, 0:QB] + p_c[:, QB:2 * QB] + p_c[:, 2 * QB:3 * QB] + p_c[:, 3 * QB:4 * QB]
    p_hi = psum.astype(BF16)
    p_lo = (psum - p_hi.astype(F32)).astype(BF16)
    imp = _mm(ovt_ref[...], p_hi) + _mm(ovt_ref[...], p_lo)
    n_i = lax.broadcasted_iota(jnp.int32, (ns, QB), 0)
    pos_l = i * QB + lax.broadcasted_iota(jnp.int32, (ns, QB), 1)
    forced = (n_i == 0) | (n_i == pos_l // SEL_BLOCK)
    val = jnp.where(forced, FORCE_SCORE, jnp.where(n_i * SEL_BLOCK <= pos_l, imp, -FORCE_SCORE))
    rank = jnp.zeros((ns, QB), F32)
    for m_ in range(ns):
        vm = val[m_:m_ + 1, :]
        rank = rank + jnp.where((vm > val) | ((vm == val) & (n_i > m_)), 1.0, 0.0)
    sel_t = jnp.where(rank < float(min(SEL_TOPN, ns)), 1.0, 0.0).astype(BF16)

    for cb in range(n_buckets):
        @pl.when(i // (SEL_BUCKET // QB) == cb)
        def _(cb=cb):
            n_keys = (cb + 1) * SEL_BUCKET
            member = _mm(expt_ref[0:n_keys, :], sel_t)
            qcol = i * QB + lax.broadcasted_iota(jnp.int32, (n_keys, QB), 1)
            ok = (member > 0.5) & (lax.broadcasted_iota(jnp.int32, (n_keys, QB), 0) <= qcol)
            sc = _mm_nt(ks_ref[0:n_keys, :], qst) + tile4(jnp.where(ok, 0.0, NEG))
            p = jnp.exp2(sc - jnp.max(sc, axis=0, keepdims=True))
            os_ref[...] = _mm(vst_ref[:, 0:n_keys], p.astype(BF16)) / jnp.sum(p, axis=0, keepdims=True)

    o_s = os_ref[...]

    start = pl.multiple_of(jnp.maximum(i * QB - WINDOW, 0), QB)
    dist = (i * QB + lax.broadcasted_iota(jnp.int32, (WIN_LEN, QB), 1)) - (
        start + lax.broadcasted_iota(jnp.int32, (WIN_LEN, QB), 0))
    ok = (dist >= 0) & (dist < WINDOW)
    s = _mm_nt(kw_ref[pl.ds(start, WIN_LEN), :], qst) + tile4(jnp.where(ok, 0.0, NEG))
    p = jnp.exp2(s - jnp.max(s, axis=0, keepdims=True))
    o_w =_mm(vwt_ref[:, pl.ds(start, WIN_LEN)], p.astype(BF16)) / jnp.sum(p, axis=0, keepdims=True)

    g = jax.nn.sigmoid(gt_ref[...]).T
    for h in range(NSA_HPG):
        sl = slice(h * QB, (h + 1) * QB)
        o_t = (g[3 * h:3 * h + 1, :] * o_c[:, sl] + g[3 * h + 1:3 * h + 2, :] * o_s[:, sl]
               + g[3 * h + 2:3 * h + 3, :] * o_w[:, sl])
        o_ref[:, h * 128:(h + 1) * 128] = o_t.T


def _nsa_attn(q, kcmp, vcmp_t, ks, kw, vt, gt, ovt, expand_t):
    b, s, _ = q.shape
    ncp = kcmp.shape[2]
    gw = NSA_HPG * NSA_D
    slab = pl.BlockSpec((None, s, NSA_D), lambda bi, gi, i: (bi, 0, gi))
    return pl.pallas_call(
        functools.partial(_nsa_body, n_buckets=s // SEL_BUCKET),
        out_shape=jax.ShapeDtypeStruct((b, s, NSA_HEADS * NSA_D), F32),
        grid=(b, NSA_G, s // QB),
        in_specs=[pl.BlockSpec((None, QB, gw), lambda bi, gi, i: (bi, i, gi)),
                  pl.BlockSpec((None, None, ncp, NSA_D), lambda bi, gi, i: (bi, gi, 0, 0)),
                  pl.BlockSpec((None, None, NSA_D, ncp), lambda bi, gi, i: (bi, gi, 0, 0)),
                  slab, pl.BlockSpec((NSA_D, s), lambda bi, gi, i: (gi, bi)),
                  slab, pl.BlockSpec((NSA_D, s), lambda bi, gi, i: (NSA_G + gi, bi)),
                  pl.BlockSpec((None, None, QB, 128), lambda bi, gi, i: (gi, bi, i, 0)),
                  _const_spec(ovt.shape), _const_spec(expand_t.shape)],
        out_specs=pl.BlockSpec((None, QB, gw), lambda bi, gi, i: (bi, i, gi)),
        scratch_shapes=[pltpu.VMEM((NSA_D, NSA_HPG * QB), F32)],
        compiler_params=_params(("parallel", "parallel", "arbitrary")), name="nsa_attn",
    )(q, kcmp, vcmp_t, ks, vt, kw, vt, gt, ovt, expand_t)


def _post_body(ya_ref, yb_ref, h_ref, gya_ref, gyb_ref, wo_ref, gx_ref, wxq_ref, mk_ref, mv_ref, wxo_ref,
               gf_ref, wrh_ref, wrl_ref, br_ref, h2_ref, xn_ref, eid_ref, gate_ref):
    half = wo_ref.shape[0] // 2
    h1 = (h_ref[...] + _mm(_rms(ya_ref[...], gya_ref[...]).astype(BF16), wo_ref[:half, :])
          + _mm(_rms(yb_ref[...], gyb_ref[...]).astype(BF16), wo_ref[half:, :]))

    q = (_mm(_rms(h1, gx_ref[...]).astype(BF16), wxq_ref[...]) * SCALE_X).astype(BF16)
    outs = []
    for hd in range(XH):
        sl = slice(hd * XD, (hd + 1) * XD)
        s = _mm_nt(q[:, sl], mk_ref[:, sl])
        p = jnp.exp2(s - jnp.max(s, axis=1, keepdims=True))
        outs.append(_mm(p.astype(BF16), mv_ref[:, sl]) / jnp.sum(p, axis=1, keepdims=True))
    h2 = h1 + _mm(jnp.concatenate(outs, axis=1).astype(BF16), wxo_ref[...])
    h2_ref[...] = h2

    xf = _rms(h2, gf_ref[...])
    _store_slabs(xn_ref, xf, SLAB_ROWS)
    x_hi = xf.astype(BF16)
    x_lo = (xf - x_hi.astype(F32)).astype(BF16)
    logits = _mm(x_hi, wrh_ref[...]) + _mm(x_lo, wrh_ref[...]) + _mm(x_hi, wrl_ref[...]) + br_ref[...]
    lane = lax.broadcasted_iota(jnp.int32, logits.shape, 1)
    lane_f = lane.astype(F32)

    def argmax_first(x):
        v = jnp.max(x, axis=1, keepdims=True)
        return v, jnp.min(jnp.where(x == v, lane_f, 1e9), axis=1, keepdims=True)

    is_grp = lane < N_GROUPS
    lg = jnp.where(is_grp, logits, NEG)
    mx, grp = argmax_first(lg)
    p_top = 1.0 / jnp.sum(jnp.where(is_grp, jnp.exp(lg - mx), 0.0), axis=1, keepdims=True)
    in_grp = (lane >= N_GROUPS) & (lane < N_GROUPS + N_EXPERTS) & (
        ((lane - N_GROUPS) // EXP_PER_GROUP).astype(F32) == grp)
    le = jnp.where(in_grp, logits, NEG)
    v1, i1 = argmax_first(le)
    v2, i2 = argmax_first(jnp.where(lane_f == i1, NEG, le))
    t = jnp.exp(v2 - v1)
    g1 = p_top / (1.0 + t)
    g2 = p_top * t / (1.0 + t)
    eid_ref[...] = jnp.where(lane == 0, i1 - N_GROUPS, jnp.where(lane == 1, i2 - N_GROUPS, 0.0)).astype(jnp.int32)
    gate_ref[...] = jnp.where(lane == 0, g1, jnp.where(lane == 1, g2, 0.0))


def _post_mixer(ya, yb, h, gya, gyb, wo, gx, wxq, memkv, layer, wxo, gf, wrh, wrl, br, tm=256):
    b, s, _ = ya.shape
    row = lambda n: pl.BlockSpec((None, tm, n), lambda bi, i: (bi, i, 0))
    hw = XH * XD
    in_specs = [row(1024), row(1024), row(D_MODEL), _const_spec(gya.shape), _const_spec(gyb.shape),
                _const_spec(wo.shape), _const_spec(gx.shape), _const_spec(wxq.shape),
                pl.BlockSpec((None, MEM_LEN, hw), lambda bi, i: (bi, 0, 2 * layer)),
                pl.BlockSpec((None, MEM_LEN, hw), lambda bi, i: (bi, 0, 2 * layer + 1)),
                _const_spec(wxo.shape), _const_spec(gf.shape), _const_spec(wrh.shape), _const_spec(wrl.shape),
                _const_spec(br.shape)]
    out_shape = [jax.ShapeDtypeStruct((b, s, D_MODEL), F32), jax.ShapeDtypeStruct((b * s * SLAB_ROWS, 128), F32),
                 jax.ShapeDtypeStruct((b, s, 128), jnp.int32), jax.ShapeDtypeStruct((b, s, 128), F32)]
    slabs = pl.BlockSpec((tm * SLAB_ROWS, 128), lambda bi, i: (bi * (s // tm) + i, 0))
    return pl.pallas_call(
        _post_body, out_shape=out_shape, grid=(b, s // tm), in_specs=in_specs,
        out_specs=[row(D_MODEL), slabs, row(128), row(128)],
        compiler_params=_params(("parallel", "parallel")), name="post_mixer",
    )(ya, yb, h, gya, gyb, wo, gx, wxq, memkv, memkv, wxo, gf, wrh, wrl, br)


def _memkv_body(m_ref, g_ref, w_ref, o_ref):
    o_ref[...] = _mm(_rms(m_ref[...], g_ref[...]).astype(BF16), w_ref[...]).astype(BF16)


def _memkv(mem, g, w, tm=256):
    t = mem.shape[0]
    n = w.shape[1]
    return pl.pallas_call(
        _memkv_body, out_shape=jax.ShapeDtypeStruct((t, n), BF16), grid=(t // tm,),
        in_specs=[pl.BlockSpec((tm, D_MODEL), lambda i: (i, 0)), _const_spec(g.shape), _const_spec(w.shape)],
        out_specs=pl.BlockSpec((tm, n), lambda i: (i, 0)),
        compiler_params=_params(("parallel",)), name="memkv",
    )(mem, g, w)


ROW_DMA_PRIORITY = 1
WEIGHT_DMA_PRIORITY = 0

def _experts_body(src_ref, dst_ref, be_ref, bf_ref, ws_ref, nx_ref, nu_ref, xn_ref, w1_ref, w3_ref, w2_ref, y_ref,
                  xbuf, ybuf, w1f, w3f, w2f, w1b, w3b, w2b, sem_in, sem_out, sem_w, *, n_blocks, layer):
    b = pl.program_id(0)
    slot = b % 2
    n_used = nu_ref[0]

    def w_copies(e, ws):
        return [pltpu.make_async_copy(w_ref.at[layer, e], wf.at[ws], sem_w.at[ws])
                for w_ref, wf in ((w1_ref, w1f), (w3_ref, w3f), (w2_ref, w2f))]

    def in_copy(blk, sl, r):
        row0 = pl.multiple_of(src_ref[blk * MOE_BLOCK + r] * SLAB_ROWS, SLAB_ROWS)
        return pltpu.make_async_copy(xn_ref.at[pl.ds(row0, SLAB_ROWS), :],
                                     xbuf.at[sl, pl.ds(r * SLAB_PITCH_VMEM, SLAB_ROWS), :], sem_in.at[sl])

    def out_copy(blk, sl, r):
        row0 = pl.multiple_of(dst_ref[blk * MOE_BLOCK + r] * SLAB_ROWS, SLAB_ROWS)
        return pltpu.make_async_copy(ybuf.at[sl, pl.ds(r * SLAB_PITCH_VMEM, SLAB_ROWS), :],
                                     y_ref.at[pl.ds(row0, SLAB_ROWS), :], sem_out.at[sl])

    @pl.when(b == 0)
    def _():
        for c in w_copies(be_ref[0], 0):
            c.start(priority=WEIGHT_DMA_PRIORITY)
        ybuf[...] = jnp.zeros_like(ybuf)
        for r in range(MOE_BLOCK):
            in_copy(0, 0, r).start(priority=ROW_DMA_PRIORITY)
        spare = MOE_BLOCK * SLAB_ROWS
        for sl in range(2):
            zeros = ybuf.at[sl, pl.ds(0, spare), :]
            pad_rows = y_ref.at[pl.ds(y_ref.shape[0] - (2 - sl) * spare, spare), :]
            pltpu.make_async_copy(zeros, pad_rows, sem_out.at[sl]).start()
            pltpu.make_async_copy(zeros, pad_rows, sem_out.at[sl]).wait()

    @pl.when(b + 1 < n_used)
    def _():
        for r in range(MOE_BLOCK):
            in_copy(b + 1, 1 - slot, r).start(priority=ROW_DMA_PRIORITY)

    @pl.when(b < n_used)
    def _():
        @pl.when(bf_ref[b] == 1)
        def _():
            ws = ws_ref[b]
            for c in w_copies(be_ref[b], ws):
                c.wait()

            @pl.when(nx_ref[b] >= 0)
            def _():
                for c in w_copies(nx_ref[b], 1 - ws):
                    c.start(priority=WEIGHT_DMA_PRIORITY)

            w1b[...] = w1f[ws].astype(BF16)
            w3b[...] = w3f[ws].astype(BF16)
            w2b[...] = w2f[ws].astype(BF16)

        for r in range(MOE_BLOCK):
            in_copy(b, slot, r).wait()

        @pl.when(b >= 2)
        def _():
            for r in range(MOE_BLOCK):
                out_copy(b - 2, slot, r).wait()

        x = _load_slabs(xbuf.at[slot], MOE_BLOCK, SLAB_PITCH_VMEM).astype(BF16)
        hb = (jax.nn.silu(_mm(x, w1b[...])) * _mm(x, w3b[...])).astype(BF16)
        _store_slabs(ybuf.at[slot], _mm(hb, w2b[...]), SLAB_PITCH_VMEM)
        for r in range(MOE_BLOCK):
            out_copy(b, slot, r).start(priority=ROW_DMA_PRIORITY)

    @pl.when(b == n_blocks - 1)
    def _():
        for r in range(MOE_BLOCK):
            out_copy(0, (n_used - 1) % 2, r).wait()

        @pl.when(n_used >= 2)
        def _():
            for r in range(MOE_BLOCK):
                out_copy(0, n_used % 2, r).wait()


def _experts(slot_src, slot_dst, blk_expert, blk_first, blk_wslot, nxt_expert, n_used, xn, w_e1, w_e3, w_e2, layer):
    t, d = xn.shape[0] // SLAB_ROWS, D_MODEL
    n_slots = slot_src.shape[0]
    n_blocks = n_slots // MOE_BLOCK
    hbm = pl.BlockSpec(memory_space=pl.ANY)
    return pl.pallas_call(
        functools.partial(_experts_body, n_blocks=n_blocks, layer=layer),
        out_shape=jax.ShapeDtypeStruct(((TOP_K * t + 2 * MOE_BLOCK) * SLAB_ROWS, 128), F32),
        grid_spec=pltpu.PrefetchScalarGridSpec(
            num_scalar_prefetch=7, grid=(n_blocks,),
            in_specs=[hbm, hbm, hbm, hbm], out_specs=hbm,
            scratch_shapes=[pltpu.VMEM((2, MOE_BLOCK * SLAB_PITCH_VMEM, 128), F32),
                            pltpu.VMEM((2, MOE_BLOCK * SLAB_PITCH_VMEM, 128), F32),
                            pltpu.VMEM((2, d, D_EXPERT), F32), pltpu.VMEM((2, d, D_EXPERT), F32),
                            pltpu.VMEM((2, D_EXPERT, d), F32),
                            pltpu.VMEM((d, D_EXPERT), BF16), pltpu.VMEM((d, D_EXPERT), BF16),
                            pltpu.VMEM((D_EXPERT, d), BF16),
                            pltpu.SemaphoreType.DMA((2,)), pltpu.SemaphoreType.DMA((2,)),
                            pltpu.SemaphoreType.DMA((2,))]),
        compiler_params=_params(("arbitrary",)), name="experts",
    )(slot_src, slot_dst, blk_expert, blk_first, blk_wslot, nxt_expert, n_used, xn, w_e1, w_e3, w_e2)


def _combine_body(y1_ref, y2_ref, h_ref, gate_ref, gfin_ref, o_ref, *, final_norm):
    g = gate_ref[...]
    n_tok = h_ref.shape[0]
    y = (h_ref[...] + g[:, 0:1] * _load_slabs(y1_ref, n_tok, SLAB_ROWS)
         + g[:, 1:2] * _load_slabs(y2_ref, n_tok, SLAB_ROWS))
    if final_norm:
        y = _rms(y, gfin_ref[...])
    o_ref[...] = y


def _combine(y12, h, gate, gfin, final_norm, tm=256):
    t, d = h.shape
    nt = t // tm
    slabs = lambda off: pl.BlockSpec((tm * SLAB_ROWS, 128), lambda i: (i + off, 0))
    return pl.pallas_call(
        functools.partial(_combine_body, final_norm=final_norm),
        out_shape=jax.ShapeDtypeStruct((t, d), F32), grid=(nt,),
        in_specs=[slabs(0), slabs(nt),
                  pl.BlockSpec((tm, d), lambda i: (i, 0)), pl.BlockSpec((tm, 128), lambda i: (i, 0)),
                  _const_spec((1, d))],
        out_specs=pl.BlockSpec((tm, d), lambda i: (i, 0)),
        compiler_params=_params(("parallel",)), name="combine",
    )(y12, y12, h, gate, gfin)


def _dispatch(eid):
    n_tok = eid.shape[0]
    n_assign = n_tok * TOP_K
    expert = eid.reshape(-1)
    onehot = jax.nn.one_hot(expert, N_EXPERTS, dtype=jnp.int32)
    rank = jnp.take_along_axis(jnp.cumsum(onehot, axis=0), expert[:, None], axis=1)[:, 0] - 1
    counts = jnp.sum(onehot, axis=0)
    padded = (counts + MOE_BLOCK - 1) // MOE_BLOCK * MOE_BLOCK
    pad_end = jnp.cumsum(padded)
    dest = ((pad_end - padded)[expert] + rank).astype(jnp.int32)
    n_blocks = (n_assign + N_EXPERTS * (MOE_BLOCK - 1) + MOE_BLOCK - 1) // MOE_BLOCK
    n_slots = n_blocks * MOE_BLOCK
    slot_assign = jnp.zeros((n_slots,), jnp.int32).at[dest].set(
        jnp.arange(1, n_assign + 1, dtype=jnp.int32), unique_indices=True)
    a = slot_assign - 1
    live = slot_assign > 0
    slot_src = jnp.where(live, a // TOP_K, 0)
    spare = n_assign + jnp.arange(n_slots, dtype=jnp.int32) % (2 * MOE_BLOCK)
    slot_dst = jnp.where(live, (a % TOP_K) * n_tok + a // TOP_K, spare)
    blk_start = jnp.arange(n_blocks, dtype=jnp.int32) * MOE_BLOCK
    blk_expert = jnp.clip(jnp.searchsorted(pad_end, blk_start, side='right'), 0, N_EXPERTS - 1).astype(jnp.int32)
    blk_first = jnp.concatenate([jnp.ones((1,), jnp.int32), (blk_expert[1:] != blk_expert[:-1]).astype(jnp.int32)])
    n_used = (pad_end[-1:] // MOE_BLOCK).astype(jnp.int32)
    ids = jnp.arange(N_EXPERTS, dtype=jnp.int32)
    has_rows = counts > 0
    wslot = ((jnp.cumsum(has_rows.astype(jnp.int32)) - 1) % 2).astype(jnp.int32)
    later = jnp.where(has_rows[None, :] & (ids[None, :] > ids[:, None]), ids[None, :], N_EXPERTS)
    nxt = jnp.min(later, axis=1)
    nxt = jnp.where(nxt < N_EXPERTS, nxt, -1).astype(jnp.int32)
    return slot_src, slot_dst, blk_expert, blk_first, wslot[blk_expert], nxt[blk_expert], n_used


def _rope_tables(positions):
    pos = positions.reshape(-1).astype(F32)[:, None]

    def tables(rot, passthrough):
        half = rot // 2
        ang = pos * (ROPE_THETA ** (-jnp.arange(0, rot, 2, dtype=F32) / rot))
        cos, sin = jnp.cos(ang), jnp.sin(ang)
        z = lambda n: jnp.zeros((pos.shape[0], n), F32)
        tail = jnp.full((pos.shape[0], 128 - rot), passthrough, F32)
        return [jnp.concatenate([cos, cos, tail], 1), jnp.concatenate([z(half), sin, z(128 - rot)], 1),
                jnp.concatenate([-sin, z(128 - half)], 1)]

    return jnp.concatenate(tables(MLA_ROPE, 0.0) + tables(NSA_ROT, 1.0), axis=1)


def _selection_constants(s):
    ncp = s // CMP_STRIDE
    ns = s // SEL_BLOCK
    c_start = np.arange(ncp)[None, :] * CMP_STRIDE
    sel_start = np.arange(ns)[:, None] * SEL_BLOCK
    ovt = (c_start <= sel_start + SEL_BLOCK - 1) & (c_start + CMP_LEN - 1 >= sel_start)
    ovt[:, ncp - 1] = False
    expand_t = np.arange(s)[:, None] // SEL_BLOCK == np.arange(ns)[None, :]
    return jnp.asarray(ovt, BF16), jnp.asarray(expand_t, BF16)


def _arrange_w_in(w):
    cuts = np.cumsum([0, 512, 256, 64, 1024, 256, 256, 256, 256, 256, 256, 24])
    c_q, c_kv, k_r, q_n, k_c, v_c, k_s, v_s, k_w, v_w, gates = [w[:, cuts[i]:cuts[i + 1]] for i in range(11)]
    z = lambda n: jnp.zeros((w.shape[0], n), w.dtype)
    w1 = jnp.concatenate([c_q, c_kv, k_r, z(64), q_n, k_c, k_s, k_w, v_c, gates, z(104)], axis=1).astype(BF16)
    return w1, jnp.concatenate([v_s, v_w], axis=1).T.astype(BF16)


def _arrange_w_uq(w):
    w = w.reshape(MLA_Q_RANK, MLA_HEADS, MLA_NOPE + MLA_ROPE)
    w = jnp.pad(w, ((0, 0), (0, 0), (0, MLA_QK_PAD - MLA_NOPE - MLA_ROPE)))
    return w.reshape(MLA_Q_RANK, MLA_HEADS * MLA_QK_PAD).astype(BF16)


def _split_w_ukv(w):
    w = w.reshape(MLA_KV_RANK, MLA_HEADS, MLA_NOPE + MLA_V)
    return (w[:, :, :MLA_NOPE].reshape(MLA_KV_RANK, -1).astype(BF16),
            w[:, :, MLA_NOPE:].reshape(MLA_KV_RANK, -1).T.astype(BF16))


def _router_weights(w_grp, b_grp, w_exp, b_exp):
    pad = 128 - N_GROUPS - N_EXPERTS
    w = jnp.concatenate([w_grp, w_exp, jnp.zeros((w_grp.shape[0], pad), F32)], axis=1)
    hi = w.astype(BF16)
    lo = (w - hi.astype(F32)).astype(BF16)
    b = jnp.concatenate([b_grp, b_exp, jnp.zeros((pad,), F32)])[None, :]
    return hi, lo, b


def kernel(x, mem, positions, g_mem, g_final, g_mix, w_in, g_cq, w_uq, g_ckv, w_ukv, pe_k, pe_v, w_ck1, w_ck2,
           w_cv1, w_cv2, g_ya, g_yb, w_out, g_x, w_xq, w_xkv, w_xo, g_ffn, w_grp, b_grp, w_exp, b_exp,
           w_e1, w_e3, w_e2):
    b, s, d = x.shape
    depth = w_in.shape[0]
    t = b * s
    ncp = s // CMP_STRIDE
    tab = _rope_tables(positions)
    ovt, expand_t = _selection_constants(s)
    r2 = lambda v: v.reshape(1, -1)

    w_kv_all = jnp.concatenate([w_xkv[l] for l in range(depth)], axis=1).astype(BF16)
    memkv = _memkv(mem.reshape(-1, d), r2(g_mem), w_kv_all).reshape(b, mem.shape[1], -1)

    h = x.reshape(t, d)
    for l in range(depth):
        wkn, wv_t = _split_w_ukv(w_ukv[l])
        w1, wvt = _arrange_w_in(w_in[l])
        qa, ka, va_t, qn, kc, vc, ks, kw, vt, gt = _in_proj(
            h, r2(g_mix[l]), w1, wvt, r2(g_cq[l]), r2(g_ckv[l]), _arrange_w_uq(w_uq[l]), wkn, wv_t, tab)
        kcmp, vcmp_t = _compress(
            kc.reshape(NSA_G, b, ncp, CMP_STRIDE * NSA_D), vc.reshape(NSA_G, b, ncp, CMP_STRIDE * NSA_D),
            pe_k[l].reshape(1, -1), pe_v[l].reshape(1, -1), w_ck1[l].astype(BF16), w_ck2[l].astype(BF16),
            w_cv1[l].astype(BF16), w_cv2[l].astype(BF16))
        r3 = lambda a: a.reshape(b, s, -1)
        ya = _mla_attn(r3(qa), r3(ka), va_t)
        yb = _nsa_attn(r3(qn), kcmp, vcmp_t, r3(ks), r3(kw), vt, gt.reshape(NSA_G, b, s, 128), ovt, expand_t)
        wrh, wrl, br = _router_weights(w_grp[l], b_grp[l], w_exp[l], b_exp[l])
        h2, xn, eid, gate = _post_mixer(
            ya, yb, r3(h), r2(g_ya[l]), r2(g_yb[l]), w_out[l].astype(BF16), r2(g_x[l]), w_xq[l].astype(BF16),
            memkv, l, w_xo[l].astype(BF16), r2(g_ffn[l]), wrh, wrl, br)
        y12 = _experts(*_dispatch(eid.reshape(t, 128)[:, :TOP_K]), xn, w_e1, w_e3, w_e2, l)
        h = _combine(y12, h2.reshape(t, d), gate.reshape(t, 128), r2(g_final), final_norm=(l == depth - 1))
    return h.reshape(b, s, d)
```

```python
import functools

import jax
import jax.numpy as jnp
import numpy as np
from jax import lax
from jax.experimental import pallas as pl
from jax.experimental.pallas import tpu as pltpu

F32 = jnp.float32
BF16 = jnp.bfloat16

D_MODEL = 2048
RMS_EPS = 1e-6
ROPE_THETA = 500000.0
NEG = -1e30
FORCE_SCORE = 1e4

MLA_HEADS = 8
MLA_NOPE = 128
MLA_ROPE = 64
MLA_V = 128
MLA_Q_RANK = D_MODEL // 4
MLA_KV_RANK = D_MODEL // 8
MLA_QK_PAD = 256

NSA_HEADS = 8
NSA_G = 2
NSA_HPG = 4
NSA_D = 128
NSA_ROT = 32
CMP_LEN = 32
CMP_STRIDE = 16
CMP_HID = 512
SEL_BLOCK = 64
SEL_TOPN = 8
WINDOW = 512
QB = 128

XH = 4
XD = 128
MEM_LEN = 256

N_GROUPS = 8
EXP_PER_GROUP = 8
N_EXPERTS = 64
TOP_K = 2
D_EXPERT = D_MODEL // 4
MOE_BLOCK = 128

VMEM_LIMIT_BYTES = 56 * 1024 * 1024

C_CQ = 0
C_CKV = 512
C_KR = 768
C_QN = 896
C_KK = 1920
C_VC = 2688
C_GT = 2944
N_IN = 3072

LOG2E = 1.4426950408889634
SCALE_A = (MLA_NOPE + MLA_ROPE) ** -0.5 * LOG2E
SCALE_N = NSA_D ** -0.5 * LOG2E
SCALE_X = XD ** -0.5 * LOG2E


def _params(sem):
    return pltpu.CompilerParams(dimension_semantics=sem, vmem_limit_bytes=VMEM_LIMIT_BYTES)


def _mm(a, b):
    return jnp.dot(a, b, preferred_element_type=F32)


def _mm_nt(a, b):
    return lax.dot_general(a, b, (((1,), (1,)), ((), ())), preferred_element_type=F32)


def _rms(x, g):
    return x * lax.rsqrt(jnp.mean(x * x, axis=-1, keepdims=True) + RMS_EPS) * g


def _rope(x, c, sa, sb, half):
    return x * c + pltpu.roll(x, half, 1) * sa + pltpu.roll(x, 128 - half, 1) * sb


SLAB_ROWS = D_MODEL // 128
SLAB_PITCH_VMEM = 24


def _load_slabs(ref, n_tok, pitch):
    return jnp.concatenate([ref[pl.ds(c, n_tok, stride=pitch), :] for c in range(SLAB_ROWS)], axis=1)


def _store_slabs(ref, x, pitch):
    for c in range(SLAB_ROWS):
        ref[pl.ds(c, x.shape[0], stride=pitch), :] = x[:, c * 128:(c + 1) * 128]


def _const_spec(shape):
    nd = len(shape)
    return pl.BlockSpec(shape, lambda *_: (0,) * nd, pipeline_mode=pl.Buffered(1))


def _in_proj_body(h_ref, gmix_ref, w1_ref, wvt_ref, gcq_ref, gckv_ref, wq_ref, wkn_ref, wv_ref, tab_ref,
                  qa_ref, ka_ref, va_ref, qn_ref, kc_ref, vc_ref, ks_ref, kw_ref, vt_ref, gt_ref):
    xn = _rms(h_ref[...], gmix_ref[...]).astype(BF16)
    tab = tab_ref[...]
    ca, saa, sba, cn, san, sbn = [tab[:, i * 128:(i + 1) * 128] for i in range(6)]

    xq = _rms(_mm(xn, w1_ref[:, C_CQ:C_CKV]), gcq_ref[...]).astype(BF16)
    ckr = _mm(xn, w1_ref[:, C_CKV:C_QN])
    xkv = _rms(ckr[:, :MLA_KV_RANK], gckv_ref[...]).astype(BF16)
    kr = _rope(ckr[:, MLA_KV_RANK:], ca, saa, sba, MLA_ROPE // 2).astype(BF16)
    q = _mm(xq, wq_ref[...])
    kn = _mm(xkv, wkn_ref[...])
    for h in range(MLA_HEADS):
        lo = h * MLA_QK_PAD
        qa_ref[:, lo:lo + 128] = (q[:, lo:lo + 128] * SCALE_A).astype(BF16)
        qa_ref[:, lo + 128:lo + 256] = (
            _rope(q[:, lo + 128:lo + 256], ca, saa, sba, MLA_ROPE // 2) * SCALE_A).astype(BF16)
        ka_ref[:, lo:lo + 128] = kn[:, h * 128:(h + 1) * 128].astype(BF16)
        ka_ref[:, lo + 128:lo + 256] = kr
    va_ref[...] = _mm_nt(wv_ref[...], xkv).astype(BF16)

    qn = _mm(xn, w1_ref[:, C_QN:C_KK])
    for h in range(NSA_HEADS):
        sl = slice(h * 128, (h + 1) * 128)
        qn_ref[:, sl] = (_rope(qn[:, sl], cn, san, sbn, NSA_ROT // 2) * SCALE_N).astype(BF16)
    kk = _mm(xn, w1_ref[:, C_KK:C_VC])
    vc = _mm(xn, w1_ref[:, C_VC:C_GT])
    for g in range(NSA_G):
        sl = slice(g * 128, (g + 1) * 128)
        kc_ref[g] = _rope(kk[:, sl], cn, san, sbn, NSA_ROT // 2)
        ks_ref[:, sl] = _rope(kk[:, 256 + g * 128:256 + (g + 1) * 128], cn, san, sbn, NSA_ROT // 2).astype(BF16)
        kw_ref[:, sl] = _rope(kk[:, 512 + g * 128:512 + (g + 1) * 128], cn, san, sbn, NSA_ROT // 2).astype(BF16)
        vc_ref[g] = vc[:, sl]
    vt_ref[...] = _mm_nt(wvt_ref[...], xn).astype(BF16)
    gl = _mm(xn, w1_ref[:, C_GT:N_IN])
    gt_ref[0] = gl
    gt_ref[1] = pltpu.roll(gl, 128 - NSA_HPG * 3, 1)


def _in_proj(h, gmix, w1, wvt, gcq, gckv, wq, wkn, wv, tab, tm=512):
    t = h.shape[0]
    row = lambda n: pl.BlockSpec((tm, n), lambda i: (i, 0))
    col = lambda n: pl.BlockSpec((n, tm), lambda i: (0, i))
    grp = pl.BlockSpec((NSA_G, tm, 128), lambda i: (0, i, 0))
    out_shape = [
        jax.ShapeDtypeStruct((t, MLA_HEADS * MLA_QK_PAD), BF16),
        jax.ShapeDtypeStruct((t, MLA_HEADS * MLA_QK_PAD), BF16),
        jax.ShapeDtypeStruct((MLA_HEADS * MLA_V, t), BF16),
        jax.ShapeDtypeStruct((t, NSA_HEADS * NSA_D), BF16),
        jax.ShapeDtypeStruct((NSA_G, t, NSA_D), F32),
        jax.ShapeDtypeStruct((NSA_G, t, NSA_D), F32),
        jax.ShapeDtypeStruct((t, NSA_G * NSA_D), BF16),
        jax.ShapeDtypeStruct((t, NSA_G * NSA_D), BF16),
        jax.ShapeDtypeStruct((2 * NSA_G * NSA_D, t), BF16),
        jax.ShapeDtypeStruct((NSA_G, t, 128), F32),
    ]
    out_specs = [row(2048), row(2048), col(1024), row(1024), grp, grp, row(256), row(256), col(512), grp]
    in_specs = [row(D_MODEL), _const_spec((1, D_MODEL)), _const_spec((D_MODEL, N_IN)),
                _const_spec((2 * NSA_G * NSA_D, D_MODEL)),
                _const_spec((1, MLA_Q_RANK)), _const_spec((1, MLA_KV_RANK)),
                _const_spec((MLA_Q_RANK, MLA_HEADS * MLA_QK_PAD)), _const_spec((MLA_KV_RANK, MLA_HEADS * 128)),
                _const_spec((MLA_HEADS * 128, MLA_KV_RANK)), row(6 * 128)]
    return pl.pallas_call(
        _in_proj_body, out_shape=out_shape, grid=(t // tm,), in_specs=in_specs, out_specs=out_specs,
        compiler_params=_params(("parallel",)), name="in_proj",
    )(h, gmix, w1, wvt, gcq, gckv, wq, wkn, wv, tab)


def _compress_body(k_ref, v_ref, pek_ref, pev_ref, wk1_ref, wk2_ref, wv1_ref, wv2_ref, ko_ref, vo_ref):
    half = CMP_STRIDE * NSA_D

    def phi(x, pe, w1_ref, w2_ref):
        a = _mm((x + pe[:, :half]).astype(BF16), w1_ref[:half, :])
        b = _mm((x + pe[:, half:]).astype(BF16), w1_ref[half:, :])
        pre = a + pltpu.roll(b, b.shape[0] - 1, 0)
        return _mm(jax.nn.gelu(pre).astype(BF16), w2_ref[...])

    ko_ref[...] = phi(k_ref[...], pek_ref[...], wk1_ref, wk2_ref).astype(BF16)
    vo_ref[...] = phi(v_ref[...], pev_ref[...], wv1_ref, wv2_ref).T.astype(BF16)


def _compress(kc, vc, pek, pev, wk1, wk2, wv1, wv2):
    g, b, ncp, w = kc.shape
    in_blk = pl.BlockSpec((None, None, ncp, w), lambda bi, gi: (gi, bi, 0, 0))
    return pl.pallas_call(
        _compress_body,
        out_shape=[jax.ShapeDtypeStruct((b, g, ncp, NSA_D), BF16), jax.ShapeDtypeStruct((b, g, NSA_D, ncp), BF16)],
        grid=(b, g),
        in_specs=[in_blk, in_blk, _const_spec(pek.shape), _const_spec(pev.shape), _const_spec(wk1.shape),
                  _const_spec(wk2.shape), _const_spec(wv1.shape), _const_spec(wv2.shape)],
        out_specs=[pl.BlockSpec((None, None, ncp, NSA_D), lambda bi, gi: (bi, gi, 0, 0)),
                   pl.BlockSpec((None, None, NSA_D, ncp), lambda bi, gi: (bi, gi, 0, 0))],
        compiler_params=_params(("parallel", "parallel")), name="compress",
    )(kc, vc, pek, pev, wk1, wk2, wv1, wv2)


def _mla_body(q_ref, k_ref, vt_ref, o_ref, *, tq, n_q):
    i = pl.program_id(2)
    q = q_ref[...]
    tri = lax.broadcasted_iota(jnp.int32, (tq, tq), 0) <= lax.broadcasted_iota(jnp.int32, (tq, tq), 1)
    for ci in range(n_q):
        @pl.when(i == ci)
        def _(ci=ci):
            lo = ci * tq
            sd = jnp.where(tri, _mm_nt(k_ref[lo:lo + tq, :], q), NEG)
            m = jnp.max(sd, axis=0, keepdims=True)
            if ci > 0:
                sf = _mm_nt(k_ref[0:lo, :], q)
                m = jnp.maximum(m, jnp.max(sf, axis=0, keepdims=True))
                pf = jnp.exp2(sf - m)
            pd = jnp.exp2(sd - m)
            l = jnp.sum(pd, axis=0, keepdims=True)
            acc = _mm(vt_ref[:, lo:lo + tq], pd.astype(BF16))
            if ci > 0:
                l = l + jnp.sum(pf, axis=0, keepdims=True)
                acc = acc + _mm(vt_ref[:, 0:lo], pf.astype(BF16))
            o_ref[...] = (acc / l).T


def _mla_attn(q, k, vt, tq=512):
    b, s, _ = q.shape
    return pl.pallas_call(
        functools.partial(_mla_body, tq=tq, n_q=s // tq),
        out_shape=jax.ShapeDtypeStruct((b, s, MLA_HEADS * MLA_V), F32),
        grid=(b, MLA_HEADS, s // tq),
        in_specs=[pl.BlockSpec((None, tq, MLA_QK_PAD), lambda bi, hi, i: (bi, i, hi)),
                  pl.BlockSpec((None, s, MLA_QK_PAD), lambda bi, hi, i: (bi, 0, hi)),
                  pl.BlockSpec((MLA_V, s), lambda bi, hi, i: (hi, bi))],
        out_specs=pl.BlockSpec((None, tq, MLA_V), lambda bi, hi, i: (bi, i, hi)),
        compiler_params=_params(("parallel", "parallel", "arbitrary")), name="mla_attn",
    )(q, k, vt)


SEL_BUCKET = 512
WIN_LEN = WINDOW + QB


def _nsa_body(q_ref, kc_ref, vct_ref, ks_ref, vst_ref, kw_ref, vwt_ref, gt_ref, ovt_ref, expt_ref, o_ref, os_ref, *,
              n_buckets):
    i = pl.program_id(2)
    ns = ovt_ref.shape[0]
    ncp = ovt_ref.shape[1]
    cols = NSA_HPG * QB
    q = q_ref[...]
    qst = jnp.concatenate([q[:, h * 128:(h + 1) * 128] for h in range(NSA_HPG)], axis=0)

    def tile4(x):
        return jnp.concatenate([x] * NSA_HPG, axis=1)

    pos_c = i * QB + (lax.broadcasted_iota(jnp.int32, (ncp, cols), 1) & (QB - 1))
    cend = lax.broadcasted_iota(jnp.int32, (ncp, cols), 0) * CMP_STRIDE + (CMP_LEN - 1)
    vis = cend <= pos_c
    s = jnp.where(vis, _mm_nt(kc_ref[...], qst), NEG)
    e = jnp.where(vis, jnp.exp2(s - jnp.max(s, axis=0, keepdims=True)), 0.0)
    l = jnp.sum(e, axis=0, keepdims=True)
    p_c = e / jnp.where(l > 0.0, l, 1.0)
    o_c = _mm(vct_ref[...], p_c.astype(BF16))

    psum = p_c[:, 0:QB] + p_c[:, QB:2 * QB] + p_c[:, 2 * QB:3 * QB] + p_c[:, 3 * QB:4 * QB]
    p_hi = psum.astype(BF16)
    p_lo = (psum - p_hi.astype(F32)).astype(BF16)
    imp = _mm(ovt_ref[...], p_hi) + _mm(ovt_ref[...], p_lo)
    n_i = lax.broadcasted_iota(jnp.int32, (ns, QB), 0)
    pos_l = i * QB + lax.broadcasted_iota(jnp.int32, (ns, QB), 1)
    forced = (n_i == 0) | (n_i == pos_l // SEL_BLOCK)
    val = jnp.where(forced, FORCE_SCORE, jnp.where(n_i * SEL_BLOCK <= pos_l, imp, -FORCE_SCORE))
    rank = jnp.zeros((ns, QB), F32)
    for m_ in range(ns):
        vm = val[m_:m_ + 1, :]
        rank = rank + jnp.where((vm > val) | ((vm == val) & (n_i > m_)), 1.0, 0.0)
    sel_t = jnp.where(rank < float(min(SEL_TOPN, ns)), 1.0, 0.0).astype(BF16)

    for cb in range(n_buckets):
        @pl.when(i // (SEL_BUCKET // QB) == cb)
        def _(cb=cb):
            n_keys = (cb + 1) * SEL_BUCKET
            member = _mm(expt_ref[0:n_keys, :], sel_t)
            qcol = i * QB + lax.broadcasted_iota(jnp.int32, (n_keys, QB), 1)
            ok = (member > 0.5) & (lax.broadcasted_iota(jnp.int32, (n_keys, QB), 0) <= qcol)
            sc = _mm_nt(ks_ref[0:n_keys, :], qst) + tile4(jnp.where(ok, 0.0, NEG))
            p = jnp.exp2(sc - jnp.max(sc, axis=0, keepdims=True))
            os_ref[...] = _mm(vst_ref[:, 0:n_keys], p.astype(BF16)) / jnp.sum(p, axis=0, keepdims=True)

    o_s = os_ref[...]

    start = pl.multiple_of(jnp.maximum(i * QB - WINDOW, 0), QB)
    dist = (i * QB + lax.broadcasted_iota(jnp.int32, (WIN_LEN, QB), 1)) - (
        start + lax.broadcasted_iota(jnp.int32, (WIN_LEN, QB), 0))
    ok = (dist >= 0) & (dist < WINDOW)
    s = _mm_nt(kw_ref[pl.ds(start, WIN_LEN), :], qst) + tile4(jnp.where(ok, 0.0, NEG))
    p = jnp.exp2(s - jnp.max(s, axis=0, keepdims=True))
    o_w =_mm(vwt_ref[:, pl.ds(start, WIN_LEN)], p.astype(BF16)) / jnp.sum(p, axis=0, keepdims=True)

    g = jax.nn.sigmoid(gt_ref[...]).T
    for h in range(NSA_HPG):
        sl = slice(h * QB, (h + 1) * QB)
        o_t = (g[3 * h:3 * h + 1, :] * o_c[:, sl] + g[3 * h + 1:3 * h + 2, :] * o_s[:, sl]
               + g[3 * h + 2:3 * h + 3, :] * o_w[:, sl])
        o_ref[:, h * 128:(h + 1) * 128] = o_t.T


def _nsa_attn(q, kcmp, vcmp_t, ks, kw, vt, gt, ovt, expand_t):
    b, s, _ = q.shape
    ncp = kcmp.shape[2]
    gw = NSA_HPG * NSA_D
    slab = pl.BlockSpec((None, s, NSA_D), lambda bi, gi, i: (bi, 0, gi))
    return pl.pallas_call(
        functools.partial(_nsa_body, n_buckets=s // SEL_BUCKET),
        out_shape=jax.ShapeDtypeStruct((b, s, NSA_HEADS * NSA_D), F32),
        grid=(b, NSA_G, s // QB),
        in_specs=[pl.BlockSpec((None, QB, gw), lambda bi, gi, i: (bi, i, gi)),
                  pl.BlockSpec((None, None, ncp, NSA_D), lambda bi, gi, i: (bi, gi, 0, 0)),
                  pl.BlockSpec((None, None, NSA_D, ncp), lambda bi, gi, i: (bi, gi, 0, 0)),
                  slab, pl.BlockSpec((NSA_D, s), lambda bi, gi, i: (gi, bi)),
                  slab, pl.BlockSpec((NSA_D, s), lambda bi, gi, i: (NSA_G + gi, bi)),
                  pl.BlockSpec((None, None, QB, 128), lambda bi, gi, i: (gi, bi, i, 0)),
                  _const_spec(ovt.shape), _const_spec(expand_t.shape)],
        out_specs=pl.BlockSpec((None, QB, gw), lambda bi, gi, i: (bi, i, gi)),
        scratch_shapes=[pltpu.VMEM((NSA_D, NSA_HPG * QB), F32)],
        compiler_params=_params(("parallel", "parallel", "arbitrary")), name="nsa_attn",
    )(q, kcmp, vcmp_t, ks, vt, kw, vt, gt, ovt, expand_t)


def _post_body(ya_ref, yb_ref, h_ref, gya_ref, gyb_ref, wo_ref, gx_ref, wxq_ref, mk_ref, mv_ref, wxo_ref,
               gf_ref, wrh_ref, wrl_ref, br_ref, h2_ref, xn_ref, eid_ref, gate_ref, cnt_ref, cnt_scr):
    half = wo_ref.shape[0] // 2
    h1 = (h_ref[...] + _mm(_rms(ya_ref[...], gya_ref[...]).astype(BF16), wo_ref[:half, :])
          + _mm(_rms(yb_ref[...], gyb_ref[...]).astype(BF16), wo_ref[half:, :]))

    q = (_mm(_rms(h1, gx_ref[...]).astype(BF16), wxq_ref[...]) * SCALE_X).astype(BF16)
    outs = []
    for hd in range(XH):
        sl = slice(hd * XD, (hd + 1) * XD)
        s = _mm_nt(q[:, sl], mk_ref[:, sl])
        p = jnp.exp2(s - jnp.max(s, axis=1, keepdims=True))
        outs.append(_mm(p.astype(BF16), mv_ref[:, sl]) / jnp.sum(p, axis=1, keepdims=True))
    h2 = h1 + _mm(jnp.concatenate(outs, axis=1).astype(BF16), wxo_ref[...])
    h2_ref[...] = h2

    xf = _rms(h2, gf_ref[...])
    _store_slabs(xn_ref, xf, SLAB_ROWS)
    x_hi = xf.astype(BF16)
    x_lo = (xf - x_hi.astype(F32)).astype(BF16)
    logits = _mm(x_hi, wrh_ref[...]) + _mm(x_lo, wrh_ref[...]) + _mm(x_hi, wrl_ref[...]) + br_ref[...]
    lane = lax.broadcasted_iota(jnp.int32, logits.shape, 1)
    lane_f = lane.astype(F32)

    def argmax_first(x):
        v = jnp.max(x, axis=1, keepdims=True)
        return v, jnp.min(jnp.where(x == v, lane_f, 1e9), axis=1, keepdims=True)

    is_grp = lane < N_GROUPS
    lg = jnp.where(is_grp, logits, NEG)
    mx, grp = argmax_first(lg)
    p_top = 1.0 / jnp.sum(jnp.where(is_grp, jnp.exp(lg - mx), 0.0), axis=1, keepdims=True)
    in_grp = (lane >= N_GROUPS) & (lane < N_GROUPS + N_EXPERTS) & (
        ((lane - N_GROUPS) // EXP_PER_GROUP).astype(F32) == grp)
    le = jnp.where(in_grp, logits, NEG)
    v1, i1 = argmax_first(le)
    v2, i2 = argmax_first(jnp.where(lane_f == i1, NEG, le))
    t = jnp.exp(v2 - v1)
    g1 = p_top / (1.0 + t)
    g2 = p_top * t / (1.0 + t)
    gate_ref[...] = jnp.where(lane == 0, g1, jnp.where(lane == 1, g2, 0.0))

    e1 = i1 - N_GROUPS
    e2 = i2 - N_GROUPS
    oh1 = jnp.where(lane_f == e1, 1.0, 0.0)
    oh2 = jnp.where(lane_f == e2, 1.0, 0.0)
    both = oh1 + oh2
    n_tok = both.shape[0]
    earlier = jnp.where(lax.broadcasted_iota(jnp.int32, (n_tok, n_tok), 1)
                        < lax.broadcasted_iota(jnp.int32, (n_tok, n_tok), 0), 1.0, 0.0).astype(BF16)

    @pl.when((pl.program_id(0) == 0) & (pl.program_id(1) == 0))
    def _():
        cnt_scr[...] = jnp.zeros_like(cnt_scr)

    base = cnt_scr[...] + _mm(earlier, both.astype(BF16))
    r1 = jnp.sum(oh1 * base, axis=1, keepdims=True)
    r2 = jnp.sum(oh2 * base, axis=1, keepdims=True)
    cnt_scr[...] += jnp.sum(both, axis=0, keepdims=True)
    cnt_ref[...] = cnt_scr[...].astype(jnp.int32)
    eid_ref[...] = jnp.where(lane == 0, e1, jnp.where(lane == 1, e2, jnp.where(lane == 2, r1, jnp.where(
        lane == 3, r2, 0.0)))).astype(jnp.int32)


def _post_mixer(ya, yb, h, gya, gyb, wo, gx, wxq, memkv, layer, wxo, gf, wrh, wrl, br, tm=512):
    b, s, _ = ya.shape
    row = lambda n: pl.BlockSpec((None, tm, n), lambda bi, i: (bi, i, 0))
    hw = XH * XD
    in_specs = [row(1024), row(1024), row(D_MODEL), _const_spec(gya.shape), _const_spec(gyb.shape),
                _const_spec(wo.shape), _const_spec(gx.shape), _const_spec(wxq.shape),
                pl.BlockSpec((None, MEM_LEN, hw), lambda bi, i: (bi, 0, 2 * layer)),
                pl.BlockSpec((None, MEM_LEN, hw), lambda bi, i: (bi, 0, 2 * layer + 1)),
                _const_spec(wxo.shape), _const_spec(gf.shape), _const_spec(wrh.shape), _const_spec(wrl.shape),
                _const_spec(br.shape)]
    out_shape = [jax.ShapeDtypeStruct((b, s, D_MODEL), F32), jax.ShapeDtypeStruct((b * s * SLAB_ROWS, 128), F32),
                 jax.ShapeDtypeStruct((b, s, 128), jnp.int32), jax.ShapeDtypeStruct((b, s, 128), F32),
                 jax.ShapeDtypeStruct((1, 128), jnp.int32)]
    slabs = pl.BlockSpec((tm * SLAB_ROWS, 128), lambda bi, i: (bi * (s // tm) + i, 0))
    return pl.pallas_call(
        _post_body, out_shape=out_shape, grid=(b, s // tm), in_specs=in_specs,
        out_specs=[row(D_MODEL), slabs, row(128), row(128), pl.BlockSpec((1, 128), lambda bi, i: (0, 0))],
        scratch_shapes=[pltpu.VMEM((1, 128), F32)],
        compiler_params=_params(("arbitrary", "arbitrary")), name="post_mixer",
    )(ya, yb, h, gya, gyb, wo, gx, wxq, memkv, memkv, wxo, gf, wrh, wrl, br)


def _memkv_body(m_ref, g_ref, w_ref, o_ref):
    o_ref[...] = _mm(_rms(m_ref[...], g_ref[...]).astype(BF16), w_ref[...]).astype(BF16)


def _memkv(mem, g, w, tm=256):
    t = mem.shape[0]
    n = w.shape[1]
    return pl.pallas_call(
        _memkv_body, out_shape=jax.ShapeDtypeStruct((t, n), BF16), grid=(t // tm,),
        in_specs=[pl.BlockSpec((tm, D_MODEL), lambda i: (i, 0)), _const_spec(g.shape), _const_spec(w.shape)],
        out_specs=pl.BlockSpec((tm, n), lambda i: (i, 0)),
        compiler_params=_params(("parallel",)), name="memkv",
    )(mem, g, w)


ROW_DMA_PRIORITY = 1
WEIGHT_DMA_PRIORITY = 0

def _experts_body(src_ref, dst_ref, be_ref, bf_ref, ws_ref, nx_ref, nu_ref, xn_ref, w1_ref, w3_ref, w2_ref, y_ref,
                  xbuf, ybuf, w1f, w3f, w2f, w1b, w3b, w2b, sem_in, sem_out, sem_w, *, n_blocks, layer):
    b = pl.program_id(0)
    slot = b % 2
    n_used = nu_ref[0]

    def w_copies(e, ws):
        return [pltpu.make_async_copy(w_ref.at[layer, e], wf.at[ws], sem_w.at[ws])
                for w_ref, wf in ((w1_ref, w1f), (w3_ref, w3f), (w2_ref, w2f))]

    def in_copy(blk, sl, r):
        row0 = pl.multiple_of(src_ref[blk * MOE_BLOCK + r] * SLAB_ROWS, SLAB_ROWS)
        return pltpu.make_async_copy(xn_ref.at[pl.ds(row0, SLAB_ROWS), :],
                                     xbuf.at[sl, pl.ds(r * SLAB_PITCH_VMEM, SLAB_ROWS), :], sem_in.at[sl])

    def out_copy(blk, sl, r):
        row0 = pl.multiple_of(dst_ref[blk * MOE_BLOCK + r] * SLAB_ROWS, SLAB_ROWS)
        return pltpu.make_async_copy(ybuf.at[sl, pl.ds(r * SLAB_PITCH_VMEM, SLAB_ROWS), :],
                                     y_ref.at[pl.ds(row0, SLAB_ROWS), :], sem_out.at[sl])

    @pl.when(b == 0)
    def _():
        for c in w_copies(be_ref[0], 0):
            c.start(priority=WEIGHT_DMA_PRIORITY)
        ybuf[...] = jnp.zeros_like(ybuf)
        for r in range(MOE_BLOCK):
            in_copy(0, 0, r).start(priority=ROW_DMA_PRIORITY)
        spare = MOE_BLOCK * SLAB_ROWS
        for sl in range(2):
            zeros = ybuf.at[sl, pl.ds(0, spare), :]
            pad_rows = y_ref.at[pl.ds(y_ref.shape[0] - (2 - sl) * spare, spare), :]
            pltpu.make_async_copy(zeros, pad_rows, sem_out.at[sl]).start()
            pltpu.make_async_copy(zeros, pad_rows, sem_out.at[sl]).wait()

    @pl.when(b + 1 < n_used)
    def _():
        for r in range(MOE_BLOCK):
            in_copy(b + 1, 1 - slot, r).start(priority=ROW_DMA_PRIORITY)

    @pl.when(b < n_used)
    def _():
        @pl.when(bf_ref[b] == 1)
        def _():
            ws = ws_ref[b]

            @pl.when(nx_ref[b] >= 0)
            def _():
                for c in w_copies(nx_ref[b], 1 - ws):
                    c.start(priority=WEIGHT_DMA_PRIORITY)

            for c in w_copies(be_ref[b], ws):
                c.wait()
            w1b[...] = w1f[ws].astype(BF16)
            w3b[...] = w3f[ws].astype(BF16)
            w2b[...] = w2f[ws].astype(BF16)

        for r in range(MOE_BLOCK):
            in_copy(b, slot, r).wait()

        @pl.when(b >= 2)
        def _():
            for r in range(MOE_BLOCK):
                out_copy(b - 2, slot, r).wait()

        x = _load_slabs(xbuf.at[slot], MOE_BLOCK, SLAB_PITCH_VMEM).astype(BF16)
        hb = (jax.nn.silu(_mm(x, w1b[...])) * _mm(x, w3b[...])).astype(BF16)
        _store_slabs(ybuf.at[slot], _mm(hb, w2b[...]), SLAB_PITCH_VMEM)
        for r in range(MOE_BLOCK):
            out_copy(b, slot, r).start(priority=ROW_DMA_PRIORITY)

    @pl.when(b == n_blocks - 1)
    def _():
        for r in range(MOE_BLOCK):
            out_copy(0, (n_used - 1) % 2, r).wait()

        @pl.when(n_used >= 2)
        def _():
            for r in range(MOE_BLOCK):
                out_copy(0, n_used % 2, r).wait()


def _experts(slot_src, slot_dst, blk_expert, blk_first, blk_wslot, nxt_expert, n_used, xn, w_e1, w_e3, w_e2, layer):
    t, d = xn.shape[0] // SLAB_ROWS, D_MODEL
    n_slots = slot_src.shape[0]
    n_blocks = n_slots // MOE_BLOCK
    hbm = pl.BlockSpec(memory_space=pl.ANY)
    return pl.pallas_call(
        functools.partial(_experts_body, n_blocks=n_blocks, layer=layer),
        out_shape=jax.ShapeDtypeStruct(((TOP_K * t + 2 * MOE_BLOCK) * SLAB_ROWS, 128), F32),
        grid_spec=pltpu.PrefetchScalarGridSpec(
            num_scalar_prefetch=7, grid=(n_blocks,),
            in_specs=[hbm, hbm, hbm, hbm], out_specs=hbm,
            scratch_shapes=[pltpu.VMEM((2, MOE_BLOCK * SLAB_PITCH_VMEM, 128), F32),
                            pltpu.VMEM((2, MOE_BLOCK * SLAB_PITCH_VMEM, 128), F32),
                            pltpu.VMEM((2, d, D_EXPERT), F32), pltpu.VMEM((2, d, D_EXPERT), F32),
                            pltpu.VMEM((2, D_EXPERT, d), F32),
                            pltpu.VMEM((d, D_EXPERT), BF16), pltpu.VMEM((d, D_EXPERT), BF16),
                            pltpu.VMEM((D_EXPERT, d), BF16),
                            pltpu.SemaphoreType.DMA((2,)), pltpu.SemaphoreType.DMA((2,)),
                            pltpu.SemaphoreType.DMA((2,))]),
        compiler_params=_params(("arbitrary",)), name="experts",
    )(slot_src, slot_dst, blk_expert, blk_first, blk_wslot, nxt_expert, n_used, xn, w_e1, w_e3, w_e2)


def _combine_body(y1_ref, y2_ref, h_ref, gate_ref, gfin_ref, o_ref, *, final_norm):
    g = gate_ref[...]
    n_tok = h_ref.shape[0]
    y = (h_ref[...] + g[:, 0:1] * _load_slabs(y1_ref, n_tok, SLAB_ROWS)
         + g[:, 1:2] * _load_slabs(y2_ref, n_tok, SLAB_ROWS))
    if final_norm:
        y = _rms(y, gfin_ref[...])
    o_ref[...] = y


def _combine(y12, h, gate, gfin, final_norm, tm=256):
    t, d = h.shape
    nt = t // tm
    slabs = lambda off: pl.BlockSpec((tm * SLAB_ROWS, 128), lambda i: (i + off, 0))
    return pl.pallas_call(
        functools.partial(_combine_body, final_norm=final_norm),
        out_shape=jax.ShapeDtypeStruct((t, d), F32), grid=(nt,),
        in_specs=[slabs(0), slabs(nt),
                  pl.BlockSpec((tm, d), lambda i: (i, 0)), pl.BlockSpec((tm, 128), lambda i: (i, 0)),
                  _const_spec((1, d))],
        out_specs=pl.BlockSpec((tm, d), lambda i: (i, 0)),
        compiler_params=_params(("parallel",)), name="combine",
    )(y12, y12, h, gate, gfin)


def _dispatch(eid, rank, counts):
    n_tok = eid.shape[0]
    n_assign = n_tok * TOP_K
    expert = eid.reshape(-1)
    padded = (counts + MOE_BLOCK - 1) // MOE_BLOCK * MOE_BLOCK
    pad_end = jnp.cumsum(padded)
    dest = ((pad_end - padded)[expert] + rank.reshape(-1)).astype(jnp.int32)
    n_blocks = (n_assign + N_EXPERTS * (MOE_BLOCK - 1) + MOE_BLOCK - 1) // MOE_BLOCK
    n_slots = n_blocks * MOE_BLOCK
    slot_assign = jnp.zeros((n_slots,), jnp.int32).at[dest].set(
        jnp.arange(1, n_assign + 1, dtype=jnp.int32), unique_indices=True)
    a = slot_assign - 1
    live = slot_assign > 0
    slot_src = jnp.where(live, a // TOP_K, 0)
    spare = n_assign + jnp.arange(n_slots, dtype=jnp.int32) % (2 * MOE_BLOCK)
    slot_dst = jnp.where(live, (a % TOP_K) * n_tok + a // TOP_K, spare)
    blk_start = jnp.arange(n_blocks, dtype=jnp.int32) * MOE_BLOCK
    blk_expert = jnp.clip(jnp.searchsorted(pad_end, blk_start, side='right'), 0, N_EXPERTS - 1).astype(jnp.int32)
    blk_first = jnp.concatenate([jnp.ones((1,), jnp.int32), (blk_expert[1:] != blk_expert[:-1]).astype(jnp.int32)])
    n_used = (pad_end[-1:] // MOE_BLOCK).astype(jnp.int32)
    ids = jnp.arange(N_EXPERTS, dtype=jnp.int32)
    has_rows = counts > 0
    wslot = ((jnp.cumsum(has_rows.astype(jnp.int32)) - 1) % 2).astype(jnp.int32)
    later = jnp.where(has_rows[None, :] & (ids[None, :] > ids[:, None]), ids[None, :], N_EXPERTS)
    nxt = jnp.min(later, axis=1)
    nxt = jnp.where(nxt < N_EXPERTS, nxt, -1).astype(jnp.int32)
    return slot_src, slot_dst, blk_expert, blk_first, wslot[blk_expert], nxt[blk_expert], n_used


def _rope_tables(positions):
    pos = positions.reshape(-1).astype(F32)[:, None]

    def tables(rot, passthrough):
        half = rot // 2
        ang = pos * (ROPE_THETA ** (-jnp.arange(0, rot, 2, dtype=F32) / rot))
        cos, sin = jnp.cos(ang), jnp.sin(ang)
        z = lambda n: jnp.zeros((pos.shape[0], n), F32)
        tail = jnp.full((pos.shape[0], 128 - rot), passthrough, F32)
        return [jnp.concatenate([cos, cos, tail], 1), jnp.concatenate([z(half), sin, z(128 - rot)], 1),
                jnp.concatenate([-sin, z(128 - half)], 1)]

    return jnp.concatenate(tables(MLA_ROPE, 0.0) + tables(NSA_ROT, 1.0), axis=1)


def _selection_constants(s):
    ncp = s // CMP_STRIDE
    ns = s // SEL_BLOCK
    c_start = np.arange(ncp)[None, :] * CMP_STRIDE
    sel_start = np.arange(ns)[:, None] * SEL_BLOCK
    ovt = (c_start <= sel_start + SEL_BLOCK - 1) & (c_start + CMP_LEN - 1 >= sel_start)
    ovt[:, ncp - 1] = False
    expand_t = np.arange(s)[:, None] // SEL_BLOCK == np.arange(ns)[None, :]
    return jnp.asarray(ovt, BF16), jnp.asarray(expand_t, BF16)


def _arrange_w_in(w):
    cuts = np.cumsum([0, 512, 256, 64, 1024, 256, 256, 256, 256, 256, 256, 24])
    c_q, c_kv, k_r, q_n, k_c, v_c, k_s, v_s, k_w, v_w, gates = [w[:, cuts[i]:cuts[i + 1]] for i in range(11)]
    z = lambda n: jnp.zeros((w.shape[0], n), w.dtype)
    w1 = jnp.concatenate([c_q, c_kv, k_r, z(64), q_n, k_c, k_s, k_w, v_c, gates, z(104)], axis=1).astype(BF16)
    return w1, jnp.concatenate([v_s, v_w], axis=1).T.astype(BF16)


def _arrange_w_uq(w):
    w = w.reshape(MLA_Q_RANK, MLA_HEADS, MLA_NOPE + MLA_ROPE)
    w = jnp.pad(w, ((0, 0), (0, 0), (0, MLA_QK_PAD - MLA_NOPE - MLA_ROPE)))
    return w.reshape(MLA_Q_RANK, MLA_HEADS * MLA_QK_PAD).astype(BF16)


def _split_w_ukv(w):
    w = w.reshape(MLA_KV_RANK, MLA_HEADS, MLA_NOPE + MLA_V)
    return (w[:, :, :MLA_NOPE].reshape(MLA_KV_RANK, -1).astype(BF16),
            w[:, :, MLA_NOPE:].reshape(MLA_KV_RANK, -1).T.astype(BF16))


def _router_weights(w_grp, b_grp, w_exp, b_exp):
    pad = 128 - N_GROUPS - N_EXPERTS
    w = jnp.concatenate([w_grp, w_exp, jnp.zeros((w_grp.shape[0], pad), F32)], axis=1)
    hi = w.astype(BF16)
    lo = (w - hi.astype(F32)).astype(BF16)
    b = jnp.concatenate([b_grp, b_exp, jnp.zeros((pad,), F32)])[None, :]
    return hi, lo, b


def kernel(x, mem, positions, g_mem, g_final, g_mix, w_in, g_cq, w_uq, g_ckv, w_ukv, pe_k, pe_v, w_ck1, w_ck2,
           w_cv1, w_cv2, g_ya, g_yb, w_out, g_x, w_xq, w_xkv, w_xo, g_ffn, w_grp, b_grp, w_exp, b_exp,
           w_e1, w_e3, w_e2):
    b, s, d = x.shape
    depth = w_in.shape[0]
    t = b * s
    ncp = s // CMP_STRIDE
    tab = _rope_tables(positions)
    ovt, expand_t = _selection_constants(s)
    r2 = lambda v: v.reshape(1, -1)

    w_kv_all = jnp.concatenate([w_xkv[l] for l in range(depth)], axis=1).astype(BF16)
    memkv = _memkv(mem.reshape(-1, d), r2(g_mem), w_kv_all).reshape(b, mem.shape[1], -1)

    h = x.reshape(t, d)
    for l in range(depth):
        wkn, wv_t = _split_w_ukv(w_ukv[l])
        w1, wvt = _arrange_w_in(w_in[l])
        qa, ka, va_t, qn, kc, vc, ks, kw, vt, gt = _in_proj(
            h, r2(g_mix[l]), w1, wvt, r2(g_cq[l]), r2(g_ckv[l]), _arrange_w_uq(w_uq[l]), wkn, wv_t, tab)
        kcmp, vcmp_t = _compress(
            kc.reshape(NSA_G, b, ncp, CMP_STRIDE * NSA_D), vc.reshape(NSA_G, b, ncp, CMP_STRIDE * NSA_D),
            pe_k[l].reshape(1, -1), pe_v[l].reshape(1, -1), w_ck1[l].astype(BF16), w_ck2[l].astype(BF16),
            w_cv1[l].astype(BF16), w_cv2[l].astype(BF16))
        r3 = lambda a: a.reshape(b, s, -1)
        ya = _mla_attn(r3(qa), r3(ka), va_t)
        yb = _nsa_attn(r3(qn), kcmp, vcmp_t, r3(ks), r3(kw), vt, gt.reshape(NSA_G, b, s, 128), ovt, expand_t)
        wrh, wrl, br = _router_weights(w_grp[l], b_grp[l], w_exp[l], b_exp[l])
        h2, xn, eid, gate, counts = _post_mixer(
            ya, yb, r3(h), r2(g_ya[l]), r2(g_yb[l]), w_out[l].astype(BF16), r2(g_x[l]), w_xq[l].astype(BF16),
            memkv, l, w_xo[l].astype(BF16), r2(g_ffn[l]), wrh, wrl, br)
        eid = eid.reshape(t, 128)
        routing = _dispatch(eid[:, :TOP_K], eid[:, TOP_K:2 * TOP_K], counts[0, :N_EXPERTS])
        y12 = _experts(*routing, xn, w_e1, w_e3, w_e2, l)
        h = _combine(y12, h2.reshape(t, d), gate.reshape(t, 128), r2(g_final), final_norm=(l == depth - 1))
    return h.reshape(b, s, d)
```

```python
import functools

import jax
import jax.numpy as jnp
import numpy as np
from jax import lax
from jax.experimental import pallas as pl
from jax.experimental.pallas import tpu as pltpu

F32 = jnp.float32
BF16 = jnp.bfloat16

D_MODEL = 2048
RMS_EPS = 1e-6
ROPE_THETA = 500000.0
NEG = -1e30
FORCE_SCORE = 1e4

MLA_HEADS = 8
MLA_NOPE = 128
MLA_ROPE = 64
MLA_V = 128
MLA_Q_RANK = D_MODEL // 4
MLA_KV_RANK = D_MODEL // 8
MLA_QK_PAD = 256

NSA_HEADS = 8
NSA_G = 2
NSA_HPG = 4
NSA_D = 128
NSA_ROT = 32
CMP_LEN = 32
CMP_STRIDE = 16
CMP_HID = 512
SEL_BLOCK = 64
SEL_TOPN = 8
WINDOW = 512
QB = 128

XH = 4
XD = 128
MEM_LEN = 256

N_GROUPS = 8
EXP_PER_GROUP = 8
N_EXPERTS = 64
TOP_K = 2
D_EXPERT = D_MODEL // 4
MOE_BLOCK = 128

VMEM_LIMIT_BYTES = 56 * 1024 * 1024

C_CQ = 0
C_CKV = 512
C_KR = 768
C_QN = 896
C_KK = 1920
C_VC = 2688
C_GT = 2944
N_IN = 3072

LOG2E = 1.4426950408889634
SCALE_A = (MLA_NOPE + MLA_ROPE) ** -0.5 * LOG2E
SCALE_N = NSA_D ** -0.5 * LOG2E
SCALE_X = XD ** -0.5 * LOG2E


def _params(sem):
    return pltpu.CompilerParams(dimension_semantics=sem, vmem_limit_bytes=VMEM_LIMIT_BYTES)


def _mm(a, b):
    return jnp.dot(a, b, preferred_element_type=F32)


def _mm_nt(a, b):
    return lax.dot_general(a, b, (((1,), (1,)), ((), ())), preferred_element_type=F32)


def _rms(x, g):
    return x * lax.rsqrt(jnp.mean(x * x, axis=-1, keepdims=True) + RMS_EPS) * g


def _rope(x, c, sa, sb, half):
    return x * c + pltpu.roll(x, half, 1) * sa + pltpu.roll(x, 128 - half, 1) * sb


SLAB_ROWS = D_MODEL // 128
SLAB_PITCH_VMEM = 24


def _load_slabs(ref, n_tok, pitch):
    return jnp.concatenate([ref[pl.ds(c, n_tok, stride=pitch), :] for c in range(SLAB_ROWS)], axis=1)


def _store_slabs(ref, x, pitch):
    for c in range(SLAB_ROWS):
        ref[pl.ds(c, x.shape[0], stride=pitch), :] = x[:, c * 128:(c + 1) * 128]


def _const_spec(shape):
    nd = len(shape)
    return pl.BlockSpec(shape, lambda *_: (0,) * nd, pipeline_mode=pl.Buffered(1))


def _in_proj_body(h_ref, gmix_ref, w1_ref, wvt_ref, gcq_ref, gckv_ref, wq_ref, wkn_ref, wv_ref, tab_ref,
                  qa_ref, ka_ref, va_ref, qn_ref, kc_ref, vc_ref, ks_ref, kw_ref, vt_ref, gt_ref):
    xn = _rms(h_ref[...], gmix_ref[...]).astype(BF16)
    tab = tab_ref[...]
    ca, saa, sba, cn, san, sbn = [tab[:, i * 128:(i + 1) * 128] for i in range(6)]

    xq = _rms(_mm(xn, w1_ref[:, C_CQ:C_CKV]), gcq_ref[...]).astype(BF16)
    ckr = _mm(xn, w1_ref[:, C_CKV:C_QN])
    xkv = _rms(ckr[:, :MLA_KV_RANK], gckv_ref[...]).astype(BF16)
    kr = _rope(ckr[:, MLA_KV_RANK:], ca, saa, sba, MLA_ROPE // 2).astype(BF16)
    q = _mm(xq, wq_ref[...])
    kn = _mm(xkv, wkn_ref[...])
    for h in range(MLA_HEADS):
        lo = h * MLA_QK_PAD
        qa_ref[:, lo:lo + 128] = (q[:, lo:lo + 128] * SCALE_A).astype(BF16)
        qa_ref[:, lo + 128:lo + 256] = (
            _rope(q[:, lo + 128:lo + 256], ca, saa, sba, MLA_ROPE // 2) * SCALE_A).astype(BF16)
        ka_ref[:, lo:lo + 128] = kn[:, h * 128:(h + 1) * 128].astype(BF16)
        ka_ref[:, lo + 128:lo + 256] = kr
    va_ref[...] = _mm_nt(wv_ref[...], xkv).astype(BF16)

    qn = _mm(xn, w1_ref[:, C_QN:C_KK])
    for h in range(NSA_HEADS):
        sl = slice(h * 128, (h + 1) * 128)
        qn_ref[:, sl] = (_rope(qn[:, sl], cn, san, sbn, NSA_ROT // 2) * SCALE_N).astype(BF16)
    kk = _mm(xn, w1_ref[:, C_KK:C_VC])
    vc = _mm(xn, w1_ref[:, C_VC:C_GT])
    for g in range(NSA_G):
        sl = slice(g * 128, (g + 1) * 128)
        kc_ref[g] = _rope(kk[:, sl], cn, san, sbn, NSA_ROT // 2)
        ks_ref[:, sl] = _rope(kk[:, 256 + g * 128:256 + (g + 1) * 128], cn, san, sbn, NSA_ROT // 2).astype(BF16)
        kw_ref[:, sl] = _rope(kk[:, 512 + g * 128:512 + (g + 1) * 128], cn, san, sbn, NSA_ROT // 2).astype(BF16)
        vc_ref[g] = vc[:, sl]
    vt_ref[...] = _mm_nt(wvt_ref[...], xn).astype(BF16)
    gl = _mm(xn, w1_ref[:, C_GT:N_IN])
    gt_ref[0] = gl
    gt_ref[1] = pltpu.roll(gl, 128 - NSA_HPG * 3, 1)


def _in_proj(h, gmix, w1, wvt, gcq, gckv, wq, wkn, wv, tab, tm=512):
    t = h.shape[0]
    row = lambda n: pl.BlockSpec((tm, n), lambda i: (i, 0))
    col = lambda n: pl.BlockSpec((n, tm), lambda i: (0, i))
    grp = pl.BlockSpec((NSA_G, tm, 128), lambda i: (0, i, 0))
    out_shape = [
        jax.ShapeDtypeStruct((t, MLA_HEADS * MLA_QK_PAD), BF16),
        jax.ShapeDtypeStruct((t, MLA_HEADS * MLA_QK_PAD), BF16),
        jax.ShapeDtypeStruct((MLA_HEADS * MLA_V, t), BF16),
        jax.ShapeDtypeStruct((t, NSA_HEADS * NSA_D), BF16),
        jax.ShapeDtypeStruct((NSA_G, t, NSA_D), F32),
        jax.ShapeDtypeStruct((NSA_G, t, NSA_D), F32),
        jax.ShapeDtypeStruct((t, NSA_G * NSA_D), BF16),
        jax.ShapeDtypeStruct((t, NSA_G * NSA_D), BF16),
        jax.ShapeDtypeStruct((2 * NSA_G * NSA_D, t), BF16),
        jax.ShapeDtypeStruct((NSA_G, t, 128), F32),
    ]
    out_specs = [row(2048), row(2048), col(1024), row(1024), grp, grp, row(256), row(256), col(512), grp]
    in_specs = [row(D_MODEL), _const_spec((1, D_MODEL)), _const_spec((D_MODEL, N_IN)),
                _const_spec((2 * NSA_G * NSA_D, D_MODEL)),
                _const_spec((1, MLA_Q_RANK)), _const_spec((1, MLA_KV_RANK)),
                _const_spec((MLA_Q_RANK, MLA_HEADS * MLA_QK_PAD)), _const_spec((MLA_KV_RANK, MLA_HEADS * 128)),
                _const_spec((MLA_HEADS * 128, MLA_KV_RANK)), row(6 * 128)]
    return pl.pallas_call(
        _in_proj_body, out_shape=out_shape, grid=(t // tm,), in_specs=in_specs, out_specs=out_specs,
        compiler_params=_params(("parallel",)), name="in_proj",
    )(h, gmix, w1, wvt, gcq, gckv, wq, wkn, wv, tab)


def _compress_body(k_ref, v_ref, pek_ref, pev_ref, wk1_ref, wk2_ref, wv1_ref, wv2_ref, ko_ref, vo_ref):
    half = CMP_STRIDE * NSA_D

    def phi(x, pe, w1_ref, w2_ref):
        a = _mm((x + pe[:, :half]).astype(BF16), w1_ref[:half, :])
        b = _mm((x + pe[:, half:]).astype(BF16), w1_ref[half:, :])
        pre = a + pltpu.roll(b, b.shape[0] - 1, 0)
        return _mm(jax.nn.gelu(pre).astype(BF16), w2_ref[...])

    ko_ref[...] = phi(k_ref[...], pek_ref[...], wk1_ref, wk2_ref).astype(BF16)
    vo_ref[...] = phi(v_ref[...], pev_ref[...], wv1_ref, wv2_ref).T.astype(BF16)


def _compress(kc, vc, pek, pev, wk1, wk2, wv1, wv2):
    g, b, ncp, w = kc.shape
    in_blk = pl.BlockSpec((None, None, ncp, w), lambda bi, gi: (gi, bi, 0, 0))
    return pl.pallas_call(
        _compress_body,
        out_shape=[jax.ShapeDtypeStruct((b, g, ncp, NSA_D), BF16), jax.ShapeDtypeStruct((b, g, NSA_D, ncp), BF16)],
        grid=(b, g),
        in_specs=[in_blk, in_blk, _const_spec(pek.shape), _const_spec(pev.shape), _const_spec(wk1.shape),
                  _const_spec(wk2.shape), _const_spec(wv1.shape), _const_spec(wv2.shape)],
        out_specs=[pl.BlockSpec((None, None, ncp, NSA_D), lambda bi, gi: (bi, gi, 0, 0)),
                   pl.BlockSpec((None, None, NSA_D, ncp), lambda bi, gi: (bi, gi, 0, 0))],
        compiler_params=_params(("parallel", "parallel")), name="compress",
    )(kc, vc, pek, pev, wk1, wk2, wv1, wv2)


def _mla_body(q_ref, k_ref, vt_ref, o_ref, *, tq, n_q):
    i = pl.program_id(2)
    q = q_ref[...]
    tri = lax.broadcasted_iota(jnp.int32, (tq, tq), 0) <= lax.broadcasted_iota(jnp.int32, (tq, tq), 1)
    for ci in range(n_q):
        @pl.when(i == ci)
        def _(ci=ci):
            lo = ci * tq
            sd = jnp.where(tri, _mm_nt(k_ref[lo:lo + tq, :], q), NEG)
            m = jnp.max(sd, axis=0, keepdims=True)
            if ci > 0:
                sf = _mm_nt(k_ref[0:lo, :], q)
                m = jnp.maximum(m, jnp.max(sf, axis=0, keepdims=True))
                pf = jnp.exp2(sf - m)
            pd = jnp.exp2(sd - m)
            l = jnp.sum(pd, axis=0, keepdims=True)
            acc = _mm(vt_ref[:, lo:lo + tq], pd.astype(BF16))
            if ci > 0:
                l = l + jnp.sum(pf, axis=0, keepdims=True)
                acc = acc + _mm(vt_ref[:, 0:lo], pf.astype(BF16))
            o_ref[...] = (acc / l).T


def _mla_attn(q, k, vt, tq=512):
    b, s, _ = q.shape
    return pl.pallas_call(
        functools.partial(_mla_body, tq=tq, n_q=s // tq),
        out_shape=jax.ShapeDtypeStruct((b, s, MLA_HEADS * MLA_V), F32),
        grid=(b, MLA_HEADS, s // tq),
        in_specs=[pl.BlockSpec((None, tq, MLA_QK_PAD), lambda bi, hi, i: (bi, i, hi)),
                  pl.BlockSpec((None, s, MLA_QK_PAD), lambda bi, hi, i: (bi, 0, hi)),
                  pl.BlockSpec((MLA_V, s), lambda bi, hi, i: (hi, bi))],
        out_specs=pl.BlockSpec((None, tq, MLA_V), lambda bi, hi, i: (bi, i, hi)),
        compiler_params=_params(("parallel", "parallel", "arbitrary")), name="mla_attn",
    )(q, k, vt)


SEL_BUCKET = 512
WIN_LEN = WINDOW + QB


def _nsa_body(q_ref, kc_ref, vct_ref, ks_ref, vst_ref, kw_ref, vwt_ref, gt_ref, ovt_ref, expt_ref, o_ref, os_ref, *,
              n_buckets):
    i = pl.program_id(2)
    ns = ovt_ref.shape[0]
    ncp = ovt_ref.shape[1]
    cols = NSA_HPG * QB
    q = q_ref[...]
    qst = jnp.concatenate([q[:, h * 128:(h + 1) * 128] for h in range(NSA_HPG)], axis=0)

    def tile4(x):
        return jnp.concatenate([x] * NSA_HPG, axis=1)

    pos_c = i * QB + (lax.broadcasted_iota(jnp.int32, (ncp, cols), 1) & (QB - 1))
    cend = lax.broadcasted_iota(jnp.int32, (ncp, cols), 0) * CMP_STRIDE + (CMP_LEN - 1)
    vis = cend <= pos_c
    s = jnp.where(vis, _mm_nt(kc_ref[...], qst), NEG)
    e = jnp.where(vis, jnp.exp2(s - jnp.max(s, axis=0, keepdims=True)), 0.0)
    l = jnp.sum(e, axis=0, keepdims=True)
    p_c = e / jnp.where(l > 0.0, l, 1.0)
    o_c = _mm(vct_ref[...], p_c.astype(BF16))

    psum = p_c[:, 0:QB] + p_c[:, QB:2 * QB] + p_c[:, 2 * QB:3 * QB] + p_c[:, 3 * QB:4 * QB]
    p_hi = psum.astype(BF16)
    p_lo = (psum - p_hi.astype(F32)).astype(BF16)
    imp = _mm(ovt_ref[...], p_hi) + _mm(ovt_ref[...], p_lo)
    n_i = lax.broadcasted_iota(jnp.int32, (ns, QB), 0)
    pos_l = i * QB + lax.broadcasted_iota(jnp.int32, (ns, QB), 1)
    forced = (n_i == 0) | (n_i == pos_l // SEL_BLOCK)
    val = jnp.where(forced, FORCE_SCORE, jnp.where(n_i * SEL_BLOCK <= pos_l, imp, -FORCE_SCORE))
    rank = jnp.zeros((ns, QB), F32)
    for m_ in range(ns):
        vm = val[m_:m_ + 1, :]
        rank = rank + jnp.where((vm > val) | ((vm == val) & (n_i > m_)), 1.0, 0.0)
    sel_t = jnp.where(rank < float(min(SEL_TOPN, ns)), 1.0, 0.0).astype(BF16)

    for cb in range(n_buckets):
        @pl.when(i // (SEL_BUCKET // QB) == cb)
        def _(cb=cb):
            n_keys = (cb + 1) * SEL_BUCKET
            member = _mm(expt_ref[0:n_keys, :], sel_t)
            qcol = i * QB + lax.broadcasted_iota(jnp.int32, (n_keys, QB), 1)
            ok = (member > 0.5) & (lax.broadcasted_iota(jnp.int32, (n_keys, QB), 0) <= qcol)
            sc = _mm_nt(ks_ref[0:n_keys, :], qst) + tile4(jnp.where(ok, 0.0, NEG))
            p = jnp.exp2(sc - jnp.max(sc, axis=0, keepdims=True))
            os_ref[...] = _mm(vst_ref[:, 0:n_keys], p.astype(BF16)) / jnp.sum(p, axis=0, keepdims=True)

    o_s = os_ref[...]

    start = pl.multiple_of(jnp.maximum(i * QB - WINDOW, 0), QB)
    dist = (i * QB + lax.broadcasted_iota(jnp.int32, (WIN_LEN, QB), 1)) - (
        start + lax.broadcasted_iota(jnp.int32, (WIN_LEN, QB), 0))
    ok = (dist >= 0) & (dist < WINDOW)
    s = _mm_nt(kw_ref[pl.ds(start, WIN_LEN), :], qst) + tile4(jnp.where(ok, 0.0, NEG))
    p = jnp.exp2(s - jnp.max(s, axis=0, keepdims=True))
    o_w =_mm(vwt_ref[:, pl.ds(start, WIN_LEN)], p.astype(BF16)) / jnp.sum(p, axis=0, keepdims=True)

    g = jax.nn.sigmoid(gt_ref[...]).T
    for h in range(NSA_HPG):
        sl = slice(h * QB, (h + 1) * QB)
        o_t = (g[3 * h:3 * h + 1, :] * o_c[:, sl] + g[3 * h + 1:3 * h + 2, :] * o_s[:, sl]
               + g[3 * h + 2:3 * h + 3, :] * o_w[:, sl])
        o_ref[:, h * 128:(h + 1) * 128] = o_t.T


def _nsa_attn(q, kcmp, vcmp_t, ks, kw, vt, gt, ovt, expand_t):
    b, s, _ = q.shape
    ncp = kcmp.shape[2]
    gw = NSA_HPG * NSA_D
    slab = pl.BlockSpec((None, s, NSA_D), lambda bi, gi, i: (bi, 0, gi))
    return pl.pallas_call(
        functools.partial(_nsa_body, n_buckets=s // SEL_BUCKET),
        out_shape=jax.ShapeDtypeStruct((b, s, NSA_HEADS * NSA_D), F32),
        grid=(b, NSA_G, s // QB),
        in_specs=[pl.BlockSpec((None, QB, gw), lambda bi, gi, i: (bi, i, gi)),
                  pl.BlockSpec((None, None, ncp, NSA_D), lambda bi, gi, i: (bi, gi, 0, 0)),
                  pl.BlockSpec((None, None, NSA_D, ncp), lambda bi, gi, i: (bi, gi, 0, 0)),
                  slab, pl.BlockSpec((NSA_D, s), lambda bi, gi, i: (gi, bi)),
                  slab, pl.BlockSpec((NSA_D, s), lambda bi, gi, i: (NSA_G + gi, bi)),
                  pl.BlockSpec((None, None, QB, 128), lambda bi, gi, i: (gi, bi, i, 0)),
                  _const_spec(ovt.shape), _const_spec(expand_t.shape)],
        out_specs=pl.BlockSpec((None, QB, gw), lambda bi, gi, i: (bi, i, gi)),
        scratch_shapes=[pltpu.VMEM((NSA_D, NSA_HPG * QB), F32)],
        compiler_params=_params(("parallel", "parallel", "arbitrary")), name="nsa_attn",
    )(q, kcmp, vcmp_t, ks, vt, kw, vt, gt, ovt, expand_t)


def _post_body(ya_ref, yb_ref, h_ref, gya_ref, gyb_ref, wo_ref, gx_ref, wxq_ref, mk_ref, mv_ref, wxo_ref,
               gf_ref, wrh_ref, wrl_ref, br_ref, h2_ref, xn_ref, eid_ref, gate_ref, cnt_ref, cnt_scr):
    half = wo_ref.shape[0] // 2
    h1 = (h_ref[...] + _mm(_rms(ya_ref[...], gya_ref[...]).astype(BF16), wo_ref[:half, :])
          + _mm(_rms(yb_ref[...], gyb_ref[...]).astype(BF16), wo_ref[half:, :]))

    q = (_mm(_rms(h1, gx_ref[...]).astype(BF16), wxq_ref[...]) * SCALE_X).astype(BF16)
    outs = []
    for hd in range(XH):
        sl = slice(hd * XD, (hd + 1) * XD)
        s = _mm_nt(q[:, sl], mk_ref[:, sl])
        p = jnp.exp2(s - jnp.max(s, axis=1, keepdims=True))
        outs.append(_mm(p.astype(BF16), mv_ref[:, sl]) / jnp.sum(p, axis=1, keepdims=True))
    h2 = h1 + _mm(jnp.concatenate(outs, axis=1).astype(BF16), wxo_ref[...])
    h2_ref[...] = h2

    xf = _rms(h2, gf_ref[...])
    _store_slabs(xn_ref, xf, SLAB_ROWS)
    x_hi = xf.astype(BF16)
    x_lo = (xf - x_hi.astype(F32)).astype(BF16)
    logits = _mm(x_hi, wrh_ref[...]) + _mm(x_lo, wrh_ref[...]) + _mm(x_hi, wrl_ref[...]) + br_ref[...]
    lane = lax.broadcasted_iota(jnp.int32, logits.shape, 1)
    lane_f = lane.astype(F32)

    def argmax_first(x):
        v = jnp.max(x, axis=1, keepdims=True)
        return v, jnp.min(jnp.where(x == v, lane_f, 1e9), axis=1, keepdims=True)

    is_grp = lane < N_GROUPS
    lg = jnp.where(is_grp, logits, NEG)
    mx, grp = argmax_first(lg)
    p_top = 1.0 / jnp.sum(jnp.where(is_grp, jnp.exp(lg - mx), 0.0), axis=1, keepdims=True)
    in_grp = (lane >= N_GROUPS) & (lane < N_GROUPS + N_EXPERTS) & (
        ((lane - N_GROUPS) // EXP_PER_GROUP).astype(F32) == grp)
    le = jnp.where(in_grp, logits, NEG)
    v1, i1 = argmax_first(le)
    v2, i2 = argmax_first(jnp.where(lane_f == i1, NEG, le))
    t = jnp.exp(v2 - v1)
    g1 = p_top / (1.0 + t)
    g2 = p_top * t / (1.0 + t)
    gate_ref[...] = jnp.where(lane == 0, g1, jnp.where(lane == 1, g2, 0.0))

    e1 = i1 - N_GROUPS
    e2 = i2 - N_GROUPS
    oh1 = jnp.where(lane_f == e1, 1.0, 0.0)
    oh2 = jnp.where(lane_f == e2, 1.0, 0.0)
    both = oh1 + oh2
    n_tok = both.shape[0]
    earlier = jnp.where(lax.broadcasted_iota(jnp.int32, (n_tok, n_tok), 1)
                        < lax.broadcasted_iota(jnp.int32, (n_tok, n_tok), 0), 1.0, 0.0).astype(BF16)

    @pl.when((pl.program_id(0) == 0) & (pl.program_id(1) == 0))
    def _():
        cnt_scr[...] = jnp.zeros_like(cnt_scr)

    base = cnt_scr[...] + _mm(earlier, both.astype(BF16))
    r1 = jnp.sum(oh1 * base, axis=1, keepdims=True)
    r2 = jnp.sum(oh2 * base, axis=1, keepdims=True)
    cnt_scr[...] += jnp.sum(both, axis=0, keepdims=True)
    cnt_ref[...] = cnt_scr[...].astype(jnp.int32)
    eid_ref[...] = jnp.where(lane == 0, e1, jnp.where(lane == 1, e2, jnp.where(lane == 2, r1, jnp.where(
        lane == 3, r2, 0.0)))).astype(jnp.int32)


def _post_mixer(ya, yb, h, gya, gyb, wo, gx, wxq, memkv, layer, wxo, gf, wrh, wrl, br, tm=512):
    b, s, _ = ya.shape
    row = lambda n: pl.BlockSpec((None, tm, n), lambda bi, i: (bi, i, 0))
    hw = XH * XD
    in_specs = [row(1024), row(1024), row(D_MODEL), _const_spec(gya.shape), _const_spec(gyb.shape),
                _const_spec(wo.shape), _const_spec(gx.shape), _const_spec(wxq.shape),
                pl.BlockSpec((None, MEM_LEN, hw), lambda bi, i: (bi, 0, 2 * layer)),
                pl.BlockSpec((None, MEM_LEN, hw), lambda bi, i: (bi, 0, 2 * layer + 1)),
                _const_spec(wxo.shape), _const_spec(gf.shape), _const_spec(wrh.shape), _const_spec(wrl.shape),
                _const_spec(br.shape)]
    out_shape = [jax.ShapeDtypeStruct((b, s, D_MODEL), F32), jax.ShapeDtypeStruct((b * s * SLAB_ROWS, 128), F32),
                 jax.ShapeDtypeStruct((b, s, 128), jnp.int32), jax.ShapeDtypeStruct((b, s, 128), F32),
                 jax.ShapeDtypeStruct((1, 128), jnp.int32)]
    slabs = pl.BlockSpec((tm * SLAB_ROWS, 128), lambda bi, i: (bi * (s // tm) + i, 0))
    return pl.pallas_call(
        _post_body, out_shape=out_shape, grid=(b, s // tm), in_specs=in_specs,
        out_specs=[row(D_MODEL), slabs, row(128), row(128), pl.BlockSpec((1, 128), lambda bi, i: (0, 0))],
        scratch_shapes=[pltpu.VMEM((1, 128), F32)],
        compiler_params=_params(("arbitrary", "arbitrary")), name="post_mixer",
    )(ya, yb, h, gya, gyb, wo, gx, wxq, memkv, memkv, wxo, gf, wrh, wrl, br)


def _memkv_body(m_ref, g_ref, w_ref, o_ref):
    o_ref[...] = _mm(_rms(m_ref[...], g_ref[...]).astype(BF16), w_ref[...]).astype(BF16)


def _memkv(mem, g, w, tm=256):
    t = mem.shape[0]
    n = w.shape[1]
    return pl.pallas_call(
        _memkv_body, out_shape=jax.ShapeDtypeStruct((t, n), BF16), grid=(t // tm,),
        in_specs=[pl.BlockSpec((tm, D_MODEL), lambda i: (i, 0)), _const_spec(g.shape), _const_spec(w.shape)],
        out_specs=pl.BlockSpec((tm, n), lambda i: (i, 0)),
        compiler_params=_params(("parallel",)), name="memkv",
    )(mem, g, w)


def _tok_copy(src_ref, src_tok, dst_ref, dst_tok, sem):
    s0 = pl.multiple_of(src_tok * SLAB_ROWS, SLAB_ROWS)
    d0 = pl.multiple_of(dst_tok * SLAB_ROWS, SLAB_ROWS)
    return pltpu.make_async_copy(src_ref.at[pl.ds(s0, SLAB_ROWS), :], dst_ref.at[pl.ds(d0, SLAB_ROWS), :], sem)


SCATTER_TOKENS = 256


def _scatter_body(dest_ref, lastblk_ref, nu_ref, xn_ref, xs_ref, zbuf, sem, zsem, *, n_steps, n_blocks):
    i = pl.program_id(0)
    blk_rows = MOE_BLOCK * SLAB_ROWS

    def zero_copy(blk):
        return pltpu.make_async_copy(zbuf, xs_ref.at[pl.ds(blk * blk_rows, blk_rows), :], zsem)

    def zero_fill(go):
        for e in range(N_EXPERTS):
            @pl.when(lastblk_ref[e] >= 0)
            def _(e=e):
                go(zero_copy(lastblk_ref[e]))
        for blk in range(n_blocks):
            @pl.when(blk >= nu_ref[0])
            def _(blk=blk):
                go(zero_copy(blk))

    @pl.when(i == 0)
    def _():
        zbuf[...] = jnp.zeros_like(zbuf)
        zero_fill(lambda c: c.start())
        zero_fill(lambda c: c.wait())

    def row_copy(step, sl, j):
        a = step * (SCATTER_TOKENS * TOP_K) + j
        return _tok_copy(xn_ref, a // TOP_K, xs_ref, dest_ref[a], sem.at[sl])

    slot = i % 2
    for j in range(SCATTER_TOKENS * TOP_K):
        row_copy(i, slot, j).start()

    @pl.when(i >= 1)
    def _():
        for j in range(SCATTER_TOKENS * TOP_K):
            row_copy(i - 1, 1 - slot, j).wait()

    @pl.when(i == n_steps - 1)
    def _():
        for j in range(SCATTER_TOKENS * TOP_K):
            row_copy(i, slot, j).wait()


def _scatter_rows(dest, last_block, n_used, xn, n_slots):
    t = xn.shape[0] // SLAB_ROWS
    n_steps = t // SCATTER_TOKENS
    hbm = pl.BlockSpec(memory_space=pl.ANY)
    return pl.pallas_call(
        functools.partial(_scatter_body, n_steps=n_steps, n_blocks=n_slots // MOE_BLOCK),
        out_shape=jax.ShapeDtypeStruct((n_slots * SLAB_ROWS, 128), F32),
        grid_spec=pltpu.PrefetchScalarGridSpec(
            num_scalar_prefetch=3, grid=(n_steps,), in_specs=[hbm], out_specs=hbm,
            scratch_shapes=[pltpu.VMEM((MOE_BLOCK * SLAB_ROWS, 128), F32), pltpu.SemaphoreType.DMA((2,)),
                            pltpu.SemaphoreType.DMA(())]),
        compiler_params=_params(("arbitrary",)), name="scatter_rows",
    )(dest, last_block, n_used, xn)


def _experts_body(be_ref, bf_ref, ws_ref, nx_ref, nu_ref, x_ref, w1_ref, w3_ref, w2_ref, y_ref,
                  w1f, w3f, w2f, w1b, w3b, w2b, sem_w, *, layer):
    b = pl.program_id(0)
    n_used = nu_ref[0]

    def w_copies(e, ws):
        return [pltpu.make_async_copy(w_ref.at[layer, e], wf.at[ws], sem_w.at[ws])
                for w_ref, wf in ((w1_ref, w1f), (w3_ref, w3f), (w2_ref, w2f))]

    @pl.when(b == 0)
    def _():
        for c in w_copies(be_ref[0], 0):
            c.start()

    @pl.when(b >= n_used)
    def _():
        y_ref[...] = jnp.zeros_like(y_ref)

    @pl.when(b < n_used)
    def _():
        @pl.when(bf_ref[b] == 1)
        def _():
            ws = ws_ref[b]

            @pl.when(nx_ref[b] >= 0)
            def _():
                for c in w_copies(nx_ref[b], 1 - ws):
                    c.start()

            for c in w_copies(be_ref[b], ws):
                c.wait()
            w1b[...] = w1f[ws].astype(BF16)
            w3b[...] = w3f[ws].astype(BF16)
            w2b[...] = w2f[ws].astype(BF16)

        x = _load_slabs(x_ref, MOE_BLOCK, SLAB_ROWS).astype(BF16)
        hb = (jax.nn.silu(_mm(x, w1b[...])) * _mm(x, w3b[...])).astype(BF16)
        _store_slabs(y_ref, _mm(hb, w2b[...]), SLAB_ROWS)


def _experts(blk_expert, blk_first, blk_wslot, nxt_expert, n_used, xs, w_e1, w_e3, w_e2, layer):
    d = D_MODEL
    n_blocks = xs.shape[0] // (MOE_BLOCK * SLAB_ROWS)
    hbm = pl.BlockSpec(memory_space=pl.ANY)
    x_spec = pl.BlockSpec((MOE_BLOCK * SLAB_ROWS, 128), lambda b, be, bf, ws, nx, nu: (jnp.minimum(b, nu[0] - 1), 0))
    y_spec = pl.BlockSpec((MOE_BLOCK * SLAB_ROWS, 128), lambda b, be, bf, ws, nx, nu: (b, 0))
    return pl.pallas_call(
        functools.partial(_experts_body, layer=layer),
        out_shape=jax.ShapeDtypeStruct(xs.shape, F32),
        grid_spec=pltpu.PrefetchScalarGridSpec(
            num_scalar_prefetch=5, grid=(n_blocks,),
            in_specs=[x_spec, hbm, hbm, hbm], out_specs=y_spec,
            scratch_shapes=[pltpu.VMEM((2, d, D_EXPERT), F32), pltpu.VMEM((2, d, D_EXPERT), F32),
                            pltpu.VMEM((2, D_EXPERT, d), F32),
                            pltpu.VMEM((d, D_EXPERT), BF16), pltpu.VMEM((d, D_EXPERT), BF16),
                            pltpu.VMEM((D_EXPERT, d), BF16), pltpu.SemaphoreType.DMA((2,))]),
        compiler_params=_params(("arbitrary",)), name="experts",
    )(blk_expert, blk_first, blk_wslot, nxt_expert, n_used, xs, w_e1, w_e3, w_e2)


COMBINE_TOKENS = 128


def _combine_body(dest_ref, ys_ref, h_ref, gate_ref, gfin_ref, o_ref, abuf, bbuf, sem, *, n_steps, final_norm):
    i = pl.program_id(0)
    slot = i % 2

    def copies(step, sl, j):
        a = (step * COMBINE_TOKENS + j) * TOP_K
        out = []
        for k, buf in enumerate((abuf, bbuf)):
            s0 = pl.multiple_of(dest_ref[a + k] * SLAB_ROWS, SLAB_ROWS)
            out.append(pltpu.make_async_copy(ys_ref.at[pl.ds(s0, SLAB_ROWS), :],
                                             buf.at[sl, pl.ds(j * SLAB_PITCH_VMEM, SLAB_ROWS), :], sem.at[sl]))
        return out

    def start(step, sl):
        for j in range(COMBINE_TOKENS):
            for c in copies(step, sl, j):
                c.start()

    @pl.when(i == 0)
    def _():
        start(0, 0)

    @pl.when(i + 1 < n_steps)
    def _():
        start(i + 1, 1 - slot)

    for j in range(COMBINE_TOKENS):
        for c in copies(i, slot, j):
            c.wait()
    g = gate_ref[...]
    y = (h_ref[...] + g[:, 0:1] * _load_slabs(abuf.at[slot], COMBINE_TOKENS, SLAB_PITCH_VMEM)
         + g[:, 1:2] * _load_slabs(bbuf.at[slot], COMBINE_TOKENS, SLAB_PITCH_VMEM))
    if final_norm:
        y = _rms(y, gfin_ref[...])
    o_ref[...] = y


def _combine(dest, ys, h, gate, gfin, final_norm):
    t, d = h.shape
    tm = COMBINE_TOKENS
    n_steps = t // tm
    buf = pltpu.VMEM((2, tm * SLAB_PITCH_VMEM, 128), F32)
    return pl.pallas_call(
        functools.partial(_combine_body, n_steps=n_steps, final_norm=final_norm),
        out_shape=jax.ShapeDtypeStruct((t, d), F32),
        grid_spec=pltpu.PrefetchScalarGridSpec(
            num_scalar_prefetch=1, grid=(n_steps,),
            in_specs=[pl.BlockSpec(memory_space=pl.ANY), pl.BlockSpec((tm, d), lambda i, dr: (i, 0)),
                      pl.BlockSpec((tm, 128), lambda i, dr: (i, 0)), pl.BlockSpec((1, d), lambda i, dr: (0, 0))],
            out_specs=pl.BlockSpec((tm, d), lambda i, dr: (i, 0)),
            scratch_shapes=[buf, buf, pltpu.SemaphoreType.DMA((2,))]),
        compiler_params=_params(("arbitrary",)), name="combine",
    )(dest, ys, h, gate, gfin)


def _dispatch(eid, rank, counts):
    n_assign = eid.shape[0] * TOP_K
    n_blocks = (n_assign + N_EXPERTS * (MOE_BLOCK - 1) + MOE_BLOCK - 1) // MOE_BLOCK
    ids = jnp.arange(N_EXPERTS, dtype=jnp.int32)
    blocks = (counts + MOE_BLOCK - 1) // MOE_BLOCK
    blk_end = jnp.cumsum(blocks)
    blk_begin = blk_end - blocks
    onehot = (eid.reshape(-1)[:, None] == ids[None, :]).astype(F32)
    first_blk = jnp.dot(onehot, blk_begin.astype(F32)).astype(jnp.int32)
    dest = first_blk * MOE_BLOCK + rank.reshape(-1)
    blk_ids = jnp.arange(n_blocks, dtype=jnp.int32)
    blk_expert = jnp.minimum(jnp.sum((blk_end[None, :] <= blk_ids[:, None]).astype(jnp.int32), axis=1), N_EXPERTS - 1)
    blk_first = jnp.concatenate([jnp.ones((1,), jnp.int32), (blk_expert[1:] != blk_expert[:-1]).astype(jnp.int32)])
    n_used = blk_end[-1:].astype(jnp.int32)
    has_rows = counts > 0
    last_block = jnp.where(has_rows, blk_end - 1, -1).astype(jnp.int32)
    wslot = ((jnp.cumsum(has_rows.astype(jnp.int32)) - 1) % 2).astype(jnp.int32)
    later = jnp.where(has_rows[None, :] & (ids[None, :] > ids[:, None]), ids[None, :], N_EXPERTS)
    nxt = jnp.min(later, axis=1)
    nxt = jnp.where(nxt < N_EXPERTS, nxt, -1).astype(jnp.int32)
    return dest, last_block, (blk_expert, blk_first, wslot[blk_expert], nxt[blk_expert], n_used), n_blocks * MOE_BLOCK


def _rope_tables(positions):
    pos = positions.reshape(-1).astype(F32)[:, None]

    def tables(rot, passthrough):
        half = rot // 2
        ang = pos * (ROPE_THETA ** (-jnp.arange(0, rot, 2, dtype=F32) / rot))
        cos, sin = jnp.cos(ang), jnp.sin(ang)
        z = lambda n: jnp.zeros((pos.shape[0], n), F32)
        tail = jnp.full((pos.shape[0], 128 - rot), passthrough, F32)
        return [jnp.concatenate([cos, cos, tail], 1), jnp.concatenate([z(half), sin, z(128 - rot)], 1),
                jnp.concatenate([-sin, z(128 - half)], 1)]

    return jnp.concatenate(tables(MLA_ROPE, 0.0) + tables(NSA_ROT, 1.0), axis=1)


def _selection_constants(s):
    ncp = s // CMP_STRIDE
    ns = s // SEL_BLOCK
    c_start = np.arange(ncp)[None, :] * CMP_STRIDE
    sel_start = np.arange(ns)[:, None] * SEL_BLOCK
    ovt = (c_start <= sel_start + SEL_BLOCK - 1) & (c_start + CMP_LEN - 1 >= sel_start)
    ovt[:, ncp - 1] = False
    expand_t = np.arange(s)[:, None] // SEL_BLOCK == np.arange(ns)[None, :]
    return jnp.asarray(ovt, BF16), jnp.asarray(expand_t, BF16)


def _arrange_w_in(w):
    cuts = np.cumsum([0, 512, 256, 64, 1024, 256, 256, 256, 256, 256, 256, 24])
    c_q, c_kv, k_r, q_n, k_c, v_c, k_s, v_s, k_w, v_w, gates = [w[:, cuts[i]:cuts[i + 1]] for i in range(11)]
    z = lambda n: jnp.zeros((w.shape[0], n), w.dtype)
    w1 = jnp.concatenate([c_q, c_kv, k_r, z(64), q_n, k_c, k_s, k_w, v_c, gates, z(104)], axis=1).astype(BF16)
    return w1, jnp.concatenate([v_s, v_w], axis=1).T.astype(BF16)


def _arrange_w_uq(w):
    w = w.reshape(MLA_Q_RANK, MLA_HEADS, MLA_NOPE + MLA_ROPE)
    w = jnp.pad(w, ((0, 0), (0, 0), (0, MLA_QK_PAD - MLA_NOPE - MLA_ROPE)))
    return w.reshape(MLA_Q_RANK, MLA_HEADS * MLA_QK_PAD).astype(BF16)


def _split_w_ukv(w):
    w = w.reshape(MLA_KV_RANK, MLA_HEADS, MLA_NOPE + MLA_V)
    return (w[:, :, :MLA_NOPE].reshape(MLA_KV_RANK, -1).astype(BF16),
            w[:, :, MLA_NOPE:].reshape(MLA_KV_RANK, -1).T.astype(BF16))


def _router_weights(w_grp, b_grp, w_exp, b_exp):
    pad = 128 - N_GROUPS - N_EXPERTS
    w = jnp.concatenate([w_grp, w_exp, jnp.zeros((w_grp.shape[0], pad), F32)], axis=1)
    hi = w.astype(BF16)
    lo = (w - hi.astype(F32)).astype(BF16)
    b = jnp.concatenate([b_grp, b_exp, jnp.zeros((pad,), F32)])[None, :]
    return hi, lo, b


def kernel(x, mem, positions, g_mem, g_final, g_mix, w_in, g_cq, w_uq, g_ckv, w_ukv, pe_k, pe_v, w_ck1, w_ck2,
           w_cv1, w_cv2, g_ya, g_yb, w_out, g_x, w_xq, w_xkv, w_xo, g_ffn, w_grp, b_grp, w_exp, b_exp,
           w_e1, w_e3, w_e2):
    b, s, d = x.shape
    depth = w_in.shape[0]
    t = b * s
    ncp = s // CMP_STRIDE
    tab = _rope_tables(positions)
    ovt, expand_t = _selection_constants(s)
    r2 = lambda v: v.reshape(1, -1)

    w_kv_all = jnp.concatenate([w_xkv[l] for l in range(depth)], axis=1).astype(BF16)
    memkv = _memkv(mem.reshape(-1, d), r2(g_mem), w_kv_all).reshape(b, mem.shape[1], -1)

    h = x.reshape(t, d)
    for l in range(depth):
        wkn, wv_t = _split_w_ukv(w_ukv[l])
        w1, wvt = _arrange_w_in(w_in[l])
        qa, ka, va_t, qn, kc, vc, ks, kw, vt, gt = _in_proj(
            h, r2(g_mix[l]), w1, wvt, r2(g_cq[l]), r2(g_ckv[l]), _arrange_w_uq(w_uq[l]), wkn, wv_t, tab)
        kcmp, vcmp_t = _compress(
            kc.reshape(NSA_G, b, ncp, CMP_STRIDE * NSA_D), vc.reshape(NSA_G, b, ncp, CMP_STRIDE * NSA_D),
            pe_k[l].reshape(1, -1), pe_v[l].reshape(1, -1), w_ck1[l].astype(BF16), w_ck2[l].astype(BF16),
            w_cv1[l].astype(BF16), w_cv2[l].astype(BF16))
        r3 = lambda a: a.reshape(b, s, -1)
        ya = _mla_attn(r3(qa), r3(ka), va_t)
        yb = _nsa_attn(r3(qn), kcmp, vcmp_t, r3(ks), r3(kw), vt, gt.reshape(NSA_G, b, s, 128), ovt, expand_t)
        wrh, wrl, br = _router_weights(w_grp[l], b_grp[l], w_exp[l], b_exp[l])
        h2, xn, eid, gate, counts = _post_mixer(
            ya, yb, r3(h), r2(g_ya[l]), r2(g_yb[l]), w_out[l].astype(BF16), r2(g_x[l]), w_xq[l].astype(BF16),
            memkv, l, w_xo[l].astype(BF16), r2(g_ffn[l]), wrh, wrl, br)
        eid = eid.reshape(t, 128)
        dest, last_block, blocks, n_slots = _dispatch(
            eid[:, :TOP_K], eid[:, TOP_K:2 * TOP_K], counts[0, :N_EXPERTS])
        xs = _scatter_rows(dest, last_block, blocks[-1], xn, n_slots)
        ys = _experts(*blocks, xs, w_e1, w_e3, w_e2, l)
        h = _combine(dest, ys, h2.reshape(t, d), gate.reshape(t, 128), r2(g_final), final_norm=(l == depth - 1))
    return h.reshape(b, s, d)
```

```python
import functools

import jax
import jax.numpy as jnp
import numpy as np
from jax import lax
from jax.experimental import pallas as pl
from jax.experimental.pallas import tpu as pltpu

F32 = jnp.float32
BF16 = jnp.bfloat16

D_MODEL = 2048
RMS_EPS = 1e-6
ROPE_THETA = 500000.0
NEG = -1e30
FORCE_SCORE = 1e4

MLA_HEADS = 8
MLA_NOPE = 128
MLA_ROPE = 64
MLA_V = 128
MLA_Q_RANK = D_MODEL // 4
MLA_KV_RANK = D_MODEL // 8
MLA_QK_PAD = 256

NSA_HEADS = 8
NSA_G = 2
NSA_HPG = 4
NSA_D = 128
NSA_ROT = 32
CMP_LEN = 32
CMP_STRIDE = 16
CMP_HID = 512
SEL_BLOCK = 64
SEL_TOPN = 8
WINDOW = 512
QB = 128

XH = 4
XD = 128
MEM_LEN = 256

N_GROUPS = 8
EXP_PER_GROUP = 8
N_EXPERTS = 64
TOP_K = 2
D_EXPERT = D_MODEL // 4
MOE_BLOCK = 128

VMEM_LIMIT_BYTES = 56 * 1024 * 1024

C_CQ = 0
C_CKV = 512
C_KR = 768
C_QN = 896
C_KK = 1920
C_VC = 2688
C_GT = 2944
N_IN = 3072

LOG2E = 1.4426950408889634
SCALE_A = (MLA_NOPE + MLA_ROPE) ** -0.5 * LOG2E
SCALE_N = NSA_D ** -0.5 * LOG2E
SCALE_X = XD ** -0.5 * LOG2E


def _params(sem):
    return pltpu.CompilerParams(dimension_semantics=sem, vmem_limit_bytes=VMEM_LIMIT_BYTES)


def _mm(a, b):
    return jnp.dot(a, b, preferred_element_type=F32)


def _mm_nt(a, b):
    return lax.dot_general(a, b, (((1,), (1,)), ((), ())), preferred_element_type=F32)


def _rms(x, g):
    return x * lax.rsqrt(jnp.mean(x * x, axis=-1, keepdims=True) + RMS_EPS) * g


def _rope(x, c, sa, sb, half):
    return x * c + pltpu.roll(x, half, 1) * sa + pltpu.roll(x, 128 - half, 1) * sb


SLAB_ROWS = D_MODEL // 128
SLAB_PITCH_VMEM = 24


def _load_slabs(ref, n_tok, pitch):
    return jnp.concatenate([ref[pl.ds(c, n_tok, stride=pitch), :] for c in range(SLAB_ROWS)], axis=1)


def _store_slabs(ref, x, pitch):
    for c in range(SLAB_ROWS):
        ref[pl.ds(c, x.shape[0], stride=pitch), :] = x[:, c * 128:(c + 1) * 128]


def _const_spec(shape):
    nd = len(shape)
    return pl.BlockSpec(shape, lambda *_: (0,) * nd, pipeline_mode=pl.Buffered(1))


def _in_proj_body(h_ref, gmix_ref, w1_ref, wvt_ref, gcq_ref, gckv_ref, wq_ref, wkn_ref, wv_ref, tab_ref,
                  qa_ref, ka_ref, va_ref, qn_ref, kc_ref, vc_ref, ks_ref, kw_ref, vt_ref, gt_ref):
    xn = _rms(h_ref[...], gmix_ref[...]).astype(BF16)
    tab = tab_ref[...]
    ca, saa, sba, cn, san, sbn = [tab[:, i * 128:(i + 1) * 128] for i in range(6)]

    xq = _rms(_mm(xn, w1_ref[:, C_CQ:C_CKV]), gcq_ref[...]).astype(BF16)
    ckr = _mm(xn, w1_ref[:, C_CKV:C_QN])
    xkv = _rms(ckr[:, :MLA_KV_RANK], gckv_ref[...]).astype(BF16)
    kr = _rope(ckr[:, MLA_KV_RANK:], ca, saa, sba, MLA_ROPE // 2).astype(BF16)
    q = _mm(xq, wq_ref[...])
    kn = _mm(xkv, wkn_ref[...])
    for h in range(MLA_HEADS):
        lo = h * MLA_QK_PAD
        qa_ref[:, lo:lo + 128] = (q[:, lo:lo + 128] * SCALE_A).astype(BF16)
        qa_ref[:, lo + 128:lo + 256] = (
            _rope(q[:, lo + 128:lo + 256], ca, saa, sba, MLA_ROPE // 2) * SCALE_A).astype(BF16)
        ka_ref[:, lo:lo + 128] = kn[:, h * 128:(h + 1) * 128].astype(BF16)
        ka_ref[:, lo + 128:lo + 256] = kr
    va_ref[...] = _mm_nt(wv_ref[...], xkv).astype(BF16)

    qn = _mm(xn, w1_ref[:, C_QN:C_KK])
    for h in range(NSA_HEADS):
        sl = slice(h * 128, (h + 1) * 128)
        qn_ref[:, sl] = (_rope(qn[:, sl], cn, san, sbn, NSA_ROT // 2) * SCALE_N).astype(BF16)
    kk = _mm(xn, w1_ref[:, C_KK:C_VC])
    vc = _mm(xn, w1_ref[:, C_VC:C_GT])
    for g in range(NSA_G):
        sl = slice(g * 128, (g + 1) * 128)
        kc_ref[g] = _rope(kk[:, sl], cn, san, sbn, NSA_ROT // 2)
        ks_ref[:, sl] = _rope(kk[:, 256 + g * 128:256 + (g + 1) * 128], cn, san, sbn, NSA_ROT // 2).astype(BF16)
        kw_ref[:, sl] = _rope(kk[:, 512 + g * 128:512 + (g + 1) * 128], cn, san, sbn, NSA_ROT // 2).astype(BF16)
        vc_ref[g] = vc[:, sl]
    vt_ref[...] = _mm_nt(wvt_ref[...], xn).astype(BF16)
    gl = _mm(xn, w1_ref[:, C_GT:N_IN])
    gt_ref[0] = gl
    gt_ref[1] = pltpu.roll(gl, 128 - NSA_HPG * 3, 1)


def _in_proj(h, gmix, w1, wvt, gcq, gckv, wq, wkn, wv, tab, tm=512):
    t = h.shape[0]
    row = lambda n: pl.BlockSpec((tm, n), lambda i: (i, 0))
    col = lambda n: pl.BlockSpec((n, tm), lambda i: (0, i))
    grp = pl.BlockSpec((NSA_G, tm, 128), lambda i: (0, i, 0))
    out_shape = [
        jax.ShapeDtypeStruct((t, MLA_HEADS * MLA_QK_PAD), BF16),
        jax.ShapeDtypeStruct((t, MLA_HEADS * MLA_QK_PAD), BF16),
        jax.ShapeDtypeStruct((MLA_HEADS * MLA_V, t), BF16),
        jax.ShapeDtypeStruct((t, NSA_HEADS * NSA_D), BF16),
        jax.ShapeDtypeStruct((NSA_G, t, NSA_D), F32),
        jax.ShapeDtypeStruct((NSA_G, t, NSA_D), F32),
        jax.ShapeDtypeStruct((t, NSA_G * NSA_D), BF16),
        jax.ShapeDtypeStruct((t, NSA_G * NSA_D), BF16),
        jax.ShapeDtypeStruct((2 * NSA_G * NSA_D, t), BF16),
        jax.ShapeDtypeStruct((NSA_G, t, 128), F32),
    ]
    out_specs = [row(2048), row(2048), col(1024), row(1024), grp, grp, row(256), row(256), col(512), grp]
    in_specs = [row(D_MODEL), _const_spec((1, D_MODEL)), _const_spec((D_MODEL, N_IN)),
                _const_spec((2 * NSA_G * NSA_D, D_MODEL)),
                _const_spec((1, MLA_Q_RANK)), _const_spec((1, MLA_KV_RANK)),
                _const_spec((MLA_Q_RANK, MLA_HEADS * MLA_QK_PAD)), _const_spec((MLA_KV_RANK, MLA_HEADS * 128)),
                _const_spec((MLA_HEADS * 128, MLA_KV_RANK)), row(6 * 128)]
    return pl.pallas_call(
        _in_proj_body, out_shape=out_shape, grid=(t // tm,), in_specs=in_specs, out_specs=out_specs,
        compiler_params=_params(("parallel",)), name="in_proj",
    )(h, gmix, w1, wvt, gcq, gckv, wq, wkn, wv, tab)


def _compress_body(k_ref, v_ref, pek_ref, pev_ref, wk1_ref, wk2_ref, wv1_ref, wv2_ref, ko_ref, vo_ref):
    half = CMP_STRIDE * NSA_D

    def phi(x, pe, w1_ref, w2_ref):
        a = _mm((x + pe[:, :half]).astype(BF16), w1_ref[:half, :])
        b = _mm((x + pe[:, half:]).astype(BF16), w1_ref[half:, :])
        pre = a + pltpu.roll(b, b.shape[0] - 1, 0)
        return _mm(jax.nn.gelu(pre).astype(BF16), w2_ref[...])

    ko_ref[...] = phi(k_ref[...], pek_ref[...], wk1_ref, wk2_ref).astype(BF16)
    vo_ref[...] = phi(v_ref[...], pev_ref[...], wv1_ref, wv2_ref).T.astype(BF16)


def _compress(kc, vc, pek, pev, wk1, wk2, wv1, wv2):
    g, b, ncp, w = kc.shape
    in_blk = pl.BlockSpec((None, None, ncp, w), lambda bi, gi: (gi, bi, 0, 0))
    return pl.pallas_call(
        _compress_body,
        out_shape=[jax.ShapeDtypeStruct((b, g, ncp, NSA_D), BF16), jax.ShapeDtypeStruct((b, g, NSA_D, ncp), BF16)],
        grid=(b, g),
        in_specs=[in_blk, in_blk, _const_spec(pek.shape), _const_spec(pev.shape), _const_spec(wk1.shape),
                  _const_spec(wk2.shape), _const_spec(wv1.shape), _const_spec(wv2.shape)],
        out_specs=[pl.BlockSpec((None, None, ncp, NSA_D), lambda bi, gi: (bi, gi, 0, 0)),
                   pl.BlockSpec((None, None, NSA_D, ncp), lambda bi, gi: (bi, gi, 0, 0))],
        compiler_params=_params(("parallel", "parallel")), name="compress",
    )(kc, vc, pek, pev, wk1, wk2, wv1, wv2)


def _mla_body(q_ref, k_ref, vt_ref, o_ref, *, tq, n_q):
    i = pl.program_id(2)
    q = q_ref[...]
    tri = lax.broadcasted_iota(jnp.int32, (tq, tq), 0) <= lax.broadcasted_iota(jnp.int32, (tq, tq), 1)
    for ci in range(n_q):
        @pl.when(i == ci)
        def _(ci=ci):
            lo = ci * tq
            sd = jnp.where(tri, _mm_nt(k_ref[lo:lo + tq, :], q), NEG)
            m = jnp.max(sd, axis=0, keepdims=True)
            if ci > 0:
                sf = _mm_nt(k_ref[0:lo, :], q)
                m = jnp.maximum(m, jnp.max(sf, axis=0, keepdims=True))
                pf = jnp.exp2(sf - m)
            pd = jnp.exp2(sd - m)
            l = jnp.sum(pd, axis=0, keepdims=True)
            acc = _mm(vt_ref[:, lo:lo + tq], pd.astype(BF16))
            if ci > 0:
                l = l + jnp.sum(pf, axis=0, keepdims=True)
                acc = acc + _mm(vt_ref[:, 0:lo], pf.astype(BF16))
            o_ref[...] = (acc / l).T


def _mla_attn(q, k, vt, tq=512):
    b, s, _ = q.shape
    return pl.pallas_call(
        functools.partial(_mla_body, tq=tq, n_q=s // tq),
        out_shape=jax.ShapeDtypeStruct((b, s, MLA_HEADS * MLA_V), F32),
        grid=(b, MLA_HEADS, s // tq),
        in_specs=[pl.BlockSpec((None, tq, MLA_QK_PAD), lambda bi, hi, i: (bi, i, hi)),
                  pl.BlockSpec((None, s, MLA_QK_PAD), lambda bi, hi, i: (bi, 0, hi)),
                  pl.BlockSpec((MLA_V, s), lambda bi, hi, i: (hi, bi))],
        out_specs=pl.BlockSpec((None, tq, MLA_V), lambda bi, hi, i: (bi, i, hi)),
        compiler_params=_params(("parallel", "parallel", "arbitrary")), name="mla_attn",
    )(q, k, vt)


SEL_BUCKET = 512
WIN_LEN = WINDOW + QB


def _nsa_body(q_ref, kc_ref, vct_ref, ks_ref, vst_ref, kw_ref, vwt_ref, gt_ref, ovt_ref, expt_ref, o_ref, os_ref, *,
              n_buckets):
    i = pl.program_id(2)
    ns = ovt_ref.shape[0]
    ncp = ovt_ref.shape[1]
    cols = NSA_HPG * QB
    q = q_ref[...]
    qst = jnp.concatenate([q[:, h * 128:(h + 1) * 128] for h in range(NSA_HPG)], axis=0)

    def tile4(x):
        return jnp.concatenate([x] * NSA_HPG, axis=1)

    pos_c = i * QB + (lax.broadcasted_iota(jnp.int32, (ncp, cols), 1) & (QB - 1))
    cend = lax.broadcasted_iota(jnp.int32, (ncp, cols), 0) * CMP_STRIDE + (CMP_LEN - 1)
    vis = cend <= pos_c
    s = jnp.where(vis, _mm_nt(kc_ref[...], qst), NEG)
    e = jnp.where(vis, jnp.exp2(s - jnp.max(s, axis=0, keepdims=True)), 0.0)
    l = jnp.sum(e, axis=0, keepdims=True)
    p_c = e / jnp.where(l > 0.0, l, 1.0)
    o_c = _mm(vct_ref[...], p_c.astype(BF16))

    psum = p_c[:, 0:QB] + p_c[:, QB:2 * QB] + p_c[:, 2 * QB:3 * QB] + p_c[:, 3 * QB:4 * QB]
    p_hi = psum.astype(BF16)
    p_lo = (psum - p_hi.astype(F32)).astype(BF16)
    imp = _mm(ovt_ref[...], p_hi) + _mm(ovt_ref[...], p_lo)
    n_i = lax.broadcasted_iota(jnp.int32, (ns, QB), 0)
    pos_l = i * QB + lax.broadcasted_iota(jnp.int32, (ns, QB), 1)
    forced = (n_i == 0) | (n_i == pos_l // SEL_BLOCK)
    val = jnp.where(forced, FORCE_SCORE, jnp.where(n_i * SEL_BLOCK <= pos_l, imp, -FORCE_SCORE))
    rank = jnp.zeros((ns, QB), F32)
    for m_ in range(ns):
        vm = val[m_:m_ + 1, :]
        rank = rank + jnp.where((vm > val) | ((vm == val) & (n_i > m_)), 1.0, 0.0)
    sel_t = jnp.where(rank < float(min(SEL_TOPN, ns)), 1.0, 0.0).astype(BF16)

    for cb in range(n_buckets):
        @pl.when(i // (SEL_BUCKET // QB) == cb)
        def _(cb=cb):
            n_keys = (cb + 1) * SEL_BUCKET
            member = _mm(expt_ref[0:n_keys, :], sel_t)
            qcol = i * QB + lax.broadcasted_iota(jnp.int32, (n_keys, QB), 1)
            ok = (member > 0.5) & (lax.broadcasted_iota(jnp.int32, (n_keys, QB), 0) <= qcol)
            sc = _mm_nt(ks_ref[0:n_keys, :], qst) + tile4(jnp.where(ok, 0.0, NEG))
            p = jnp.exp2(sc - jnp.max(sc, axis=0, keepdims=True))
            os_ref[...] = _mm(vst_ref[:, 0:n_keys], p.astype(BF16)) / jnp.sum(p, axis=0, keepdims=True)

    o_s = os_ref[...]

    start = pl.multiple_of(jnp.maximum(i * QB - WINDOW, 0), QB)
    dist = (i * QB + lax.broadcasted_iota(jnp.int32, (WIN_LEN, QB), 1)) - (
        start + lax.broadcasted_iota(jnp.int32, (WIN_LEN, QB), 0))
    ok = (dist >= 0) & (dist < WINDOW)
    s = _mm_nt(kw_ref[pl.ds(start, WIN_LEN), :], qst) + tile4(jnp.where(ok, 0.0, NEG))
    p = jnp.exp2(s - jnp.max(s, axis=0, keepdims=True))
    o_w =_mm(vwt_ref[:, pl.ds(start, WIN_LEN)], p.astype(BF16)) / jnp.sum(p, axis=0, keepdims=True)

    g = jax.nn.sigmoid(gt_ref[...]).T
    for h in range(NSA_HPG):
        sl = slice(h * QB, (h + 1) * QB)
        o_t = (g[3 * h:3 * h + 1, :] * o_c[:, sl] + g[3 * h + 1:3 * h + 2, :] * o_s[:, sl]
               + g[3 * h + 2:3 * h + 3, :] * o_w[:, sl])
        o_ref[:, h * 128:(h + 1) * 128] = o_t.T


def _nsa_attn(q, kcmp, vcmp_t, ks, kw, vt, gt, ovt, expand_t):
    b, s, _ = q.shape
    ncp = kcmp.shape[2]
    gw = NSA_HPG * NSA_D
    slab = pl.BlockSpec((None, s, NSA_D), lambda bi, gi, i: (bi, 0, gi))
    return pl.pallas_call(
        functools.partial(_nsa_body, n_buckets=s // SEL_BUCKET),
        out_shape=jax.ShapeDtypeStruct((b, s, NSA_HEADS * NSA_D), F32),
        grid=(b, NSA_G, s // QB),
        in_specs=[pl.BlockSpec((None, QB, gw), lambda bi, gi, i: (bi, i, gi)),
                  pl.BlockSpec((None, None, ncp, NSA_D), lambda bi, gi, i: (bi, gi, 0, 0)),
                  pl.BlockSpec((None, None, NSA_D, ncp), lambda bi, gi, i: (bi, gi, 0, 0)),
                  slab, pl.BlockSpec((NSA_D, s), lambda bi, gi, i: (gi, bi)),
                  slab, pl.BlockSpec((NSA_D, s), lambda bi, gi, i: (NSA_G + gi, bi)),
                  pl.BlockSpec((None, None, QB, 128), lambda bi, gi, i: (gi, bi, i, 0)),
                  _const_spec(ovt.shape), _const_spec(expand_t.shape)],
        out_specs=pl.BlockSpec((None, QB, gw), lambda bi, gi, i: (bi, i, gi)),
        scratch_shapes=[pltpu.VMEM((NSA_D, NSA_HPG * QB), F32)],
        compiler_params=_params(("parallel", "parallel", "arbitrary")), name="nsa_attn",
    )(q, kcmp, vcmp_t, ks, vt, kw, vt, gt, ovt, expand_t)


def _post_body(ya_ref, yb_ref, h_ref, gya_ref, gyb_ref, wo_ref, gx_ref, wxq_ref, mk_ref, mv_ref, wxo_ref,
               gf_ref, wrh_ref, wrl_ref, br_ref, h2_ref, xn_ref, eid_ref, gate_ref, cnt_ref, cnt_scr):
    half = wo_ref.shape[0] // 2
    h1 = (h_ref[...] + _mm(_rms(ya_ref[...], gya_ref[...]).astype(BF16), wo_ref[:half, :])
          + _mm(_rms(yb_ref[...], gyb_ref[...]).astype(BF16), wo_ref[half:, :]))

    q = (_mm(_rms(h1, gx_ref[...]).astype(BF16), wxq_ref[...]) * SCALE_X).astype(BF16)
    outs = []
    for hd in range(XH):
        sl = slice(hd * XD, (hd + 1) * XD)
        s = _mm_nt(q[:, sl], mk_ref[:, sl])
        p = jnp.exp2(s - jnp.max(s, axis=1, keepdims=True))
        outs.append(_mm(p.astype(BF16), mv_ref[:, sl]) / jnp.sum(p, axis=1, keepdims=True))
    h2 = h1 + _mm(jnp.concatenate(outs, axis=1).astype(BF16), wxo_ref[...])
    h2_ref[...] = h2

    xf = _rms(h2, gf_ref[...])
    _store_slabs(xn_ref, xf, SLAB_ROWS)
    x_hi = xf.astype(BF16)
    x_lo = (xf - x_hi.astype(F32)).astype(BF16)
    logits = _mm(x_hi, wrh_ref[...]) + _mm(x_lo, wrh_ref[...]) + _mm(x_hi, wrl_ref[...]) + br_ref[...]
    lane = lax.broadcasted_iota(jnp.int32, logits.shape, 1)
    lane_f = lane.astype(F32)

    def argmax_first(x):
        v = jnp.max(x, axis=1, keepdims=True)
        return v, jnp.min(jnp.where(x == v, lane_f, 1e9), axis=1, keepdims=True)

    is_grp = lane < N_GROUPS
    lg = jnp.where(is_grp, logits, NEG)
    mx, grp = argmax_first(lg)
    p_top = 1.0 / jnp.sum(jnp.where(is_grp, jnp.exp(lg - mx), 0.0), axis=1, keepdims=True)
    in_grp = (lane >= N_GROUPS) & (lane < N_GROUPS + N_EXPERTS) & (
        ((lane - N_GROUPS) // EXP_PER_GROUP).astype(F32) == grp)
    le = jnp.where(in_grp, logits, NEG)
    v1, i1 = argmax_first(le)
    v2, i2 = argmax_first(jnp.where(lane_f == i1, NEG, le))
    t = jnp.exp(v2 - v1)
    g1 = p_top / (1.0 + t)
    g2 = p_top * t / (1.0 + t)
    gate_ref[...] = jnp.where(lane == 0, g1, jnp.where(lane == 1, g2, 0.0))

    e1 = i1 - N_GROUPS
    e2 = i2 - N_GROUPS
    oh1 = jnp.where(lane_f == e1, 1.0, 0.0)
    oh2 = jnp.where(lane_f == e2, 1.0, 0.0)
    both = oh1 + oh2
    n_tok = both.shape[0]
    earlier = jnp.where(lax.broadcasted_iota(jnp.int32, (n_tok, n_tok), 1)
                        < lax.broadcasted_iota(jnp.int32, (n_tok, n_tok), 0), 1.0, 0.0).astype(BF16)

    @pl.when((pl.program_id(0) == 0) & (pl.program_id(1) == 0))
    def _():
        cnt_scr[...] = jnp.zeros_like(cnt_scr)

    base = cnt_scr[...] + _mm(earlier, both.astype(BF16))
    r1 = jnp.sum(oh1 * base, axis=1, keepdims=True)
    r2 = jnp.sum(oh2 * base, axis=1, keepdims=True)
    cnt_scr[...] += jnp.sum(both, axis=0, keepdims=True)
    cnt_ref[...] = cnt_scr[...].astype(jnp.int32)
    eid_ref[...] = jnp.where(lane == 0, e1, jnp.where(lane == 1, e2, jnp.where(lane == 2, r1, jnp.where(
        lane == 3, r2, 0.0)))).astype(jnp.int32)


def _post_mixer(ya, yb, h, gya, gyb, wo, gx, wxq, memkv, layer, wxo, gf, wrh, wrl, br, tm=512):
    b, s, _ = ya.shape
    row = lambda n: pl.BlockSpec((None, tm, n), lambda bi, i: (bi, i, 0))
    hw = XH * XD
    in_specs = [row(1024), row(1024), row(D_MODEL), _const_spec(gya.shape), _const_spec(gyb.shape),
                _const_spec(wo.shape), _const_spec(gx.shape), _const_spec(wxq.shape),
                pl.BlockSpec((None, MEM_LEN, hw), lambda bi, i: (bi, 0, 2 * layer)),
                pl.BlockSpec((None, MEM_LEN, hw), lambda bi, i: (bi, 0, 2 * layer + 1)),
                _const_spec(wxo.shape), _const_spec(gf.shape), _const_spec(wrh.shape), _const_spec(wrl.shape),
                _const_spec(br.shape)]
    out_shape = [jax.ShapeDtypeStruct((b, s, D_MODEL), F32), jax.ShapeDtypeStruct((b * s * SLAB_ROWS, 128), F32),
                 jax.ShapeDtypeStruct((b, s, 128), jnp.int32), jax.ShapeDtypeStruct((b, s, 128), F32),
                 jax.ShapeDtypeStruct((1, 128), jnp.int32)]
    slabs = pl.BlockSpec((tm * SLAB_ROWS, 128), lambda bi, i: (bi * (s // tm) + i, 0))
    return pl.pallas_call(
        _post_body, out_shape=out_shape, grid=(b, s // tm), in_specs=in_specs,
        out_specs=[row(D_MODEL), slabs, row(128), row(128), pl.BlockSpec((1, 128), lambda bi, i: (0, 0))],
        scratch_shapes=[pltpu.VMEM((1, 128), F32)],
        compiler_params=_params(("arbitrary", "arbitrary")), name="post_mixer",
    )(ya, yb, h, gya, gyb, wo, gx, wxq, memkv, memkv, wxo, gf, wrh, wrl, br)


def _memkv_body(m_ref, g_ref, w_ref, o_ref):
    o_ref[...] = _mm(_rms(m_ref[...], g_ref[...]).astype(BF16), w_ref[...]).astype(BF16)


def _memkv(mem, g, w, tm=256):
    t = mem.shape[0]
    n = w.shape[1]
    return pl.pallas_call(
        _memkv_body, out_shape=jax.ShapeDtypeStruct((t, n), BF16), grid=(t // tm,),
        in_specs=[pl.BlockSpec((tm, D_MODEL), lambda i: (i, 0)), _const_spec(g.shape), _const_spec(w.shape)],
        out_specs=pl.BlockSpec((tm, n), lambda i: (i, 0)),
        compiler_params=_params(("parallel",)), name="memkv",
    )(mem, g, w)


SCATTER_TOKENS = 256


def _scatter_body(dest_ref, lastblk_ref, nu_ref, xn_ref, xs_ref, zbuf, sem, zsem, *, n_steps, n_blocks):
    i = pl.program_id(0)
    blk_rows = MOE_BLOCK * SLAB_ROWS

    def zero_copy(blk):
        return pltpu.make_async_copy(zbuf, xs_ref.at[pl.ds(blk * blk_rows, blk_rows), :], zsem)

    def zero_fill(go):
        for e in range(N_EXPERTS):
            @pl.when(lastblk_ref[e] >= 0)
            def _(e=e):
                go(zero_copy(lastblk_ref[e]))
        for blk in range(n_blocks):
            @pl.when(blk >= nu_ref[0])
            def _(blk=blk):
                go(zero_copy(blk))

    @pl.when(i == 0)
    def _():
        zbuf[...] = jnp.zeros_like(zbuf)
        zero_fill(lambda c: c.start())
        zero_fill(lambda c: c.wait())

    def row_copy(j):
        d0 = pl.multiple_of(dest_ref[i * (SCATTER_TOKENS * TOP_K) + j] * SLAB_ROWS, SLAB_ROWS)
        return pltpu.make_async_copy(xn_ref.at[pl.ds((j // TOP_K) * SLAB_ROWS, SLAB_ROWS), :],
                                     xs_ref.at[pl.ds(d0, SLAB_ROWS), :], sem)

    for j in range(SCATTER_TOKENS * TOP_K):
        row_copy(j).start()
    for j in range(SCATTER_TOKENS * TOP_K):
        row_copy(j).wait()


def _scatter_rows(dest, last_block, n_used, xn, n_slots):
    t = xn.shape[0] // SLAB_ROWS
    n_steps = t // SCATTER_TOKENS
    return pl.pallas_call(
        functools.partial(_scatter_body, n_steps=n_steps, n_blocks=n_slots // MOE_BLOCK),
        out_shape=jax.ShapeDtypeStruct((n_slots * SLAB_ROWS, 128), F32),
        grid_spec=pltpu.PrefetchScalarGridSpec(
            num_scalar_prefetch=3, grid=(n_steps,),
            in_specs=[pl.BlockSpec((SCATTER_TOKENS * SLAB_ROWS, 128), lambda i, d, lb, nu: (i, 0))],
            out_specs=pl.BlockSpec(memory_space=pl.ANY),
            scratch_shapes=[pltpu.VMEM((MOE_BLOCK * SLAB_ROWS, 128), F32), pltpu.SemaphoreType.DMA(()),
                            pltpu.SemaphoreType.DMA(())]),
        compiler_params=_params(("arbitrary",)), name="scatter_rows",
    )(dest, last_block, n_used, xn)


def _experts_body(be_ref, bf_ref, ws_ref, nx_ref, nu_ref, x_ref, w1_ref, w3_ref, w2_ref, y_ref,
                  w1f, w3f, w2f, w1b, w3b, w2b, sem_w, *, layer):
    b = pl.program_id(0)
    n_used = nu_ref[0]

    def w_copies(e, ws):
        return [pltpu.make_async_copy(w_ref.at[layer, e], wf.at[ws], sem_w.at[ws])
                for w_ref, wf in ((w1_ref, w1f), (w3_ref, w3f), (w2_ref, w2f))]

    @pl.when(b == 0)
    def _():
        for c in w_copies(be_ref[0], 0):
            c.start()

    @pl.when(b >= n_used)
    def _():
        y_ref[...] = jnp.zeros_like(y_ref)

    @pl.when(b < n_used)
    def _():
        @pl.when(bf_ref[b] == 1)
        def _():
            ws = ws_ref[b]

            @pl.when(nx_ref[b] >= 0)
            def _():
                for c in w_copies(nx_ref[b], 1 - ws):
                    c.start()

            for c in w_copies(be_ref[b], ws):
                c.wait()
            w1b[...] = w1f[ws].astype(BF16)
            w3b[...] = w3f[ws].astype(BF16)
            w2b[...] = w2f[ws].astype(BF16)

        x = _load_slabs(x_ref, MOE_BLOCK, SLAB_ROWS).astype(BF16)
        hb = (jax.nn.silu(_mm(x, w1b[...])) * _mm(x, w3b[...])).astype(BF16)
        _store_slabs(y_ref, _mm(hb, w2b[...]), SLAB_ROWS)


def _experts(blk_expert, blk_first, blk_wslot, nxt_expert, n_used, xs, w_e1, w_e3, w_e2, layer):
    d = D_MODEL
    n_blocks = xs.shape[0] // (MOE_BLOCK * SLAB_ROWS)
    hbm = pl.BlockSpec(memory_space=pl.ANY)
    x_spec = pl.BlockSpec((MOE_BLOCK * SLAB_ROWS, 128), lambda b, be, bf, ws, nx, nu: (jnp.minimum(b, nu[0] - 1), 0))
    y_spec = pl.BlockSpec((MOE_BLOCK * SLAB_ROWS, 128), lambda b, be, bf, ws, nx, nu: (b, 0))
    return pl.pallas_call(
        functools.partial(_experts_body, layer=layer),
        out_shape=jax.ShapeDtypeStruct(xs.shape, F32),
        grid_spec=pltpu.PrefetchScalarGridSpec(
            num_scalar_prefetch=5, grid=(n_blocks,),
            in_specs=[x_spec, hbm, hbm, hbm], out_specs=y_spec,
            scratch_shapes=[pltpu.VMEM((2, d, D_EXPERT), F32), pltpu.VMEM((2, d, D_EXPERT), F32),
                            pltpu.VMEM((2, D_EXPERT, d), F32),
                            pltpu.VMEM((d, D_EXPERT), BF16), pltpu.VMEM((d, D_EXPERT), BF16),
                            pltpu.VMEM((D_EXPERT, d), BF16), pltpu.SemaphoreType.DMA((2,))]),
        compiler_params=_params(("arbitrary",)), name="experts",
    )(blk_expert, blk_first, blk_wslot, nxt_expert, n_used, xs, w_e1, w_e3, w_e2)


COMBINE_TOKENS = 128


def _combine_body(dest_ref, ys_ref, h_ref, gate_ref, gfin_ref, o_ref, abuf, bbuf, sem, *, n_steps, final_norm):
    i = pl.program_id(0)
    slot = i % 2

    def copies(step, sl, j):
        a = (step * COMBINE_TOKENS + j) * TOP_K
        out = []
        for k, buf in enumerate((abuf, bbuf)):
            s0 = pl.multiple_of(dest_ref[a + k] * SLAB_ROWS, SLAB_ROWS)
            out.append(pltpu.make_async_copy(ys_ref.at[pl.ds(s0, SLAB_ROWS), :],
                                             buf.at[sl, pl.ds(j * SLAB_PITCH_VMEM, SLAB_ROWS), :], sem.at[sl]))
        return out

    def start(step, sl):
        for j in range(COMBINE_TOKENS):
            for c in copies(step, sl, j):
                c.start()

    @pl.when(i == 0)
    def _():
        start(0, 0)

    @pl.when(i + 1 < n_steps)
    def _():
        start(i + 1, 1 - slot)

    for j in range(COMBINE_TOKENS):
        for c in copies(i, slot, j):
            c.wait()
    g = gate_ref[...]
    y = (h_ref[...] + g[:, 0:1] * _load_slabs(abuf.at[slot], COMBINE_TOKENS, SLAB_PITCH_VMEM)
         + g[:, 1:2] * _load_slabs(bbuf.at[slot], COMBINE_TOKENS, SLAB_PITCH_VMEM))
    if final_norm:
        y = _rms(y, gfin_ref[...])
    o_ref[...] = y


def _combine(dest, ys, h, gate, gfin, final_norm):
    t, d = h.shape
    tm = COMBINE_TOKENS
    n_steps = t // tm
    buf = pltpu.VMEM((2, tm * SLAB_PITCH_VMEM, 128), F32)
    return pl.pallas_call(
        functools.partial(_combine_body, n_steps=n_steps, final_norm=final_norm),
        out_shape=jax.ShapeDtypeStruct((t, d), F32),
        grid_spec=pltpu.PrefetchScalarGridSpec(
            num_scalar_prefetch=1, grid=(n_steps,),
            in_specs=[pl.BlockSpec(memory_space=pl.ANY), pl.BlockSpec((tm, d), lambda i, dr: (i, 0)),
                      pl.BlockSpec((tm, 128), lambda i, dr: (i, 0)), pl.BlockSpec((1, d), lambda i, dr: (0, 0))],
            out_specs=pl.BlockSpec((tm, d), lambda i, dr: (i, 0)),
            scratch_shapes=[buf, buf, pltpu.SemaphoreType.DMA((2,))]),
        compiler_params=_params(("arbitrary",)), name="combine",
    )(dest, ys, h, gate, gfin)


def _dispatch(eid, rank, counts):
    n_assign = eid.shape[0] * TOP_K
    n_blocks = (n_assign + N_EXPERTS * (MOE_BLOCK - 1) + MOE_BLOCK - 1) // MOE_BLOCK
    ids = jnp.arange(N_EXPERTS, dtype=jnp.int32)
    blocks = (counts + MOE_BLOCK - 1) // MOE_BLOCK
    blk_end = jnp.cumsum(blocks)
    blk_begin = blk_end - blocks
    onehot = (eid.reshape(-1)[:, None] == ids[None, :]).astype(F32)
    first_blk = jnp.dot(onehot, blk_begin.astype(F32)).astype(jnp.int32)
    dest = first_blk * MOE_BLOCK + rank.reshape(-1)
    blk_ids = jnp.arange(n_blocks, dtype=jnp.int32)
    blk_expert = jnp.minimum(jnp.sum((blk_end[None, :] <= blk_ids[:, None]).astype(jnp.int32), axis=1), N_EXPERTS - 1)
    blk_first = jnp.concatenate([jnp.ones((1,), jnp.int32), (blk_expert[1:] != blk_expert[:-1]).astype(jnp.int32)])
    n_used = blk_end[-1:].astype(jnp.int32)
    has_rows = counts > 0
    last_block = jnp.where(has_rows, blk_end - 1, -1).astype(jnp.int32)
    wslot = ((jnp.cumsum(has_rows.astype(jnp.int32)) - 1) % 2).astype(jnp.int32)
    later = jnp.where(has_rows[None, :] & (ids[None, :] > ids[:, None]), ids[None, :], N_EXPERTS)
    nxt = jnp.min(later, axis=1)
    nxt = jnp.where(nxt < N_EXPERTS, nxt, -1).astype(jnp.int32)
    return dest, last_block, (blk_expert, blk_first, wslot[blk_expert], nxt[blk_expert], n_used), n_blocks * MOE_BLOCK


def _rope_tables(positions):
    pos = positions.reshape(-1).astype(F32)[:, None]

    def tables(rot, passthrough):
        half = rot // 2
        ang = pos * (ROPE_THETA ** (-jnp.arange(0, rot, 2, dtype=F32) / rot))
        cos, sin = jnp.cos(ang), jnp.sin(ang)
        z = lambda n: jnp.zeros((pos.shape[0], n), F32)
        tail = jnp.full((pos.shape[0], 128 - rot), passthrough, F32)
        return [jnp.concatenate([cos, cos, tail], 1), jnp.concatenate([z(half), sin, z(128 - rot)], 1),
                jnp.concatenate([-sin, z(128 - half)], 1)]

    return jnp.concatenate(tables(MLA_ROPE, 0.0) + tables(NSA_ROT, 1.0), axis=1)


def _selection_constants(s):
    ncp = s // CMP_STRIDE
    ns = s // SEL_BLOCK
    c_start = np.arange(ncp)[None, :] * CMP_STRIDE
    sel_start = np.arange(ns)[:, None] * SEL_BLOCK
    ovt = (c_start <= sel_start + SEL_BLOCK - 1) & (c_start + CMP_LEN - 1 >= sel_start)
    ovt[:, ncp - 1] = False
    expand_t = np.arange(s)[:, None] // SEL_BLOCK == np.arange(ns)[None, :]
    return jnp.asarray(ovt, BF16), jnp.asarray(expand_t, BF16)


def _arrange_w_in(w):
    cuts = np.cumsum([0, 512, 256, 64, 1024, 256, 256, 256, 256, 256, 256, 24])
    c_q, c_kv, k_r, q_n, k_c, v_c, k_s, v_s, k_w, v_w, gates = [w[:, cuts[i]:cuts[i + 1]] for i in range(11)]
    z = lambda n: jnp.zeros((w.shape[0], n), w.dtype)
    w1 = jnp.concatenate([c_q, c_kv, k_r, z(64), q_n, k_c, k_s, k_w, v_c, gates, z(104)], axis=1).astype(BF16)
    return w1, jnp.concatenate([v_s, v_w], axis=1).T.astype(BF16)


def _arrange_w_uq(w):
    w = w.reshape(MLA_Q_RANK, MLA_HEADS, MLA_NOPE + MLA_ROPE)
    w = jnp.pad(w, ((0, 0), (0, 0), (0, MLA_QK_PAD - MLA_NOPE - MLA_ROPE)))
    return w.reshape(MLA_Q_RANK, MLA_HEADS * MLA_QK_PAD).astype(BF16)


def _split_w_ukv(w):
    w = w.reshape(MLA_KV_RANK, MLA_HEADS, MLA_NOPE + MLA_V)
    return (w[:, :, :MLA_NOPE].reshape(MLA_KV_RANK, -1).astype(BF16),
            w[:, :, MLA_NOPE:].reshape(MLA_KV_RANK, -1).T.astype(BF16))


def _router_weights(w_grp, b_grp, w_exp, b_exp):
    pad = 128 - N_GROUPS - N_EXPERTS
    w = jnp.concatenate([w_grp, w_exp, jnp.zeros((w_grp.shape[0], pad), F32)], axis=1)
    hi = w.astype(BF16)
    lo = (w - hi.astype(F32)).astype(BF16)
    b = jnp.concatenate([b_grp, b_exp, jnp.zeros((pad,), F32)])[None, :]
    return hi, lo, b


def kernel(x, mem, positions, g_mem, g_final, g_mix, w_in, g_cq, w_uq, g_ckv, w_ukv, pe_k, pe_v, w_ck1, w_ck2,
           w_cv1, w_cv2, g_ya, g_yb, w_out, g_x, w_xq, w_xkv, w_xo, g_ffn, w_grp, b_grp, w_exp, b_exp,
           w_e1, w_e3, w_e2):
    b, s, d = x.shape
    depth = w_in.shape[0]
    t = b * s
    ncp = s // CMP_STRIDE
    tab = _rope_tables(positions)
    ovt, expand_t = _selection_constants(s)
    r2 = lambda v: v.reshape(1, -1)

    w_kv_all = jnp.concatenate([w_xkv[l] for l in range(depth)], axis=1).astype(BF16)
    memkv = _memkv(mem.reshape(-1, d), r2(g_mem), w_kv_all).reshape(b, mem.shape[1], -1)

    h = x.reshape(t, d)
    for l in range(depth):
        wkn, wv_t = _split_w_ukv(w_ukv[l])
        w1, wvt = _arrange_w_in(w_in[l])
        qa, ka, va_t, qn, kc, vc, ks, kw, vt, gt = _in_proj(
            h, r2(g_mix[l]), w1, wvt, r2(g_cq[l]), r2(g_ckv[l]), _arrange_w_uq(w_uq[l]), wkn, wv_t, tab)
        kcmp, vcmp_t = _compress(
            kc.reshape(NSA_G, b, ncp, CMP_STRIDE * NSA_D), vc.reshape(NSA_G, b, ncp, CMP_STRIDE * NSA_D),
            pe_k[l].reshape(1, -1), pe_v[l].reshape(1, -1), w_ck1[l].astype(BF16), w_ck2[l].astype(BF16),
            w_cv1[l].astype(BF16), w_cv2[l].astype(BF16))
        r3 = lambda a: a.reshape(b, s, -1)
        ya = _mla_attn(r3(qa), r3(ka), va_t)
        yb = _nsa_attn(r3(qn), kcmp, vcmp_t, r3(ks), r3(kw), vt, gt.reshape(NSA_G, b, s, 128), ovt, expand_t)
        wrh, wrl, br = _router_weights(w_grp[l], b_grp[l], w_exp[l], b_exp[l])
        h2, xn, eid, gate, counts = _post_mixer(
            ya, yb, r3(h), r2(g_ya[l]), r2(g_yb[l]), w_out[l].astype(BF16), r2(g_x[l]), w_xq[l].astype(BF16),
            memkv, l, w_xo[l].astype(BF16), r2(g_ffn[l]), wrh, wrl, br)
        eid = eid.reshape(t, 128)
        dest, last_block, blocks, n_slots = _dispatch(
            eid[:, :TOP_K], eid[:, TOP_K:2 * TOP_K], counts[0, :N_EXPERTS])
        xs = _scatter_rows(dest, last_block, blocks[-1], xn, n_slots)
        ys = _experts(*blocks, xs, w_e1, w_e3, w_e2, l)
        h = _combine(dest, ys, h2.reshape(t, d), gate.reshape(t, 128), r2(g_final), final_norm=(l == depth - 1))
    return h.reshape(b, s, d)
```

```python
import functools

import jax
import jax.numpy as jnp
import numpy as np
from jax import lax
from jax.experimental import pallas as pl
from jax.experimental.pallas import tpu as pltpu

F32 = jnp.float32
BF16 = jnp.bfloat16

D_MODEL = 2048
RMS_EPS = 1e-6
ROPE_THETA = 500000.0
NEG = -1e30
FORCE_SCORE = 1e4

MLA_HEADS = 8
MLA_NOPE = 128
MLA_ROPE = 64
MLA_V = 128
MLA_Q_RANK = D_MODEL // 4
MLA_KV_RANK = D_MODEL // 8
MLA_QK_PAD = 256

NSA_HEADS = 8
NSA_G = 2
NSA_HPG = 4
NSA_D = 128
NSA_ROT = 32
CMP_LEN = 32
CMP_STRIDE = 16
CMP_HID = 512
SEL_BLOCK = 64
SEL_TOPN = 8
WINDOW = 512
QB = 128

XH = 4
XD = 128
MEM_LEN = 256

N_GROUPS = 8
EXP_PER_GROUP = 8
N_EXPERTS = 64
TOP_K = 2
D_EXPERT = D_MODEL // 4
MOE_BLOCK = 128

VMEM_LIMIT_BYTES = 56 * 1024 * 1024

C_CQ = 0
C_CKV = 512
C_KR = 768
C_QN = 896
C_KK = 1920
C_VC = 2688
C_GT = 2944
N_IN = 3072

LOG2E = 1.4426950408889634
SCALE_A = (MLA_NOPE + MLA_ROPE) ** -0.5 * LOG2E
SCALE_N = NSA_D ** -0.5 * LOG2E
SCALE_X = XD ** -0.5 * LOG2E


def _params(sem):
    return pltpu.CompilerParams(dimension_semantics=sem, vmem_limit_bytes=VMEM_LIMIT_BYTES)


def _mm(a, b):
    return jnp.dot(a, b, preferred_element_type=F32)


def _mm_nt(a, b):
    return lax.dot_general(a, b, (((1,), (1,)), ((), ())), preferred_element_type=F32)


def _rms(x, g):
    return x * lax.rsqrt(jnp.mean(x * x, axis=-1, keepdims=True) + RMS_EPS) * g


def _rope(x, c, sa, sb, half):
    return x * c + pltpu.roll(x, half, 1) * sa + pltpu.roll(x, 128 - half, 1) * sb


SLAB_ROWS = D_MODEL // 128
SLAB_PITCH_VMEM = 24


def _load_slabs(ref, n_tok, pitch):
    return jnp.concatenate([ref[pl.ds(c, n_tok, stride=pitch), :] for c in range(SLAB_ROWS)], axis=1)


def _store_slabs(ref, x, pitch):
    for c in range(SLAB_ROWS):
        ref[pl.ds(c, x.shape[0], stride=pitch), :] = x[:, c * 128:(c + 1) * 128]


def _const_spec(shape):
    nd = len(shape)
    return pl.BlockSpec(shape, lambda *_: (0,) * nd, pipeline_mode=pl.Buffered(1))


def _in_proj_body(h_ref, gmix_ref, w1_ref, wvt_ref, gcq_ref, gckv_ref, wq_ref, wkn_ref, wv_ref, tab_ref,
                  qa_ref, ka_ref, va_ref, qn_ref, kc_ref, vc_ref, ks_ref, kw_ref, vt_ref, gt_ref):
    xn = _rms(h_ref[...], gmix_ref[...]).astype(BF16)
    tab = tab_ref[...]
    ca, saa, sba, cn, san, sbn = [tab[:, i * 128:(i + 1) * 128] for i in range(6)]

    xq = _rms(_mm(xn, w1_ref[:, C_CQ:C_CKV]), gcq_ref[...]).astype(BF16)
    ckr = _mm(xn, w1_ref[:, C_CKV:C_QN])
    xkv = _rms(ckr[:, :MLA_KV_RANK], gckv_ref[...]).astype(BF16)
    kr = _rope(ckr[:, MLA_KV_RANK:], ca, saa, sba, MLA_ROPE // 2).astype(BF16)
    q = _mm(xq, wq_ref[...])
    kn = _mm(xkv, wkn_ref[...])
    for h in range(MLA_HEADS):
        lo = h * MLA_QK_PAD
        qa_ref[:, lo:lo + 128] = (q[:, lo:lo + 128] * SCALE_A).astype(BF16)
        qa_ref[:, lo + 128:lo + 256] = (
            _rope(q[:, lo + 128:lo + 256], ca, saa, sba, MLA_ROPE // 2) * SCALE_A).astype(BF16)
        ka_ref[:, lo:lo + 128] = kn[:, h * 128:(h + 1) * 128].astype(BF16)
        ka_ref[:, lo + 128:lo + 256] = kr
    va_ref[...] = _mm_nt(wv_ref[...], xkv).astype(BF16)

    qn = _mm(xn, w1_ref[:, C_QN:C_KK])
    for h in range(NSA_HEADS):
        sl = slice(h * 128, (h + 1) * 128)
        qn_ref[:, sl] = (_rope(qn[:, sl], cn, san, sbn, NSA_ROT // 2) * SCALE_N).astype(BF16)
    kk = _mm(xn, w1_ref[:, C_KK:C_VC])
    vc = _mm(xn, w1_ref[:, C_VC:C_GT])
    for g in range(NSA_G):
        sl = slice(g * 128, (g + 1) * 128)
        kc_ref[g] = _rope(kk[:, sl], cn, san, sbn, NSA_ROT // 2)
        ks_ref[:, sl] = _rope(kk[:, 256 + g * 128:256 + (g + 1) * 128], cn, san, sbn, NSA_ROT // 2).astype(BF16)
        kw_ref[:, sl] = _rope(kk[:, 512 + g * 128:512 + (g + 1) * 128], cn, san, sbn, NSA_ROT // 2).astype(BF16)
        vc_ref[g] = vc[:, sl]
    vt_ref[...] = _mm_nt(wvt_ref[...], xn).astype(BF16)
    gl = _mm(xn, w1_ref[:, C_GT:N_IN])
    gt_ref[0] = gl
    gt_ref[1] = pltpu.roll(gl, 128 - NSA_HPG * 3, 1)


def _in_proj(h, gmix, w1, wvt, gcq, gckv, wq, wkn, wv, tab, tm=512):
    t = h.shape[0]
    row = lambda n: pl.BlockSpec((tm, n), lambda i: (i, 0))
    col = lambda n: pl.BlockSpec((n, tm), lambda i: (0, i))
    grp = pl.BlockSpec((NSA_G, tm, 128), lambda i: (0, i, 0))
    out_shape = [
        jax.ShapeDtypeStruct((t, MLA_HEADS * MLA_QK_PAD), BF16),
        jax.ShapeDtypeStruct((t, MLA_HEADS * MLA_QK_PAD), BF16),
        jax.ShapeDtypeStruct((MLA_HEADS * MLA_V, t), BF16),
        jax.ShapeDtypeStruct((t, NSA_HEADS * NSA_D), BF16),
        jax.ShapeDtypeStruct((NSA_G, t, NSA_D), F32),
        jax.ShapeDtypeStruct((NSA_G, t, NSA_D), F32),
        jax.ShapeDtypeStruct((t, NSA_G * NSA_D), BF16),
        jax.ShapeDtypeStruct((t, NSA_G * NSA_D), BF16),
        jax.ShapeDtypeStruct((2 * NSA_G * NSA_D, t), BF16),
        jax.ShapeDtypeStruct((NSA_G, t, 128), F32),
    ]
    out_specs = [row(2048), row(2048), col(1024), row(1024), grp, grp, row(256), row(256), col(512), grp]
    in_specs = [row(D_MODEL), _const_spec((1, D_MODEL)), _const_spec((D_MODEL, N_IN)),
                _const_spec((2 * NSA_G * NSA_D, D_MODEL)),
                _const_spec((1, MLA_Q_RANK)), _const_spec((1, MLA_KV_RANK)),
                _const_spec((MLA_Q_RANK, MLA_HEADS * MLA_QK_PAD)), _const_spec((MLA_KV_RANK, MLA_HEADS * 128)),
                _const_spec((MLA_HEADS * 128, MLA_KV_RANK)), row(6 * 128)]
    return pl.pallas_call(
        _in_proj_body, out_shape=out_shape, grid=(t // tm,), in_specs=in_specs, out_specs=out_specs,
        compiler_params=_params(("parallel",)), name="in_proj",
    )(h, gmix, w1, wvt, gcq, gckv, wq, wkn, wv, tab)


def _compress_body(k_ref, v_ref, pek_ref, pev_ref, wk1_ref, wk2_ref, wv1_ref, wv2_ref, ko_ref, vo_ref):
    half = CMP_STRIDE * NSA_D

    def phi(x, pe, w1_ref, w2_ref):
        a = _mm((x + pe[:, :half]).astype(BF16), w1_ref[:half, :])
        b = _mm((x + pe[:, half:]).astype(BF16), w1_ref[half:, :])
        pre = a + pltpu.roll(b, b.shape[0] - 1, 0)
        return _mm(jax.nn.gelu(pre).astype(BF16), w2_ref[...])

    ko_ref[...] = phi(k_ref[...], pek_ref[...], wk1_ref, wk2_ref).astype(BF16)
    vo_ref[...] = phi(v_ref[...], pev_ref[...], wv1_ref, wv2_ref).T.astype(BF16)


def _compress(kc, vc, pek, pev, wk1, wk2, wv1, wv2):
    g, b, ncp, w = kc.shape
    in_blk = pl.BlockSpec((None, None, ncp, w), lambda bi, gi: (gi, bi, 0, 0))
    return pl.pallas_call(
        _compress_body,
        out_shape=[jax.ShapeDtypeStruct((b, g, ncp, NSA_D), BF16), jax.ShapeDtypeStruct((b, g, NSA_D, ncp), BF16)],
        grid=(b, g),
        in_specs=[in_blk, in_blk, _const_spec(pek.shape), _const_spec(pev.shape), _const_spec(wk1.shape),
                  _const_spec(wk2.shape), _const_spec(wv1.shape), _const_spec(wv2.shape)],
        out_specs=[pl.BlockSpec((None, None, ncp, NSA_D), lambda bi, gi: (bi, gi, 0, 0)),
                   pl.BlockSpec((None, None, NSA_D, ncp), lambda bi, gi: (bi, gi, 0, 0))],
        compiler_params=_params(("parallel", "parallel")), name="compress",
    )(kc, vc, pek, pev, wk1, wk2, wv1, wv2)


def _mla_body(q_ref, k_ref, vt_ref, o_ref, *, tq, n_q):
    i = pl.program_id(2)
    q = q_ref[...]
    tri = lax.broadcasted_iota(jnp.int32, (tq, tq), 0) <= lax.broadcasted_iota(jnp.int32, (tq, tq), 1)
    for ci in range(n_q):
        @pl.when(i == ci)
        def _(ci=ci):
            lo = ci * tq
            sd = jnp.where(tri, _mm_nt(k_ref[lo:lo + tq, :], q), NEG)
            m = jnp.max(sd, axis=0, keepdims=True)
            if ci > 0:
                sf = _mm_nt(k_ref[0:lo, :], q)
                m = jnp.maximum(m, jnp.max(sf, axis=0, keepdims=True))
                pf = jnp.exp2(sf - m)
            pd = jnp.exp2(sd - m)
            l = jnp.sum(pd, axis=0, keepdims=True)
            acc = _mm(vt_ref[:, lo:lo + tq], pd.astype(BF16))
            if ci > 0:
                l = l + jnp.sum(pf, axis=0, keepdims=True)
                acc = acc + _mm(vt_ref[:, 0:lo], pf.astype(BF16))
            o_ref[...] = (acc / l).T


def _mla_attn(q, k, vt, tq=512):
    b, s, _ = q.shape
    return pl.pallas_call(
        functools.partial(_mla_body, tq=tq, n_q=s // tq),
        out_shape=jax.ShapeDtypeStruct((b, s, MLA_HEADS * MLA_V), F32),
        grid=(b, MLA_HEADS, s // tq),
        in_specs=[pl.BlockSpec((None, tq, MLA_QK_PAD), lambda bi, hi, i: (bi, i, hi)),
                  pl.BlockSpec((None, s, MLA_QK_PAD), lambda bi, hi, i: (bi, 0, hi)),
                  pl.BlockSpec((MLA_V, s), lambda bi, hi, i: (hi, bi))],
        out_specs=pl.BlockSpec((None, tq, MLA_V), lambda bi, hi, i: (bi, i, hi)),
        compiler_params=_params(("parallel", "parallel", "arbitrary")), name="mla_attn",
    )(q, k, vt)


SEL_BUCKET = 512
WIN_LEN = WINDOW + QB


def _nsa_body(q_ref, kc_ref, vct_ref, ks_ref, vst_ref, kw_ref, vwt_ref, gt_ref, ovt_ref, expt_ref, o_ref, os_ref, *,
              n_buckets):
    i = pl.program_id(2)
    ns = ovt_ref.shape[0]
    ncp = ovt_ref.shape[1]
    cols = NSA_HPG * QB
    q = q_ref[...]
    qst = jnp.concatenate([q[:, h * 128:(h + 1) * 128] for h in range(NSA_HPG)], axis=0)

    def tile4(x):
        return jnp.concatenate([x] * NSA_HPG, axis=1)

    pos_c = i * QB + (lax.broadcasted_iota(jnp.int32, (ncp, cols), 1) & (QB - 1))
    cend = lax.broadcasted_iota(jnp.int32, (ncp, cols), 0) * CMP_STRIDE + (CMP_LEN - 1)
    vis = cend <= pos_c
    s = jnp.where(vis, _mm_nt(kc_ref[...], qst), NEG)
    e = jnp.where(vis, jnp.exp2(s - jnp.max(s, axis=0, keepdims=True)), 0.0)
    l = jnp.sum(e, axis=0, keepdims=True)
    p_c = e / jnp.where(l > 0.0, l, 1.0)
    o_c = _mm(vct_ref[...], p_c.astype(BF16))

    psum = p_c[:, 0:QB] + p_c[:, QB:2 * QB] + p_c[:, 2 * QB:3 * QB] + p_c[:, 3 * QB:4 * QB]
    p_hi = psum.astype(BF16)
    p_lo = (psum - p_hi.astype(F32)).astype(BF16)
    imp = _mm(ovt_ref[...], p_hi) + _mm(ovt_ref[...], p_lo)
    n_i = lax.broadcasted_iota(jnp.int32, (ns, QB), 0)
    pos_l = i * QB + lax.broadcasted_iota(jnp.int32, (ns, QB), 1)
    forced = (n_i == 0) | (n_i == pos_l // SEL_BLOCK)
    val = jnp.where(forced, FORCE_SCORE, jnp.where(n_i * SEL_BLOCK <= pos_l, imp, -FORCE_SCORE))
    rank = jnp.zeros((ns, QB), F32)
    for m_ in range(ns):
        vm = val[m_:m_ + 1, :]
        rank = rank + jnp.where((vm > val) | ((vm == val) & (n_i > m_)), 1.0, 0.0)
    sel_t = jnp.where(rank < float(min(SEL_TOPN, ns)), 1.0, 0.0).astype(BF16)

    for cb in range(n_buckets):
        @pl.when(i // (SEL_BUCKET // QB) == cb)
        def _(cb=cb):
            n_keys = (cb + 1) * SEL_BUCKET
            member = _mm(expt_ref[0:n_keys, :], sel_t)
            qcol = i * QB + lax.broadcasted_iota(jnp.int32, (n_keys, QB), 1)
            ok = (member > 0.5) & (lax.broadcasted_iota(jnp.int32, (n_keys, QB), 0) <= qcol)
            sc = _mm_nt(ks_ref[0:n_keys, :], qst) + tile4(jnp.where(ok, 0.0, NEG))
            p = jnp.exp2(sc - jnp.max(sc, axis=0, keepdims=True))
            os_ref[...] = _mm(vst_ref[:, 0:n_keys], p.astype(BF16)) / jnp.sum(p, axis=0, keepdims=True)

    o_s = os_ref[...]

    start = pl.multiple_of(jnp.maximum(i * QB - WINDOW, 0), QB)
    dist = (i * QB + lax.broadcasted_iota(jnp.int32, (WIN_LEN, QB), 1)) - (
        start + lax.broadcasted_iota(jnp.int32, (WIN_LEN, QB), 0))
    ok = (dist >= 0) & (dist < WINDOW)
    s = _mm_nt(kw_ref[pl.ds(start, WIN_LEN), :], qst) + tile4(jnp.where(ok, 0.0, NEG))
    p = jnp.exp2(s - jnp.max(s, axis=0, keepdims=True))
    o_w =_mm(vwt_ref[:, pl.ds(start, WIN_LEN)], p.astype(BF16)) / jnp.sum(p, axis=0, keepdims=True)

    g = jax.nn.sigmoid(gt_ref[...]).T
    for h in range(NSA_HPG):
        sl = slice(h * QB, (h + 1) * QB)
        o_t = (g[3 * h:3 * h + 1, :] * o_c[:, sl] + g[3 * h + 1:3 * h + 2, :] * o_s[:, sl]
               + g[3 * h + 2:3 * h + 3, :] * o_w[:, sl])
        o_ref[:, h * 128:(h + 1) * 128] = o_t.T


def _nsa_attn(q, kcmp, vcmp_t, ks, kw, vt, gt, ovt, expand_t):
    b, s, _ = q.shape
    ncp = kcmp.shape[2]
    gw = NSA_HPG * NSA_D
    slab = pl.BlockSpec((None, s, NSA_D), lambda bi, gi, i: (bi, 0, gi))
    return pl.pallas_call(
        functools.partial(_nsa_body, n_buckets=s // SEL_BUCKET),
        out_shape=jax.ShapeDtypeStruct((b, s, NSA_HEADS * NSA_D), F32),
        grid=(b, NSA_G, s // QB),
        in_specs=[pl.BlockSpec((None, QB, gw), lambda bi, gi, i: (bi, i, gi)),
                  pl.BlockSpec((None, None, ncp, NSA_D), lambda bi, gi, i: (bi, gi, 0, 0)),
                  pl.BlockSpec((None, None, NSA_D, ncp), lambda bi, gi, i: (bi, gi, 0, 0)),
                  slab, pl.BlockSpec((NSA_D, s), lambda bi, gi, i: (gi, bi)),
                  slab, pl.BlockSpec((NSA_D, s), lambda bi, gi, i: (NSA_G + gi, bi)),
                  pl.BlockSpec((None, None, QB, 128), lambda bi, gi, i: (gi, bi, i, 0)),
                  _const_spec(ovt.shape), _const_spec(expand_t.shape)],
        out_specs=pl.BlockSpec((None, QB, gw), lambda bi, gi, i: (bi, i, gi)),
        scratch_shapes=[pltpu.VMEM((NSA_D, NSA_HPG * QB), F32)],
        compiler_params=_params(("parallel", "parallel", "arbitrary")), name="nsa_attn",
    )(q, kcmp, vcmp_t, ks, vt, kw, vt, gt, ovt, expand_t)


def _post_body(ya_ref, yb_ref, h_ref, gya_ref, gyb_ref, wo_ref, gx_ref, wxq_ref, mk_ref, mv_ref, wxo_ref,
               gf_ref, wrh_ref, wrl_ref, br_ref, h2_ref, xn_ref, eid_ref, gate_ref, cnt_ref, cnt_scr):
    half = wo_ref.shape[0] // 2
    h1 = (h_ref[...] + _mm(_rms(ya_ref[...], gya_ref[...]).astype(BF16), wo_ref[:half, :])
          + _mm(_rms(yb_ref[...], gyb_ref[...]).astype(BF16), wo_ref[half:, :]))

    q = (_mm(_rms(h1, gx_ref[...]).astype(BF16), wxq_ref[...]) * SCALE_X).astype(BF16)
    outs = []
    for hd in range(XH):
        sl = slice(hd * XD, (hd + 1) * XD)
        s = _mm_nt(q[:, sl], mk_ref[:, sl])
        p = jnp.exp2(s - jnp.max(s, axis=1, keepdims=True))
        outs.append(_mm(p.astype(BF16), mv_ref[:, sl]) / jnp.sum(p, axis=1, keepdims=True))
    h2 = h1 + _mm(jnp.concatenate(outs, axis=1).astype(BF16), wxo_ref[...])
    h2_ref[...] = h2

    xf = _rms(h2, gf_ref[...])
    _store_slabs(xn_ref, xf, SLAB_ROWS)
    x_hi = xf.astype(BF16)
    x_lo = (xf - x_hi.astype(F32)).astype(BF16)
    logits = _mm(x_hi, wrh_ref[...]) + _mm(x_lo, wrh_ref[...]) + _mm(x_hi, wrl_ref[...]) + br_ref[...]
    lane = lax.broadcasted_iota(jnp.int32, logits.shape, 1)
    lane_f = lane.astype(F32)

    def argmax_first(x):
        v = jnp.max(x, axis=1, keepdims=True)
        return v, jnp.min(jnp.where(x == v, lane_f, 1e9), axis=1, keepdims=True)

    is_grp = lane < N_GROUPS
    lg = jnp.where(is_grp, logits, NEG)
    mx, grp = argmax_first(lg)
    p_top = 1.0 / jnp.sum(jnp.where(is_grp, jnp.exp(lg - mx), 0.0), axis=1, keepdims=True)
    in_grp = (lane >= N_GROUPS) & (lane < N_GROUPS + N_EXPERTS) & (
        ((lane - N_GROUPS) // EXP_PER_GROUP).astype(F32) == grp)
    le = jnp.where(in_grp, logits, NEG)
    v1, i1 = argmax_first(le)
    v2, i2 = argmax_first(jnp.where(lane_f == i1, NEG, le))
    t = jnp.exp(v2 - v1)
    g1 = p_top / (1.0 + t)
    g2 = p_top * t / (1.0 + t)
    gate_ref[...] = jnp.where(lane == 0, g1, jnp.where(lane == 1, g2, 0.0))

    e1 = i1 - N_GROUPS
    e2 = i2 - N_GROUPS
    oh1 = jnp.where(lane_f == e1, 1.0, 0.0)
    oh2 = jnp.where(lane_f == e2, 1.0, 0.0)
    both = oh1 + oh2
    n_tok = both.shape[0]
    earlier = jnp.where(lax.broadcasted_iota(jnp.int32, (n_tok, n_tok), 1)
                        < lax.broadcasted_iota(jnp.int32, (n_tok, n_tok), 0), 1.0, 0.0).astype(BF16)

    @pl.when((pl.program_id(0) == 0) & (pl.program_id(1) == 0))
    def _():
        cnt_scr[...] = jnp.zeros_like(cnt_scr)

    base = cnt_scr[...] + _mm(earlier, both.astype(BF16))
    r1 = jnp.sum(oh1 * base, axis=1, keepdims=True)
    r2 = jnp.sum(oh2 * base, axis=1, keepdims=True)
    cnt_scr[...] += jnp.sum(both, axis=0, keepdims=True)
    cnt_ref[...] = cnt_scr[...].astype(jnp.int32)
    eid_ref[...] = jnp.where(lane == 0, e1, jnp.where(lane == 1, e2, jnp.where(lane == 2, r1, jnp.where(
        lane == 3, r2, 0.0)))).astype(jnp.int32)


def _post_mixer(ya, yb, h, gya, gyb, wo, gx, wxq, memkv, layer, wxo, gf, wrh, wrl, br, tm=512):
    b, s, _ = ya.shape
    row = lambda n: pl.BlockSpec((None, tm, n), lambda bi, i: (bi, i, 0))
    hw = XH * XD
    in_specs = [row(1024), row(1024), row(D_MODEL), _const_spec(gya.shape), _const_spec(gyb.shape),
                _const_spec(wo.shape), _const_spec(gx.shape), _const_spec(wxq.shape),
                pl.BlockSpec((None, MEM_LEN, hw), lambda bi, i: (bi, 0, 2 * layer)),
                pl.BlockSpec((None, MEM_LEN, hw), lambda bi, i: (bi, 0, 2 * layer + 1)),
                _const_spec(wxo.shape), _const_spec(gf.shape), _const_spec(wrh.shape), _const_spec(wrl.shape),
                _const_spec(br.shape)]
    out_shape = [jax.ShapeDtypeStruct((b, s, D_MODEL), F32), jax.ShapeDtypeStruct((b * s * SLAB_ROWS, 128), F32),
                 jax.ShapeDtypeStruct((b, s, 128), jnp.int32), jax.ShapeDtypeStruct((b, s, 128), F32),
                 jax.ShapeDtypeStruct((1, 128), jnp.int32)]
    slabs = pl.BlockSpec((tm * SLAB_ROWS, 128), lambda bi, i: (bi * (s // tm) + i, 0))
    return pl.pallas_call(
        _post_body, out_shape=out_shape, grid=(b, s // tm), in_specs=in_specs,
        out_specs=[row(D_MODEL), slabs, row(128), row(128), pl.BlockSpec((1, 128), lambda bi, i: (0, 0))],
        scratch_shapes=[pltpu.VMEM((1, 128), F32)],
        compiler_params=_params(("arbitrary", "arbitrary")), name="post_mixer",
    )(ya, yb, h, gya, gyb, wo, gx, wxq, memkv, memkv, wxo, gf, wrh, wrl, br)


def _memkv_body(m_ref, g_ref, w_ref, o_ref):
    o_ref[...] = _mm(_rms(m_ref[...], g_ref[...]).astype(BF16), w_ref[...]).astype(BF16)


def _memkv(mem, g, w, tm=256):
    t = mem.shape[0]
    n = w.shape[1]
    return pl.pallas_call(
        _memkv_body, out_shape=jax.ShapeDtypeStruct((t, n), BF16), grid=(t // tm,),
        in_specs=[pl.BlockSpec((tm, D_MODEL), lambda i: (i, 0)), _const_spec(g.shape), _const_spec(w.shape)],
        out_specs=pl.BlockSpec((tm, n), lambda i: (i, 0)),
        compiler_params=_params(("parallel",)), name="memkv",
    )(mem, g, w)


SCATTER_TOKENS = 512


def _scatter_body(dest_ref, lastblk_ref, nu_ref, xn_ref, xs_ref, zbuf, sem, zsem, *, n_steps, n_blocks):
    i = pl.program_id(0)
    blk_rows = MOE_BLOCK * SLAB_ROWS

    def zero_copy(blk):
        return pltpu.make_async_copy(zbuf, xs_ref.at[pl.ds(blk * blk_rows, blk_rows), :], zsem)

    def zero_fill(go):
        for e in range(N_EXPERTS):
            @pl.when(lastblk_ref[e] >= 0)
            def _(e=e):
                go(zero_copy(lastblk_ref[e]))
        for blk in range(n_blocks):
            @pl.when(blk >= nu_ref[0])
            def _(blk=blk):
                go(zero_copy(blk))

    @pl.when(i == 0)
    def _():
        zbuf[...] = jnp.zeros_like(zbuf)
        zero_fill(lambda c: c.start())
        zero_fill(lambda c: c.wait())

    def row_copy(j):
        d0 = pl.multiple_of(dest_ref[i * (SCATTER_TOKENS * TOP_K) + j] * SLAB_ROWS, SLAB_ROWS)
        return pltpu.make_async_copy(xn_ref.at[pl.ds((j // TOP_K) * SLAB_ROWS, SLAB_ROWS), :],
                                     xs_ref.at[pl.ds(d0, SLAB_ROWS), :], sem)

    for j in range(SCATTER_TOKENS * TOP_K):
        row_copy(j).start(priority=j % TOP_K)
    for j in range(SCATTER_TOKENS * TOP_K):
        row_copy(j).wait()


def _scatter_rows(dest, last_block, n_used, xn, n_slots):
    t = xn.shape[0] // SLAB_ROWS
    n_steps = t // SCATTER_TOKENS
    return pl.pallas_call(
        functools.partial(_scatter_body, n_steps=n_steps, n_blocks=n_slots // MOE_BLOCK),
        out_shape=jax.ShapeDtypeStruct((n_slots * SLAB_ROWS, 128), F32),
        grid_spec=pltpu.PrefetchScalarGridSpec(
            num_scalar_prefetch=3, grid=(n_steps,),
            in_specs=[pl.BlockSpec((SCATTER_TOKENS * SLAB_ROWS, 128), lambda i, d, lb, nu: (i, 0))],
            out_specs=pl.BlockSpec(memory_space=pl.ANY),
            scratch_shapes=[pltpu.VMEM((MOE_BLOCK * SLAB_ROWS, 128), F32), pltpu.SemaphoreType.DMA(()),
                            pltpu.SemaphoreType.DMA(())]),
        compiler_params=_params(("arbitrary",)), name="scatter_rows",
    )(dest, last_block, n_used, xn)


def _experts_body(be_ref, bf_ref, ws_ref, nx_ref, nu_ref, x_ref, w1_ref, w3_ref, w2_ref, y_ref,
                  w1f, w3f, w2f, w1b, w3b, w2b, sem_w, *, layer):
    b = pl.program_id(0)
    n_used = nu_ref[0]

    def w_copies(e, ws):
        return [pltpu.make_async_copy(w_ref.at[layer, e], wf.at[ws], sem_w.at[ws])
                for w_ref, wf in ((w1_ref, w1f), (w3_ref, w3f), (w2_ref, w2f))]

    @pl.when(b == 0)
    def _():
        for c in w_copies(be_ref[0], 0):
            c.start()

    @pl.when(b >= n_used)
    def _():
        y_ref[...] = jnp.zeros_like(y_ref)

    @pl.when(b < n_used)
    def _():
        @pl.when(bf_ref[b] == 1)
        def _():
            ws = ws_ref[b]

            @pl.when(nx_ref[b] >= 0)
            def _():
                for c in w_copies(nx_ref[b], 1 - ws):
                    c.start()

            for c in w_copies(be_ref[b], ws):
                c.wait()
            w1b[...] = w1f[ws].astype(BF16)
            w3b[...] = w3f[ws].astype(BF16)
            w2b[...] = w2f[ws].astype(BF16)

        x = _load_slabs(x_ref, MOE_BLOCK, SLAB_ROWS).astype(BF16)
        hb = (jax.nn.silu(_mm(x, w1b[...])) * _mm(x, w3b[...])).astype(BF16)
        _store_slabs(y_ref, _mm(hb, w2b[...]), SLAB_ROWS)


def _experts(blk_expert, blk_first, blk_wslot, nxt_expert, n_used, xs, w_e1, w_e3, w_e2, layer):
    d = D_MODEL
    n_blocks = xs.shape[0] // (MOE_BLOCK * SLAB_ROWS)
    hbm = pl.BlockSpec(memory_space=pl.ANY)
    x_spec = pl.BlockSpec((MOE_BLOCK * SLAB_ROWS, 128), lambda b, be, bf, ws, nx, nu: (jnp.minimum(b, nu[0] - 1), 0))
    y_spec = pl.BlockSpec((MOE_BLOCK * SLAB_ROWS, 128), lambda b, be, bf, ws, nx, nu: (b, 0))
    return pl.pallas_call(
        functools.partial(_experts_body, layer=layer),
        out_shape=jax.ShapeDtypeStruct(xs.shape, F32),
        grid_spec=pltpu.PrefetchScalarGridSpec(
            num_scalar_prefetch=5, grid=(n_blocks,),
            in_specs=[x_spec, hbm, hbm, hbm], out_specs=y_spec,
            scratch_shapes=[pltpu.VMEM((2, d, D_EXPERT), F32), pltpu.VMEM((2, d, D_EXPERT), F32),
                            pltpu.VMEM((2, D_EXPERT, d), F32),
                            pltpu.VMEM((d, D_EXPERT), BF16), pltpu.VMEM((d, D_EXPERT), BF16),
                            pltpu.VMEM((D_EXPERT, d), BF16), pltpu.SemaphoreType.DMA((2,))]),
        compiler_params=_params(("arbitrary",)), name="experts",
    )(blk_expert, blk_first, blk_wslot, nxt_expert, n_used, xs, w_e1, w_e3, w_e2)


COMBINE_TOKENS = 128


def _combine_body(dest_ref, ys_ref, h_ref, gate_ref, gfin_ref, o_ref, abuf, bbuf, sem, *, n_steps, final_norm):
    i = pl.program_id(0)
    slot = i % 2

    def copies(step, sl, j):
        a = (step * COMBINE_TOKENS + j) * TOP_K
        out = []
        for k, buf in enumerate((abuf, bbuf)):
            s0 = pl.multiple_of(dest_ref[a + k] * SLAB_ROWS, SLAB_ROWS)
            out.append(pltpu.make_async_copy(ys_ref.at[pl.ds(s0, SLAB_ROWS), :],
                                             buf.at[sl, pl.ds(j * SLAB_PITCH_VMEM, SLAB_ROWS), :], sem.at[sl]))
        return out

    def start(step, sl):
        for j in range(COMBINE_TOKENS):
            for k, c in enumerate(copies(step, sl, j)):
                c.start(priority=k)

    @pl.when(i == 0)
    def _():
        start(0, 0)

    @pl.when(i + 1 < n_steps)
    def _():
        start(i + 1, 1 - slot)

    for j in range(COMBINE_TOKENS):
        for c in copies(i, slot, j):
            c.wait()
    g = gate_ref[...]
    y = (h_ref[...] + g[:, 0:1] * _load_slabs(abuf.at[slot], COMBINE_TOKENS, SLAB_PITCH_VMEM)
         + g[:, 1:2] * _load_slabs(bbuf.at[slot], COMBINE_TOKENS, SLAB_PITCH_VMEM))
    if final_norm:
        y = _rms(y, gfin_ref[...])
    o_ref[...] = y


def _combine(dest, ys, h, gate, gfin, final_norm):
    t, d = h.shape
    tm = COMBINE_TOKENS
    n_steps = t // tm
    buf = pltpu.VMEM((2, tm * SLAB_PITCH_VMEM, 128), F32)
    return pl.pallas_call(
        functools.partial(_combine_body, n_steps=n_steps, final_norm=final_norm),
        out_shape=jax.ShapeDtypeStruct((t, d), F32),
        grid_spec=pltpu.PrefetchScalarGridSpec(
            num_scalar_prefetch=1, grid=(n_steps,),
            in_specs=[pl.BlockSpec(memory_space=pl.ANY), pl.BlockSpec((tm, d), lambda i, dr: (i, 0)),
                      pl.BlockSpec((tm, 128), lambda i, dr: (i, 0)), pl.BlockSpec((1, d), lambda i, dr: (0, 0))],
            out_specs=pl.BlockSpec((tm, d), lambda i, dr: (i, 0)),
            scratch_shapes=[buf, buf, pltpu.SemaphoreType.DMA((2,))]),
        compiler_params=_params(("arbitrary",)), name="combine",
    )(dest, ys, h, gate, gfin)


def _dispatch(eid, rank, counts):
    n_assign = eid.shape[0] * TOP_K
    n_blocks = (n_assign + N_EXPERTS * (MOE_BLOCK - 1) + MOE_BLOCK - 1) // MOE_BLOCK
    ids = jnp.arange(N_EXPERTS, dtype=jnp.int32)
    blocks = (counts + MOE_BLOCK - 1) // MOE_BLOCK
    blk_end = jnp.cumsum(blocks)
    blk_begin = blk_end - blocks
    onehot = (eid.reshape(-1)[:, None] == ids[None, :]).astype(F32)
    first_blk = jnp.dot(onehot, blk_begin.astype(F32)).astype(jnp.int32)
    dest = first_blk * MOE_BLOCK + rank.reshape(-1)
    blk_ids = jnp.arange(n_blocks, dtype=jnp.int32)
    blk_expert = jnp.minimum(jnp.sum((blk_end[None, :] <= blk_ids[:, None]).astype(jnp.int32), axis=1), N_EXPERTS - 1)
    blk_first = jnp.concatenate([jnp.ones((1,), jnp.int32), (blk_expert[1:] != blk_expert[:-1]).astype(jnp.int32)])
    n_used = blk_end[-1:].astype(jnp.int32)
    has_rows = counts > 0
    last_block = jnp.where(has_rows, blk_end - 1, -1).astype(jnp.int32)
    wslot = ((jnp.cumsum(has_rows.astype(jnp.int32)) - 1) % 2).astype(jnp.int32)
    later = jnp.where(has_rows[None, :] & (ids[None, :] > ids[:, None]), ids[None, :], N_EXPERTS)
    nxt = jnp.min(later, axis=1)
    nxt = jnp.where(nxt < N_EXPERTS, nxt, -1).astype(jnp.int32)
    return dest, last_block, (blk_expert, blk_first, wslot[blk_expert], nxt[blk_expert], n_used), n_blocks * MOE_BLOCK


def _rope_tables(positions):
    pos = positions.reshape(-1).astype(F32)[:, None]
    base, place, row0 = [], np.zeros((MLA_ROPE + NSA_ROT + 1, 6 * 128), np.float32), 0
    for tbl, (rot, passthrough) in enumerate(((MLA_ROPE, False), (NSA_ROT, True))):
        half = rot // 2
        ang = pos * (ROPE_THETA ** (-jnp.arange(0, rot, 2, dtype=F32) / rot))
        base += [jnp.cos(ang), jnp.sin(ang)]
        cos_rows, sin_rows, col0 = row0 + np.arange(half), row0 + half + np.arange(half), tbl * 3 * 128
        place[cos_rows, col0 + np.arange(half)] = 1.0
        place[cos_rows, col0 + half + np.arange(half)] = 1.0
        if passthrough:
            place[-1, col0 + rot:col0 + 128] = 1.0
        place[sin_rows, col0 + 128 + half + np.arange(half)] = 1.0
        place[sin_rows, col0 + 256 + np.arange(half)] = -1.0
        row0 += rot
    base = jnp.concatenate(base + [jnp.ones_like(pos)], axis=1)
    return jnp.dot(base, jnp.asarray(place), precision=lax.Precision.HIGHEST)


def _selection_constants(s):
    ncp = s // CMP_STRIDE
    ns = s // SEL_BLOCK
    c_start = np.arange(ncp)[None, :] * CMP_STRIDE
    sel_start = np.arange(ns)[:, None] * SEL_BLOCK
    ovt = (c_start <= sel_start + SEL_BLOCK - 1) & (c_start + CMP_LEN - 1 >= sel_start)
    ovt[:, ncp - 1] = False
    expand_t = np.arange(s)[:, None] // SEL_BLOCK == np.arange(ns)[None, :]
    return jnp.asarray(ovt, BF16), jnp.asarray(expand_t, BF16)


def _arrange_w_in(w):
    cuts = np.cumsum([0, 512, 256, 64, 1024, 256, 256, 256, 256, 256, 256, 24])
    c_q, c_kv, k_r, q_n, k_c, v_c, k_s, v_s, k_w, v_w, gates = [w[:, cuts[i]:cuts[i + 1]] for i in range(11)]
    z = lambda n: jnp.zeros((w.shape[0], n), w.dtype)
    w1 = jnp.concatenate([c_q, c_kv, k_r, z(64), q_n, k_c, k_s, k_w, v_c, gates, z(104)], axis=1).astype(BF16)
    return w1, jnp.concatenate([v_s, v_w], axis=1).T.astype(BF16)


def _arrange_w_uq(w):
    w = w.reshape(MLA_Q_RANK, MLA_HEADS, MLA_NOPE + MLA_ROPE)
    w = jnp.pad(w, ((0, 0), (0, 0), (0, MLA_QK_PAD - MLA_NOPE - MLA_ROPE)))
    return w.reshape(MLA_Q_RANK, MLA_HEADS * MLA_QK_PAD).astype(BF16)


def _split_w_ukv(w):
    w = w.reshape(MLA_KV_RANK, MLA_HEADS, MLA_NOPE + MLA_V)
    return (w[:, :, :MLA_NOPE].reshape(MLA_KV_RANK, -1).astype(BF16),
            w[:, :, MLA_NOPE:].reshape(MLA_KV_RANK, -1).T.astype(BF16))


def _router_weights(w_grp, b_grp, w_exp, b_exp):
    pad = 128 - N_GROUPS - N_EXPERTS
    w = jnp.concatenate([w_grp, w_exp, jnp.zeros((w_grp.shape[0], pad), F32)], axis=1)
    hi = w.astype(BF16)
    lo = (w - hi.astype(F32)).astype(BF16)
    b = jnp.concatenate([b_grp, b_exp, jnp.zeros((pad,), F32)])[None, :]
    return hi, lo, b


def kernel(x, mem, positions, g_mem, g_final, g_mix, w_in, g_cq, w_uq, g_ckv, w_ukv, pe_k, pe_v, w_ck1, w_ck2,
           w_cv1, w_cv2, g_ya, g_yb, w_out, g_x, w_xq, w_xkv, w_xo, g_ffn, w_grp, b_grp, w_exp, b_exp,
           w_e1, w_e3, w_e2):
    b, s, d = x.shape
    depth = w_in.shape[0]
    t = b * s
    ncp = s // CMP_STRIDE
    tab = _rope_tables(positions)
    ovt, expand_t = _selection_constants(s)
    r2 = lambda v: v.reshape(1, -1)

    w_kv_all = jnp.concatenate([w_xkv[l] for l in range(depth)], axis=1).astype(BF16)
    memkv = _memkv(mem.reshape(-1, d), r2(g_mem), w_kv_all).reshape(b, mem.shape[1], -1)

    h = x.reshape(t, d)
    for l in range(depth):
        wkn, wv_t = _split_w_ukv(w_ukv[l])
        w1, wvt = _arrange_w_in(w_in[l])
        qa, ka, va_t, qn, kc, vc, ks, kw, vt, gt = _in_proj(
            h, r2(g_mix[l]), w1, wvt, r2(g_cq[l]), r2(g_ckv[l]), _arrange_w_uq(w_uq[l]), wkn, wv_t, tab)
        kcmp, vcmp_t = _compress(
            kc.reshape(NSA_G, b, ncp, CMP_STRIDE * NSA_D), vc.reshape(NSA_G, b, ncp, CMP_STRIDE * NSA_D),
            pe_k[l].reshape(1, -1), pe_v[l].reshape(1, -1), w_ck1[l].astype(BF16), w_ck2[l].astype(BF16),
            w_cv1[l].astype(BF16), w_cv2[l].astype(BF16))
        r3 = lambda a: a.reshape(b, s, -1)
        ya = _mla_attn(r3(qa), r3(ka), va_t)
        yb = _nsa_attn(r3(qn), kcmp, vcmp_t, r3(ks), r3(kw), vt, gt.reshape(NSA_G, b, s, 128), ovt, expand_t)
        wrh, wrl, br = _router_weights(w_grp[l], b_grp[l], w_exp[l], b_exp[l])
        h2, xn, eid, gate, counts = _post_mixer(
            ya, yb, r3(h), r2(g_ya[l]), r2(g_yb[l]), w_out[l].astype(BF16), r2(g_x[l]), w_xq[l].astype(BF16),
            memkv, l, w_xo[l].astype(BF16), r2(g_ffn[l]), wrh, wrl, br)
        eid = eid.reshape(t, 128)
        dest, last_block, blocks, n_slots = _dispatch(
            eid[:, :TOP_K], eid[:, TOP_K:2 * TOP_K], counts[0, :N_EXPERTS])
        xs = _scatter_rows(dest, last_block, blocks[-1], xn, n_slots)
        ys = _experts(*blocks, xs, w_e1, w_e3, w_e2, l)
        h = _combine(dest, ys, h2.reshape(t, d), gate.reshape(t, 128), r2(g_final), final_norm=(l == depth - 1))
    return h.reshape(b, s, d)
```

```python
import functools

import jax
import jax.numpy as jnp
import numpy as np
from jax import lax
from jax.experimental import pallas as pl
from jax.experimental.pallas import tpu as pltpu

F32 = jnp.float32
BF16 = jnp.bfloat16

D_MODEL = 2048
RMS_EPS = 1e-6
ROPE_THETA = 500000.0
NEG = -1e30
FORCE_SCORE = 1e4

MLA_HEADS = 8
MLA_NOPE = 128
MLA_ROPE = 64
MLA_V = 128
MLA_Q_RANK = D_MODEL // 4
MLA_KV_RANK = D_MODEL // 8
MLA_QK_PAD = 256

NSA_HEADS = 8
NSA_G = 2
NSA_HPG = 4
NSA_D = 128
NSA_ROT = 32
CMP_LEN = 32
CMP_STRIDE = 16
CMP_HID = 512
SEL_BLOCK = 64
SEL_TOPN = 8
WINDOW = 512
QB = 128

XH = 4
XD = 128
MEM_LEN = 256

N_GROUPS = 8
EXP_PER_GROUP = 8
N_EXPERTS = 64
TOP_K = 2
D_EXPERT = D_MODEL // 4
MOE_BLOCK = 128

VMEM_LIMIT_BYTES = 56 * 1024 * 1024

C_CQ = 0
C_CKV = 512
C_KR = 768
C_QN = 896
C_KK = 1920
C_VC = 2688
C_GT = 2944
N_IN = 3072

LOG2E = 1.4426950408889634
SCALE_A = (MLA_NOPE + MLA_ROPE) ** -0.5 * LOG2E
SCALE_N = NSA_D ** -0.5 * LOG2E
SCALE_X = XD ** -0.5 * LOG2E


def _params(sem):
    return pltpu.CompilerParams(dimension_semantics=sem, vmem_limit_bytes=VMEM_LIMIT_BYTES)


def _mm(a, b):
    return jnp.dot(a, b, preferred_element_type=F32)


def _mm_nt(a, b):
    return lax.dot_general(a, b, (((1,), (1,)), ((), ())), preferred_element_type=F32)


def _rms(x, g):
    return x * lax.rsqrt(jnp.mean(x * x, axis=-1, keepdims=True) + RMS_EPS) * g


def _rope(x, c, sa, sb, half):
    return x * c + pltpu.roll(x, half, 1) * sa + pltpu.roll(x, 128 - half, 1) * sb


SLAB_ROWS = D_MODEL // 128
SLAB_PITCH_VMEM = 24


def _load_slabs(ref, n_tok, pitch):
    return jnp.concatenate([ref[pl.ds(c, n_tok, stride=pitch), :] for c in range(SLAB_ROWS)], axis=1)


def _store_slabs(ref, x, pitch):
    for c in range(SLAB_ROWS):
        ref[pl.ds(c, x.shape[0], stride=pitch), :] = x[:, c * 128:(c + 1) * 128]


PACK_ROWS = SLAB_ROWS // 2
HI_HALF = 0xFFFF0000


def _store_packed(ref, x):
    bits = pltpu.bitcast(x.astype(BF16).astype(F32), jnp.uint32)
    for c in range(PACK_ROWS):
        lo = bits[:, c * 128:(c + 1) * 128] >> 16
        hi = bits[:, (c + PACK_ROWS) * 128:(c + PACK_ROWS + 1) * 128] & jnp.uint32(HI_HALF)
        ref[pl.ds(c, x.shape[0], stride=PACK_ROWS), :] = lo | hi


def _load_packed(ref, n_tok):
    words = [ref[pl.ds(c, n_tok, stride=PACK_ROWS), :] for c in range(PACK_ROWS)]
    chunks = ([pltpu.bitcast(w << 16, F32) for w in words]
              + [pltpu.bitcast(w & jnp.uint32(HI_HALF), F32) for w in words])
    return jnp.concatenate(chunks, axis=1).astype(BF16)


def _const_spec(shape):
    nd = len(shape)
    return pl.BlockSpec(shape, lambda *_: (0,) * nd, pipeline_mode=pl.Buffered(1))


def _in_proj_body(h_ref, gmix_ref, w1_ref, wvt_ref, gcq_ref, gckv_ref, wq_ref, wkn_ref, wv_ref, tab_ref,
                  qa_ref, ka_ref, va_ref, qn_ref, kc_ref, vc_ref, ks_ref, kw_ref, vt_ref, gt_ref):
    xn = _rms(h_ref[...], gmix_ref[...]).astype(BF16)
    tab = tab_ref[...]
    ca, saa, sba, cn, san, sbn = [tab[:, i * 128:(i + 1) * 128] for i in range(6)]

    xq = _rms(_mm(xn, w1_ref[:, C_CQ:C_CKV]), gcq_ref[...]).astype(BF16)
    ckr = _mm(xn, w1_ref[:, C_CKV:C_QN])
    xkv = _rms(ckr[:, :MLA_KV_RANK], gckv_ref[...]).astype(BF16)
    kr = _rope(ckr[:, MLA_KV_RANK:], ca, saa, sba, MLA_ROPE // 2).astype(BF16)
    q = _mm(xq, wq_ref[...])
    kn = _mm(xkv, wkn_ref[...])
    for h in range(MLA_HEADS):
        lo = h * MLA_QK_PAD
        qa_ref[:, lo:lo + 128] = (q[:, lo:lo + 128] * SCALE_A).astype(BF16)
        qa_ref[:, lo + 128:lo + 256] = (
            _rope(q[:, lo + 128:lo + 256], ca, saa, sba, MLA_ROPE // 2) * SCALE_A).astype(BF16)
        ka_ref[:, lo:lo + 128] = kn[:, h * 128:(h + 1) * 128].astype(BF16)
        ka_ref[:, lo + 128:lo + 256] = kr
    va_ref[...] = _mm_nt(wv_ref[...], xkv).astype(BF16)

    qn = _mm(xn, w1_ref[:, C_QN:C_KK])
    for h in range(NSA_HEADS):
        sl = slice(h * 128, (h + 1) * 128)
        qn_ref[:, sl] = (_rope(qn[:, sl], cn, san, sbn, NSA_ROT // 2) * SCALE_N).astype(BF16)
    kk = _mm(xn, w1_ref[:, C_KK:C_VC])
    vc = _mm(xn, w1_ref[:, C_VC:C_GT])
    for g in range(NSA_G):
        sl = slice(g * 128, (g + 1) * 128)
        kc_ref[g] = _rope(kk[:, sl], cn, san, sbn, NSA_ROT // 2)
        ks_ref[:, sl] = _rope(kk[:, 256 + g * 128:256 + (g + 1) * 128], cn, san, sbn, NSA_ROT // 2).astype(BF16)
        kw_ref[:, sl] = _rope(kk[:, 512 + g * 128:512 + (g + 1) * 128], cn, san, sbn, NSA_ROT // 2).astype(BF16)
        vc_ref[g] = vc[:, sl]
    vt_ref[...] = _mm_nt(wvt_ref[...], xn).astype(BF16)
    gl = _mm(xn, w1_ref[:, C_GT:N_IN])
    gt_ref[0] = gl
    gt_ref[1] = pltpu.roll(gl, 128 - NSA_HPG * 3, 1)


def _in_proj(h, gmix, w1, wvt, gcq, gckv, wq, wkn, wv, tab, tm=512):
    t = h.shape[0]
    row = lambda n: pl.BlockSpec((tm, n), lambda i: (i, 0))
    col = lambda n: pl.BlockSpec((n, tm), lambda i: (0, i))
    grp = pl.BlockSpec((NSA_G, tm, 128), lambda i: (0, i, 0))
    out_shape = [
        jax.ShapeDtypeStruct((t, MLA_HEADS * MLA_QK_PAD), BF16),
        jax.ShapeDtypeStruct((t, MLA_HEADS * MLA_QK_PAD), BF16),
        jax.ShapeDtypeStruct((MLA_HEADS * MLA_V, t), BF16),
        jax.ShapeDtypeStruct((t, NSA_HEADS * NSA_D), BF16),
        jax.ShapeDtypeStruct((NSA_G, t, NSA_D), F32),
        jax.ShapeDtypeStruct((NSA_G, t, NSA_D), F32),
        jax.ShapeDtypeStruct((t, NSA_G * NSA_D), BF16),
        jax.ShapeDtypeStruct((t, NSA_G * NSA_D), BF16),
        jax.ShapeDtypeStruct((2 * NSA_G * NSA_D, t), BF16),
        jax.ShapeDtypeStruct((NSA_G, t, 128), F32),
    ]
    out_specs = [row(2048), row(2048), col(1024), row(1024), grp, grp, row(256), row(256), col(512), grp]
    in_specs = [row(D_MODEL), _const_spec((1, D_MODEL)), _const_spec((D_MODEL, N_IN)),
                _const_spec((2 * NSA_G * NSA_D, D_MODEL)),
                _const_spec((1, MLA_Q_RANK)), _const_spec((1, MLA_KV_RANK)),
                _const_spec((MLA_Q_RANK, MLA_HEADS * MLA_QK_PAD)), _const_spec((MLA_KV_RANK, MLA_HEADS * 128)),
                _const_spec((MLA_HEADS * 128, MLA_KV_RANK)), row(6 * 128)]
    return pl.pallas_call(
        _in_proj_body, out_shape=out_shape, grid=(t // tm,), in_specs=in_specs, out_specs=out_specs,
        compiler_params=_params(("parallel",)), name="in_proj",
    )(h, gmix, w1, wvt, gcq, gckv, wq, wkn, wv, tab)


def _compress_body(k_ref, v_ref, pek_ref, pev_ref, wk1_ref, wk2_ref, wv1_ref, wv2_ref, ko_ref, vo_ref):
    half = CMP_STRIDE * NSA_D

    def phi(x, pe, w1_ref, w2_ref):
        a = _mm((x + pe[:, :half]).astype(BF16), w1_ref[:half, :])
        b = _mm((x + pe[:, half:]).astype(BF16), w1_ref[half:, :])
        pre = a + pltpu.roll(b, b.shape[0] - 1, 0)
        return _mm(jax.nn.gelu(pre).astype(BF16), w2_ref[...])

    ko_ref[...] = phi(k_ref[...], pek_ref[...], wk1_ref, wk2_ref).astype(BF16)
    vo_ref[...] = phi(v_ref[...], pev_ref[...], wv1_ref, wv2_ref).T.astype(BF16)


def _compress(kc, vc, pek, pev, wk1, wk2, wv1, wv2):
    g, b, ncp, w = kc.shape
    in_blk = pl.BlockSpec((None, None, ncp, w), lambda bi, gi: (gi, bi, 0, 0))
    return pl.pallas_call(
        _compress_body,
        out_shape=[jax.ShapeDtypeStruct((b, g, ncp, NSA_D), BF16), jax.ShapeDtypeStruct((b, g, NSA_D, ncp), BF16)],
        grid=(b, g),
        in_specs=[in_blk, in_blk, _const_spec(pek.shape), _const_spec(pev.shape), _const_spec(wk1.shape),
                  _const_spec(wk2.shape), _const_spec(wv1.shape), _const_spec(wv2.shape)],
        out_specs=[pl.BlockSpec((None, None, ncp, NSA_D), lambda bi, gi: (bi, gi, 0, 0)),
                   pl.BlockSpec((None, None, NSA_D, ncp), lambda bi, gi: (bi, gi, 0, 0))],
        compiler_params=_params(("parallel", "parallel")), name="compress",
    )(kc, vc, pek, pev, wk1, wk2, wv1, wv2)


def _mla_body(q_ref, k_ref, vt_ref, o_ref, *, tq, n_q):
    i = pl.program_id(2)
    q = q_ref[...]
    tri = lax.broadcasted_iota(jnp.int32, (tq, tq), 0) <= lax.broadcasted_iota(jnp.int32, (tq, tq), 1)
    for ci in range(n_q):
        @pl.when(i == ci)
        def _(ci=ci):
            lo = ci * tq
            sd = jnp.where(tri, _mm_nt(k_ref[lo:lo + tq, :], q), NEG)
            m = jnp.max(sd, axis=0, keepdims=True)
            if ci > 0:
                sf = _mm_nt(k_ref[0:lo, :], q)
                m = jnp.maximum(m, jnp.max(sf, axis=0, keepdims=True))
                pf = jnp.exp2(sf - m)
            pd = jnp.exp2(sd - m)
            l = jnp.sum(pd, axis=0, keepdims=True)
            acc = _mm(vt_ref[:, lo:lo + tq], pd.astype(BF16))
            if ci > 0:
                l = l + jnp.sum(pf, axis=0, keepdims=True)
                acc = acc + _mm(vt_ref[:, 0:lo], pf.astype(BF16))
            o_ref[...] = (acc / l).T


def _mla_attn(q, k, vt, tq=512):
    b, s, _ = q.shape
    return pl.pallas_call(
        functools.partial(_mla_body, tq=tq, n_q=s // tq),
        out_shape=jax.ShapeDtypeStruct((b, s, MLA_HEADS * MLA_V), F32),
        grid=(b, MLA_HEADS, s // tq),
        in_specs=[pl.BlockSpec((None, tq, MLA_QK_PAD), lambda bi, hi, i: (bi, i, hi)),
                  pl.BlockSpec((None, s, MLA_QK_PAD), lambda bi, hi, i: (bi, 0, hi)),
                  pl.BlockSpec((MLA_V, s), lambda bi, hi, i: (hi, bi))],
        out_specs=pl.BlockSpec((None, tq, MLA_V), lambda bi, hi, i: (bi, i, hi)),
        compiler_params=_params(("parallel", "parallel", "arbitrary")), name="mla_attn",
    )(q, k, vt)


SEL_BUCKET = 512
WIN_LEN = WINDOW + QB


def _nsa_body(q_ref, kc_ref, vct_ref, ks_ref, vst_ref, kw_ref, vwt_ref, gt_ref, ovt_ref, expt_ref, o_ref, os_ref, *,
              n_buckets):
    i = pl.program_id(2)
    ns = ovt_ref.shape[0]
    ncp = ovt_ref.shape[1]
    cols = NSA_HPG * QB
    q = q_ref[...]
    qst = jnp.concatenate([q[:, h * 128:(h + 1) * 128] for h in range(NSA_HPG)], axis=0)

    def tile4(x):
        return jnp.concatenate([x] * NSA_HPG, axis=1)

    pos_c = i * QB + (lax.broadcasted_iota(jnp.int32, (ncp, cols), 1) & (QB - 1))
    cend = lax.broadcasted_iota(jnp.int32, (ncp, cols), 0) * CMP_STRIDE + (CMP_LEN - 1)
    vis = cend <= pos_c
    s = jnp.where(vis, _mm_nt(kc_ref[...], qst), NEG)
    e = jnp.where(vis, jnp.exp2(s - jnp.max(s, axis=0, keepdims=True)), 0.0)
    l = jnp.sum(e, axis=0, keepdims=True)
    p_c = e / jnp.where(l > 0.0, l, 1.0)
    o_c = _mm(vct_ref[...], p_c.astype(BF16))

    psum = p_c[:, 0:QB] + p_c[:, QB:2 * QB] + p_c[:, 2 * QB:3 * QB] + p_c[:, 3 * QB:4 * QB]
    p_hi = psum.astype(BF16)
    p_lo = (psum - p_hi.astype(F32)).astype(BF16)
    imp = _mm(ovt_ref[...], p_hi) + _mm(ovt_ref[...], p_lo)
    n_i = lax.broadcasted_iota(jnp.int32, (ns, QB), 0)
    pos_l = i * QB + lax.broadcasted_iota(jnp.int32, (ns, QB), 1)
    forced = (n_i == 0) | (n_i == pos_l // SEL_BLOCK)
    val = jnp.where(forced, FORCE_SCORE, jnp.where(n_i * SEL_BLOCK <= pos_l, imp, -FORCE_SCORE))
    rank = jnp.zeros((ns, QB), F32)
    for m_ in range(ns):
        vm = val[m_:m_ + 1, :]
        rank = rank + jnp.where((vm > val) | ((vm == val) & (n_i > m_)), 1.0, 0.0)
    sel_bias = jnp.where(rank < float(min(SEL_TOPN, ns)), 0.0, NEG).astype(BF16)

    for cb in range(n_buckets):
        @pl.when(i // (SEL_BUCKET // QB) == cb)
        def _(cb=cb):
            n_keys = (cb + 1) * SEL_BUCKET
            bias = _mm(expt_ref[0:n_keys, :], sel_bias)
            qcol = i * QB + lax.broadcasted_iota(jnp.int32, (n_keys, QB), 1)
            bias = jnp.where(lax.broadcasted_iota(jnp.int32, (n_keys, QB), 0) <= qcol, bias, NEG)
            sc = _mm_nt(ks_ref[0:n_keys, :], qst) + tile4(bias)
            p = jnp.exp2(sc - jnp.max(sc, axis=0, keepdims=True))
            os_ref[...] = _mm(vst_ref[:, 0:n_keys], p.astype(BF16)) / jnp.sum(p, axis=0, keepdims=True)

    o_s = os_ref[...]

    start = pl.multiple_of(jnp.maximum(i * QB - WINDOW, 0), QB)
    dist = (i * QB + lax.broadcasted_iota(jnp.int32, (WIN_LEN, QB), 1)) - (
        start + lax.broadcasted_iota(jnp.int32, (WIN_LEN, QB), 0))
    ok = (dist >= 0) & (dist < WINDOW)
    s = _mm_nt(kw_ref[pl.ds(start, WIN_LEN), :], qst) + tile4(jnp.where(ok, 0.0, NEG))
    p = jnp.exp2(s - jnp.max(s, axis=0, keepdims=True))
    o_w =_mm(vwt_ref[:, pl.ds(start, WIN_LEN)], p.astype(BF16)) / jnp.sum(p, axis=0, keepdims=True)

    g = jax.nn.sigmoid(gt_ref[...]).T
    for h in range(NSA_HPG):
        sl = slice(h * QB, (h + 1) * QB)
        o_t = (g[3 * h:3 * h + 1, :] * o_c[:, sl] + g[3 * h + 1:3 * h + 2, :] * o_s[:, sl]
               + g[3 * h + 2:3 * h + 3, :] * o_w[:, sl])
        o_ref[:, h * 128:(h + 1) * 128] = o_t.T


def _nsa_attn(q, kcmp, vcmp_t, ks, kw, vt, gt, ovt, expand_t):
    b, s, _ = q.shape
    ncp = kcmp.shape[2]
    gw = NSA_HPG * NSA_D
    slab = pl.BlockSpec((None, s, NSA_D), lambda bi, gi, i: (bi, 0, gi))
    return pl.pallas_call(
        functools.partial(_nsa_body, n_buckets=s // SEL_BUCKET),
        out_shape=jax.ShapeDtypeStruct((b, s, NSA_HEADS * NSA_D), F32),
        grid=(b, NSA_G, s // QB),
        in_specs=[pl.BlockSpec((None, QB, gw), lambda bi, gi, i: (bi, i, gi)),
                  pl.BlockSpec((None, None, ncp, NSA_D), lambda bi, gi, i: (bi, gi, 0, 0)),
                  pl.BlockSpec((None, None, NSA_D, ncp), lambda bi, gi, i: (bi, gi, 0, 0)),
                  slab, pl.BlockSpec((NSA_D, s), lambda bi, gi, i: (gi, bi)),
                  slab, pl.BlockSpec((NSA_D, s), lambda bi, gi, i: (NSA_G + gi, bi)),
                  pl.BlockSpec((None, None, QB, 128), lambda bi, gi, i: (gi, bi, i, 0)),
                  _const_spec(ovt.shape), _const_spec(expand_t.shape)],
        out_specs=pl.BlockSpec((None, QB, gw), lambda bi, gi, i: (bi, i, gi)),
        scratch_shapes=[pltpu.VMEM((NSA_D, NSA_HPG * QB), F32)],
        compiler_params=_params(("parallel", "parallel", "arbitrary")), name="nsa_attn",
    )(q, kcmp, vcmp_t, ks, vt, kw, vt, gt, ovt, expand_t)


def _post_body(ya_ref, yb_ref, h_ref, gya_ref, gyb_ref, wo_ref, gx_ref, wxq_ref, mk_ref, mv_ref, wxo_ref,
               gf_ref, wrh_ref, wrl_ref, br_ref, h2_ref, xn_ref, eid_ref, gate_ref, cnt_ref, cnt_scr):
    half = wo_ref.shape[0] // 2
    h1 = (h_ref[...] + _mm(_rms(ya_ref[...], gya_ref[...]).astype(BF16), wo_ref[:half, :])
          + _mm(_rms(yb_ref[...], gyb_ref[...]).astype(BF16), wo_ref[half:, :]))

    q = (_mm(_rms(h1, gx_ref[...]).astype(BF16), wxq_ref[...]) * SCALE_X).astype(BF16)
    outs = []
    for hd in range(XH):
        sl = slice(hd * XD, (hd + 1) * XD)
        s = _mm_nt(q[:, sl], mk_ref[:, sl])
        p = jnp.exp2(s - jnp.max(s, axis=1, keepdims=True))
        outs.append(_mm(p.astype(BF16), mv_ref[:, sl]) / jnp.sum(p, axis=1, keepdims=True))
    h2 = h1 + _mm(jnp.concatenate(outs, axis=1).astype(BF16), wxo_ref[...])
    h2_ref[...] = h2

    xf = _rms(h2, gf_ref[...])
    _store_packed(xn_ref, xf)
    x_hi = xf.astype(BF16)
    x_lo = (xf - x_hi.astype(F32)).astype(BF16)
    logits = _mm(x_hi, wrh_ref[...]) + _mm(x_lo, wrh_ref[...]) + _mm(x_hi, wrl_ref[...]) + br_ref[...]
    lane = lax.broadcasted_iota(jnp.int32, logits.shape, 1)
    lane_f = lane.astype(F32)

    def argmax_first(x):
        v = jnp.max(x, axis=1, keepdims=True)
        return v, jnp.min(jnp.where(x == v, lane_f, 1e9), axis=1, keepdims=True)

    is_grp = lane < N_GROUPS
    lg = jnp.where(is_grp, logits, NEG)
    mx, grp = argmax_first(lg)
    p_top = 1.0 / jnp.sum(jnp.where(is_grp, jnp.exp(lg - mx), 0.0), axis=1, keepdims=True)
    in_grp = (lane >= N_GROUPS) & (lane < N_GROUPS + N_EXPERTS) & (
        ((lane - N_GROUPS) // EXP_PER_GROUP).astype(F32) == grp)
    le = jnp.where(in_grp, logits, NEG)
    v1, i1 = argmax_first(le)
    v2, i2 = argmax_first(jnp.where(lane_f == i1, NEG, le))
    t = jnp.exp(v2 - v1)
    g1 = p_top / (1.0 + t)
    g2 = p_top * t / (1.0 + t)
    gate_ref[...] = jnp.where(lane == 0, g1, jnp.where(lane == 1, g2, 0.0))

    e1 = i1 - N_GROUPS
    e2 = i2 - N_GROUPS
    oh1 = jnp.where(lane_f == e1, 1.0, 0.0)
    oh2 = jnp.where(lane_f == e2, 1.0, 0.0)
    both = oh1 + oh2
    n_tok = both.shape[0]
    earlier = jnp.where(lax.broadcasted_iota(jnp.int32, (n_tok, n_tok), 1)
                        < lax.broadcasted_iota(jnp.int32, (n_tok, n_tok), 0), 1.0, 0.0).astype(BF16)

    @pl.when((pl.program_id(0) == 0) & (pl.program_id(1) == 0))
    def _():
        cnt_scr[...] = jnp.zeros_like(cnt_scr)

    base = cnt_scr[...] + _mm(earlier, both.astype(BF16))
    r1 = jnp.sum(oh1 * base, axis=1, keepdims=True)
    r2 = jnp.sum(oh2 * base, axis=1, keepdims=True)
    cnt_scr[...] += jnp.sum(both, axis=0, keepdims=True)
    cnt_ref[...] = cnt_scr[...].astype(jnp.int32)
    eid_ref[...] = jnp.where(lane == 0, e1, jnp.where(lane == 1, e2, jnp.where(lane == 2, r1, jnp.where(
        lane == 3, r2, 0.0)))).astype(jnp.int32)


def _post_mixer(ya, yb, h, gya, gyb, wo, gx, wxq, memkv, layer, wxo, gf, wrh, wrl, br, tm=512):
    b, s, _ = ya.shape
    row = lambda n: pl.BlockSpec((None, tm, n), lambda bi, i: (bi, i, 0))
    hw = XH * XD
    in_specs = [row(1024), row(1024), row(D_MODEL), _const_spec(gya.shape), _const_spec(gyb.shape),
                _const_spec(wo.shape), _const_spec(gx.shape), _const_spec(wxq.shape),
                pl.BlockSpec((None, MEM_LEN, hw), lambda bi, i: (bi, 0, 2 * layer)),
                pl.BlockSpec((None, MEM_LEN, hw), lambda bi, i: (bi, 0, 2 * layer + 1)),
                _const_spec(wxo.shape), _const_spec(gf.shape), _const_spec(wrh.shape), _const_spec(wrl.shape),
                _const_spec(br.shape)]
    out_shape = [jax.ShapeDtypeStruct((b, s, D_MODEL), F32),
                 jax.ShapeDtypeStruct((b * s * PACK_ROWS, 128), jnp.uint32),
                 jax.ShapeDtypeStruct((b, s, 128), jnp.int32), jax.ShapeDtypeStruct((b, s, 128), F32),
                 jax.ShapeDtypeStruct((1, 128), jnp.int32)]
    slabs = pl.BlockSpec((tm * PACK_ROWS, 128), lambda bi, i: (bi * (s // tm) + i, 0))
    return pl.pallas_call(
        _post_body, out_shape=out_shape, grid=(b, s // tm), in_specs=in_specs,
        out_specs=[row(D_MODEL), slabs, row(128), row(128), pl.BlockSpec((1, 128), lambda bi, i: (0, 0))],
        scratch_shapes=[pltpu.VMEM((1, 128), F32)],
        compiler_params=_params(("arbitrary", "arbitrary")), name="post_mixer",
    )(ya, yb, h, gya, gyb, wo, gx, wxq, memkv, memkv, wxo, gf, wrh, wrl, br)


def _memkv_body(m_ref, g_ref, w_ref, o_ref):
    o_ref[...] = _mm(_rms(m_ref[...], g_ref[...]).astype(BF16), w_ref[...]).astype(BF16)


def _memkv(mem, g, w, tm=256):
    t = mem.shape[0]
    n = w.shape[1]
    return pl.pallas_call(
        _memkv_body, out_shape=jax.ShapeDtypeStruct((t, n), BF16), grid=(t // tm,),
        in_specs=[pl.BlockSpec((tm, D_MODEL), lambda i: (i, 0)), _const_spec(g.shape), _const_spec(w.shape)],
        out_specs=pl.BlockSpec((tm, n), lambda i: (i, 0)),
        compiler_params=_params(("parallel",)), name="memkv",
    )(mem, g, w)


SCATTER_TOKENS = 512


def _scatter_body(dest_ref, lastblk_ref, nu_ref, xn_ref, xs_ref, zbuf, sem, zsem, *, n_steps, n_blocks):
    i = pl.program_id(0)
    blk_rows = MOE_BLOCK * PACK_ROWS

    def zero_copy(blk):
        return pltpu.make_async_copy(zbuf, xs_ref.at[pl.ds(blk * blk_rows, blk_rows), :], zsem)

    def zero_fill(go):
        for e in range(N_EXPERTS):
            @pl.when(lastblk_ref[e] >= 0)
            def _(e=e):
                go(zero_copy(lastblk_ref[e]))
        for blk in range(n_blocks):
            @pl.when(blk >= nu_ref[0])
            def _(blk=blk):
                go(zero_copy(blk))

    @pl.when(i == 0)
    def _():
        zbuf[...] = jnp.zeros_like(zbuf)
        zero_fill(lambda c: c.start())
        zero_fill(lambda c: c.wait())

    def row_copy(j):
        d0 = pl.multiple_of(dest_ref[i * (SCATTER_TOKENS * TOP_K) + j] * PACK_ROWS, PACK_ROWS)
        return pltpu.make_async_copy(xn_ref.at[pl.ds((j // TOP_K) * PACK_ROWS, PACK_ROWS), :],
                                     xs_ref.at[pl.ds(d0, PACK_ROWS), :], sem)

    for j in range(SCATTER_TOKENS * TOP_K):
        row_copy(j).start(priority=j % TOP_K)
    for j in range(SCATTER_TOKENS * TOP_K):
        row_copy(j).wait()


def _scatter_rows(dest, last_block, n_used, xn, n_slots):
    t = xn.shape[0] // PACK_ROWS
    n_steps = t // SCATTER_TOKENS
    return pl.pallas_call(
        functools.partial(_scatter_body, n_steps=n_steps, n_blocks=n_slots // MOE_BLOCK),
        out_shape=jax.ShapeDtypeStruct((n_slots * PACK_ROWS, 128), xn.dtype),
        grid_spec=pltpu.PrefetchScalarGridSpec(
            num_scalar_prefetch=3, grid=(n_steps,),
            in_specs=[pl.BlockSpec((SCATTER_TOKENS * PACK_ROWS, 128), lambda i, d, lb, nu: (i, 0))],
            out_specs=pl.BlockSpec(memory_space=pl.ANY),
            scratch_shapes=[pltpu.VMEM((MOE_BLOCK * PACK_ROWS, 128), xn.dtype), pltpu.SemaphoreType.DMA(()),
                            pltpu.SemaphoreType.DMA(())]),
        compiler_params=_params(("arbitrary",)), name="scatter_rows",
    )(dest, last_block, n_used, xn)


def _experts_body(be_ref, bf_ref, ws_ref, nx_ref, nu_ref, x_ref, w1_ref, w3_ref, w2_ref, y_ref,
                  w1f, w3f, w2f, w1b, w3b, w2b, sem_w, *, layer):
    b = pl.program_id(0)
    n_used = nu_ref[0]

    def w_copies(e, ws):
        return [pltpu.make_async_copy(w_ref.at[layer, e], wf.at[ws], sem_w.at[ws])
                for w_ref, wf in ((w1_ref, w1f), (w3_ref, w3f), (w2_ref, w2f))]

    @pl.when(b == 0)
    def _():
        for c in w_copies(be_ref[0], 0):
            c.start()

    @pl.when(b >= n_used)
    def _():
        y_ref[...] = jnp.zeros_like(y_ref)

    @pl.when(b < n_used)
    def _():
        @pl.when(bf_ref[b] == 1)
        def _():
            ws = ws_ref[b]

            @pl.when(nx_ref[b] >= 0)
            def _():
                for c in w_copies(nx_ref[b], 1 - ws):
                    c.start()

            for c in w_copies(be_ref[b], ws):
                c.wait()
            w1b[...] = w1f[ws].astype(BF16)
            w3b[...] = w3f[ws].astype(BF16)
            w2b[...] = w2f[ws].astype(BF16)

        x = _load_packed(x_ref, MOE_BLOCK)
        hb = (jax.nn.silu(_mm(x, w1b[...])) * _mm(x, w3b[...])).astype(BF16)
        _store_slabs(y_ref, _mm(hb, w2b[...]), SLAB_ROWS)


def _experts(blk_expert, blk_first, blk_wslot, nxt_expert, n_used, xs, w_e1, w_e3, w_e2, layer):
    d = D_MODEL
    n_blocks = xs.shape[0] // (MOE_BLOCK * PACK_ROWS)
    hbm = pl.BlockSpec(memory_space=pl.ANY)
    x_spec = pl.BlockSpec((MOE_BLOCK * PACK_ROWS, 128), lambda b, be, bf, ws, nx, nu: (jnp.minimum(b, nu[0] - 1), 0))
    y_spec = pl.BlockSpec((MOE_BLOCK * SLAB_ROWS, 128), lambda b, be, bf, ws, nx, nu: (b, 0))
    return pl.pallas_call(
        functools.partial(_experts_body, layer=layer),
        out_shape=jax.ShapeDtypeStruct((n_blocks * MOE_BLOCK * SLAB_ROWS, 128), F32),
        grid_spec=pltpu.PrefetchScalarGridSpec(
            num_scalar_prefetch=5, grid=(n_blocks,),
            in_specs=[x_spec, hbm, hbm, hbm], out_specs=y_spec,
            scratch_shapes=[pltpu.VMEM((2, d, D_EXPERT), F32), pltpu.VMEM((2, d, D_EXPERT), F32),
                            pltpu.VMEM((2, D_EXPERT, d), F32),
                            pltpu.VMEM((d, D_EXPERT), BF16), pltpu.VMEM((d, D_EXPERT), BF16),
                            pltpu.VMEM((D_EXPERT, d), BF16), pltpu.SemaphoreType.DMA((2,))]),
        compiler_params=_params(("arbitrary",)), name="experts",
    )(blk_expert, blk_first, blk_wslot, nxt_expert, n_used, xs, w_e1, w_e3, w_e2)


COMBINE_TOKENS = 128


def _combine_body(dest_ref, ys_ref, h_ref, gate_ref, gfin_ref, o_ref, abuf, bbuf, sem, *, n_steps, final_norm):
    i = pl.program_id(0)
    slot = i % 2

    def copies(step, sl, j):
        a = (step * COMBINE_TOKENS + j) * TOP_K
        out = []
        for k, buf in enumerate((abuf, bbuf)):
            s0 = pl.multiple_of(dest_ref[a + k] * SLAB_ROWS, SLAB_ROWS)
            out.append(pltpu.make_async_copy(ys_ref.at[pl.ds(s0, SLAB_ROWS), :],
                                             buf.at[sl, pl.ds(j * SLAB_PITCH_VMEM, SLAB_ROWS), :], sem.at[sl]))
        return out

    def start(step, sl):
        for j in range(COMBINE_TOKENS):
            for k, c in enumerate(copies(step, sl, j)):
                c.start(priority=k)

    @pl.when(i == 0)
    def _():
        start(0, 0)

    @pl.when(i + 1 < n_steps)
    def _():
        start(i + 1, 1 - slot)

    for j in range(COMBINE_TOKENS):
        for c in copies(i, slot, j):
            c.wait()
    g = gate_ref[...]
    y = (h_ref[...] + g[:, 0:1] * _load_slabs(abuf.at[slot], COMBINE_TOKENS, SLAB_PITCH_VMEM)
         + g[:, 1:2] * _load_slabs(bbuf.at[slot], COMBINE_TOKENS, SLAB_PITCH_VMEM))
    if final_norm:
        y = _rms(y, gfin_ref[...])
    o_ref[...] = y


def _combine(dest, ys, h, gate, gfin, final_norm):
    t, d = h.shape
    tm = COMBINE_TOKENS
    n_steps = t // tm
    buf = pltpu.VMEM((2, tm * SLAB_PITCH_VMEM, 128), F32)
    return pl.pallas_call(
        functools.partial(_combine_body, n_steps=n_steps, final_norm=final_norm),
        out_shape=jax.ShapeDtypeStruct((t, d), F32),
        grid_spec=pltpu.PrefetchScalarGridSpec(
            num_scalar_prefetch=1, grid=(n_steps,),
            in_specs=[pl.BlockSpec(memory_space=pl.ANY), pl.BlockSpec((tm, d), lambda i, dr: (i, 0)),
                      pl.BlockSpec((tm, 128), lambda i, dr: (i, 0)), pl.BlockSpec((1, d), lambda i, dr: (0, 0))],
            out_specs=pl.BlockSpec((tm, d), lambda i, dr: (i, 0)),
            scratch_shapes=[buf, buf, pltpu.SemaphoreType.DMA((2,))]),
        compiler_params=_params(("arbitrary",)), name="combine",
    )(dest, ys, h, gate, gfin)


def _dispatch(eid, rank, counts):
    n_assign = eid.shape[0] * TOP_K
    n_blocks = (n_assign + N_EXPERTS * (MOE_BLOCK - 1) + MOE_BLOCK - 1) // MOE_BLOCK
    ids = jnp.arange(N_EXPERTS, dtype=jnp.int32)
    blocks = (counts + MOE_BLOCK - 1) // MOE_BLOCK
    blk_end = jnp.cumsum(blocks)
    blk_begin = blk_end - blocks
    onehot = (eid.reshape(-1)[:, None] == ids[None, :]).astype(F32)
    first_blk = jnp.dot(onehot, blk_begin.astype(F32)).astype(jnp.int32)
    dest = first_blk * MOE_BLOCK + rank.reshape(-1)
    blk_ids = jnp.arange(n_blocks, dtype=jnp.int32)
    blk_expert = jnp.minimum(jnp.sum((blk_end[None, :] <= blk_ids[:, None]).astype(jnp.int32), axis=1), N_EXPERTS - 1)
    blk_first = jnp.concatenate([jnp.ones((1,), jnp.int32), (blk_expert[1:] != blk_expert[:-1]).astype(jnp.int32)])
    n_used = blk_end[-1:].astype(jnp.int32)
    has_rows = counts > 0
    last_block = jnp.where(has_rows, blk_end - 1, -1).astype(jnp.int32)
    wslot = ((jnp.cumsum(has_rows.astype(jnp.int32)) - 1) % 2).astype(jnp.int32)
    later = jnp.where(has_rows[None, :] & (ids[None, :] > ids[:, None]), ids[None, :], N_EXPERTS)
    nxt = jnp.min(later, axis=1)
    nxt = jnp.where(nxt < N_EXPERTS, nxt, -1).astype(jnp.int32)
    return dest, last_block, (blk_expert, blk_first, wslot[blk_expert], nxt[blk_expert], n_used), n_blocks * MOE_BLOCK


def _rope_tables(positions):
    pos = positions.reshape(-1).astype(F32)[:, None]
    base, place, row0 = [], np.zeros((MLA_ROPE + NSA_ROT + 1, 6 * 128), np.float32), 0
    for tbl, (rot, passthrough) in enumerate(((MLA_ROPE, False), (NSA_ROT, True))):
        half = rot // 2
        ang = pos * (ROPE_THETA ** (-jnp.arange(0, rot, 2, dtype=F32) / rot))
        base += [jnp.cos(ang), jnp.sin(ang)]
        cos_rows, sin_rows, col0 = row0 + np.arange(half), row0 + half + np.arange(half), tbl * 3 * 128
        place[cos_rows, col0 + np.arange(half)] = 1.0
        place[cos_rows, col0 + half + np.arange(half)] = 1.0
        if passthrough:
            place[-1, col0 + rot:col0 + 128] = 1.0
        place[sin_rows, col0 + 128 + half + np.arange(half)] = 1.0
        place[sin_rows, col0 + 256 + np.arange(half)] = -1.0
        row0 += rot
    base = jnp.concatenate(base + [jnp.ones_like(pos)], axis=1)
    return jnp.dot(base, jnp.asarray(place), precision=lax.Precision.HIGHEST)


def _selection_constants(s):
    ncp = s // CMP_STRIDE
    ns = s // SEL_BLOCK
    c_start = np.arange(ncp)[None, :] * CMP_STRIDE
    sel_start = np.arange(ns)[:, None] * SEL_BLOCK
    ovt = (c_start <= sel_start + SEL_BLOCK - 1) & (c_start + CMP_LEN - 1 >= sel_start)
    ovt[:, ncp - 1] = False
    expand_t = np.arange(s)[:, None] // SEL_BLOCK == np.arange(ns)[None, :]
    return jnp.asarray(ovt, BF16), jnp.asarray(expand_t, BF16)


def _arrange_w_in(w):
    cuts = np.cumsum([0, 512, 256, 64, 1024, 256, 256, 256, 256, 256, 256, 24])
    c_q, c_kv, k_r, q_n, k_c, v_c, k_s, v_s, k_w, v_w, gates = [w[:, cuts[i]:cuts[i + 1]] for i in range(11)]
    z = lambda n: jnp.zeros((w.shape[0], n), w.dtype)
    w1 = jnp.concatenate([c_q, c_kv, k_r, z(64), q_n, k_c, k_s, k_w, v_c, gates, z(104)], axis=1).astype(BF16)
    return w1, jnp.concatenate([v_s, v_w], axis=1).T.astype(BF16)


def _arrange_w_uq(w):
    w = w.reshape(MLA_Q_RANK, MLA_HEADS, MLA_NOPE + MLA_ROPE)
    w = jnp.pad(w, ((0, 0), (0, 0), (0, MLA_QK_PAD - MLA_NOPE - MLA_ROPE)))
    return w.reshape(MLA_Q_RANK, MLA_HEADS * MLA_QK_PAD).astype(BF16)


def _split_w_ukv(w):
    w = w.reshape(MLA_KV_RANK, MLA_HEADS, MLA_NOPE + MLA_V)
    return (w[:, :, :MLA_NOPE].reshape(MLA_KV_RANK, -1).astype(BF16),
            w[:, :, MLA_NOPE:].reshape(MLA_KV_RANK, -1).T.astype(BF16))


def _router_weights(w_grp, b_grp, w_exp, b_exp):
    pad = 128 - N_GROUPS - N_EXPERTS
    w = jnp.concatenate([w_grp, w_exp, jnp.zeros((w_grp.shape[0], pad), F32)], axis=1)
    hi = w.astype(BF16)
    lo = (w - hi.astype(F32)).astype(BF16)
    b = jnp.concatenate([b_grp, b_exp, jnp.zeros((pad,), F32)])[None, :]
    return hi, lo, b


def kernel(x, mem, positions, g_mem, g_final, g_mix, w_in, g_cq, w_uq, g_ckv, w_ukv, pe_k, pe_v, w_ck1, w_ck2,
           w_cv1, w_cv2, g_ya, g_yb, w_out, g_x, w_xq, w_xkv, w_xo, g_ffn, w_grp, b_grp, w_exp, b_exp,
           w_e1, w_e3, w_e2):
    b, s, d = x.shape
    depth = w_in.shape[0]
    t = b * s
    ncp = s // CMP_STRIDE
    tab = _rope_tables(positions)
    ovt, expand_t = _selection_constants(s)
    r2 = lambda v: v.reshape(1, -1)

    w_kv_all = jnp.concatenate([w_xkv[l] for l in range(depth)], axis=1).astype(BF16)
    memkv = _memkv(mem.reshape(-1, d), r2(g_mem), w_kv_all).reshape(b, mem.shape[1], -1)

    h = x.reshape(t, d)
    for l in range(depth):
        wkn, wv_t = _split_w_ukv(w_ukv[l])
        w1, wvt = _arrange_w_in(w_in[l])
        qa, ka, va_t, qn, kc, vc, ks, kw, vt, gt = _in_proj(
            h, r2(g_mix[l]), w1, wvt, r2(g_cq[l]), r2(g_ckv[l]), _arrange_w_uq(w_uq[l]), wkn, wv_t, tab)
        kcmp, vcmp_t = _compress(
            kc.reshape(NSA_G, b, ncp, CMP_STRIDE * NSA_D), vc.reshape(NSA_G, b, ncp, CMP_STRIDE * NSA_D),
            pe_k[l].reshape(1, -1), pe_v[l].reshape(1, -1), w_ck1[l].astype(BF16), w_ck2[l].astype(BF16),
            w_cv1[l].astype(BF16), w_cv2[l].astype(BF16))
        r3 = lambda a: a.reshape(b, s, -1)
        ya = _mla_attn(r3(qa), r3(ka), va_t)
        yb = _nsa_attn(r3(qn), kcmp, vcmp_t, r3(ks), r3(kw), vt, gt.reshape(NSA_G, b, s, 128), ovt, expand_t)
        wrh, wrl, br = _router_weights(w_grp[l], b_grp[l], w_exp[l], b_exp[l])
        h2, xn, eid, gate, counts = _post_mixer(
            ya, yb, r3(h), r2(g_ya[l]), r2(g_yb[l]), w_out[l].astype(BF16), r2(g_x[l]), w_xq[l].astype(BF16),
            memkv, l, w_xo[l].astype(BF16), r2(g_ffn[l]), wrh, wrl, br)
        eid = eid.reshape(t, 128)
        dest, last_block, blocks, n_slots = _dispatch(
            eid[:, :TOP_K], eid[:, TOP_K:2 * TOP_K], counts[0, :N_EXPERTS])
        xs = _scatter_rows(dest, last_block, blocks[-1], xn, n_slots)
        ys = _experts(*blocks, xs, w_e1, w_e3, w_e2, l)
        h = _combine(dest, ys, h2.reshape(t, d), gate.reshape(t, 128), r2(g_final), final_norm=(l == depth - 1))
    return h.reshape(b, s, d)
```

```python
import functools

import jax
import jax.numpy as jnp
import numpy as np
from jax import lax
from jax.experimental import pallas as pl
from jax.experimental.pallas import tpu as pltpu

F32 = jnp.float32
BF16 = jnp.bfloat16

D_MODEL = 2048
RMS_EPS = 1e-6
ROPE_THETA = 500000.0
NEG = -1e30
FORCE_SCORE = 1e4

MLA_HEADS = 8
MLA_NOPE = 128
MLA_ROPE = 64
MLA_V = 128
MLA_Q_RANK = D_MODEL // 4
MLA_KV_RANK = D_MODEL // 8
MLA_QK_PAD = 256

NSA_HEADS = 8
NSA_G = 2
NSA_HPG = 4
NSA_D = 128
NSA_ROT = 32
CMP_LEN = 32
CMP_STRIDE = 16
CMP_HID = 512
SEL_BLOCK = 64
SEL_TOPN = 8
WINDOW = 512
QB = 128

XH = 4
XD = 128
MEM_LEN = 256

N_GROUPS = 8
EXP_PER_GROUP = 8
N_EXPERTS = 64
TOP_K = 2
D_EXPERT = D_MODEL // 4
MOE_BLOCK = 128

VMEM_LIMIT_BYTES = 56 * 1024 * 1024

C_CQ = 0
C_CKV = 512
C_KR = 768
C_QN = 896
C_KK = 1920
C_VC = 2688
C_GT = 2944
N_IN = 3072

LOG2E = 1.4426950408889634
SCALE_A = (MLA_NOPE + MLA_ROPE) ** -0.5 * LOG2E
SCALE_N = NSA_D ** -0.5 * LOG2E
SCALE_X = XD ** -0.5 * LOG2E


def _params(sem):
    return pltpu.CompilerParams(dimension_semantics=sem, vmem_limit_bytes=VMEM_LIMIT_BYTES)


def _mm(a, b):
    return jnp.dot(a, b, preferred_element_type=F32)


def _mm_nt(a, b):
    return lax.dot_general(a, b, (((1,), (1,)), ((), ())), preferred_element_type=F32)


def _rms(x, g):
    return x * lax.rsqrt(jnp.mean(x * x, axis=-1, keepdims=True) + RMS_EPS) * g


def _rope(x, c, sa, sb, half):
    return x * c + pltpu.roll(x, half, 1) * sa + pltpu.roll(x, 128 - half, 1) * sb


PACK_ROWS = D_MODEL // 128 // 2
HI_HALF = 0xFFFF0000


def _store_packed(ref, x):
    bits = pltpu.bitcast(x.astype(BF16).astype(F32), jnp.uint32)
    for c in range(PACK_ROWS):
        lo = bits[:, c * 128:(c + 1) * 128] >> 16
        hi = bits[:, (c + PACK_ROWS) * 128:(c + PACK_ROWS + 1) * 128] & jnp.uint32(HI_HALF)
        ref[pl.ds(c, x.shape[0], stride=PACK_ROWS), :] = lo | hi


def _load_packed(ref, n_tok):
    words = [ref[pl.ds(c, n_tok, stride=PACK_ROWS), :] for c in range(PACK_ROWS)]
    chunks = ([pltpu.bitcast(w << 16, F32) for w in words]
              + [pltpu.bitcast(w & jnp.uint32(HI_HALF), F32) for w in words])
    return jnp.concatenate(chunks, axis=1)


def _const_spec(shape):
    nd = len(shape)
    return pl.BlockSpec(shape, lambda *_: (0,) * nd, pipeline_mode=pl.Buffered(1))


def _in_proj_body(h_ref, gmix_ref, w1_ref, wvt_ref, gcq_ref, gckv_ref, wq_ref, wkn_ref, wv_ref, tab_ref,
                  qa_ref, ka_ref, va_ref, qn_ref, kc_ref, vc_ref, ks_ref, kw_ref, vt_ref, gt_ref):
    xn = _rms(h_ref[...], gmix_ref[...]).astype(BF16)
    tab = tab_ref[...]
    ca, saa, sba, cn, san, sbn = [tab[:, i * 128:(i + 1) * 128] for i in range(6)]

    xq = _rms(_mm(xn, w1_ref[:, C_CQ:C_CKV]), gcq_ref[...]).astype(BF16)
    ckr = _mm(xn, w1_ref[:, C_CKV:C_QN])
    xkv = _rms(ckr[:, :MLA_KV_RANK], gckv_ref[...]).astype(BF16)
    kr = _rope(ckr[:, MLA_KV_RANK:], ca, saa, sba, MLA_ROPE // 2).astype(BF16)
    q = _mm(xq, wq_ref[...])
    kn = _mm(xkv, wkn_ref[...])
    for h in range(MLA_HEADS):
        lo = h * MLA_QK_PAD
        qa_ref[:, lo:lo + 128] = (q[:, lo:lo + 128] * SCALE_A).astype(BF16)
        qa_ref[:, lo + 128:lo + 256] = (
            _rope(q[:, lo + 128:lo + 256], ca, saa, sba, MLA_ROPE // 2) * SCALE_A).astype(BF16)
        ka_ref[:, lo:lo + 128] = kn[:, h * 128:(h + 1) * 128].astype(BF16)
        ka_ref[:, lo + 128:lo + 256] = kr
    va_ref[...] = _mm_nt(wv_ref[...], xkv).astype(BF16)

    qn = _mm(xn, w1_ref[:, C_QN:C_KK])
    for h in range(NSA_HEADS):
        sl = slice(h * 128, (h + 1) * 128)
        qn_ref[:, sl] = (_rope(qn[:, sl], cn, san, sbn, NSA_ROT // 2) * SCALE_N).astype(BF16)
    kk = _mm(xn, w1_ref[:, C_KK:C_VC])
    vc = _mm(xn, w1_ref[:, C_VC:C_GT])
    for g in range(NSA_G):
        sl = slice(g * 128, (g + 1) * 128)
        kc_ref[g] = _rope(kk[:, sl], cn, san, sbn, NSA_ROT // 2)
        ks_ref[:, sl] = _rope(kk[:, 256 + g * 128:256 + (g + 1) * 128], cn, san, sbn, NSA_ROT // 2).astype(BF16)
        kw_ref[:, sl] = _rope(kk[:, 512 + g * 128:512 + (g + 1) * 128], cn, san, sbn, NSA_ROT // 2).astype(BF16)
        vc_ref[g] = vc[:, sl]
    vt_ref[...] = _mm_nt(wvt_ref[...], xn).astype(BF16)
    gl = _mm(xn, w1_ref[:, C_GT:N_IN])
    gt_ref[0] = gl
    gt_ref[1] = pltpu.roll(gl, 128 - NSA_HPG * 3, 1)


def _in_proj(h, gmix, w1, wvt, gcq, gckv, wq, wkn, wv, tab, tm=512):
    t = h.shape[0]
    row = lambda n: pl.BlockSpec((tm, n), lambda i: (i, 0))
    col = lambda n: pl.BlockSpec((n, tm), lambda i: (0, i))
    grp = pl.BlockSpec((NSA_G, tm, 128), lambda i: (0, i, 0))
    out_shape = [
        jax.ShapeDtypeStruct((t, MLA_HEADS * MLA_QK_PAD), BF16),
        jax.ShapeDtypeStruct((t, MLA_HEADS * MLA_QK_PAD), BF16),
        jax.ShapeDtypeStruct((MLA_HEADS * MLA_V, t), BF16),
        jax.ShapeDtypeStruct((t, NSA_HEADS * NSA_D), BF16),
        jax.ShapeDtypeStruct((NSA_G, t, NSA_D), F32),
        jax.ShapeDtypeStruct((NSA_G, t, NSA_D), F32),
        jax.ShapeDtypeStruct((t, NSA_G * NSA_D), BF16),
        jax.ShapeDtypeStruct((t, NSA_G * NSA_D), BF16),
        jax.ShapeDtypeStruct((2 * NSA_G * NSA_D, t), BF16),
        jax.ShapeDtypeStruct((NSA_G, t, 128), F32),
    ]
    out_specs = [row(2048), row(2048), col(1024), row(1024), grp, grp, row(256), row(256), col(512), grp]
    in_specs = [row(D_MODEL), _const_spec((1, D_MODEL)), _const_spec((D_MODEL, N_IN)),
                _const_spec((2 * NSA_G * NSA_D, D_MODEL)),
                _const_spec((1, MLA_Q_RANK)), _const_spec((1, MLA_KV_RANK)),
                _const_spec((MLA_Q_RANK, MLA_HEADS * MLA_QK_PAD)), _const_spec((MLA_KV_RANK, MLA_HEADS * 128)),
                _const_spec((MLA_HEADS * 128, MLA_KV_RANK)), row(6 * 128)]
    return pl.pallas_call(
        _in_proj_body, out_shape=out_shape, grid=(t // tm,), in_specs=in_specs, out_specs=out_specs,
        compiler_params=_params(("parallel",)), name="in_proj",
    )(h, gmix, w1, wvt, gcq, gckv, wq, wkn, wv, tab)


def _compress_body(k_ref, v_ref, pek_ref, pev_ref, wk1_ref, wk2_ref, wv1_ref, wv2_ref, ko_ref, vo_ref):
    half = CMP_STRIDE * NSA_D

    def phi(x, pe, w1_ref, w2_ref):
        a = _mm((x + pe[:, :half]).astype(BF16), w1_ref[:half, :])
        b = _mm((x + pe[:, half:]).astype(BF16), w1_ref[half:, :])
        pre = a + pltpu.roll(b, b.shape[0] - 1, 0)
        return _mm(jax.nn.gelu(pre).astype(BF16), w2_ref[...])

    ko_ref[...] = phi(k_ref[...], pek_ref[...], wk1_ref, wk2_ref).astype(BF16)
    vo_ref[...] = phi(v_ref[...], pev_ref[...], wv1_ref, wv2_ref).T.astype(BF16)


def _compress(kc, vc, pek, pev, wk1, wk2, wv1, wv2):
    g, b, ncp, w = kc.shape
    in_blk = pl.BlockSpec((None, None, ncp, w), lambda bi, gi: (gi, bi, 0, 0))
    return pl.pallas_call(
        _compress_body,
        out_shape=[jax.ShapeDtypeStruct((b, g, ncp, NSA_D), BF16), jax.ShapeDtypeStruct((b, g, NSA_D, ncp), BF16)],
        grid=(b, g),
        in_specs=[in_blk, in_blk, _const_spec(pek.shape), _const_spec(pev.shape), _const_spec(wk1.shape),
                  _const_spec(wk2.shape), _const_spec(wv1.shape), _const_spec(wv2.shape)],
        out_specs=[pl.BlockSpec((None, None, ncp, NSA_D), lambda bi, gi: (bi, gi, 0, 0)),
                   pl.BlockSpec((None, None, NSA_D, ncp), lambda bi, gi: (bi, gi, 0, 0))],
        compiler_params=_params(("parallel", "parallel")), name="compress",
    )(kc, vc, pek, pev, wk1, wk2, wv1, wv2)


def _mla_body(q_ref, k_ref, vt_ref, o_ref, *, tq, n_q):
    i = pl.program_id(2)
    q = q_ref[...]
    tri = lax.broadcasted_iota(jnp.int32, (tq, tq), 0) <= lax.broadcasted_iota(jnp.int32, (tq, tq), 1)
    for ci in range(n_q):
        @pl.when(i == ci)
        def _(ci=ci):
            lo = ci * tq
            sd = jnp.where(tri, _mm_nt(k_ref[lo:lo + tq, :], q), NEG)
            m = jnp.max(sd, axis=0, keepdims=True)
            if ci > 0:
                sf = _mm_nt(k_ref[0:lo, :], q)
                m = jnp.maximum(m, jnp.max(sf, axis=0, keepdims=True))
                pf = jnp.exp2(sf - m)
            pd = jnp.exp2(sd - m)
            l = jnp.sum(pd, axis=0, keepdims=True)
            acc = _mm(vt_ref[:, lo:lo + tq], pd.astype(BF16))
            if ci > 0:
                l = l + jnp.sum(pf, axis=0, keepdims=True)
                acc = acc + _mm(vt_ref[:, 0:lo], pf.astype(BF16))
            o_ref[...] = (acc / l).T


def _mla_attn(q, k, vt, tq=512):
    b, s, _ = q.shape
    return pl.pallas_call(
        functools.partial(_mla_body, tq=tq, n_q=s // tq),
        out_shape=jax.ShapeDtypeStruct((b, s, MLA_HEADS * MLA_V), F32),
        grid=(b, MLA_HEADS, s // tq),
        in_specs=[pl.BlockSpec((None, tq, MLA_QK_PAD), lambda bi, hi, i: (bi, i, hi)),
                  pl.BlockSpec((None, s, MLA_QK_PAD), lambda bi, hi, i: (bi, 0, hi)),
                  pl.BlockSpec((MLA_V, s), lambda bi, hi, i: (hi, bi))],
        out_specs=pl.BlockSpec((None, tq, MLA_V), lambda bi, hi, i: (bi, i, hi)),
        compiler_params=_params(("parallel", "parallel", "arbitrary")), name="mla_attn",
    )(q, k, vt)


SEL_BUCKET = 512
WIN_LEN = WINDOW + QB


def _nsa_body(q_ref, kc_ref, vct_ref, ks_ref, vst_ref, kw_ref, vwt_ref, gt_ref, ovt_ref, expt_ref, o_ref, os_ref, *,
              n_buckets):
    i = pl.program_id(2)
    ns = ovt_ref.shape[0]
    ncp = ovt_ref.shape[1]
    cols = NSA_HPG * QB
    q = q_ref[...]
    qst = jnp.concatenate([q[:, h * 128:(h + 1) * 128] for h in range(NSA_HPG)], axis=0)

    def tile4(x):
        return jnp.concatenate([x] * NSA_HPG, axis=1)

    pos_c = i * QB + (lax.broadcasted_iota(jnp.int32, (ncp, cols), 1) & (QB - 1))
    cend = lax.broadcasted_iota(jnp.int32, (ncp, cols), 0) * CMP_STRIDE + (CMP_LEN - 1)
    vis = cend <= pos_c
    s = jnp.where(vis, _mm_nt(kc_ref[...], qst), NEG)
    e = jnp.where(vis, jnp.exp2(s - jnp.max(s, axis=0, keepdims=True)), 0.0)
    l = jnp.sum(e, axis=0, keepdims=True)
    p_c = e / jnp.where(l > 0.0, l, 1.0)
    o_c = _mm(vct_ref[...], p_c.astype(BF16))

    psum = p_c[:, 0:QB] + p_c[:, QB:2 * QB] + p_c[:, 2 * QB:3 * QB] + p_c[:, 3 * QB:4 * QB]
    p_hi = psum.astype(BF16)
    p_lo = (psum - p_hi.astype(F32)).astype(BF16)
    imp = _mm(ovt_ref[...], p_hi) + _mm(ovt_ref[...], p_lo)
    n_i = lax.broadcasted_iota(jnp.int32, (ns, QB), 0)
    pos_l = i * QB + lax.broadcasted_iota(jnp.int32, (ns, QB), 1)
    forced = (n_i == 0) | (n_i == pos_l // SEL_BLOCK)
    val = jnp.where(forced, FORCE_SCORE, jnp.where(n_i * SEL_BLOCK <= pos_l, imp, -FORCE_SCORE))
    rank = jnp.zeros((ns, QB), F32)
    for m_ in range(ns):
        vm = val[m_:m_ + 1, :]
        rank = rank + jnp.where((vm > val) | ((vm == val) & (n_i > m_)), 1.0, 0.0)
    sel_bias = jnp.where(rank < float(min(SEL_TOPN, ns)), 0.0, NEG).astype(BF16)

    for cb in range(n_buckets):
        @pl.when(i // (SEL_BUCKET // QB) == cb)
        def _(cb=cb):
            n_keys = (cb + 1) * SEL_BUCKET
            bias = _mm(expt_ref[0:n_keys, :], sel_bias)
            qcol = i * QB + lax.broadcasted_iota(jnp.int32, (n_keys, QB), 1)
            bias = jnp.where(lax.broadcasted_iota(jnp.int32, (n_keys, QB), 0) <= qcol, bias, NEG)
            sc = _mm_nt(ks_ref[0:n_keys, :], qst) + tile4(bias)
            p = jnp.exp2(sc - jnp.max(sc, axis=0, keepdims=True))
            os_ref[...] = _mm(vst_ref[:, 0:n_keys], p.astype(BF16)) / jnp.sum(p, axis=0, keepdims=True)

    o_s = os_ref[...]

    start = pl.multiple_of(jnp.maximum(i * QB - WINDOW, 0), QB)
    dist = (i * QB + lax.broadcasted_iota(jnp.int32, (WIN_LEN, QB), 1)) - (
        start + lax.broadcasted_iota(jnp.int32, (WIN_LEN, QB), 0))
    ok = (dist >= 0) & (dist < WINDOW)
    s = _mm_nt(kw_ref[pl.ds(start, WIN_LEN), :], qst) + tile4(jnp.where(ok, 0.0, NEG))
    p = jnp.exp2(s - jnp.max(s, axis=0, keepdims=True))
    o_w =_mm(vwt_ref[:, pl.ds(start, WIN_LEN)], p.astype(BF16)) / jnp.sum(p, axis=0, keepdims=True)

    g = jax.nn.sigmoid(gt_ref[...]).T
    for h in range(NSA_HPG):
        sl = slice(h * QB, (h + 1) * QB)
        o_t = (g[3 * h:3 * h + 1, :] * o_c[:, sl] + g[3 * h + 1:3 * h + 2, :] * o_s[:, sl]
               + g[3 * h + 2:3 * h + 3, :] * o_w[:, sl])
        o_ref[:, h * 128:(h + 1) * 128] = o_t.T


def _nsa_attn(q, kcmp, vcmp_t, ks, kw, vt, gt, ovt, expand_t):
    b, s, _ = q.shape
    ncp = kcmp.shape[2]
    gw = NSA_HPG * NSA_D
    slab = pl.BlockSpec((None, s, NSA_D), lambda bi, gi, i: (bi, 0, gi))
    return pl.pallas_call(
        functools.partial(_nsa_body, n_buckets=s // SEL_BUCKET),
        out_shape=jax.ShapeDtypeStruct((b, s, NSA_HEADS * NSA_D), F32),
        grid=(b, NSA_G, s // QB),
        in_specs=[pl.BlockSpec((None, QB, gw), lambda bi, gi, i: (bi, i, gi)),
                  pl.BlockSpec((None, None, ncp, NSA_D), lambda bi, gi, i: (bi, gi, 0, 0)),
                  pl.BlockSpec((None, None, NSA_D, ncp), lambda bi, gi, i: (bi, gi, 0, 0)),
                  slab, pl.BlockSpec((NSA_D, s), lambda bi, gi, i: (gi, bi)),
                  slab, pl.BlockSpec((NSA_D, s), lambda bi, gi, i: (NSA_G + gi, bi)),
                  pl.BlockSpec((None, None, QB, 128), lambda bi, gi, i: (gi, bi, i, 0)),
                  _const_spec(ovt.shape), _const_spec(expand_t.shape)],
        out_specs=pl.BlockSpec((None, QB, gw), lambda bi, gi, i: (bi, i, gi)),
        scratch_shapes=[pltpu.VMEM((NSA_D, NSA_HPG * QB), F32)],
        compiler_params=_params(("parallel", "parallel", "arbitrary")), name="nsa_attn",
    )(q, kcmp, vcmp_t, ks, vt, kw, vt, gt, ovt, expand_t)


def _post_body(ya_ref, yb_ref, h_ref, gya_ref, gyb_ref, wo_ref, gx_ref, wxq_ref, mk_ref, mv_ref, wxo_ref,
               gf_ref, wrh_ref, wrl_ref, br_ref, h2_ref, xn_ref, eid_ref, gate_ref, cnt_ref, cnt_scr):
    half = wo_ref.shape[0] // 2
    h1 = (h_ref[...] + _mm(_rms(ya_ref[...], gya_ref[...]).astype(BF16), wo_ref[:half, :])
          + _mm(_rms(yb_ref[...], gyb_ref[...]).astype(BF16), wo_ref[half:, :]))

    q = (_mm(_rms(h1, gx_ref[...]).astype(BF16), wxq_ref[...]) * SCALE_X).astype(BF16)
    outs = []
    for hd in range(XH):
        sl = slice(hd * XD, (hd + 1) * XD)
        s = _mm_nt(q[:, sl], mk_ref[:, sl])
        p = jnp.exp2(s - jnp.max(s, axis=1, keepdims=True))
        outs.append(_mm(p.astype(BF16), mv_ref[:, sl]) / jnp.sum(p, axis=1, keepdims=True))
    h2 = h1 + _mm(jnp.concatenate(outs, axis=1).astype(BF16), wxo_ref[...])
    h2_ref[...] = h2

    xf = _rms(h2, gf_ref[...])
    _store_packed(xn_ref, xf)
    x_hi = xf.astype(BF16)
    x_lo = (xf - x_hi.astype(F32)).astype(BF16)
    logits = _mm(x_hi, wrh_ref[...]) + _mm(x_lo, wrh_ref[...]) + _mm(x_hi, wrl_ref[...]) + br_ref[...]
    lane = lax.broadcasted_iota(jnp.int32, logits.shape, 1)
    lane_f = lane.astype(F32)

    def argmax_first(x):
        v = jnp.max(x, axis=1, keepdims=True)
        return v, jnp.min(jnp.where(x == v, lane_f, 1e9), axis=1, keepdims=True)

    is_grp = lane < N_GROUPS
    lg = jnp.where(is_grp, logits, NEG)
    mx, grp = argmax_first(lg)
    p_top = 1.0 / jnp.sum(jnp.where(is_grp, jnp.exp(lg - mx), 0.0), axis=1, keepdims=True)
    in_grp = (lane >= N_GROUPS) & (lane < N_GROUPS + N_EXPERTS) & (
        ((lane - N_GROUPS) // EXP_PER_GROUP).astype(F32) == grp)
    le = jnp.where(in_grp, logits, NEG)
    v1, i1 = argmax_first(le)
    v2, i2 = argmax_first(jnp.where(lane_f == i1, NEG, le))
    t = jnp.exp(v2 - v1)
    g1 = p_top / (1.0 + t)
    g2 = p_top * t / (1.0 + t)
    gate_ref[...] = jnp.where(lane == 0, g1, jnp.where(lane == 1, g2, 0.0))

    e1 = i1 - N_GROUPS
    e2 = i2 - N_GROUPS
    oh1 = jnp.where(lane_f == e1, 1.0, 0.0)
    oh2 = jnp.where(lane_f == e2, 1.0, 0.0)
    both = oh1 + oh2
    n_tok = both.shape[0]
    earlier = jnp.where(lax.broadcasted_iota(jnp.int32, (n_tok, n_tok), 1)
                        < lax.broadcasted_iota(jnp.int32, (n_tok, n_tok), 0), 1.0, 0.0).astype(BF16)

    @pl.when((pl.program_id(0) == 0) & (pl.program_id(1) == 0))
    def _():
        cnt_scr[...] = jnp.zeros_like(cnt_scr)

    base = cnt_scr[...] + _mm(earlier, both.astype(BF16))
    r1 = jnp.sum(oh1 * base, axis=1, keepdims=True)
    r2 = jnp.sum(oh2 * base, axis=1, keepdims=True)
    cnt_scr[...] += jnp.sum(both, axis=0, keepdims=True)
    cnt_ref[...] = cnt_scr[...].astype(jnp.int32)
    eid_ref[...] = jnp.where(lane == 0, e1, jnp.where(lane == 1, e2, jnp.where(lane == 2, r1, jnp.where(
        lane == 3, r2, 0.0)))).astype(jnp.int32)


def _post_mixer(ya, yb, h, gya, gyb, wo, gx, wxq, memkv, layer, wxo, gf, wrh, wrl, br, tm=512):
    b, s, _ = ya.shape
    row = lambda n: pl.BlockSpec((None, tm, n), lambda bi, i: (bi, i, 0))
    hw = XH * XD
    in_specs = [row(1024), row(1024), row(D_MODEL), _const_spec(gya.shape), _const_spec(gyb.shape),
                _const_spec(wo.shape), _const_spec(gx.shape), _const_spec(wxq.shape),
                pl.BlockSpec((None, MEM_LEN, hw), lambda bi, i: (bi, 0, 2 * layer)),
                pl.BlockSpec((None, MEM_LEN, hw), lambda bi, i: (bi, 0, 2 * layer + 1)),
                _const_spec(wxo.shape), _const_spec(gf.shape), _const_spec(wrh.shape), _const_spec(wrl.shape),
                _const_spec(br.shape)]
    out_shape = [jax.ShapeDtypeStruct((b, s, D_MODEL), F32),
                 jax.ShapeDtypeStruct((b * s * PACK_ROWS, 128), jnp.uint32),
                 jax.ShapeDtypeStruct((b, s, 128), jnp.int32), jax.ShapeDtypeStruct((b, s, 128), F32),
                 jax.ShapeDtypeStruct((1, 128), jnp.int32)]
    slabs = pl.BlockSpec((tm * PACK_ROWS, 128), lambda bi, i: (bi * (s // tm) + i, 0))
    return pl.pallas_call(
        _post_body, out_shape=out_shape, grid=(b, s // tm), in_specs=in_specs,
        out_specs=[row(D_MODEL), slabs, row(128), row(128), pl.BlockSpec((1, 128), lambda bi, i: (0, 0))],
        scratch_shapes=[pltpu.VMEM((1, 128), F32)],
        compiler_params=_params(("arbitrary", "arbitrary")), name="post_mixer",
    )(ya, yb, h, gya, gyb, wo, gx, wxq, memkv, memkv, wxo, gf, wrh, wrl, br)


def _memkv_body(m_ref, g_ref, w_ref, o_ref):
    o_ref[...] = _mm(_rms(m_ref[...], g_ref[...]).astype(BF16), w_ref[...]).astype(BF16)


def _memkv(mem, g, w, tm=256):
    t = mem.shape[0]
    n = w.shape[1]
    return pl.pallas_call(
        _memkv_body, out_shape=jax.ShapeDtypeStruct((t, n), BF16), grid=(t // tm,),
        in_specs=[pl.BlockSpec((tm, D_MODEL), lambda i: (i, 0)), _const_spec(g.shape), _const_spec(w.shape)],
        out_specs=pl.BlockSpec((tm, n), lambda i: (i, 0)),
        compiler_params=_params(("parallel",)), name="memkv",
    )(mem, g, w)


SCATTER_TOKENS = 512


def _scatter_body(dest_ref, lastblk_ref, nu_ref, xn_ref, xs_ref, zbuf, sem, zsem, *, n_steps, n_blocks):
    i = pl.program_id(0)
    blk_rows = MOE_BLOCK * PACK_ROWS

    def zero_copy(blk):
        return pltpu.make_async_copy(zbuf, xs_ref.at[pl.ds(blk * blk_rows, blk_rows), :], zsem)

    def zero_fill(go):
        for e in range(N_EXPERTS):
            @pl.when(lastblk_ref[e] >= 0)
            def _(e=e):
                go(zero_copy(lastblk_ref[e]))
        for blk in range(n_blocks):
            @pl.when(blk >= nu_ref[0])
            def _(blk=blk):
                go(zero_copy(blk))

    @pl.when(i == 0)
    def _():
        zbuf[...] = jnp.zeros_like(zbuf)
        zero_fill(lambda c: c.start())
        zero_fill(lambda c: c.wait())

    def row_copy(j):
        d0 = pl.multiple_of(dest_ref[i * (SCATTER_TOKENS * TOP_K) + j] * PACK_ROWS, PACK_ROWS)
        return pltpu.make_async_copy(xn_ref.at[pl.ds((j // TOP_K) * PACK_ROWS, PACK_ROWS), :],
                                     xs_ref.at[pl.ds(d0, PACK_ROWS), :], sem)

    for j in range(SCATTER_TOKENS * TOP_K):
        row_copy(j).start(priority=j % TOP_K)
    for j in range(SCATTER_TOKENS * TOP_K):
        row_copy(j).wait()


def _scatter_rows(dest, last_block, n_used, xn, n_slots):
    t = xn.shape[0] // PACK_ROWS
    n_steps = t // SCATTER_TOKENS
    return pl.pallas_call(
        functools.partial(_scatter_body, n_steps=n_steps, n_blocks=n_slots // MOE_BLOCK),
        out_shape=jax.ShapeDtypeStruct((n_slots * PACK_ROWS, 128), xn.dtype),
        grid_spec=pltpu.PrefetchScalarGridSpec(
            num_scalar_prefetch=3, grid=(n_steps,),
            in_specs=[pl.BlockSpec((SCATTER_TOKENS * PACK_ROWS, 128), lambda i, d, lb, nu: (i, 0))],
            out_specs=pl.BlockSpec(memory_space=pl.ANY),
            scratch_shapes=[pltpu.VMEM((MOE_BLOCK * PACK_ROWS, 128), xn.dtype), pltpu.SemaphoreType.DMA(()),
                            pltpu.SemaphoreType.DMA(())]),
        compiler_params=_params(("arbitrary",)), name="scatter_rows",
    )(dest, last_block, n_used, xn)


def _experts_body(be_ref, bf_ref, ws_ref, nx_ref, nu_ref, x_ref, w1_ref, w3_ref, w2_ref, y_ref,
                  w1f, w3f, w2f, w1b, w3b, w2b, sem_w, *, layer):
    b = pl.program_id(0)
    n_used = nu_ref[0]

    def w_copies(e, ws):
        return [pltpu.make_async_copy(w_ref.at[layer, e], wf.at[ws], sem_w.at[ws])
                for w_ref, wf in ((w1_ref, w1f), (w3_ref, w3f), (w2_ref, w2f))]

    @pl.when(b == 0)
    def _():
        for c in w_copies(be_ref[0], 0):
            c.start()

    @pl.when(b >= n_used)
    def _():
        y_ref[...] = jnp.zeros_like(y_ref)

    @pl.when(b < n_used)
    def _():
        @pl.when(bf_ref[b] == 1)
        def _():
            ws = ws_ref[b]

            @pl.when(nx_ref[b] >= 0)
            def _():
                for c in w_copies(nx_ref[b], 1 - ws):
                    c.start()

            for c in w_copies(be_ref[b], ws):
                c.wait()
            w1b[...] = w1f[ws].astype(BF16)
            w3b[...] = w3f[ws].astype(BF16)
            w2b[...] = w2f[ws].astype(BF16)

        x = _load_packed(x_ref, MOE_BLOCK).astype(BF16)
        hb = (jax.nn.silu(_mm(x, w1b[...])) * _mm(x, w3b[...])).astype(BF16)
        _store_packed(y_ref, _mm(hb, w2b[...]))


def _experts(blk_expert, blk_first, blk_wslot, nxt_expert, n_used, xs, w_e1, w_e3, w_e2, layer):
    d = D_MODEL
    n_blocks = xs.shape[0] // (MOE_BLOCK * PACK_ROWS)
    hbm = pl.BlockSpec(memory_space=pl.ANY)
    x_spec = pl.BlockSpec((MOE_BLOCK * PACK_ROWS, 128), lambda b, be, bf, ws, nx, nu: (jnp.minimum(b, nu[0] - 1), 0))
    y_spec = pl.BlockSpec((MOE_BLOCK * PACK_ROWS, 128), lambda b, be, bf, ws, nx, nu: (b, 0))
    return pl.pallas_call(
        functools.partial(_experts_body, layer=layer),
        out_shape=jax.ShapeDtypeStruct(xs.shape, xs.dtype),
        grid_spec=pltpu.PrefetchScalarGridSpec(
            num_scalar_prefetch=5, grid=(n_blocks,),
            in_specs=[x_spec, hbm, hbm, hbm], out_specs=y_spec,
            scratch_shapes=[pltpu.VMEM((2, d, D_EXPERT), F32), pltpu.VMEM((2, d, D_EXPERT), F32),
                            pltpu.VMEM((2, D_EXPERT, d), F32),
                            pltpu.VMEM((d, D_EXPERT), BF16), pltpu.VMEM((d, D_EXPERT), BF16),
                            pltpu.VMEM((D_EXPERT, d), BF16), pltpu.SemaphoreType.DMA((2,))]),
        compiler_params=_params(("arbitrary",)), name="experts",
    )(blk_expert, blk_first, blk_wslot, nxt_expert, n_used, xs, w_e1, w_e3, w_e2)


COMBINE_TOKENS = 128


def _combine_body(dest_ref, ys_ref, h_ref, gate_ref, gfin_ref, o_ref, abuf, bbuf, sem, *, n_steps, final_norm):
    i = pl.program_id(0)
    slot = i % 2

    def copies(step, sl, j):
        a = (step * COMBINE_TOKENS + j) * TOP_K
        out = []
        for k, buf in enumerate((abuf, bbuf)):
            s0 = pl.multiple_of(dest_ref[a + k] * PACK_ROWS, PACK_ROWS)
            out.append(pltpu.make_async_copy(ys_ref.at[pl.ds(s0, PACK_ROWS), :],
                                             buf.at[sl, pl.ds(j * PACK_ROWS, PACK_ROWS), :], sem.at[sl]))
        return out

    def start(step, sl):
        for j in range(COMBINE_TOKENS):
            for k, c in enumerate(copies(step, sl, j)):
                c.start(priority=k)

    @pl.when(i == 0)
    def _():
        start(0, 0)

    @pl.when(i + 1 < n_steps)
    def _():
        start(i + 1, 1 - slot)

    for j in range(COMBINE_TOKENS):
        for c in copies(i, slot, j):
            c.wait()
    g = gate_ref[...]
    y = (h_ref[...] + g[:, 0:1] * _load_packed(abuf.at[slot], COMBINE_TOKENS)
         + g[:, 1:2] * _load_packed(bbuf.at[slot], COMBINE_TOKENS))
    if final_norm:
        y = _rms(y, gfin_ref[...])
    o_ref[...] = y


def _combine(dest, ys, h, gate, gfin, final_norm):
    t, d = h.shape
    tm = COMBINE_TOKENS
    n_steps = t // tm
    buf = pltpu.VMEM((2, tm * PACK_ROWS, 128), ys.dtype)
    return pl.pallas_call(
        functools.partial(_combine_body, n_steps=n_steps, final_norm=final_norm),
        out_shape=jax.ShapeDtypeStruct((t, d), F32),
        grid_spec=pltpu.PrefetchScalarGridSpec(
            num_scalar_prefetch=1, grid=(n_steps,),
            in_specs=[pl.BlockSpec(memory_space=pl.ANY), pl.BlockSpec((tm, d), lambda i, dr: (i, 0)),
                      pl.BlockSpec((tm, 128), lambda i, dr: (i, 0)), pl.BlockSpec((1, d), lambda i, dr: (0, 0))],
            out_specs=pl.BlockSpec((tm, d), lambda i, dr: (i, 0)),
            scratch_shapes=[buf, buf, pltpu.SemaphoreType.DMA((2,))]),
        compiler_params=_params(("arbitrary",)), name="combine",
    )(dest, ys, h, gate, gfin)


def _dispatch(eid, rank, counts):
    n_assign = eid.shape[0] * TOP_K
    n_blocks = (n_assign + N_EXPERTS * (MOE_BLOCK - 1) + MOE_BLOCK - 1) // MOE_BLOCK
    ids = jnp.arange(N_EXPERTS, dtype=jnp.int32)
    blocks = (counts + MOE_BLOCK - 1) // MOE_BLOCK
    blk_end = jnp.cumsum(blocks)
    blk_begin = blk_end - blocks
    onehot = (eid.reshape(-1)[:, None] == ids[None, :]).astype(F32)
    first_blk = jnp.dot(onehot, blk_begin.astype(F32)).astype(jnp.int32)
    dest = first_blk * MOE_BLOCK + rank.reshape(-1)
    blk_ids = jnp.arange(n_blocks, dtype=jnp.int32)
    blk_expert = jnp.minimum(jnp.sum((blk_end[None, :] <= blk_ids[:, None]).astype(jnp.int32), axis=1), N_EXPERTS - 1)
    blk_first = jnp.concatenate([jnp.ones((1,), jnp.int32), (blk_expert[1:] != blk_expert[:-1]).astype(jnp.int32)])
    n_used = blk_end[-1:].astype(jnp.int32)
    has_rows = counts > 0
    last_block = jnp.where(has_rows, blk_end - 1, -1).astype(jnp.int32)
    wslot = ((jnp.cumsum(has_rows.astype(jnp.int32)) - 1) % 2).astype(jnp.int32)
    later = jnp.where(has_rows[None, :] & (ids[None, :] > ids[:, None]), ids[None, :], N_EXPERTS)
    nxt = jnp.min(later, axis=1)
    nxt = jnp.where(nxt < N_EXPERTS, nxt, -1).astype(jnp.int32)
    return dest, last_block, (blk_expert, blk_first, wslot[blk_expert], nxt[blk_expert], n_used), n_blocks * MOE_BLOCK


def _rope_tables(positions):
    pos = positions.reshape(-1).astype(F32)[:, None]

    def tables(rot, passthrough):
        half = rot // 2
        ang = pos * (ROPE_THETA ** (-jnp.arange(0, rot, 2, dtype=F32) / rot))
        cos, sin = jnp.cos(ang), jnp.sin(ang)
        z = lambda n: jnp.zeros((pos.shape[0], n), F32)
        tail = jnp.full((pos.shape[0], 128 - rot), passthrough, F32)
        return [jnp.concatenate([cos, cos, tail], 1), jnp.concatenate([z(half), sin, z(128 - rot)], 1),
                jnp.concatenate([-sin, z(128 - half)], 1)]

    return jnp.concatenate(tables(MLA_ROPE, 0.0) + tables(NSA_ROT, 1.0), axis=1)


def _selection_constants(s):
    ncp = s // CMP_STRIDE
    ns = s // SEL_BLOCK
    c_start = np.arange(ncp)[None, :] * CMP_STRIDE
    sel_start = np.arange(ns)[:, None] * SEL_BLOCK
    ovt = (c_start <= sel_start + SEL_BLOCK - 1) & (c_start + CMP_LEN - 1 >= sel_start)
    ovt[:, ncp - 1] = False
    expand_t = np.arange(s)[:, None] // SEL_BLOCK == np.arange(ns)[None, :]
    return jnp.asarray(ovt, BF16), jnp.asarray(expand_t, BF16)


def _arrange_w_in(w):
    cuts = np.cumsum([0, 512, 256, 64, 1024, 256, 256, 256, 256, 256, 256, 24])
    c_q, c_kv, k_r, q_n, k_c, v_c, k_s, v_s, k_w, v_w, gates = [w[:, cuts[i]:cuts[i + 1]] for i in range(11)]
    z = lambda n: jnp.zeros((w.shape[0], n), w.dtype)
    w1 = jnp.concatenate([c_q, c_kv, k_r, z(64), q_n, k_c, k_s, k_w, v_c, gates, z(104)], axis=1).astype(BF16)
    return w1, jnp.concatenate([v_s, v_w], axis=1).T.astype(BF16)


def _arrange_w_uq(w):
    w = w.reshape(MLA_Q_RANK, MLA_HEADS, MLA_NOPE + MLA_ROPE)
    w = jnp.pad(w, ((0, 0), (0, 0), (0, MLA_QK_PAD - MLA_NOPE - MLA_ROPE)))
    return w.reshape(MLA_Q_RANK, MLA_HEADS * MLA_QK_PAD).astype(BF16)


def _split_w_ukv(w):
    w = w.reshape(MLA_KV_RANK, MLA_HEADS, MLA_NOPE + MLA_V)
    return (w[:, :, :MLA_NOPE].reshape(MLA_KV_RANK, -1).astype(BF16),
            w[:, :, MLA_NOPE:].reshape(MLA_KV_RANK, -1).T.astype(BF16))


def _router_weights(w_grp, b_grp, w_exp, b_exp):
    pad = 128 - N_GROUPS - N_EXPERTS
    w = jnp.concatenate([w_grp, w_exp, jnp.zeros((w_grp.shape[0], pad), F32)], axis=1)
    hi = w.astype(BF16)
    lo = (w - hi.astype(F32)).astype(BF16)
    b = jnp.concatenate([b_grp, b_exp, jnp.zeros((pad,), F32)])[None, :]
    return hi, lo, b


def kernel(x, mem, positions, g_mem, g_final, g_mix, w_in, g_cq, w_uq, g_ckv, w_ukv, pe_k, pe_v, w_ck1, w_ck2,
           w_cv1, w_cv2, g_ya, g_yb, w_out, g_x, w_xq, w_xkv, w_xo, g_ffn, w_grp, b_grp, w_exp, b_exp,
           w_e1, w_e3, w_e2):
    b, s, d = x.shape
    depth = w_in.shape[0]
    t = b * s
    ncp = s // CMP_STRIDE
    tab = _rope_tables(positions)
    ovt, expand_t = _selection_constants(s)
    r2 = lambda v: v.reshape(1, -1)

    w_kv_all = jnp.concatenate([w_xkv[l] for l in range(depth)], axis=1).astype(BF16)
    memkv = _memkv(mem.reshape(-1, d), r2(g_mem), w_kv_all).reshape(b, mem.shape[1], -1)

    h = x.reshape(t, d)
    for l in range(depth):
        wkn, wv_t = _split_w_ukv(w_ukv[l])
        w1, wvt = _arrange_w_in(w_in[l])
        qa, ka, va_t, qn, kc, vc, ks, kw, vt, gt = _in_proj(
            h, r2(g_mix[l]), w1, wvt, r2(g_cq[l]), r2(g_ckv[l]), _arrange_w_uq(w_uq[l]), wkn, wv_t, tab)
        kcmp, vcmp_t = _compress(
            kc.reshape(NSA_G, b, ncp, CMP_STRIDE * NSA_D), vc.reshape(NSA_G, b, ncp, CMP_STRIDE * NSA_D),
            pe_k[l].reshape(1, -1), pe_v[l].reshape(1, -1), w_ck1[l].astype(BF16), w_ck2[l].astype(BF16),
            w_cv1[l].astype(BF16), w_cv2[l].astype(BF16))
        r3 = lambda a: a.reshape(b, s, -1)
        ya = _mla_attn(r3(qa), r3(ka), va_t)
        yb = _nsa_attn(r3(qn), kcmp, vcmp_t, r3(ks), r3(kw), vt, gt.reshape(NSA_G, b, s, 128), ovt, expand_t)
        wrh, wrl, br = _router_weights(w_grp[l], b_grp[l], w_exp[l], b_exp[l])
        h2, xn, eid, gate, counts = _post_mixer(
            ya, yb, r3(h), r2(g_ya[l]), r2(g_yb[l]), w_out[l].astype(BF16), r2(g_x[l]), w_xq[l].astype(BF16),
            memkv, l, w_xo[l].astype(BF16), r2(g_ffn[l]), wrh, wrl, br)
        eid = eid.reshape(t, 128)
        dest, last_block, blocks, n_slots = _dispatch(
            eid[:, :TOP_K], eid[:, TOP_K:2 * TOP_K], counts[0, :N_EXPERTS])
        xs = _scatter_rows(dest, last_block, blocks[-1], xn, n_slots)
        ys = _experts(*blocks, xs, w_e1, w_e3, w_e2, l)
        h = _combine(dest, ys, h2.reshape(t, d), gate.reshape(t, 128), r2(g_final), final_norm=(l == depth - 1))
    return h.reshape(b, s, d)
```

```python
import functools

import jax
import jax.numpy as jnp
import numpy as np
from jax import lax
from jax.experimental import pallas as pl
from jax.experimental.pallas import tpu as pltpu

F32 = jnp.float32
BF16 = jnp.bfloat16

D_MODEL = 2048
RMS_EPS = 1e-6
ROPE_THETA = 500000.0
NEG = -1e30
FORCE_SCORE = 1e4

MLA_HEADS = 8
MLA_NOPE = 128
MLA_ROPE = 64
MLA_V = 128
MLA_Q_RANK = D_MODEL // 4
MLA_KV_RANK = D_MODEL // 8
MLA_QK_PAD = 256

NSA_HEADS = 8
NSA_G = 2
NSA_HPG = 4
NSA_D = 128
NSA_ROT = 32
CMP_LEN = 32
CMP_STRIDE = 16
CMP_HID = 512
SEL_BLOCK = 64
SEL_TOPN = 8
WINDOW = 512
QB = 128

XH = 4
XD = 128
MEM_LEN = 256

N_GROUPS = 8
EXP_PER_GROUP = 8
N_EXPERTS = 64
TOP_K = 2
D_EXPERT = D_MODEL // 4
MOE_BLOCK = 128

VMEM_LIMIT_BYTES = 56 * 1024 * 1024

C_CQ = 0
C_CKV = 512
C_KR = 768
C_QN = 896
C_KK = 1920
C_VC = 2688
C_GT = 2944
N_IN = 3072

LOG2E = 1.4426950408889634
SCALE_A = (MLA_NOPE + MLA_ROPE) ** -0.5 * LOG2E
SCALE_N = NSA_D ** -0.5 * LOG2E
SCALE_X = XD ** -0.5 * LOG2E


def _params(sem):
    return pltpu.CompilerParams(dimension_semantics=sem, vmem_limit_bytes=VMEM_LIMIT_BYTES)


def _mm(a, b):
    return jnp.dot(a, b, preferred_element_type=F32)


def _mm_nt(a, b):
    return lax.dot_general(a, b, (((1,), (1,)), ((), ())), preferred_element_type=F32)


def _rms(x, g):
    return x * lax.rsqrt(jnp.mean(x * x, axis=-1, keepdims=True) + RMS_EPS) * g


def _rope(x, c, sa, sb, half):
    return x * c + pltpu.roll(x, half, 1) * sa + pltpu.roll(x, 128 - half, 1) * sb


PACK_ROWS = D_MODEL // 128 // 2
HI_HALF = 0xFFFF0000


def _store_packed(ref, x):
    bits = pltpu.bitcast(x.astype(BF16).astype(F32), jnp.uint32)
    for c in range(PACK_ROWS):
        lo = bits[:, c * 128:(c + 1) * 128] >> 16
        hi = bits[:, (c + PACK_ROWS) * 128:(c + PACK_ROWS + 1) * 128] & jnp.uint32(HI_HALF)
        ref[pl.ds(c, x.shape[0], stride=PACK_ROWS), :] = lo | hi


def _load_packed(ref, n_tok):
    words = [ref[pl.ds(c, n_tok, stride=PACK_ROWS), :] for c in range(PACK_ROWS)]
    chunks = ([pltpu.bitcast(w << 16, F32) for w in words]
              + [pltpu.bitcast(w & jnp.uint32(HI_HALF), F32) for w in words])
    return jnp.concatenate(chunks, axis=1)


def _const_spec(shape):
    nd = len(shape)
    return pl.BlockSpec(shape, lambda *_: (0,) * nd, pipeline_mode=pl.Buffered(1))


def _in_proj_body(h_ref, gmix_ref, w1_ref, wvt_ref, gcq_ref, gckv_ref, wq_ref, wkn_ref, wv_ref, tab_ref,
                  qa_ref, ka_ref, va_ref, qn_ref, kc_ref, vc_ref, ks_ref, kw_ref, vt_ref, gt_ref):
    xn = _rms(h_ref[...], gmix_ref[...]).astype(BF16)
    tab = tab_ref[...]
    ca, saa, sba, cn, san, sbn = [tab[:, i * 128:(i + 1) * 128] for i in range(6)]

    xq = _rms(_mm(xn, w1_ref[:, C_CQ:C_CKV]), gcq_ref[...]).astype(BF16)
    ckr = _mm(xn, w1_ref[:, C_CKV:C_QN])
    xkv = _rms(ckr[:, :MLA_KV_RANK], gckv_ref[...]).astype(BF16)
    kr = _rope(ckr[:, MLA_KV_RANK:], ca, saa, sba, MLA_ROPE // 2).astype(BF16)
    q = _mm(xq, wq_ref[...])
    kn = _mm(xkv, wkn_ref[...])
    for h in range(MLA_HEADS):
        lo = h * MLA_QK_PAD
        qa_ref[:, lo:lo + 128] = (q[:, lo:lo + 128] * SCALE_A).astype(BF16)
        qa_ref[:, lo + 128:lo + 256] = (
            _rope(q[:, lo + 128:lo + 256], ca, saa, sba, MLA_ROPE // 2) * SCALE_A).astype(BF16)
        ka_ref[:, lo:lo + 128] = kn[:, h * 128:(h + 1) * 128].astype(BF16)
        ka_ref[:, lo + 128:lo + 256] = kr
    va_ref[...] = _mm_nt(wv_ref[...], xkv).astype(BF16)

    qn = _mm(xn, w1_ref[:, C_QN:C_KK])
    for h in range(NSA_HEADS):
        sl = slice(h * 128, (h + 1) * 128)
        qn_ref[:, sl] = (_rope(qn[:, sl], cn, san, sbn, NSA_ROT // 2) * SCALE_N).astype(BF16)
    kk = _mm(xn, w1_ref[:, C_KK:C_VC])
    vc = _mm(xn, w1_ref[:, C_VC:C_GT])
    for g in range(NSA_G):
        sl = slice(g * 128, (g + 1) * 128)
        kc_ref[g] = _rope(kk[:, sl], cn, san, sbn, NSA_ROT // 2)
        ks_ref[:, sl] = _rope(kk[:, 256 + g * 128:256 + (g + 1) * 128], cn, san, sbn, NSA_ROT // 2).astype(BF16)
        kw_ref[:, sl] = _rope(kk[:, 512 + g * 128:512 + (g + 1) * 128], cn, san, sbn, NSA_ROT // 2).astype(BF16)
        vc_ref[g] = vc[:, sl]
    vt_ref[...] = _mm_nt(wvt_ref[...], xn).astype(BF16)
    gl = _mm(xn, w1_ref[:, C_GT:N_IN])
    gt_ref[0] = gl
    gt_ref[1] = pltpu.roll(gl, 128 - NSA_HPG * 3, 1)


def _in_proj(h, gmix, w1, wvt, gcq, gckv, wq, wkn, wv, tab, tm=512):
    t = h.shape[0]
    row = lambda n: pl.BlockSpec((tm, n), lambda i: (i, 0))
    col = lambda n: pl.BlockSpec((n, tm), lambda i: (0, i))
    grp = pl.BlockSpec((NSA_G, tm, 128), lambda i: (0, i, 0))
    out_shape = [
        jax.ShapeDtypeStruct((t, MLA_HEADS * MLA_QK_PAD), BF16),
        jax.ShapeDtypeStruct((t, MLA_HEADS * MLA_QK_PAD), BF16),
        jax.ShapeDtypeStruct((MLA_HEADS * MLA_V, t), BF16),
        jax.ShapeDtypeStruct((t, NSA_HEADS * NSA_D), BF16),
        jax.ShapeDtypeStruct((NSA_G, t, NSA_D), F32),
        jax.ShapeDtypeStruct((NSA_G, t, NSA_D), F32),
        jax.ShapeDtypeStruct((t, NSA_G * NSA_D), BF16),
        jax.ShapeDtypeStruct((t, NSA_G * NSA_D), BF16),
        jax.ShapeDtypeStruct((2 * NSA_G * NSA_D, t), BF16),
        jax.ShapeDtypeStruct((NSA_G, t, 128), F32),
    ]
    out_specs = [row(2048), row(2048), col(1024), row(1024), grp, grp, row(256), row(256), col(512), grp]
    in_specs = [row(D_MODEL), _const_spec((1, D_MODEL)), _const_spec((D_MODEL, N_IN)),
                _const_spec((2 * NSA_G * NSA_D, D_MODEL)),
                _const_spec((1, MLA_Q_RANK)), _const_spec((1, MLA_KV_RANK)),
                _const_spec((MLA_Q_RANK, MLA_HEADS * MLA_QK_PAD)), _const_spec((MLA_KV_RANK, MLA_HEADS * 128)),
                _const_spec((MLA_HEADS * 128, MLA_KV_RANK)), row(6 * 128)]
    return pl.pallas_call(
        _in_proj_body, out_shape=out_shape, grid=(t // tm,), in_specs=in_specs, out_specs=out_specs,
        compiler_params=_params(("parallel",)), name="in_proj",
    )(h, gmix, w1, wvt, gcq, gckv, wq, wkn, wv, tab)


def _compress_body(k_ref, v_ref, pek_ref, pev_ref, wk1_ref, wk2_ref, wv1_ref, wv2_ref, ko_ref, vo_ref):
    half = CMP_STRIDE * NSA_D

    def phi(x, pe, w1_ref, w2_ref):
        a = _mm((x + pe[:, :half]).astype(BF16), w1_ref[:half, :])
        b = _mm((x + pe[:, half:]).astype(BF16), w1_ref[half:, :])
        pre = a + pltpu.roll(b, b.shape[0] - 1, 0)
        return _mm(jax.nn.gelu(pre).astype(BF16), w2_ref[...])

    ko_ref[...] = phi(k_ref[...], pek_ref[...], wk1_ref, wk2_ref).astype(BF16)
    vo_ref[...] = phi(v_ref[...], pev_ref[...], wv1_ref, wv2_ref).T.astype(BF16)


def _compress(kc, vc, pek, pev, wk1, wk2, wv1, wv2):
    g, b, ncp, w = kc.shape
    in_blk = pl.BlockSpec((None, None, ncp, w), lambda bi, gi: (gi, bi, 0, 0))
    return pl.pallas_call(
        _compress_body,
        out_shape=[jax.ShapeDtypeStruct((b, g, ncp, NSA_D), BF16), jax.ShapeDtypeStruct((b, g, NSA_D, ncp), BF16)],
        grid=(b, g),
        in_specs=[in_blk, in_blk, _const_spec(pek.shape), _const_spec(pev.shape), _const_spec(wk1.shape),
                  _const_spec(wk2.shape), _const_spec(wv1.shape), _const_spec(wv2.shape)],
        out_specs=[pl.BlockSpec((None, None, ncp, NSA_D), lambda bi, gi: (bi, gi, 0, 0)),
                   pl.BlockSpec((None, None, NSA_D, ncp), lambda bi, gi: (bi, gi, 0, 0))],
        compiler_params=_params(("parallel", "parallel")), name="compress",
    )(kc, vc, pek, pev, wk1, wk2, wv1, wv2)


def _mla_body(q_ref, k_ref, vt_ref, o_ref, *, tq, n_q):
    i = pl.program_id(2)
    tri = lax.broadcasted_iota(jnp.int32, (tq, tq), 0) <= lax.broadcasted_iota(jnp.int32, (tq, tq), 1)
    for ci in range(n_q):
        @pl.when(i == ci)
        def _(ci=ci):
            lo = ci * tq
            for hh in range(MLA_HEADS_PER_STEP):
                qk = slice(hh * MLA_QK_PAD, (hh + 1) * MLA_QK_PAD)
                vr = slice(hh * MLA_V, (hh + 1) * MLA_V)
                q = q_ref[:, qk]
                sd = jnp.where(tri, _mm_nt(k_ref[lo:lo + tq, qk], q), NEG)
                m = jnp.max(sd, axis=0, keepdims=True)
                if ci > 0:
                    sf = _mm_nt(k_ref[0:lo, qk], q)
                    m = jnp.maximum(m, jnp.max(sf, axis=0, keepdims=True))
                    pf = jnp.exp2(sf - m)
                pd = jnp.exp2(sd - m)
                l = jnp.sum(pd, axis=0, keepdims=True)
                acc = _mm(vt_ref[vr, lo:lo + tq], pd.astype(BF16))
                if ci > 0:
                    l = l + jnp.sum(pf, axis=0, keepdims=True)
                    acc = acc + _mm(vt_ref[vr, 0:lo], pf.astype(BF16))
                o_ref[:, vr] = (acc / l).T


MLA_HEADS_PER_STEP = 4


def _mla_attn(q, k, vt, tq=512):
    b, s, _ = q.shape
    hp = MLA_HEADS_PER_STEP
    return pl.pallas_call(
        functools.partial(_mla_body, tq=tq, n_q=s // tq),
        out_shape=jax.ShapeDtypeStruct((b, s, MLA_HEADS * MLA_V), F32),
        grid=(b, MLA_HEADS // hp, s // tq),
        in_specs=[pl.BlockSpec((None, tq, hp * MLA_QK_PAD), lambda bi, hi, i: (bi, i, hi)),
                  pl.BlockSpec((None, s, hp * MLA_QK_PAD), lambda bi, hi, i: (bi, 0, hi)),
                  pl.BlockSpec((hp * MLA_V, s), lambda bi, hi, i: (hi, bi))],
        out_specs=pl.BlockSpec((None, tq, hp * MLA_V), lambda bi, hi, i: (bi, i, hi)),
        compiler_params=_params(("parallel", "parallel", "arbitrary")), name="mla_attn",
    )(q, k, vt)


SEL_BUCKET = 512
WIN_LEN = WINDOW + QB


def _nsa_body(q_ref, kc_ref, vct_ref, ks_ref, vst_ref, kw_ref, vwt_ref, gt_ref, ovt_ref, expt_ref, o_ref, *,
              n_buckets):
    i = pl.program_id(2)
    ns = ovt_ref.shape[0]
    ncp = ovt_ref.shape[1]
    cols = NSA_HPG * QB
    q = q_ref[...]
    qst = jnp.concatenate([q[:, h * 128:(h + 1) * 128] for h in range(NSA_HPG)], axis=0)

    def tile4(x):
        return jnp.concatenate([x] * NSA_HPG, axis=1)

    pos_c = i * QB + (lax.broadcasted_iota(jnp.int32, (ncp, cols), 1) & (QB - 1))
    cend = lax.broadcasted_iota(jnp.int32, (ncp, cols), 0) * CMP_STRIDE + (CMP_LEN - 1)
    vis = cend <= pos_c
    s = jnp.where(vis, _mm_nt(kc_ref[...], qst), NEG)
    e = jnp.where(vis, jnp.exp2(s - jnp.max(s, axis=0, keepdims=True)), 0.0)
    l = jnp.sum(e, axis=0, keepdims=True)
    p_c = e / jnp.where(l > 0.0, l, 1.0)
    o_c = _mm(vct_ref[...], p_c.astype(BF16))

    psum = p_c[:, 0:QB] + p_c[:, QB:2 * QB] + p_c[:, 2 * QB:3 * QB] + p_c[:, 3 * QB:4 * QB]
    p_hi = psum.astype(BF16)
    p_lo = (psum - p_hi.astype(F32)).astype(BF16)
    imp = _mm(ovt_ref[...], p_hi) + _mm(ovt_ref[...], p_lo)
    n_i = lax.broadcasted_iota(jnp.int32, (ns, QB), 0)
    pos_l = i * QB + lax.broadcasted_iota(jnp.int32, (ns, QB), 1)
    forced = (n_i == 0) | (n_i == pos_l // SEL_BLOCK)
    val = jnp.where(forced, FORCE_SCORE, jnp.where(n_i * SEL_BLOCK <= pos_l, imp, -FORCE_SCORE))
    rank = jnp.zeros((ns, QB), F32)
    for m_ in range(ns):
        vm = val[m_:m_ + 1, :]
        rank = rank + jnp.where((vm > val) | ((vm == val) & (n_i > m_)), 1.0, 0.0)
    sel_bias = jnp.where(rank < float(min(SEL_TOPN, ns)), 0.0, NEG).astype(BF16)

    def window_and_mix(o_s):
        start = pl.multiple_of(jnp.maximum(i * QB - WINDOW, 0), QB)
        dist = (i * QB + lax.broadcasted_iota(jnp.int32, (WIN_LEN, QB), 1)) - (
            start + lax.broadcasted_iota(jnp.int32, (WIN_LEN, QB), 0))
        ok = (dist >= 0) & (dist < WINDOW)
        s = _mm_nt(kw_ref[pl.ds(start, WIN_LEN), :], qst) + tile4(jnp.where(ok, 0.0, NEG))
        p = jnp.exp2(s - jnp.max(s, axis=0, keepdims=True))
        o_w = _mm(vwt_ref[:, pl.ds(start, WIN_LEN)], p.astype(BF16)) / jnp.sum(p, axis=0, keepdims=True)

        g = jax.nn.sigmoid(gt_ref[...]).T
        for h in range(NSA_HPG):
            sl = slice(h * QB, (h + 1) * QB)
            o_t = (g[3 * h:3 * h + 1, :] * o_c[:, sl] + g[3 * h + 1:3 * h + 2, :] * o_s[:, sl]
                   + g[3 * h + 2:3 * h + 3, :] * o_w[:, sl])
            o_ref[:, h * 128:(h + 1) * 128] = o_t.T

    for cb in range(n_buckets):
        @pl.when(i // (SEL_BUCKET // QB) == cb)
        def _(cb=cb):
            n_keys = (cb + 1) * SEL_BUCKET
            bias = _mm(expt_ref[0:n_keys, :], sel_bias)
            qcol = i * QB + lax.broadcasted_iota(jnp.int32, (n_keys, QB), 1)
            bias = jnp.where(lax.broadcasted_iota(jnp.int32, (n_keys, QB), 0) <= qcol, bias, NEG)
            sc = _mm_nt(ks_ref[0:n_keys, :], qst) + tile4(bias)
            p = jnp.exp2(sc - jnp.max(sc, axis=0, keepdims=True))
            window_and_mix(_mm(vst_ref[:, 0:n_keys], p.astype(BF16)) / jnp.sum(p, axis=0, keepdims=True))


def _nsa_attn(q, kcmp, vcmp_t, ks, kw, vt, gt, ovt, expand_t):
    b, s, _ = q.shape
    ncp = kcmp.shape[2]
    gw = NSA_HPG * NSA_D
    slab = pl.BlockSpec((None, s, NSA_D), lambda bi, gi, i: (bi, 0, gi))
    return pl.pallas_call(
        functools.partial(_nsa_body, n_buckets=s // SEL_BUCKET),
        out_shape=jax.ShapeDtypeStruct((b, s, NSA_HEADS * NSA_D), F32),
        grid=(b, NSA_G, s // QB),
        in_specs=[pl.BlockSpec((None, QB, gw), lambda bi, gi, i: (bi, i, gi)),
                  pl.BlockSpec((None, None, ncp, NSA_D), lambda bi, gi, i: (bi, gi, 0, 0)),
                  pl.BlockSpec((None, None, NSA_D, ncp), lambda bi, gi, i: (bi, gi, 0, 0)),
                  slab, pl.BlockSpec((NSA_D, s), lambda bi, gi, i: (gi, bi)),
                  slab, pl.BlockSpec((NSA_D, s), lambda bi, gi, i: (NSA_G + gi, bi)),
                  pl.BlockSpec((None, None, QB, 128), lambda bi, gi, i: (gi, bi, i, 0)),
                  _const_spec(ovt.shape), _const_spec(expand_t.shape)],
        out_specs=pl.BlockSpec((None, QB, gw), lambda bi, gi, i: (bi, i, gi)),
        compiler_params=_params(("parallel", "parallel", "arbitrary")), name="nsa_attn",
    )(q, kcmp, vcmp_t, ks, vt, kw, vt, gt, ovt, expand_t)


def _post_body(ya_ref, yb_ref, h_ref, gya_ref, gyb_ref, wo_ref, gx_ref, wxq_ref, mk_ref, mv_ref, wxo_ref,
               gf_ref, wrh_ref, wrl_ref, br_ref, h2_ref, xn_ref, eid_ref, gate_ref, cnt_ref, cnt_scr):
    half = wo_ref.shape[0] // 2
    h1 = (h_ref[...] + _mm(_rms(ya_ref[...], gya_ref[...]).astype(BF16), wo_ref[:half, :])
          + _mm(_rms(yb_ref[...], gyb_ref[...]).astype(BF16), wo_ref[half:, :]))

    q = (_mm(_rms(h1, gx_ref[...]).astype(BF16), wxq_ref[...]) * SCALE_X).astype(BF16)
    outs = []
    for hd in range(XH):
        sl = slice(hd * XD, (hd + 1) * XD)
        s = _mm_nt(q[:, sl], mk_ref[:, sl])
        p = jnp.exp2(s - jnp.max(s, axis=1, keepdims=True))
        outs.append(_mm(p.astype(BF16), mv_ref[:, sl]) / jnp.sum(p, axis=1, keepdims=True))
    h2 = h1 + _mm(jnp.concatenate(outs, axis=1).astype(BF16), wxo_ref[...])
    h2_ref[...] = h2

    xf = _rms(h2, gf_ref[...])
    _store_packed(xn_ref, xf)
    x_hi = xf.astype(BF16)
    x_lo = (xf - x_hi.astype(F32)).astype(BF16)
    logits = _mm(x_hi, wrh_ref[...]) + _mm(x_lo, wrh_ref[...]) + _mm(x_hi, wrl_ref[...]) + br_ref[...]
    lane = lax.broadcasted_iota(jnp.int32, logits.shape, 1)
    lane_f = lane.astype(F32)

    def argmax_first(x):
        v = jnp.max(x, axis=1, keepdims=True)
        return v, jnp.min(jnp.where(x == v, lane_f, 1e9), axis=1, keepdims=True)

    is_grp = lane < N_GROUPS
    lg = jnp.where(is_grp, logits, NEG)
    mx, grp = argmax_first(lg)
    p_top = 1.0 / jnp.sum(jnp.where(is_grp, jnp.exp(lg - mx), 0.0), axis=1, keepdims=True)
    in_grp = (lane >= N_GROUPS) & (lane < N_GROUPS + N_EXPERTS) & (
        ((lane - N_GROUPS) // EXP_PER_GROUP).astype(F32) == grp)
    le = jnp.where(in_grp, logits, NEG)
    v1, i1 = argmax_first(le)
    v2, i2 = argmax_first(jnp.where(lane_f == i1, NEG, le))
    t = jnp.exp(v2 - v1)
    g1 = p_top / (1.0 + t)
    g2 = p_top * t / (1.0 + t)
    gate_ref[...] = jnp.where(lane == 0, g1, jnp.where(lane == 1, g2, 0.0))

    e1 = i1 - N_GROUPS
    e2 = i2 - N_GROUPS
    oh1 = jnp.where(lane_f == e1, 1.0, 0.0)
    oh2 = jnp.where(lane_f == e2, 1.0, 0.0)
    both = oh1 + oh2
    n_tok = both.shape[0]
    earlier = jnp.where(lax.broadcasted_iota(jnp.int32, (n_tok, n_tok), 1)
                        < lax.broadcasted_iota(jnp.int32, (n_tok, n_tok), 0), 1.0, 0.0).astype(BF16)

    @pl.when((pl.program_id(0) == 0) & (pl.program_id(1) == 0))
    def _():
        cnt_scr[...] = jnp.zeros_like(cnt_scr)

    base = cnt_scr[...] + _mm(earlier, both.astype(BF16))
    r1 = jnp.sum(oh1 * base, axis=1, keepdims=True)
    r2 = jnp.sum(oh2 * base, axis=1, keepdims=True)
    cnt_scr[...] += jnp.sum(both, axis=0, keepdims=True)
    cnt_ref[...] = cnt_scr[...].astype(jnp.int32)
    eid_ref[...] = jnp.where(lane == 0, e1, jnp.where(lane == 1, e2, jnp.where(lane == 2, r1, jnp.where(
        lane == 3, r2, 0.0)))).astype(jnp.int32)


def _post_mixer(ya, yb, h, gya, gyb, wo, gx, wxq, memkv, layer, wxo, gf, wrh, wrl, br, tm=512):
    b, s, _ = ya.shape
    row = lambda n: pl.BlockSpec((None, tm, n), lambda bi, i: (bi, i, 0))
    hw = XH * XD
    in_specs = [row(1024), row(1024), row(D_MODEL), _const_spec(gya.shape), _const_spec(gyb.shape),
                _const_spec(wo.shape), _const_spec(gx.shape), _const_spec(wxq.shape),
                pl.BlockSpec((None, MEM_LEN, hw), lambda bi, i: (bi, 0, 2 * layer)),
                pl.BlockSpec((None, MEM_LEN, hw), lambda bi, i: (bi, 0, 2 * layer + 1)),
                _const_spec(wxo.shape), _const_spec(gf.shape), _const_spec(wrh.shape), _const_spec(wrl.shape),
                _const_spec(br.shape)]
    out_shape = [jax.ShapeDtypeStruct((b, s, D_MODEL), F32),
                 jax.ShapeDtypeStruct((b * s * PACK_ROWS, 128), jnp.uint32),
                 jax.ShapeDtypeStruct((b, s, 128), jnp.int32), jax.ShapeDtypeStruct((b, s, 128), F32),
                 jax.ShapeDtypeStruct((1, 128), jnp.int32)]
    slabs = pl.BlockSpec((tm * PACK_ROWS, 128), lambda bi, i: (bi * (s // tm) + i, 0))
    return pl.pallas_call(
        _post_body, out_shape=out_shape, grid=(b, s // tm), in_specs=in_specs,
        out_specs=[row(D_MODEL), slabs, row(128), row(128), pl.BlockSpec((1, 128), lambda bi, i: (0, 0))],
        scratch_shapes=[pltpu.VMEM((1, 128), F32)],
        compiler_params=_params(("arbitrary", "arbitrary")), name="post_mixer",
    )(ya, yb, h, gya, gyb, wo, gx, wxq, memkv, memkv, wxo, gf, wrh, wrl, br)


def _memkv_body(m_ref, g_ref, w_ref, o_ref):
    o_ref[...] = _mm(_rms(m_ref[...], g_ref[...]).astype(BF16), w_ref[...]).astype(BF16)


def _memkv(mem, g, w, tm=256):
    t = mem.shape[0]
    n = w.shape[1]
    return pl.pallas_call(
        _memkv_body, out_shape=jax.ShapeDtypeStruct((t, n), BF16), grid=(t // tm,),
        in_specs=[pl.BlockSpec((tm, D_MODEL), lambda i: (i, 0)), _const_spec(g.shape), _const_spec(w.shape)],
        out_specs=pl.BlockSpec((tm, n), lambda i: (i, 0)),
        compiler_params=_params(("parallel",)), name="memkv",
    )(mem, g, w)


SCATTER_TOKENS = 512


def _scatter_body(dest_ref, lastblk_ref, nu_ref, xn_ref, xs_ref, zbuf, sem, zsem, *, n_steps, n_blocks):
    i = pl.program_id(0)
    blk_rows = MOE_BLOCK * PACK_ROWS

    def zero_copy(blk):
        return pltpu.make_async_copy(zbuf, xs_ref.at[pl.ds(blk * blk_rows, blk_rows), :], zsem)

    def zero_fill(go):
        for e in range(N_EXPERTS):
            @pl.when(lastblk_ref[e] >= 0)
            def _(e=e):
                go(zero_copy(lastblk_ref[e]))
        for blk in range(n_blocks):
            @pl.when(blk >= nu_ref[0])
            def _(blk=blk):
                go(zero_copy(blk))

    @pl.when(i == 0)
    def _():
        zbuf[...] = jnp.zeros_like(zbuf)
        zero_fill(lambda c: c.start())
        zero_fill(lambda c: c.wait())

    def row_copy(j):
        d0 = pl.multiple_of(dest_ref[i * (SCATTER_TOKENS * TOP_K) + j] * PACK_ROWS, PACK_ROWS)
        return pltpu.make_async_copy(xn_ref.at[pl.ds((j // TOP_K) * PACK_ROWS, PACK_ROWS), :],
                                     xs_ref.at[pl.ds(d0, PACK_ROWS), :], sem)

    for j in range(SCATTER_TOKENS * TOP_K):
        row_copy(j).start(priority=j % TOP_K)
    for j in range(SCATTER_TOKENS * TOP_K):
        row_copy(j).wait()


def _scatter_rows(dest, last_block, n_used, xn, n_slots):
    t = xn.shape[0] // PACK_ROWS
    n_steps = t // SCATTER_TOKENS
    return pl.pallas_call(
        functools.partial(_scatter_body, n_steps=n_steps, n_blocks=n_slots // MOE_BLOCK),
        out_shape=jax.ShapeDtypeStruct((n_slots * PACK_ROWS, 128), xn.dtype),
        grid_spec=pltpu.PrefetchScalarGridSpec(
            num_scalar_prefetch=3, grid=(n_steps,),
            in_specs=[pl.BlockSpec((SCATTER_TOKENS * PACK_ROWS, 128), lambda i, d, lb, nu: (i, 0))],
            out_specs=pl.BlockSpec(memory_space=pl.ANY),
            scratch_shapes=[pltpu.VMEM((MOE_BLOCK * PACK_ROWS, 128), xn.dtype), pltpu.SemaphoreType.DMA(()),
                            pltpu.SemaphoreType.DMA(())]),
        compiler_params=_params(("arbitrary",)), name="scatter_rows",
    )(dest, last_block, n_used, xn)


def _experts_body(be_ref, bf_ref, ws_ref, nx_ref, nu_ref, x_ref, w1_ref, w3_ref, w2_ref, y_ref,
                  w1f, w3f, w2f, w1b, w3b, w2b, sem_w, *, layer):
    b = pl.program_id(0)
    n_used = nu_ref[0]

    def w_copies(e, ws):
        return [pltpu.make_async_copy(w_ref.at[layer, e], wf.at[ws], sem_w.at[ws])
                for w_ref, wf in ((w1_ref, w1f), (w3_ref, w3f), (w2_ref, w2f))]

    @pl.when(b == 0)
    def _():
        for c in w_copies(be_ref[0], 0):
            c.start()

    @pl.when(b >= n_used)
    def _():
        y_ref[...] = jnp.zeros_like(y_ref)

    @pl.when(b < n_used)
    def _():
        @pl.when(bf_ref[b] == 1)
        def _():
            ws = ws_ref[b]

            @pl.when(nx_ref[b] >= 0)
            def _():
                for c in w_copies(nx_ref[b], 1 - ws):
                    c.start()

            for c in w_copies(be_ref[b], ws):
                c.wait()
            w1b[...] = w1f[ws].astype(BF16)
            w3b[...] = w3f[ws].astype(BF16)
            w2b[...] = w2f[ws].astype(BF16)

        x = _load_packed(x_ref, MOE_BLOCK).astype(BF16)
        hb = (jax.nn.silu(_mm(x, w1b[...])) * _mm(x, w3b[...])).astype(BF16)
        _store_packed(y_ref, _mm(hb, w2b[...]))


def _experts(blk_expert, blk_first, blk_wslot, nxt_expert, n_used, xs, w_e1, w_e3, w_e2, layer):
    d = D_MODEL
    n_blocks = xs.shape[0] // (MOE_BLOCK * PACK_ROWS)
    hbm = pl.BlockSpec(memory_space=pl.ANY)
    x_spec = pl.BlockSpec((MOE_BLOCK * PACK_ROWS, 128), lambda b, be, bf, ws, nx, nu: (jnp.minimum(b, nu[0] - 1), 0))
    y_spec = pl.BlockSpec((MOE_BLOCK * PACK_ROWS, 128), lambda b, be, bf, ws, nx, nu: (b, 0))
    return pl.pallas_call(
        functools.partial(_experts_body, layer=layer),
        out_shape=jax.ShapeDtypeStruct(xs.shape, xs.dtype),
        grid_spec=pltpu.PrefetchScalarGridSpec(
            num_scalar_prefetch=5, grid=(n_blocks,),
            in_specs=[x_spec, hbm, hbm, hbm], out_specs=y_spec,
            scratch_shapes=[pltpu.VMEM((2, d, D_EXPERT), F32), pltpu.VMEM((2, d, D_EXPERT), F32),
                            pltpu.VMEM((2, D_EXPERT, d), F32),
                            pltpu.VMEM((d, D_EXPERT), BF16), pltpu.VMEM((d, D_EXPERT), BF16),
                            pltpu.VMEM((D_EXPERT, d), BF16), pltpu.SemaphoreType.DMA((2,))]),
        compiler_params=_params(("arbitrary",)), name="experts",
    )(blk_expert, blk_first, blk_wslot, nxt_expert, n_used, xs, w_e1, w_e3, w_e2)


COMBINE_TOKENS = 128


def _combine_body(dest_ref, ys_ref, h_ref, gate_ref, gfin_ref, o_ref, abuf, bbuf, sem, *, n_steps, final_norm):
    i = pl.program_id(0)
    slot = i % 2

    def copies(step, sl, j):
        a = (step * COMBINE_TOKENS + j) * TOP_K
        out = []
        for k, buf in enumerate((abuf, bbuf)):
            s0 = pl.multiple_of(dest_ref[a + k] * PACK_ROWS, PACK_ROWS)
            out.append(pltpu.make_async_copy(ys_ref.at[pl.ds(s0, PACK_ROWS), :],
                                             buf.at[sl, pl.ds(j * PACK_ROWS, PACK_ROWS), :], sem.at[sl]))
        return out

    def start(step, sl):
        for j in range(COMBINE_TOKENS):
            for k, c in enumerate(copies(step, sl, j)):
                c.start(priority=k)

    @pl.when(i == 0)
    def _():
        start(0, 0)

    @pl.when(i + 1 < n_steps)
    def _():
        start(i + 1, 1 - slot)

    for j in range(COMBINE_TOKENS):
        for c in copies(i, slot, j):
            c.wait()
    g = gate_ref[...]
    y = (h_ref[...] + g[:, 0:1] * _load_packed(abuf.at[slot], COMBINE_TOKENS)
         + g[:, 1:2] * _load_packed(bbuf.at[slot], COMBINE_TOKENS))
    if final_norm:
        y = _rms(y, gfin_ref[...])
    o_ref[...] = y


def _combine(dest, ys, h, gate, gfin, final_norm):
    t, d = h.shape
    tm = COMBINE_TOKENS
    n_steps = t // tm
    buf = pltpu.VMEM((2, tm * PACK_ROWS, 128), ys.dtype)
    return pl.pallas_call(
        functools.partial(_combine_body, n_steps=n_steps, final_norm=final_norm),
        out_shape=jax.ShapeDtypeStruct((t, d), F32),
        grid_spec=pltpu.PrefetchScalarGridSpec(
            num_scalar_prefetch=1, grid=(n_steps,),
            in_specs=[pl.BlockSpec(memory_space=pl.ANY), pl.BlockSpec((tm, d), lambda i, dr: (i, 0)),
                      pl.BlockSpec((tm, 128), lambda i, dr: (i, 0)), pl.BlockSpec((1, d), lambda i, dr: (0, 0))],
            out_specs=pl.BlockSpec((tm, d), lambda i, dr: (i, 0)),
            scratch_shapes=[buf, buf, pltpu.SemaphoreType.DMA((2,))]),
        compiler_params=_params(("arbitrary",)), name="combine",
    )(dest, ys, h, gate, gfin)


def _dispatch(eid, rank, counts):
    n_assign = eid.shape[0] * TOP_K
    n_blocks = (n_assign + N_EXPERTS * (MOE_BLOCK - 1) + MOE_BLOCK - 1) // MOE_BLOCK
    ids = jnp.arange(N_EXPERTS, dtype=jnp.int32)
    blocks = (counts + MOE_BLOCK - 1) // MOE_BLOCK
    blk_end = jnp.cumsum(blocks)
    blk_begin = blk_end - blocks
    onehot = (eid.reshape(-1)[:, None] == ids[None, :]).astype(F32)
    first_blk = jnp.dot(onehot, blk_begin.astype(F32)).astype(jnp.int32)
    dest = first_blk * MOE_BLOCK + rank.reshape(-1)
    blk_ids = jnp.arange(n_blocks, dtype=jnp.int32)
    blk_expert = jnp.minimum(jnp.sum((blk_end[None, :] <= blk_ids[:, None]).astype(jnp.int32), axis=1), N_EXPERTS - 1)
    blk_first = jnp.concatenate([jnp.ones((1,), jnp.int32), (blk_expert[1:] != blk_expert[:-1]).astype(jnp.int32)])
    n_used = blk_end[-1:].astype(jnp.int32)
    has_rows = counts > 0
    last_block = jnp.where(has_rows, blk_end - 1, -1).astype(jnp.int32)
    wslot = ((jnp.cumsum(has_rows.astype(jnp.int32)) - 1) % 2).astype(jnp.int32)
    later = jnp.where(has_rows[None, :] & (ids[None, :] > ids[:, None]), ids[None, :], N_EXPERTS)
    nxt = jnp.min(later, axis=1)
    nxt = jnp.where(nxt < N_EXPERTS, nxt, -1).astype(jnp.int32)
    return dest, last_block, (blk_expert, blk_first, wslot[blk_expert], nxt[blk_expert], n_used), n_blocks * MOE_BLOCK


def _rope_tables(positions):
    pos = positions.reshape(-1).astype(F32)[:, None]

    def tables(rot, passthrough):
        half = rot // 2
        ang = pos * (ROPE_THETA ** (-jnp.arange(0, rot, 2, dtype=F32) / rot))
        cos, sin = jnp.cos(ang), jnp.sin(ang)
        z = lambda n: jnp.zeros((pos.shape[0], n), F32)
        tail = jnp.full((pos.shape[0], 128 - rot), passthrough, F32)
        return [jnp.concatenate([cos, cos, tail], 1), jnp.concatenate([z(half), sin, z(128 - rot)], 1),
                jnp.concatenate([-sin, z(128 - half)], 1)]

    return jnp.concatenate(tables(MLA_ROPE, 0.0) + tables(NSA_ROT, 1.0), axis=1)


def _selection_constants(s):
    ncp = s // CMP_STRIDE
    ns = s // SEL_BLOCK
    c_start = np.arange(ncp)[None, :] * CMP_STRIDE
    sel_start = np.arange(ns)[:, None] * SEL_BLOCK
    ovt = (c_start <= sel_start + SEL_BLOCK - 1) & (c_start + CMP_LEN - 1 >= sel_start)
    ovt[:, ncp - 1] = False
    expand_t = np.arange(s)[:, None] // SEL_BLOCK == np.arange(ns)[None, :]
    return jnp.asarray(ovt, BF16), jnp.asarray(expand_t, BF16)


def _arrange_w_in(w):
    cuts = np.cumsum([0, 512, 256, 64, 1024, 256, 256, 256, 256, 256, 256, 24])
    c_q, c_kv, k_r, q_n, k_c, v_c, k_s, v_s, k_w, v_w, gates = [w[:, cuts[i]:cuts[i + 1]] for i in range(11)]
    z = lambda n: jnp.zeros((w.shape[0], n), w.dtype)
    w1 = jnp.concatenate([c_q, c_kv, k_r, z(64), q_n, k_c, k_s, k_w, v_c, gates, z(104)], axis=1).astype(BF16)
    return w1, jnp.concatenate([v_s, v_w], axis=1).T.astype(BF16)


def _arrange_w_uq(w):
    w = w.reshape(MLA_Q_RANK, MLA_HEADS, MLA_NOPE + MLA_ROPE)
    w = jnp.pad(w, ((0, 0), (0, 0), (0, MLA_QK_PAD - MLA_NOPE - MLA_ROPE)))
    return w.reshape(MLA_Q_RANK, MLA_HEADS * MLA_QK_PAD).astype(BF16)


def _split_w_ukv(w):
    w = w.reshape(MLA_KV_RANK, MLA_HEADS, MLA_NOPE + MLA_V)
    return (w[:, :, :MLA_NOPE].reshape(MLA_KV_RANK, -1).astype(BF16),
            w[:, :, MLA_NOPE:].reshape(MLA_KV_RANK, -1).T.astype(BF16))


def _router_weights(w_grp, b_grp, w_exp, b_exp):
    pad = 128 - N_GROUPS - N_EXPERTS
    w = jnp.concatenate([w_grp, w_exp, jnp.zeros((w_grp.shape[0], pad), F32)], axis=1)
    hi = w.astype(BF16)
    lo = (w - hi.astype(F32)).astype(BF16)
    b = jnp.concatenate([b_grp, b_exp, jnp.zeros((pad,), F32)])[None, :]
    return hi, lo, b


def kernel(x, mem, positions, g_mem, g_final, g_mix, w_in, g_cq, w_uq, g_ckv, w_ukv, pe_k, pe_v, w_ck1, w_ck2,
           w_cv1, w_cv2, g_ya, g_yb, w_out, g_x, w_xq, w_xkv, w_xo, g_ffn, w_grp, b_grp, w_exp, b_exp,
           w_e1, w_e3, w_e2):
    b, s, d = x.shape
    depth = w_in.shape[0]
    t = b * s
    ncp = s // CMP_STRIDE
    tab = _rope_tables(positions)
    ovt, expand_t = _selection_constants(s)
    r2 = lambda v: v.reshape(1, -1)

    w_kv_all = jnp.concatenate([w_xkv[l] for l in range(depth)], axis=1).astype(BF16)
    memkv = _memkv(mem.reshape(-1, d), r2(g_mem), w_kv_all).reshape(b, mem.shape[1], -1)

    h = x.reshape(t, d)
    for l in range(depth):
        wkn, wv_t = _split_w_ukv(w_ukv[l])
        w1, wvt = _arrange_w_in(w_in[l])
        qa, ka, va_t, qn, kc, vc, ks, kw, vt, gt = _in_proj(
            h, r2(g_mix[l]), w1, wvt, r2(g_cq[l]), r2(g_ckv[l]), _arrange_w_uq(w_uq[l]), wkn, wv_t, tab)
        kcmp, vcmp_t = _compress(
            kc.reshape(NSA_G, b, ncp, CMP_STRIDE * NSA_D), vc.reshape(NSA_G, b, ncp, CMP_STRIDE * NSA_D),
            pe_k[l].reshape(1, -1), pe_v[l].reshape(1, -1), w_ck1[l].astype(BF16), w_ck2[l].astype(BF16),
            w_cv1[l].astype(BF16), w_cv2[l].astype(BF16))
        r3 = lambda a: a.reshape(b, s, -1)
        ya = _mla_attn(r3(qa), r3(ka), va_t)
        yb = _nsa_attn(r3(qn), kcmp, vcmp_t, r3(ks), r3(kw), vt, gt.reshape(NSA_G, b, s, 128), ovt, expand_t)
        wrh, wrl, br = _router_weights(w_grp[l], b_grp[l], w_exp[l], b_exp[l])
        h2, xn, eid, gate, counts = _post_mixer(
            ya, yb, r3(h), r2(g_ya[l]), r2(g_yb[l]), w_out[l].astype(BF16), r2(g_x[l]), w_xq[l].astype(BF16),
            memkv, l, w_xo[l].astype(BF16), r2(g_ffn[l]), wrh, wrl, br)
        eid = eid.reshape(t, 128)
        dest, last_block, blocks, n_slots = _dispatch(
            eid[:, :TOP_K], eid[:, TOP_K:2 * TOP_K], counts[0, :N_EXPERTS])
        xs = _scatter_rows(dest, last_block, blocks[-1], xn, n_slots)
        ys = _experts(*blocks, xs, w_e1, w_e3, w_e2, l)
        h = _combine(dest, ys, h2.reshape(t, d), gate.reshape(t, 128), r2(g_final), final_norm=(l == depth - 1))
    return h.reshape(b, s, d)
```

```python
import functools

import jax
import jax.numpy as jnp
import numpy as np
from jax import lax
from jax.experimental import pallas as pl
from jax.experimental.pallas import tpu as pltpu

F32 = jnp.float32
BF16 = jnp.bfloat16

D_MODEL = 2048
RMS_EPS = 1e-6
ROPE_THETA = 500000.0
NEG = -1e30
FORCE_SCORE = 1e4

MLA_HEADS = 8
MLA_NOPE = 128
MLA_ROPE = 64
MLA_V = 128
MLA_Q_RANK = D_MODEL // 4
MLA_KV_RANK = D_MODEL // 8
MLA_QK_PAD = 256

NSA_HEADS = 8
NSA_G = 2
NSA_HPG = 4
NSA_D = 128
NSA_ROT = 32
CMP_LEN = 32
CMP_STRIDE = 16
CMP_HID = 512
SEL_BLOCK = 64
SEL_TOPN = 8
WINDOW = 512
QB = 128

XH = 4
XD = 128
MEM_LEN = 256

N_GROUPS = 8
EXP_PER_GROUP = 8
N_EXPERTS = 64
TOP_K = 2
D_EXPERT = D_MODEL // 4
MOE_BLOCK = 128

VMEM_LIMIT_BYTES = 56 * 1024 * 1024

C_CQ = 0
C_CKV = 512
C_KR = 768
C_QN = 896
C_KK = 1920
C_VC = 2688
C_GT = 2944
N_IN = 3072

LOG2E = 1.4426950408889634
SCALE_A = (MLA_NOPE + MLA_ROPE) ** -0.5 * LOG2E
SCALE_N = NSA_D ** -0.5 * LOG2E
SCALE_X = XD ** -0.5 * LOG2E


def _params(sem):
    return pltpu.CompilerParams(dimension_semantics=sem, vmem_limit_bytes=VMEM_LIMIT_BYTES)


def _mm(a, b):
    return jnp.dot(a, b, preferred_element_type=F32)


def _mm_nt(a, b):
    return lax.dot_general(a, b, (((1,), (1,)), ((), ())), preferred_element_type=F32)


def _rms(x, g):
    return x * lax.rsqrt(jnp.mean(x * x, axis=-1, keepdims=True) + RMS_EPS) * g


def _rope(x, c, sa, sb, half):
    return x * c + pltpu.roll(x, half, 1) * sa + pltpu.roll(x, 128 - half, 1) * sb


PACK_ROWS = D_MODEL // 128 // 2
HI_HALF = 0xFFFF0000


def _store_packed(ref, x):
    bits = pltpu.bitcast(x.astype(BF16).astype(F32), jnp.uint32)
    for c in range(PACK_ROWS):
        lo = bits[:, c * 128:(c + 1) * 128] >> 16
        hi = bits[:, (c + PACK_ROWS) * 128:(c + PACK_ROWS + 1) * 128] & jnp.uint32(HI_HALF)
        ref[pl.ds(c, x.shape[0], stride=PACK_ROWS), :] = lo | hi


def _load_packed(ref, n_tok):
    words = [ref[pl.ds(c, n_tok, stride=PACK_ROWS), :] for c in range(PACK_ROWS)]
    chunks = ([pltpu.bitcast(w << 16, F32) for w in words]
              + [pltpu.bitcast(w & jnp.uint32(HI_HALF), F32) for w in words])
    return jnp.concatenate(chunks, axis=1)


def _const_spec(shape):
    nd = len(shape)
    return pl.BlockSpec(shape, lambda *_: (0,) * nd, pipeline_mode=pl.Buffered(1))


def _in_proj_body(h_ref, gmix_ref, w1_ref, wvt_ref, gcq_ref, gckv_ref, wq_ref, wkn_ref, wv_ref, tab_ref,
                  qa_ref, ka_ref, va_ref, qn_ref, kc_ref, vc_ref, ks_ref, kw_ref, vt_ref, gt_ref):
    xn = _rms(h_ref[...], gmix_ref[...]).astype(BF16)
    tab = tab_ref[...]
    ca, saa, sba, cn, san, sbn = [tab[:, i * 128:(i + 1) * 128] for i in range(6)]

    xq = _rms(_mm(xn, w1_ref[:, C_CQ:C_CKV]), gcq_ref[...]).astype(BF16)
    ckr = _mm(xn, w1_ref[:, C_CKV:C_QN])
    xkv = _rms(ckr[:, :MLA_KV_RANK], gckv_ref[...]).astype(BF16)
    kr = _rope(ckr[:, MLA_KV_RANK:], ca, saa, sba, MLA_ROPE // 2).astype(BF16)
    q = _mm(xq, wq_ref[...])
    kn = _mm(xkv, wkn_ref[...])
    for h in range(MLA_HEADS):
        lo = h * MLA_QK_PAD
        qa_ref[:, lo:lo + 128] = (q[:, lo:lo + 128] * SCALE_A).astype(BF16)
        qa_ref[:, lo + 128:lo + 256] = (
            _rope(q[:, lo + 128:lo + 256], ca, saa, sba, MLA_ROPE // 2) * SCALE_A).astype(BF16)
        ka_ref[:, lo:lo + 128] = kn[:, h * 128:(h + 1) * 128].astype(BF16)
        ka_ref[:, lo + 128:lo + 256] = kr
    va_ref[...] = _mm_nt(wv_ref[...], xkv).astype(BF16)

    qn = _mm(xn, w1_ref[:, C_QN:C_KK])
    for h in range(NSA_HEADS):
        sl = slice(h * 128, (h + 1) * 128)
        qn_ref[:, sl] = (_rope(qn[:, sl], cn, san, sbn, NSA_ROT // 2) * SCALE_N).astype(BF16)
    kk = _mm(xn, w1_ref[:, C_KK:C_VC])
    vc = _mm(xn, w1_ref[:, C_VC:C_GT])
    for g in range(NSA_G):
        sl = slice(g * 128, (g + 1) * 128)
        kc_ref[g] = _rope(kk[:, sl], cn, san, sbn, NSA_ROT // 2)
        ks_ref[:, sl] = _rope(kk[:, 256 + g * 128:256 + (g + 1) * 128], cn, san, sbn, NSA_ROT // 2).astype(BF16)
        kw_ref[:, sl] = _rope(kk[:, 512 + g * 128:512 + (g + 1) * 128], cn, san, sbn, NSA_ROT // 2).astype(BF16)
        vc_ref[g] = vc[:, sl]
    vt_ref[...] = _mm_nt(wvt_ref[...], xn).astype(BF16)
    gl = _mm(xn, w1_ref[:, C_GT:N_IN])
    gt_ref[0] = gl
    gt_ref[1] = pltpu.roll(gl, 128 - NSA_HPG * 3, 1)


def _in_proj(h, gmix, w1, wvt, gcq, gckv, wq, wkn, wv, tab, tm=512):
    t = h.shape[0]
    row = lambda n: pl.BlockSpec((tm, n), lambda i: (i, 0))
    col = lambda n: pl.BlockSpec((n, tm), lambda i: (0, i))
    grp = pl.BlockSpec((NSA_G, tm, 128), lambda i: (0, i, 0))
    out_shape = [
        jax.ShapeDtypeStruct((t, MLA_HEADS * MLA_QK_PAD), BF16),
        jax.ShapeDtypeStruct((t, MLA_HEADS * MLA_QK_PAD), BF16),
        jax.ShapeDtypeStruct((MLA_HEADS * MLA_V, t), BF16),
        jax.ShapeDtypeStruct((t, NSA_HEADS * NSA_D), BF16),
        jax.ShapeDtypeStruct((NSA_G, t, NSA_D), F32),
        jax.ShapeDtypeStruct((NSA_G, t, NSA_D), F32),
        jax.ShapeDtypeStruct((t, NSA_G * NSA_D), BF16),
        jax.ShapeDtypeStruct((t, NSA_G * NSA_D), BF16),
        jax.ShapeDtypeStruct((2 * NSA_G * NSA_D, t), BF16),
        jax.ShapeDtypeStruct((NSA_G, t, 128), F32),
    ]
    out_specs = [row(2048), row(2048), col(1024), row(1024), grp, grp, row(256), row(256), col(512), grp]
    in_specs = [row(D_MODEL), _const_spec((1, D_MODEL)), _const_spec((D_MODEL, N_IN)),
                _const_spec((2 * NSA_G * NSA_D, D_MODEL)),
                _const_spec((1, MLA_Q_RANK)), _const_spec((1, MLA_KV_RANK)),
                _const_spec((MLA_Q_RANK, MLA_HEADS * MLA_QK_PAD)), _const_spec((MLA_KV_RANK, MLA_HEADS * 128)),
                _const_spec((MLA_HEADS * 128, MLA_KV_RANK)), row(6 * 128)]
    return pl.pallas_call(
        _in_proj_body, out_shape=out_shape, grid=(t // tm,), in_specs=in_specs, out_specs=out_specs,
        compiler_params=_params(("parallel",)), name="in_proj",
    )(h, gmix, w1, wvt, gcq, gckv, wq, wkn, wv, tab)


def _compress_body(k_ref, v_ref, pek_ref, pev_ref, wk1_ref, wk2_ref, wv1_ref, wv2_ref, ko_ref, vo_ref):
    half = CMP_STRIDE * NSA_D

    def phi(x, pe, w1_ref, w2_ref):
        a = _mm((x + pe[:, :half]).astype(BF16), w1_ref[:half, :])
        b = _mm((x + pe[:, half:]).astype(BF16), w1_ref[half:, :])
        pre = a + pltpu.roll(b, b.shape[0] - 1, 0)
        return _mm(jax.nn.gelu(pre).astype(BF16), w2_ref[...])

    ko_ref[...] = phi(k_ref[...], pek_ref[...], wk1_ref, wk2_ref).astype(BF16)
    vo_ref[...] = phi(v_ref[...], pev_ref[...], wv1_ref, wv2_ref).T.astype(BF16)


def _compress(kc, vc, pek, pev, wk1, wk2, wv1, wv2):
    g, b, ncp, w = kc.shape
    in_blk = pl.BlockSpec((None, None, ncp, w), lambda bi, gi: (gi, bi, 0, 0))
    return pl.pallas_call(
        _compress_body,
        out_shape=[jax.ShapeDtypeStruct((b, g, ncp, NSA_D), BF16), jax.ShapeDtypeStruct((b, g, NSA_D, ncp), BF16)],
        grid=(b, g),
        in_specs=[in_blk, in_blk, _const_spec(pek.shape), _const_spec(pev.shape), _const_spec(wk1.shape),
                  _const_spec(wk2.shape), _const_spec(wv1.shape), _const_spec(wv2.shape)],
        out_specs=[pl.BlockSpec((None, None, ncp, NSA_D), lambda bi, gi: (bi, gi, 0, 0)),
                   pl.BlockSpec((None, None, NSA_D, ncp), lambda bi, gi: (bi, gi, 0, 0))],
        compiler_params=_params(("parallel", "parallel")), name="compress",
    )(kc, vc, pek, pev, wk1, wk2, wv1, wv2)


def _mla_body(q_ref, k_ref, vt_ref, o_ref, *, tq, n_q):
    i = pl.program_id(2)
    tri = lax.broadcasted_iota(jnp.int32, (tq, tq), 0) <= lax.broadcasted_iota(jnp.int32, (tq, tq), 1)
    for ci in range(n_q):
        @pl.when(i == ci)
        def _(ci=ci):
            lo = ci * tq
            for hh in range(MLA_HEADS_PER_STEP):
                qk = slice(hh * MLA_QK_PAD, (hh + 1) * MLA_QK_PAD)
                vr = slice(hh * MLA_V, (hh + 1) * MLA_V)
                q = q_ref[:, qk]
                sd = jnp.where(tri, _mm_nt(k_ref[lo:lo + tq, qk], q), NEG)
                m = jnp.max(sd, axis=0, keepdims=True)
                if ci > 0:
                    sf = _mm_nt(k_ref[0:lo, qk], q)
                    m = jnp.maximum(m, jnp.max(sf, axis=0, keepdims=True))
                    pf = jnp.exp2(sf - m)
                pd = jnp.exp2(sd - m)
                l = jnp.sum(pd, axis=0, keepdims=True)
                acc = _mm(vt_ref[vr, lo:lo + tq], pd.astype(BF16))
                if ci > 0:
                    l = l + jnp.sum(pf, axis=0, keepdims=True)
                    acc = acc + _mm(vt_ref[vr, 0:lo], pf.astype(BF16))
                o_ref[:, vr] = (acc / l).T


MLA_HEADS_PER_STEP = 4


def _mla_attn(q, k, vt, tq=512):
    b, s, _ = q.shape
    hp = MLA_HEADS_PER_STEP
    return pl.pallas_call(
        functools.partial(_mla_body, tq=tq, n_q=s // tq),
        out_shape=jax.ShapeDtypeStruct((b, s, MLA_HEADS * MLA_V), F32),
        grid=(b, MLA_HEADS // hp, s // tq),
        in_specs=[pl.BlockSpec((None, tq, hp * MLA_QK_PAD), lambda bi, hi, i: (bi, i, hi)),
                  pl.BlockSpec((None, s, hp * MLA_QK_PAD), lambda bi, hi, i: (bi, 0, hi)),
                  pl.BlockSpec((hp * MLA_V, s), lambda bi, hi, i: (hi, bi))],
        out_specs=pl.BlockSpec((None, tq, hp * MLA_V), lambda bi, hi, i: (bi, i, hi)),
        compiler_params=_params(("parallel", "parallel", "arbitrary")), name="mla_attn",
    )(q, k, vt)


SEL_BUCKET = 512
WIN_LEN = WINDOW + QB


def _nsa_body(q_ref, kc_ref, vct_ref, ks_ref, vst_ref, kw_ref, vwt_ref, gt_ref, ovt_ref, expt_ref, o_ref, *,
              n_buckets):
    i = pl.program_id(1)
    ns = ovt_ref.shape[0]
    ncp = ovt_ref.shape[1]
    cols = NSA_HPG * QB
    gw = NSA_HPG * NSA_D

    def tile4(x):
        return jnp.concatenate([x] * NSA_HPG, axis=1)

    def front(g):
        q = q_ref[:, g * gw:(g + 1) * gw]
        qst = jnp.concatenate([q[:, h * 128:(h + 1) * 128] for h in range(NSA_HPG)], axis=0)

        pos_c = i * QB + (lax.broadcasted_iota(jnp.int32, (ncp, cols), 1) & (QB - 1))
        cend = lax.broadcasted_iota(jnp.int32, (ncp, cols), 0) * CMP_STRIDE + (CMP_LEN - 1)
        vis = cend <= pos_c
        s = jnp.where(vis, _mm_nt(kc_ref[g], qst), NEG)
        e = jnp.where(vis, jnp.exp2(s - jnp.max(s, axis=0, keepdims=True)), 0.0)
        l = jnp.sum(e, axis=0, keepdims=True)
        p_c = e / jnp.where(l > 0.0, l, 1.0)
        o_c = _mm(vct_ref[g], p_c.astype(BF16))

        psum = p_c[:, 0:QB] + p_c[:, QB:2 * QB] + p_c[:, 2 * QB:3 * QB] + p_c[:, 3 * QB:4 * QB]
        p_hi = psum.astype(BF16)
        p_lo = (psum - p_hi.astype(F32)).astype(BF16)
        imp = _mm(ovt_ref[...], p_hi) + _mm(ovt_ref[...], p_lo)
        n_i = lax.broadcasted_iota(jnp.int32, (ns, QB), 0)
        pos_l = i * QB + lax.broadcasted_iota(jnp.int32, (ns, QB), 1)
        forced = (n_i == 0) | (n_i == pos_l // SEL_BLOCK)
        val = jnp.where(forced, FORCE_SCORE, jnp.where(n_i * SEL_BLOCK <= pos_l, imp, -FORCE_SCORE))
        rank = jnp.zeros((ns, QB), F32)
        for m_ in range(ns):
            vm = val[m_:m_ + 1, :]
            rank = rank + jnp.where((vm > val) | ((vm == val) & (n_i > m_)), 1.0, 0.0)
        sel_bias = jnp.where(rank < float(min(SEL_TOPN, ns)), 0.0, NEG).astype(BF16)
        return qst, o_c, sel_bias

    fronts = [front(g) for g in range(NSA_G)]

    def window_and_mix(g, qst, o_c, o_s):
        kv = slice(g * NSA_D, (g + 1) * NSA_D)
        start = pl.multiple_of(jnp.maximum(i * QB - WINDOW, 0), QB)
        dist = (i * QB + lax.broadcasted_iota(jnp.int32, (WIN_LEN, QB), 1)) - (
            start + lax.broadcasted_iota(jnp.int32, (WIN_LEN, QB), 0))
        ok = (dist >= 0) & (dist < WINDOW)
        s = _mm_nt(kw_ref[pl.ds(start, WIN_LEN), kv], qst) + tile4(jnp.where(ok, 0.0, NEG))
        p = jnp.exp2(s - jnp.max(s, axis=0, keepdims=True))
        o_w = _mm(vwt_ref[kv, pl.ds(start, WIN_LEN)], p.astype(BF16)) / jnp.sum(p, axis=0, keepdims=True)

        gates = jax.nn.sigmoid(gt_ref[g]).T
        for h in range(NSA_HPG):
            sl = slice(h * QB, (h + 1) * QB)
            o_t = (gates[3 * h:3 * h + 1, :] * o_c[:, sl] + gates[3 * h + 1:3 * h + 2, :] * o_s[:, sl]
                   + gates[3 * h + 2:3 * h + 3, :] * o_w[:, sl])
            o_ref[:, g * gw + h * 128:g * gw + (h + 1) * 128] = o_t.T

    for cb in range(n_buckets):
        @pl.when(i // (SEL_BUCKET // QB) == cb)
        def _(cb=cb):
            n_keys = (cb + 1) * SEL_BUCKET
            qcol = i * QB + lax.broadcasted_iota(jnp.int32, (n_keys, QB), 1)
            causal = lax.broadcasted_iota(jnp.int32, (n_keys, QB), 0) <= qcol
            for g, (qst, o_c, sel_bias) in enumerate(fronts):
                kv = slice(g * NSA_D, (g + 1) * NSA_D)
                bias = jnp.where(causal, _mm(expt_ref[0:n_keys, :], sel_bias), NEG)
                sc = _mm_nt(ks_ref[0:n_keys, kv], qst) + tile4(bias)
                p = jnp.exp2(sc - jnp.max(sc, axis=0, keepdims=True))
                o_s = _mm(vst_ref[kv, 0:n_keys], p.astype(BF16)) / jnp.sum(p, axis=0, keepdims=True)
                window_and_mix(g, qst, o_c, o_s)


def _nsa_attn(q, kcmp, vcmp_t, ks, kw, vt, gt, ovt, expand_t):
    b, s, _ = q.shape
    ncp = kcmp.shape[2]
    kv_w = NSA_G * NSA_D
    row = pl.BlockSpec((None, QB, NSA_HEADS * NSA_D), lambda bi, i: (bi, i, 0))
    slab = pl.BlockSpec((None, s, kv_w), lambda bi, i: (bi, 0, 0))
    return pl.pallas_call(
        functools.partial(_nsa_body, n_buckets=s // SEL_BUCKET),
        out_shape=jax.ShapeDtypeStruct((b, s, NSA_HEADS * NSA_D), F32),
        grid=(b, s // QB),
        in_specs=[row,
                  pl.BlockSpec((None, NSA_G, ncp, NSA_D), lambda bi, i: (bi, 0, 0, 0)),
                  pl.BlockSpec((None, NSA_G, NSA_D, ncp), lambda bi, i: (bi, 0, 0, 0)),
                  slab, pl.BlockSpec((kv_w, s), lambda bi, i: (0, bi)),
                  slab, pl.BlockSpec((kv_w, s), lambda bi, i: (1, bi)),
                  pl.BlockSpec((NSA_G, None, QB, 128), lambda bi, i: (0, bi, i, 0)),
                  _const_spec(ovt.shape), _const_spec(expand_t.shape)],
        out_specs=row,
        compiler_params=_params(("parallel", "arbitrary")), name="nsa_attn",
    )(q, kcmp, vcmp_t, ks, vt, kw, vt, gt, ovt, expand_t)


def _post_body(ya_ref, yb_ref, h_ref, gya_ref, gyb_ref, wo_ref, gx_ref, wxq_ref, mk_ref, mv_ref, wxo_ref,
               gf_ref, wrh_ref, wrl_ref, br_ref, h2_ref, xn_ref, eid_ref, gate_ref, cnt_ref, cnt_scr):
    half = wo_ref.shape[0] // 2
    h1 = (h_ref[...] + _mm(_rms(ya_ref[...], gya_ref[...]).astype(BF16), wo_ref[:half, :])
          + _mm(_rms(yb_ref[...], gyb_ref[...]).astype(BF16), wo_ref[half:, :]))

    q = (_mm(_rms(h1, gx_ref[...]).astype(BF16), wxq_ref[...]) * SCALE_X).astype(BF16)
    outs = []
    for hd in range(XH):
        sl = slice(hd * XD, (hd + 1) * XD)
        s = _mm_nt(q[:, sl], mk_ref[:, sl])
        p = jnp.exp2(s - jnp.max(s, axis=1, keepdims=True))
        outs.append(_mm(p.astype(BF16), mv_ref[:, sl]) / jnp.sum(p, axis=1, keepdims=True))
    h2 = h1 + _mm(jnp.concatenate(outs, axis=1).astype(BF16), wxo_ref[...])
    h2_ref[...] = h2

    xf = _rms(h2, gf_ref[...])
    _store_packed(xn_ref, xf)
    x_hi = xf.astype(BF16)
    x_lo = (xf - x_hi.astype(F32)).astype(BF16)
    logits = _mm(x_hi, wrh_ref[...]) + _mm(x_lo, wrh_ref[...]) + _mm(x_hi, wrl_ref[...]) + br_ref[...]
    lane = lax.broadcasted_iota(jnp.int32, logits.shape, 1)
    lane_f = lane.astype(F32)

    def argmax_first(x):
        v = jnp.max(x, axis=1, keepdims=True)
        return v, jnp.min(jnp.where(x == v, lane_f, 1e9), axis=1, keepdims=True)

    is_grp = lane < N_GROUPS
    lg = jnp.where(is_grp, logits, NEG)
    mx, grp = argmax_first(lg)
    p_top = 1.0 / jnp.sum(jnp.where(is_grp, jnp.exp(lg - mx), 0.0), axis=1, keepdims=True)
    in_grp = (lane >= N_GROUPS) & (lane < N_GROUPS + N_EXPERTS) & (
        ((lane - N_GROUPS) // EXP_PER_GROUP).astype(F32) == grp)
    le = jnp.where(in_grp, logits, NEG)
    v1, i1 = argmax_first(le)
    v2, i2 = argmax_first(jnp.where(lane_f == i1, NEG, le))
    t = jnp.exp(v2 - v1)
    g1 = p_top / (1.0 + t)
    g2 = p_top * t / (1.0 + t)
    gate_ref[...] = jnp.where(lane == 0, g1, jnp.where(lane == 1, g2, 0.0))

    e1 = i1 - N_GROUPS
    e2 = i2 - N_GROUPS
    oh1 = jnp.where(lane_f == e1, 1.0, 0.0)
    oh2 = jnp.where(lane_f == e2, 1.0, 0.0)
    both = oh1 + oh2
    n_tok = both.shape[0]
    earlier = jnp.where(lax.broadcasted_iota(jnp.int32, (n_tok, n_tok), 1)
                        < lax.broadcasted_iota(jnp.int32, (n_tok, n_tok), 0), 1.0, 0.0).astype(BF16)

    @pl.when((pl.program_id(0) == 0) & (pl.program_id(1) == 0))
    def _():
        cnt_scr[...] = jnp.zeros_like(cnt_scr)

    base = cnt_scr[...] + _mm(earlier, both.astype(BF16))
    r1 = jnp.sum(oh1 * base, axis=1, keepdims=True)
    r2 = jnp.sum(oh2 * base, axis=1, keepdims=True)
    cnt_scr[...] += jnp.sum(both, axis=0, keepdims=True)
    cnt_ref[...] = cnt_scr[...].astype(jnp.int32)
    eid_ref[...] = jnp.where(lane == 0, e1, jnp.where(lane == 1, e2, jnp.where(lane == 2, r1, jnp.where(
        lane == 3, r2, 0.0)))).astype(jnp.int32)


def _post_mixer(ya, yb, h, gya, gyb, wo, gx, wxq, memkv, layer, wxo, gf, wrh, wrl, br, tm=512):
    b, s, _ = ya.shape
    row = lambda n: pl.BlockSpec((None, tm, n), lambda bi, i: (bi, i, 0))
    hw = XH * XD
    in_specs = [row(1024), row(1024), row(D_MODEL), _const_spec(gya.shape), _const_spec(gyb.shape),
                _const_spec(wo.shape), _const_spec(gx.shape), _const_spec(wxq.shape),
                pl.BlockSpec((None, MEM_LEN, hw), lambda bi, i: (bi, 0, 2 * layer)),
                pl.BlockSpec((None, MEM_LEN, hw), lambda bi, i: (bi, 0, 2 * layer + 1)),
                _const_spec(wxo.shape), _const_spec(gf.shape), _const_spec(wrh.shape), _const_spec(wrl.shape),
                _const_spec(br.shape)]
    out_shape = [jax.ShapeDtypeStruct((b, s, D_MODEL), F32),
                 jax.ShapeDtypeStruct((b * s * PACK_ROWS, 128), jnp.uint32),
                 jax.ShapeDtypeStruct((b, s, 128), jnp.int32), jax.ShapeDtypeStruct((b, s, 128), F32),
                 jax.ShapeDtypeStruct((1, 128), jnp.int32)]
    slabs = pl.BlockSpec((tm * PACK_ROWS, 128), lambda bi, i: (bi * (s // tm) + i, 0))
    return pl.pallas_call(
        _post_body, out_shape=out_shape, grid=(b, s // tm), in_specs=in_specs,
        out_specs=[row(D_MODEL), slabs, row(128), row(128), pl.BlockSpec((1, 128), lambda bi, i: (0, 0))],
        scratch_shapes=[pltpu.VMEM((1, 128), F32)],
        compiler_params=_params(("arbitrary", "arbitrary")), name="post_mixer",
    )(ya, yb, h, gya, gyb, wo, gx, wxq, memkv, memkv, wxo, gf, wrh, wrl, br)


def _memkv_body(m_ref, g_ref, w_ref, o_ref):
    o_ref[...] = _mm(_rms(m_ref[...], g_ref[...]).astype(BF16), w_ref[...]).astype(BF16)


def _memkv(mem, g, w, tm=256):
    t = mem.shape[0]
    n = w.shape[1]
    return pl.pallas_call(
        _memkv_body, out_shape=jax.ShapeDtypeStruct((t, n), BF16), grid=(t // tm,),
        in_specs=[pl.BlockSpec((tm, D_MODEL), lambda i: (i, 0)), _const_spec(g.shape), _const_spec(w.shape)],
        out_specs=pl.BlockSpec((tm, n), lambda i: (i, 0)),
        compiler_params=_params(("parallel",)), name="memkv",
    )(mem, g, w)


SCATTER_TOKENS = 512


def _scatter_body(dest_ref, lastblk_ref, nu_ref, xn_ref, xs_ref, zbuf, sem, zsem, *, n_steps, n_blocks):
    i = pl.program_id(0)
    blk_rows = MOE_BLOCK * PACK_ROWS

    def zero_copy(blk):
        return pltpu.make_async_copy(zbuf, xs_ref.at[pl.ds(blk * blk_rows, blk_rows), :], zsem)

    def zero_fill(go):
        for e in range(N_EXPERTS):
            @pl.when(lastblk_ref[e] >= 0)
            def _(e=e):
                go(zero_copy(lastblk_ref[e]))
        for blk in range(n_blocks):
            @pl.when(blk >= nu_ref[0])
            def _(blk=blk):
                go(zero_copy(blk))

    @pl.when(i == 0)
    def _():
        zbuf[...] = jnp.zeros_like(zbuf)
        zero_fill(lambda c: c.start())
        zero_fill(lambda c: c.wait())

    def row_copy(j):
        d0 = pl.multiple_of(dest_ref[i * (SCATTER_TOKENS * TOP_K) + j] * PACK_ROWS, PACK_ROWS)
        return pltpu.make_async_copy(xn_ref.at[pl.ds((j // TOP_K) * PACK_ROWS, PACK_ROWS), :],
                                     xs_ref.at[pl.ds(d0, PACK_ROWS), :], sem)

    for j in range(SCATTER_TOKENS * TOP_K):
        row_copy(j).start(priority=j % TOP_K)
    for j in range(SCATTER_TOKENS * TOP_K):
        row_copy(j).wait()


def _scatter_rows(dest, last_block, n_used, xn, n_slots):
    t = xn.shape[0] // PACK_ROWS
    n_steps = t // SCATTER_TOKENS
    return pl.pallas_call(
        functools.partial(_scatter_body, n_steps=n_steps, n_blocks=n_slots // MOE_BLOCK),
        out_shape=jax.ShapeDtypeStruct((n_slots * PACK_ROWS, 128), xn.dtype),
        grid_spec=pltpu.PrefetchScalarGridSpec(
            num_scalar_prefetch=3, grid=(n_steps,),
            in_specs=[pl.BlockSpec((SCATTER_TOKENS * PACK_ROWS, 128), lambda i, d, lb, nu: (i, 0))],
            out_specs=pl.BlockSpec(memory_space=pl.ANY),
            scratch_shapes=[pltpu.VMEM((MOE_BLOCK * PACK_ROWS, 128), xn.dtype), pltpu.SemaphoreType.DMA(()),
                            pltpu.SemaphoreType.DMA(())]),
        compiler_params=_params(("arbitrary",)), name="scatter_rows",
    )(dest, last_block, n_used, xn)


def _experts_body(be_ref, bf_ref, ws_ref, nx_ref, nu_ref, x_ref, w1_ref, w3_ref, w2_ref, y_ref,
                  w1f, w3f, w2f, w1b, w3b, w2b, sem_w, *, layer):
    b = pl.program_id(0)
    n_used = nu_ref[0]

    def w_copies(e, ws):
        return [pltpu.make_async_copy(w_ref.at[layer, e], wf.at[ws], sem_w.at[ws])
                for w_ref, wf in ((w1_ref, w1f), (w3_ref, w3f), (w2_ref, w2f))]

    @pl.when(b == 0)
    def _():
        for c in w_copies(be_ref[0], 0):
            c.start()

    @pl.when(b >= n_used)
    def _():
        y_ref[...] = jnp.zeros_like(y_ref)

    @pl.when(b < n_used)
    def _():
        @pl.when(bf_ref[b] == 1)
        def _():
            ws = ws_ref[b]

            @pl.when(nx_ref[b] >= 0)
            def _():
                for c in w_copies(nx_ref[b], 1 - ws):
                    c.start()

            for c in w_copies(be_ref[b], ws):
                c.wait()
            w1b[...] = w1f[ws].astype(BF16)
            w3b[...] = w3f[ws].astype(BF16)
            w2b[...] = w2f[ws].astype(BF16)

        x = _load_packed(x_ref, MOE_BLOCK).astype(BF16)
        hb = (jax.nn.silu(_mm(x, w1b[...])) * _mm(x, w3b[...])).astype(BF16)
        _store_packed(y_ref, _mm(hb, w2b[...]))


def _experts(blk_expert, blk_first, blk_wslot, nxt_expert, n_used, xs, w_e1, w_e3, w_e2, layer):
    d = D_MODEL
    n_blocks = xs.shape[0] // (MOE_BLOCK * PACK_ROWS)
    hbm = pl.BlockSpec(memory_space=pl.ANY)
    x_spec = pl.BlockSpec((MOE_BLOCK * PACK_ROWS, 128), lambda b, be, bf, ws, nx, nu: (jnp.minimum(b, nu[0] - 1), 0))
    y_spec = pl.BlockSpec((MOE_BLOCK * PACK_ROWS, 128), lambda b, be, bf, ws, nx, nu: (b, 0))
    return pl.pallas_call(
        functools.partial(_experts_body, layer=layer),
        out_shape=jax.ShapeDtypeStruct(xs.shape, xs.dtype),
        grid_spec=pltpu.PrefetchScalarGridSpec(
            num_scalar_prefetch=5, grid=(n_blocks,),
            in_specs=[x_spec, hbm, hbm, hbm], out_specs=y_spec,
            scratch_shapes=[pltpu.VMEM((2, d, D_EXPERT), F32), pltpu.VMEM((2, d, D_EXPERT), F32),
                            pltpu.VMEM((2, D_EXPERT, d), F32),
                            pltpu.VMEM((d, D_EXPERT), BF16), pltpu.VMEM((d, D_EXPERT), BF16),
                            pltpu.VMEM((D_EXPERT, d), BF16), pltpu.SemaphoreType.DMA((2,))]),
        compiler_params=_params(("arbitrary",)), name="experts",
    )(blk_expert, blk_first, blk_wslot, nxt_expert, n_used, xs, w_e1, w_e3, w_e2)


COMBINE_TOKENS = 128


def _combine_body(dest_ref, ys_ref, h_ref, gate_ref, gfin_ref, o_ref, abuf, bbuf, sem, *, n_steps, final_norm):
    i = pl.program_id(0)
    slot = i % 2

    def copies(step, sl, j):
        a = (step * COMBINE_TOKENS + j) * TOP_K
        out = []
        for k, buf in enumerate((abuf, bbuf)):
            s0 = pl.multiple_of(dest_ref[a + k] * PACK_ROWS, PACK_ROWS)
            out.append(pltpu.make_async_copy(ys_ref.at[pl.ds(s0, PACK_ROWS), :],
                                             buf.at[sl, pl.ds(j * PACK_ROWS, PACK_ROWS), :], sem.at[sl]))
        return out

    def start(step, sl):
        for j in range(COMBINE_TOKENS):
            for k, c in enumerate(copies(step, sl, j)):
                c.start(priority=k)

    @pl.when(i == 0)
    def _():
        start(0, 0)

    @pl.when(i + 1 < n_steps)
    def _():
        start(i + 1, 1 - slot)

    for j in range(COMBINE_TOKENS):
        for c in copies(i, slot, j):
            c.wait()
    g = gate_ref[...]
    y = (h_ref[...] + g[:, 0:1] * _load_packed(abuf.at[slot], COMBINE_TOKENS)
         + g[:, 1:2] * _load_packed(bbuf.at[slot], COMBINE_TOKENS))
    if final_norm:
        y = _rms(y, gfin_ref[...])
    o_ref[...] = y


def _combine(dest, ys, h, gate, gfin, final_norm):
    t, d = h.shape
    tm = COMBINE_TOKENS
    n_steps = t // tm
    buf = pltpu.VMEM((2, tm * PACK_ROWS, 128), ys.dtype)
    return pl.pallas_call(
        functools.partial(_combine_body, n_steps=n_steps, final_norm=final_norm),
        out_shape=jax.ShapeDtypeStruct((t, d), F32),
        grid_spec=pltpu.PrefetchScalarGridSpec(
            num_scalar_prefetch=1, grid=(n_steps,),
            in_specs=[pl.BlockSpec(memory_space=pl.ANY), pl.BlockSpec((tm, d), lambda i, dr: (i, 0)),
                      pl.BlockSpec((tm, 128), lambda i, dr: (i, 0)), pl.BlockSpec((1, d), lambda i, dr: (0, 0))],
            out_specs=pl.BlockSpec((tm, d), lambda i, dr: (i, 0)),
            scratch_shapes=[buf, buf, pltpu.SemaphoreType.DMA((2,))]),
        compiler_params=_params(("arbitrary",)), name="combine",
    )(dest, ys, h, gate, gfin)


def _dispatch(eid, rank, counts):
    n_assign = eid.shape[0] * TOP_K
    n_blocks = (n_assign + N_EXPERTS * (MOE_BLOCK - 1) + MOE_BLOCK - 1) // MOE_BLOCK
    ids = jnp.arange(N_EXPERTS, dtype=jnp.int32)
    blocks = (counts + MOE_BLOCK - 1) // MOE_BLOCK
    blk_end = jnp.cumsum(blocks)
    blk_begin = blk_end - blocks
    onehot = (eid.reshape(-1)[:, None] == ids[None, :]).astype(F32)
    first_blk = jnp.dot(onehot, blk_begin.astype(F32)).astype(jnp.int32)
    dest = first_blk * MOE_BLOCK + rank.reshape(-1)
    blk_ids = jnp.arange(n_blocks, dtype=jnp.int32)
    blk_expert = jnp.minimum(jnp.sum((blk_end[None, :] <= blk_ids[:, None]).astype(jnp.int32), axis=1), N_EXPERTS - 1)
    blk_first = jnp.concatenate([jnp.ones((1,), jnp.int32), (blk_expert[1:] != blk_expert[:-1]).astype(jnp.int32)])
    n_used = blk_end[-1:].astype(jnp.int32)
    has_rows = counts > 0
    last_block = jnp.where(has_rows, blk_end - 1, -1).astype(jnp.int32)
    wslot = ((jnp.cumsum(has_rows.astype(jnp.int32)) - 1) % 2).astype(jnp.int32)
    later = jnp.where(has_rows[None, :] & (ids[None, :] > ids[:, None]), ids[None, :], N_EXPERTS)
    nxt = jnp.min(later, axis=1)
    nxt = jnp.where(nxt < N_EXPERTS, nxt, -1).astype(jnp.int32)
    return dest, last_block, (blk_expert, blk_first, wslot[blk_expert], nxt[blk_expert], n_used), n_blocks * MOE_BLOCK


def _rope_tables(positions):
    pos = positions.reshape(-1).astype(F32)[:, None]

    def tables(rot, passthrough):
        half = rot // 2
        ang = pos * (ROPE_THETA ** (-jnp.arange(0, rot, 2, dtype=F32) / rot))
        cos, sin = jnp.cos(ang), jnp.sin(ang)
        z = lambda n: jnp.zeros((pos.shape[0], n), F32)
        tail = jnp.full((pos.shape[0], 128 - rot), passthrough, F32)
        return [jnp.concatenate([cos, cos, tail], 1), jnp.concatenate([z(half), sin, z(128 - rot)], 1),
                jnp.concatenate([-sin, z(128 - half)], 1)]

    return jnp.concatenate(tables(MLA_ROPE, 0.0) + tables(NSA_ROT, 1.0), axis=1)


def _selection_constants(s):
    ncp = s // CMP_STRIDE
    ns = s // SEL_BLOCK
    c_start = np.arange(ncp)[None, :] * CMP_STRIDE
    sel_start = np.arange(ns)[:, None] * SEL_BLOCK
    ovt = (c_start <= sel_start + SEL_BLOCK - 1) & (c_start + CMP_LEN - 1 >= sel_start)
    ovt[:, ncp - 1] = False
    expand_t = np.arange(s)[:, None] // SEL_BLOCK == np.arange(ns)[None, :]
    return jnp.asarray(ovt, BF16), jnp.asarray(expand_t, BF16)


def _arrange_w_in(w):
    cuts = np.cumsum([0, 512, 256, 64, 1024, 256, 256, 256, 256, 256, 256, 24])
    c_q, c_kv, k_r, q_n, k_c, v_c, k_s, v_s, k_w, v_w, gates = [w[:, cuts[i]:cuts[i + 1]] for i in range(11)]
    z = lambda n: jnp.zeros((w.shape[0], n), w.dtype)
    w1 = jnp.concatenate([c_q, c_kv, k_r, z(64), q_n, k_c, k_s, k_w, v_c, gates, z(104)], axis=1).astype(BF16)
    return w1, jnp.concatenate([v_s, v_w], axis=1).T.astype(BF16)


def _arrange_w_uq(w):
    w = w.reshape(MLA_Q_RANK, MLA_HEADS, MLA_NOPE + MLA_ROPE)
    w = jnp.pad(w, ((0, 0), (0, 0), (0, MLA_QK_PAD - MLA_NOPE - MLA_ROPE)))
    return w.reshape(MLA_Q_RANK, MLA_HEADS * MLA_QK_PAD).astype(BF16)


def _split_w_ukv(w):
    w = w.reshape(MLA_KV_RANK, MLA_HEADS, MLA_NOPE + MLA_V)
    return (w[:, :, :MLA_NOPE].reshape(MLA_KV_RANK, -1).astype(BF16),
            w[:, :, MLA_NOPE:].reshape(MLA_KV_RANK, -1).T.astype(BF16))


def _router_weights(w_grp, b_grp, w_exp, b_exp):
    pad = 128 - N_GROUPS - N_EXPERTS
    w = jnp.concatenate([w_grp, w_exp, jnp.zeros((w_grp.shape[0], pad), F32)], axis=1)
    hi = w.astype(BF16)
    lo = (w - hi.astype(F32)).astype(BF16)
    b = jnp.concatenate([b_grp, b_exp, jnp.zeros((pad,), F32)])[None, :]
    return hi, lo, b


def kernel(x, mem, positions, g_mem, g_final, g_mix, w_in, g_cq, w_uq, g_ckv, w_ukv, pe_k, pe_v, w_ck1, w_ck2,
           w_cv1, w_cv2, g_ya, g_yb, w_out, g_x, w_xq, w_xkv, w_xo, g_ffn, w_grp, b_grp, w_exp, b_exp,
           w_e1, w_e3, w_e2):
    b, s, d = x.shape
    depth = w_in.shape[0]
    t = b * s
    ncp = s // CMP_STRIDE
    tab = _rope_tables(positions)
    ovt, expand_t = _selection_constants(s)
    r2 = lambda v: v.reshape(1, -1)

    w_kv_all = jnp.concatenate([w_xkv[l] for l in range(depth)], axis=1).astype(BF16)
    memkv = _memkv(mem.reshape(-1, d), r2(g_mem), w_kv_all).reshape(b, mem.shape[1], -1)

    h = x.reshape(t, d)
    for l in range(depth):
        wkn, wv_t = _split_w_ukv(w_ukv[l])
        w1, wvt = _arrange_w_in(w_in[l])
        qa, ka, va_t, qn, kc, vc, ks, kw, vt, gt = _in_proj(
            h, r2(g_mix[l]), w1, wvt, r2(g_cq[l]), r2(g_ckv[l]), _arrange_w_uq(w_uq[l]), wkn, wv_t, tab)
        kcmp, vcmp_t = _compress(
            kc.reshape(NSA_G, b, ncp, CMP_STRIDE * NSA_D), vc.reshape(NSA_G, b, ncp, CMP_STRIDE * NSA_D),
            pe_k[l].reshape(1, -1), pe_v[l].reshape(1, -1), w_ck1[l].astype(BF16), w_ck2[l].astype(BF16),
            w_cv1[l].astype(BF16), w_cv2[l].astype(BF16))
        r3 = lambda a: a.reshape(b, s, -1)
        ya = _mla_attn(r3(qa), r3(ka), va_t)
        yb = _nsa_attn(r3(qn), kcmp, vcmp_t, r3(ks), r3(kw), vt, gt.reshape(NSA_G, b, s, 128), ovt, expand_t)
        wrh, wrl, br = _router_weights(w_grp[l], b_grp[l], w_exp[l], b_exp[l])
        h2, xn, eid, gate, counts = _post_mixer(
            ya, yb, r3(h), r2(g_ya[l]), r2(g_yb[l]), w_out[l].astype(BF16), r2(g_x[l]), w_xq[l].astype(BF16),
            memkv, l, w_xo[l].astype(BF16), r2(g_ffn[l]), wrh, wrl, br)
        eid = eid.reshape(t, 128)
        dest, last_block, blocks, n_slots = _dispatch(
            eid[:, :TOP_K], eid[:, TOP_K:2 * TOP_K], counts[0, :N_EXPERTS])
        xs = _scatter_rows(dest, last_block, blocks[-1], xn, n_slots)
        ys = _experts(*blocks, xs, w_e1, w_e3, w_e2, l)
        h = _combine(dest, ys, h2.reshape(t, d), gate.reshape(t, 128), r2(g_final), final_norm=(l == depth - 1))
    return h.reshape(b, s, d)
```

```python
import functools

import jax
import jax.numpy as jnp
import numpy as np
from jax import lax
from jax.experimental import pallas as pl
from jax.experimental.pallas import tpu as pltpu

F32 = jnp.float32
BF16 = jnp.bfloat16

D_MODEL = 2048
RMS_EPS = 1e-6
ROPE_THETA = 500000.0
NEG = -1e30
FORCE_SCORE = 1e4

MLA_HEADS = 8
MLA_NOPE = 128
MLA_ROPE = 64
MLA_V = 128
MLA_Q_RANK = D_MODEL // 4
MLA_KV_RANK = D_MODEL // 8
MLA_QK_PAD = 256

NSA_HEADS = 8
NSA_G = 2
NSA_HPG = 4
NSA_D = 128
NSA_ROT = 32
CMP_LEN = 32
CMP_STRIDE = 16
CMP_HID = 512
SEL_BLOCK = 64
SEL_TOPN = 8
WINDOW = 512
QB = 128

XH = 4
XD = 128
MEM_LEN = 256

N_GROUPS = 8
EXP_PER_GROUP = 8
N_EXPERTS = 64
TOP_K = 2
D_EXPERT = D_MODEL // 4
MOE_BLOCK = 128

VMEM_LIMIT_BYTES = 56 * 1024 * 1024

C_CQ = 0
C_CKV = 512
C_KR = 768
C_QN = 896
C_KK = 1920
C_VC = 2688
C_GT = 2944
N_IN = 3072

LOG2E = 1.4426950408889634
SCALE_A = (MLA_NOPE + MLA_ROPE) ** -0.5 * LOG2E
SCALE_N = NSA_D ** -0.5 * LOG2E
SCALE_X = XD ** -0.5 * LOG2E


def _params(sem):
    return pltpu.CompilerParams(dimension_semantics=sem, vmem_limit_bytes=VMEM_LIMIT_BYTES)


def _mm(a, b):
    return jnp.dot(a, b, preferred_element_type=F32)


def _mm_nt(a, b):
    return lax.dot_general(a, b, (((1,), (1,)), ((), ())), preferred_element_type=F32)


def _rms(x, g):
    return x * lax.rsqrt(jnp.mean(x * x, axis=-1, keepdims=True) + RMS_EPS) * g


def _rope(x, c, sa, sb, half):
    return x * c + pltpu.roll(x, half, 1) * sa + pltpu.roll(x, 128 - half, 1) * sb


PACK_ROWS = D_MODEL // 128 // 2
HI_HALF = 0xFFFF0000


def _store_packed(ref, x):
    bits = pltpu.bitcast(x.astype(BF16).astype(F32), jnp.uint32)
    for c in range(PACK_ROWS):
        lo = bits[:, c * 128:(c + 1) * 128] >> 16
        hi = bits[:, (c + PACK_ROWS) * 128:(c + PACK_ROWS + 1) * 128] & jnp.uint32(HI_HALF)
        ref[pl.ds(c, x.shape[0], stride=PACK_ROWS), :] = lo | hi


def _load_packed(ref, n_tok):
    words = [ref[pl.ds(c, n_tok, stride=PACK_ROWS), :] for c in range(PACK_ROWS)]
    chunks = ([pltpu.bitcast(w << 16, F32) for w in words]
              + [pltpu.bitcast(w & jnp.uint32(HI_HALF), F32) for w in words])
    return jnp.concatenate(chunks, axis=1)


def _const_spec(shape):
    nd = len(shape)
    return pl.BlockSpec(shape, lambda *_: (0,) * nd, pipeline_mode=pl.Buffered(1))


def _in_proj_body(h_ref, gmix_ref, w1_ref, wvt_ref, gcq_ref, gckv_ref, wq_ref, wkn_ref, wv_ref, tab_ref,
                  qa_ref, ka_ref, va_ref, qn_ref, kc_ref, vc_ref, ks_ref, kw_ref, vt_ref, gt_ref):
    xn = _rms(h_ref[...], gmix_ref[...]).astype(BF16)
    tab = tab_ref[...]
    ca, saa, sba, cn, san, sbn = [tab[:, i * 128:(i + 1) * 128] for i in range(6)]

    xq = _rms(_mm(xn, w1_ref[:, C_CQ:C_CKV]), gcq_ref[...]).astype(BF16)
    ckr = _mm(xn, w1_ref[:, C_CKV:C_QN])
    xkv = _rms(ckr[:, :MLA_KV_RANK], gckv_ref[...]).astype(BF16)
    kr = _rope(ckr[:, MLA_KV_RANK:], ca, saa, sba, MLA_ROPE // 2).astype(BF16)
    q = _mm(xq, wq_ref[...])
    kn = _mm(xkv, wkn_ref[...])
    for h in range(MLA_HEADS):
        lo = h * MLA_QK_PAD
        qa_ref[:, lo:lo + 128] = (q[:, lo:lo + 128] * SCALE_A).astype(BF16)
        qa_ref[:, lo + 128:lo + 256] = (
            _rope(q[:, lo + 128:lo + 256], ca, saa, sba, MLA_ROPE // 2) * SCALE_A).astype(BF16)
        ka_ref[:, lo:lo + 128] = kn[:, h * 128:(h + 1) * 128].astype(BF16)
        ka_ref[:, lo + 128:lo + 256] = kr
    va_ref[...] = _mm_nt(wv_ref[...], xkv).astype(BF16)

    qn = _mm(xn, w1_ref[:, C_QN:C_KK])
    for h in range(NSA_HEADS):
        sl = slice(h * 128, (h + 1) * 128)
        qn_ref[:, sl] = (_rope(qn[:, sl], cn, san, sbn, NSA_ROT // 2) * SCALE_N).astype(BF16)
    kk = _mm(xn, w1_ref[:, C_KK:C_VC])
    vc = _mm(xn, w1_ref[:, C_VC:C_GT])
    for g in range(NSA_G):
        sl = slice(g * 128, (g + 1) * 128)
        kc_ref[g] = _rope(kk[:, sl], cn, san, sbn, NSA_ROT // 2)
        ks_ref[:, sl] = _rope(kk[:, 256 + g * 128:256 + (g + 1) * 128], cn, san, sbn, NSA_ROT // 2).astype(BF16)
        kw_ref[:, sl] = _rope(kk[:, 512 + g * 128:512 + (g + 1) * 128], cn, san, sbn, NSA_ROT // 2).astype(BF16)
        vc_ref[g] = vc[:, sl]
    vt_ref[...] = _mm_nt(wvt_ref[...], xn).astype(BF16)
    gl = _mm(xn, w1_ref[:, C_GT:N_IN])
    gt_ref[0] = gl
    gt_ref[1] = pltpu.roll(gl, 128 - NSA_HPG * 3, 1)


def _in_proj(h, gmix, w1, wvt, gcq, gckv, wq, wkn, wv, tab, tm=512):
    t = h.shape[0]
    row = lambda n: pl.BlockSpec((tm, n), lambda i: (i, 0))
    col = lambda n: pl.BlockSpec((n, tm), lambda i: (0, i))
    grp = pl.BlockSpec((NSA_G, tm, 128), lambda i: (0, i, 0))
    out_shape = [
        jax.ShapeDtypeStruct((t, MLA_HEADS * MLA_QK_PAD), BF16),
        jax.ShapeDtypeStruct((t, MLA_HEADS * MLA_QK_PAD), BF16),
        jax.ShapeDtypeStruct((MLA_HEADS * MLA_V, t), BF16),
        jax.ShapeDtypeStruct((t, NSA_HEADS * NSA_D), BF16),
        jax.ShapeDtypeStruct((NSA_G, t, NSA_D), F32),
        jax.ShapeDtypeStruct((NSA_G, t, NSA_D), F32),
        jax.ShapeDtypeStruct((t, NSA_G * NSA_D), BF16),
        jax.ShapeDtypeStruct((t, NSA_G * NSA_D), BF16),
        jax.ShapeDtypeStruct((2 * NSA_G * NSA_D, t), BF16),
        jax.ShapeDtypeStruct((NSA_G, t, 128), F32),
    ]
    out_specs = [row(2048), row(2048), col(1024), row(1024), grp, grp, row(256), row(256), col(512), grp]
    in_specs = [row(D_MODEL), _const_spec((1, D_MODEL)), _const_spec((D_MODEL, N_IN)),
                _const_spec((2 * NSA_G * NSA_D, D_MODEL)),
                _const_spec((1, MLA_Q_RANK)), _const_spec((1, MLA_KV_RANK)),
                _const_spec((MLA_Q_RANK, MLA_HEADS * MLA_QK_PAD)), _const_spec((MLA_KV_RANK, MLA_HEADS * 128)),
                _const_spec((MLA_HEADS * 128, MLA_KV_RANK)), row(6 * 128)]
    return pl.pallas_call(
        _in_proj_body, out_shape=out_shape, grid=(t // tm,), in_specs=in_specs, out_specs=out_specs,
        compiler_params=_params(("parallel",)), name="in_proj",
    )(h, gmix, w1, wvt, gcq, gckv, wq, wkn, wv, tab)


def _compress_body(k_ref, v_ref, pek_ref, pev_ref, wk1_ref, wk2_ref, wv1_ref, wv2_ref, ko_ref, vo_ref):
    half = CMP_STRIDE * NSA_D

    def phi(x, pe, w1_ref, w2_ref):
        a = _mm((x + pe[:, :half]).astype(BF16), w1_ref[:half, :])
        b = _mm((x + pe[:, half:]).astype(BF16), w1_ref[half:, :])
        pre = a + pltpu.roll(b, b.shape[0] - 1, 0)
        return _mm(jax.nn.gelu(pre).astype(BF16), w2_ref[...])

    ko_ref[...] = phi(k_ref[...], pek_ref[...], wk1_ref, wk2_ref).astype(BF16)
    vo_ref[...] = phi(v_ref[...], pev_ref[...], wv1_ref, wv2_ref).T.astype(BF16)


def _compress(kc, vc, pek, pev, wk1, wk2, wv1, wv2):
    g, b, ncp, w = kc.shape
    in_blk = pl.BlockSpec((None, None, ncp, w), lambda bi, gi: (gi, bi, 0, 0))
    return pl.pallas_call(
        _compress_body,
        out_shape=[jax.ShapeDtypeStruct((b, g, ncp, NSA_D), BF16), jax.ShapeDtypeStruct((b, g, NSA_D, ncp), BF16)],
        grid=(b, g),
        in_specs=[in_blk, in_blk, _const_spec(pek.shape), _const_spec(pev.shape), _const_spec(wk1.shape),
                  _const_spec(wk2.shape), _const_spec(wv1.shape), _const_spec(wv2.shape)],
        out_specs=[pl.BlockSpec((None, None, ncp, NSA_D), lambda bi, gi: (bi, gi, 0, 0)),
                   pl.BlockSpec((None, None, NSA_D, ncp), lambda bi, gi: (bi, gi, 0, 0))],
        compiler_params=_params(("parallel", "parallel")), name="compress",
    )(kc, vc, pek, pev, wk1, wk2, wv1, wv2)


def _mla_body(q_ref, k_ref, vt_ref, o_ref, *, tq, n_q):
    i = pl.program_id(2)
    tri = lax.broadcasted_iota(jnp.int32, (tq, tq), 0) <= lax.broadcasted_iota(jnp.int32, (tq, tq), 1)
    for ci in range(n_q):
        @pl.when(i == ci)
        def _(ci=ci):
            lo = ci * tq
            for hh in range(MLA_HEADS_PER_STEP):
                qk = slice(hh * MLA_QK_PAD, (hh + 1) * MLA_QK_PAD)
                vr = slice(hh * MLA_V, (hh + 1) * MLA_V)
                q = q_ref[:, qk]
                sd = jnp.where(tri, _mm_nt(k_ref[lo:lo + tq, qk], q), NEG)
                m = jnp.max(sd, axis=0, keepdims=True)
                if ci > 0:
                    sf = _mm_nt(k_ref[0:lo, qk], q)
                    m = jnp.maximum(m, jnp.max(sf, axis=0, keepdims=True))
                    pf = jnp.exp2(sf - m)
                pd = jnp.exp2(sd - m)
                l = jnp.sum(pd, axis=0, keepdims=True)
                acc = _mm(vt_ref[vr, lo:lo + tq], pd.astype(BF16))
                if ci > 0:
                    l = l + jnp.sum(pf, axis=0, keepdims=True)
                    acc = acc + _mm(vt_ref[vr, 0:lo], pf.astype(BF16))
                o_ref[:, vr] = (acc / l).T


MLA_HEADS_PER_STEP = 4


def _mla_attn(q, k, vt, tq=512):
    b, s, _ = q.shape
    hp = MLA_HEADS_PER_STEP
    return pl.pallas_call(
        functools.partial(_mla_body, tq=tq, n_q=s // tq),
        out_shape=jax.ShapeDtypeStruct((b, s, MLA_HEADS * MLA_V), F32),
        grid=(b, MLA_HEADS // hp, s // tq),
        in_specs=[pl.BlockSpec((None, tq, hp * MLA_QK_PAD), lambda bi, hi, i: (bi, i, hi)),
                  pl.BlockSpec((None, s, hp * MLA_QK_PAD), lambda bi, hi, i: (bi, 0, hi)),
                  pl.BlockSpec((hp * MLA_V, s), lambda bi, hi, i: (hi, bi))],
        out_specs=pl.BlockSpec((None, tq, hp * MLA_V), lambda bi, hi, i: (bi, i, hi)),
        compiler_params=_params(("parallel", "parallel", "arbitrary")), name="mla_attn",
    )(q, k, vt)


SEL_BUCKET = 512
WIN_LEN = WINDOW + QB


def _nsa_body(q_ref, kc_ref, vct_ref, ks_ref, vst_ref, kw_ref, vwt_ref, gt_ref, ovt_ref, expt_ref, o_ref, *,
              n_buckets):
    i = pl.program_id(1)
    ns = ovt_ref.shape[0]
    ncp = ovt_ref.shape[1]
    cols = NSA_HPG * QB
    gw = NSA_HPG * NSA_D

    def tile4(x):
        return jnp.concatenate([x] * NSA_HPG, axis=1)

    def front(g):
        q = q_ref[:, g * gw:(g + 1) * gw]
        qst = jnp.concatenate([q[:, h * 128:(h + 1) * 128] for h in range(NSA_HPG)], axis=0)

        pos_c = i * QB + (lax.broadcasted_iota(jnp.int32, (ncp, cols), 1) & (QB - 1))
        cend = lax.broadcasted_iota(jnp.int32, (ncp, cols), 0) * CMP_STRIDE + (CMP_LEN - 1)
        vis = cend <= pos_c
        s = jnp.where(vis, _mm_nt(kc_ref[g], qst), NEG)
        e = jnp.where(vis, jnp.exp2(s - jnp.max(s, axis=0, keepdims=True)), 0.0)
        l = jnp.sum(e, axis=0, keepdims=True)
        p_c = e / jnp.where(l > 0.0, l, 1.0)
        o_c = _mm(vct_ref[g], p_c.astype(BF16))

        psum = p_c[:, 0:QB] + p_c[:, QB:2 * QB] + p_c[:, 2 * QB:3 * QB] + p_c[:, 3 * QB:4 * QB]
        p_hi = psum.astype(BF16)
        p_lo = (psum - p_hi.astype(F32)).astype(BF16)
        imp = _mm(ovt_ref[...], p_hi) + _mm(ovt_ref[...], p_lo)
        n_i = lax.broadcasted_iota(jnp.int32, (ns, QB), 0)
        pos_l = i * QB + lax.broadcasted_iota(jnp.int32, (ns, QB), 1)
        forced = (n_i == 0) | (n_i == pos_l // SEL_BLOCK)
        val = jnp.where(forced, FORCE_SCORE, jnp.where(n_i * SEL_BLOCK <= pos_l, imp, -FORCE_SCORE))
        rank = jnp.zeros((ns, QB), F32)
        for m_ in range(ns):
            vm = val[m_:m_ + 1, :]
            rank = rank + jnp.where((vm > val) | ((vm == val) & (n_i > m_)), 1.0, 0.0)
        sel_bias = jnp.where(rank < float(min(SEL_TOPN, ns)), 0.0, NEG).astype(BF16)
        return qst, o_c, sel_bias

    fronts = [front(g) for g in range(NSA_G)]

    def window_and_mix(g, qst, o_c, o_s):
        kv = slice(g * NSA_D, (g + 1) * NSA_D)
        start = pl.multiple_of(jnp.maximum(i * QB - WINDOW, 0), QB)
        dist = (i * QB + lax.broadcasted_iota(jnp.int32, (WIN_LEN, QB), 1)) - (
            start + lax.broadcasted_iota(jnp.int32, (WIN_LEN, QB), 0))
        ok = (dist >= 0) & (dist < WINDOW)
        s = _mm_nt(kw_ref[pl.ds(start, WIN_LEN), kv], qst) + tile4(jnp.where(ok, 0.0, NEG))
        p = jnp.exp2(s - jnp.max(s, axis=0, keepdims=True))
        o_w = _mm(vwt_ref[kv, pl.ds(start, WIN_LEN)], p.astype(BF16)) / jnp.sum(p, axis=0, keepdims=True)

        gates = jax.nn.sigmoid(gt_ref[g]).T
        for h in range(NSA_HPG):
            sl = slice(h * QB, (h + 1) * QB)
            o_t = (gates[3 * h:3 * h + 1, :] * o_c[:, sl] + gates[3 * h + 1:3 * h + 2, :] * o_s[:, sl]
                   + gates[3 * h + 2:3 * h + 3, :] * o_w[:, sl])
            o_ref[:, g * gw + h * 128:g * gw + (h + 1) * 128] = o_t.T

    for cb in range(n_buckets):
        @pl.when(i // (SEL_BUCKET // QB) == cb)
        def _(cb=cb):
            n_keys = (cb + 1) * SEL_BUCKET
            qcol = i * QB + lax.broadcasted_iota(jnp.int32, (n_keys, QB), 1)
            causal = lax.broadcasted_iota(jnp.int32, (n_keys, QB), 0) <= qcol
            for g, (qst, o_c, sel_bias) in enumerate(fronts):
                kv = slice(g * NSA_D, (g + 1) * NSA_D)
                bias = jnp.where(causal, _mm(expt_ref[0:n_keys, :], sel_bias), NEG)
                sc = _mm_nt(ks_ref[0:n_keys, kv], qst) + tile4(bias)
                p = jnp.exp2(sc - jnp.max(sc, axis=0, keepdims=True))
                o_s = _mm(vst_ref[kv, 0:n_keys], p.astype(BF16)) / jnp.sum(p, axis=0, keepdims=True)
                window_and_mix(g, qst, o_c, o_s)


def _nsa_attn(q, kcmp, vcmp_t, ks, kw, vt, gt, ovt, expand_t):
    b, s, _ = q.shape
    ncp = kcmp.shape[2]
    kv_w = NSA_G * NSA_D
    row = pl.BlockSpec((None, QB, NSA_HEADS * NSA_D), lambda bi, i: (bi, i, 0))
    slab = pl.BlockSpec((None, s, kv_w), lambda bi, i: (bi, 0, 0))
    return pl.pallas_call(
        functools.partial(_nsa_body, n_buckets=s // SEL_BUCKET),
        out_shape=jax.ShapeDtypeStruct((b, s, NSA_HEADS * NSA_D), F32),
        grid=(b, s // QB),
        in_specs=[row,
                  pl.BlockSpec((None, NSA_G, ncp, NSA_D), lambda bi, i: (bi, 0, 0, 0)),
                  pl.BlockSpec((None, NSA_G, NSA_D, ncp), lambda bi, i: (bi, 0, 0, 0)),
                  slab, pl.BlockSpec((kv_w, s), lambda bi, i: (0, bi)),
                  slab, pl.BlockSpec((kv_w, s), lambda bi, i: (1, bi)),
                  pl.BlockSpec((NSA_G, None, QB, 128), lambda bi, i: (0, bi, i, 0)),
                  _const_spec(ovt.shape), _const_spec(expand_t.shape)],
        out_specs=row,
        compiler_params=_params(("parallel", "arbitrary")), name="nsa_attn",
    )(q, kcmp, vcmp_t, ks, vt, kw, vt, gt, ovt, expand_t)


def _post_body(ya_ref, yb_ref, h_ref, gya_ref, gyb_ref, wo_ref, gx_ref, wxq_ref, mk_ref, mv_ref, wxo_ref,
               gf_ref, wrh_ref, wrl_ref, br_ref, h2_ref, xn_ref, eid_ref, gate_ref, cnt_ref, cnt_scr):
    half = wo_ref.shape[0] // 2
    h1 = (h_ref[...] + _mm(_rms(ya_ref[...], gya_ref[...]).astype(BF16), wo_ref[:half, :])
          + _mm(_rms(yb_ref[...], gyb_ref[...]).astype(BF16), wo_ref[half:, :]))

    q = (_mm(_rms(h1, gx_ref[...]).astype(BF16), wxq_ref[...]) * SCALE_X).astype(BF16)
    outs = []
    for hd in range(XH):
        sl = slice(hd * XD, (hd + 1) * XD)
        s = _mm_nt(q[:, sl], mk_ref[:, sl])
        p = jnp.exp2(s - jnp.max(s, axis=1, keepdims=True))
        outs.append(_mm(p.astype(BF16), mv_ref[:, sl]) / jnp.sum(p, axis=1, keepdims=True))
    h2 = h1 + _mm(jnp.concatenate(outs, axis=1).astype(BF16), wxo_ref[...])
    h2_ref[...] = h2

    xf = _rms(h2, gf_ref[...])
    _store_packed(xn_ref, xf)
    x_hi = xf.astype(BF16)
    x_lo = (xf - x_hi.astype(F32)).astype(BF16)
    logits = _mm(x_hi, wrh_ref[...]) + _mm(x_lo, wrh_ref[...]) + _mm(x_hi, wrl_ref[...]) + br_ref[...]
    lane = lax.broadcasted_iota(jnp.int32, logits.shape, 1)
    lane_f = lane.astype(F32)

    def argmax_first(x):
        v = jnp.max(x, axis=1, keepdims=True)
        return v, jnp.min(jnp.where(x == v, lane_f, 1e9), axis=1, keepdims=True)

    is_grp = lane < N_GROUPS
    lg = jnp.where(is_grp, logits, NEG)
    mx, grp = argmax_first(lg)
    p_top = 1.0 / jnp.sum(jnp.where(is_grp, jnp.exp(lg - mx), 0.0), axis=1, keepdims=True)
    in_grp = (lane >= N_GROUPS) & (lane < N_GROUPS + N_EXPERTS) & (
        ((lane - N_GROUPS) // EXP_PER_GROUP).astype(F32) == grp)
    le = jnp.where(in_grp, logits, NEG)
    v1, i1 = argmax_first(le)
    v2, i2 = argmax_first(jnp.where(lane_f == i1, NEG, le))
    t = jnp.exp(v2 - v1)
    g1 = p_top / (1.0 + t)
    g2 = p_top * t / (1.0 + t)
    gate_ref[...] = jnp.where(lane == 0, g1, jnp.where(lane == 1, g2, 0.0))

    e1 = i1 - N_GROUPS
    e2 = i2 - N_GROUPS
    oh1 = jnp.where(lane_f == e1, 1.0, 0.0)
    oh2 = jnp.where(lane_f == e2, 1.0, 0.0)
    both = oh1 + oh2
    n_tok = both.shape[0]
    earlier = jnp.where(lax.broadcasted_iota(jnp.int32, (n_tok, n_tok), 1)
                        < lax.broadcasted_iota(jnp.int32, (n_tok, n_tok), 0), 1.0, 0.0).astype(BF16)

    @pl.when((pl.program_id(0) == 0) & (pl.program_id(1) == 0))
    def _():
        cnt_scr[...] = jnp.zeros_like(cnt_scr)

    base = cnt_scr[...] + _mm(earlier, both.astype(BF16))
    r1 = jnp.sum(oh1 * base, axis=1, keepdims=True)
    r2 = jnp.sum(oh2 * base, axis=1, keepdims=True)
    cnt_scr[...] += jnp.sum(both, axis=0, keepdims=True)
    cnt_ref[...] = cnt_scr[...].astype(jnp.int32)
    eid_ref[...] = jnp.where(lane == 0, e1, jnp.where(lane == 1, e2, jnp.where(lane == 2, r1, jnp.where(
        lane == 3, r2, 0.0)))).astype(jnp.int32)


def _post_mixer(ya, yb, h, gya, gyb, wo, gx, wxq, memkv, layer, wxo, gf, wrh, wrl, br, tm=512):
    b, s, _ = ya.shape
    row = lambda n: pl.BlockSpec((None, tm, n), lambda bi, i: (bi, i, 0))
    hw = XH * XD
    in_specs = [row(1024), row(1024), row(D_MODEL), _const_spec(gya.shape), _const_spec(gyb.shape),
                _const_spec(wo.shape), _const_spec(gx.shape), _const_spec(wxq.shape),
                pl.BlockSpec((None, MEM_LEN, hw), lambda bi, i: (bi, 0, 2 * layer)),
                pl.BlockSpec((None, MEM_LEN, hw), lambda bi, i: (bi, 0, 2 * layer + 1)),
                _const_spec(wxo.shape), _const_spec(gf.shape), _const_spec(wrh.shape), _const_spec(wrl.shape),
                _const_spec(br.shape)]
    out_shape = [jax.ShapeDtypeStruct((b, s, D_MODEL), F32),
                 jax.ShapeDtypeStruct((b * s * PACK_ROWS, 128), jnp.uint32),
                 jax.ShapeDtypeStruct((b, s, 128), jnp.int32), jax.ShapeDtypeStruct((b, s, 128), F32),
                 jax.ShapeDtypeStruct((1, 128), jnp.int32)]
    slabs = pl.BlockSpec((tm * PACK_ROWS, 128), lambda bi, i: (bi * (s // tm) + i, 0))
    return pl.pallas_call(
        _post_body, out_shape=out_shape, grid=(b, s // tm), in_specs=in_specs,
        out_specs=[row(D_MODEL), slabs, row(128), row(128), pl.BlockSpec((1, 128), lambda bi, i: (0, 0))],
        scratch_shapes=[pltpu.VMEM((1, 128), F32)],
        compiler_params=_params(("arbitrary", "arbitrary")), name="post_mixer",
    )(ya, yb, h, gya, gyb, wo, gx, wxq, memkv, memkv, wxo, gf, wrh, wrl, br)


def _memkv_body(m_ref, g_ref, w_ref, o_ref):
    o_ref[...] = _mm(_rms(m_ref[...], g_ref[...]).astype(BF16), w_ref[...]).astype(BF16)


def _memkv(mem, g, w, tm=256):
    t = mem.shape[0]
    n = w.shape[1]
    return pl.pallas_call(
        _memkv_body, out_shape=jax.ShapeDtypeStruct((t, n), BF16), grid=(t // tm,),
        in_specs=[pl.BlockSpec((tm, D_MODEL), lambda i: (i, 0)), _const_spec(g.shape), _const_spec(w.shape)],
        out_specs=pl.BlockSpec((tm, n), lambda i: (i, 0)),
        compiler_params=_params(("parallel",)), name="memkv",
    )(mem, g, w)


SCATTER_TOKENS = 1024


def _scatter_body(dest_ref, lastblk_ref, nu_ref, xn_ref, xs_ref, zbuf, sem, zsem, *, n_steps, n_blocks):
    i = pl.program_id(0)
    blk_rows = MOE_BLOCK * PACK_ROWS

    def zero_copy(blk):
        return pltpu.make_async_copy(zbuf, xs_ref.at[pl.ds(blk * blk_rows, blk_rows), :], zsem)

    def zero_fill(go):
        for e in range(N_EXPERTS):
            @pl.when(lastblk_ref[e] >= 0)
            def _(e=e):
                go(zero_copy(lastblk_ref[e]))
        for blk in range(n_blocks):
            @pl.when(blk >= nu_ref[0])
            def _(blk=blk):
                go(zero_copy(blk))

    @pl.when(i == 0)
    def _():
        zbuf[...] = jnp.zeros_like(zbuf)
        zero_fill(lambda c: c.start())
        zero_fill(lambda c: c.wait())

    def row_copy(j):
        d0 = pl.multiple_of(dest_ref[i * (SCATTER_TOKENS * TOP_K) + j] * PACK_ROWS, PACK_ROWS)
        return pltpu.make_async_copy(xn_ref.at[pl.ds((j // TOP_K) * PACK_ROWS, PACK_ROWS), :],
                                     xs_ref.at[pl.ds(d0, PACK_ROWS), :], sem)

    for j in range(SCATTER_TOKENS * TOP_K):
        row_copy(j).start(priority=j % TOP_K)
    for j in range(SCATTER_TOKENS * TOP_K):
        row_copy(j).wait()


def _scatter_rows(dest, last_block, n_used, xn, n_slots):
    t = xn.shape[0] // PACK_ROWS
    n_steps = t // SCATTER_TOKENS
    return pl.pallas_call(
        functools.partial(_scatter_body, n_steps=n_steps, n_blocks=n_slots // MOE_BLOCK),
        out_shape=jax.ShapeDtypeStruct((n_slots * PACK_ROWS, 128), xn.dtype),
        grid_spec=pltpu.PrefetchScalarGridSpec(
            num_scalar_prefetch=3, grid=(n_steps,),
            in_specs=[pl.BlockSpec((SCATTER_TOKENS * PACK_ROWS, 128), lambda i, d, lb, nu: (i, 0))],
            out_specs=pl.BlockSpec(memory_space=pl.ANY),
            scratch_shapes=[pltpu.VMEM((MOE_BLOCK * PACK_ROWS, 128), xn.dtype), pltpu.SemaphoreType.DMA(()),
                            pltpu.SemaphoreType.DMA(())]),
        compiler_params=_params(("arbitrary",)), name="scatter_rows",
    )(dest, last_block, n_used, xn)


def _experts_body(be_ref, bf_ref, ws_ref, nx_ref, nu_ref, x_ref, w1_ref, w3_ref, w2_ref, y_ref,
                  w1f, w3f, w2f, w1b, w3b, w2b, sem_w, *, layer):
    b = pl.program_id(0)
    n_used = nu_ref[0]

    def w_copies(e, ws):
        return [pltpu.make_async_copy(w_ref.at[layer, e], wf.at[ws], sem_w.at[ws])
                for w_ref, wf in ((w1_ref, w1f), (w3_ref, w3f), (w2_ref, w2f))]

    @pl.when(b == 0)
    def _():
        for c in w_copies(be_ref[0], 0):
            c.start()

    @pl.when(b >= n_used)
    def _():
        y_ref[...] = jnp.zeros_like(y_ref)

    @pl.when(b < n_used)
    def _():
        @pl.when(bf_ref[b] == 1)
        def _():
            ws = ws_ref[b]

            @pl.when(nx_ref[b] >= 0)
            def _():
                for c in w_copies(nx_ref[b], 1 - ws):
                    c.start()

            for c in w_copies(be_ref[b], ws):
                c.wait()
            w1b[...] = w1f[ws].astype(BF16)
            w3b[...] = w3f[ws].astype(BF16)
            w2b[...] = w2f[ws].astype(BF16)

        x = _load_packed(x_ref, MOE_BLOCK).astype(BF16)
        hb = (jax.nn.silu(_mm(x, w1b[...])) * _mm(x, w3b[...])).astype(BF16)
        _store_packed(y_ref, _mm(hb, w2b[...]))


def _experts(blk_expert, blk_first, blk_wslot, nxt_expert, n_used, xs, w_e1, w_e3, w_e2, layer):
    d = D_MODEL
    n_blocks = xs.shape[0] // (MOE_BLOCK * PACK_ROWS)
    hbm = pl.BlockSpec(memory_space=pl.ANY)
    x_spec = pl.BlockSpec((MOE_BLOCK * PACK_ROWS, 128), lambda b, be, bf, ws, nx, nu: (jnp.minimum(b, nu[0] - 1), 0))
    y_spec = pl.BlockSpec((MOE_BLOCK * PACK_ROWS, 128), lambda b, be, bf, ws, nx, nu: (b, 0))
    return pl.pallas_call(
        functools.partial(_experts_body, layer=layer),
        out_shape=jax.ShapeDtypeStruct(xs.shape, xs.dtype),
        grid_spec=pltpu.PrefetchScalarGridSpec(
            num_scalar_prefetch=5, grid=(n_blocks,),
            in_specs=[x_spec, hbm, hbm, hbm], out_specs=y_spec,
            scratch_shapes=[pltpu.VMEM((2, d, D_EXPERT), F32), pltpu.VMEM((2, d, D_EXPERT), F32),
                            pltpu.VMEM((2, D_EXPERT, d), F32),
                            pltpu.VMEM((d, D_EXPERT), BF16), pltpu.VMEM((d, D_EXPERT), BF16),
                            pltpu.VMEM((D_EXPERT, d), BF16), pltpu.SemaphoreType.DMA((2,))]),
        compiler_params=_params(("arbitrary",)), name="experts",
    )(blk_expert, blk_first, blk_wslot, nxt_expert, n_used, xs, w_e1, w_e3, w_e2)


COMBINE_TOKENS = 256


def _combine_body(dest_ref, ys_ref, h_ref, gate_ref, gfin_ref, o_ref, abuf, bbuf, sem, *, n_steps, final_norm):
    i = pl.program_id(0)
    slot = i % 2

    def copies(step, sl, j):
        a = (step * COMBINE_TOKENS + j) * TOP_K
        out = []
        for k, buf in enumerate((abuf, bbuf)):
            s0 = pl.multiple_of(dest_ref[a + k] * PACK_ROWS, PACK_ROWS)
            out.append(pltpu.make_async_copy(ys_ref.at[pl.ds(s0, PACK_ROWS), :],
                                             buf.at[sl, pl.ds(j * PACK_ROWS, PACK_ROWS), :], sem.at[sl]))
        return out

    def start(step, sl):
        for j in range(COMBINE_TOKENS):
            for k, c in enumerate(copies(step, sl, j)):
                c.start(priority=k)

    @pl.when(i == 0)
    def _():
        start(0, 0)

    @pl.when(i + 1 < n_steps)
    def _():
        start(i + 1, 1 - slot)

    for j in range(COMBINE_TOKENS):
        for c in copies(i, slot, j):
            c.wait()
    g = gate_ref[...]
    y = (h_ref[...] + g[:, 0:1] * _load_packed(abuf.at[slot], COMBINE_TOKENS)
         + g[:, 1:2] * _load_packed(bbuf.at[slot], COMBINE_TOKENS))
    if final_norm:
        y = _rms(y, gfin_ref[...])
    o_ref[...] = y


def _combine(dest, ys, h, gate, gfin, final_norm):
    t, d = h.shape
    tm = COMBINE_TOKENS
    n_steps = t // tm
    buf = pltpu.VMEM((2, tm * PACK_ROWS, 128), ys.dtype)
    return pl.pallas_call(
        functools.partial(_combine_body, n_steps=n_steps, final_norm=final_norm),
        out_shape=jax.ShapeDtypeStruct((t, d), F32),
        grid_spec=pltpu.PrefetchScalarGridSpec(
            num_scalar_prefetch=1, grid=(n_steps,),
            in_specs=[pl.BlockSpec(memory_space=pl.ANY), pl.BlockSpec((tm, d), lambda i, dr: (i, 0)),
                      pl.BlockSpec((tm, 128), lambda i, dr: (i, 0)), pl.BlockSpec((1, d), lambda i, dr: (0, 0))],
            out_specs=pl.BlockSpec((tm, d), lambda i, dr: (i, 0)),
            scratch_shapes=[buf, buf, pltpu.SemaphoreType.DMA((2,))]),
        compiler_params=_params(("arbitrary",)), name="combine",
    )(dest, ys, h, gate, gfin)


def _dispatch(eid, rank, counts):
    n_assign = eid.shape[0] * TOP_K
    n_blocks = (n_assign + N_EXPERTS * (MOE_BLOCK - 1) + MOE_BLOCK - 1) // MOE_BLOCK
    ids = jnp.arange(N_EXPERTS, dtype=jnp.int32)
    blocks = (counts + MOE_BLOCK - 1) // MOE_BLOCK
    blk_end = jnp.cumsum(blocks)
    blk_begin = blk_end - blocks
    onehot = (eid.reshape(-1)[:, None] == ids[None, :]).astype(F32)
    first_blk = jnp.dot(onehot, blk_begin.astype(F32)).astype(jnp.int32)
    dest = first_blk * MOE_BLOCK + rank.reshape(-1)
    blk_ids = jnp.arange(n_blocks, dtype=jnp.int32)
    blk_expert = jnp.minimum(jnp.sum((blk_end[None, :] <= blk_ids[:, None]).astype(jnp.int32), axis=1), N_EXPERTS - 1)
    blk_first = jnp.concatenate([jnp.ones((1,), jnp.int32), (blk_expert[1:] != blk_expert[:-1]).astype(jnp.int32)])
    n_used = blk_end[-1:].astype(jnp.int32)
    has_rows = counts > 0
    last_block = jnp.where(has_rows, blk_end - 1, -1).astype(jnp.int32)
    wslot = ((jnp.cumsum(has_rows.astype(jnp.int32)) - 1) % 2).astype(jnp.int32)
    later = jnp.where(has_rows[None, :] & (ids[None, :] > ids[:, None]), ids[None, :], N_EXPERTS)
    nxt = jnp.min(later, axis=1)
    nxt = jnp.where(nxt < N_EXPERTS, nxt, -1).astype(jnp.int32)
    return dest, last_block, (blk_expert, blk_first, wslot[blk_expert], nxt[blk_expert], n_used), n_blocks * MOE_BLOCK


def _rope_tables(positions):
    pos = positions.reshape(-1).astype(F32)[:, None]

    def tables(rot, passthrough):
        half = rot // 2
        ang = pos * (ROPE_THETA ** (-jnp.arange(0, rot, 2, dtype=F32) / rot))
        cos, sin = jnp.cos(ang), jnp.sin(ang)
        z = lambda n: jnp.zeros((pos.shape[0], n), F32)
        tail = jnp.full((pos.shape[0], 128 - rot), passthrough, F32)
        return [jnp.concatenate([cos, cos, tail], 1), jnp.concatenate([z(half), sin, z(128 - rot)], 1),
                jnp.concatenate([-sin, z(128 - half)], 1)]

    return jnp.concatenate(tables(MLA_ROPE, 0.0) + tables(NSA_ROT, 1.0), axis=1)


def _selection_constants(s):
    ncp = s // CMP_STRIDE
    ns = s // SEL_BLOCK
    c_start = np.arange(ncp)[None, :] * CMP_STRIDE
    sel_start = np.arange(ns)[:, None] * SEL_BLOCK
    ovt = (c_start <= sel_start + SEL_BLOCK - 1) & (c_start + CMP_LEN - 1 >= sel_start)
    ovt[:, ncp - 1] = False
    expand_t = np.arange(s)[:, None] // SEL_BLOCK == np.arange(ns)[None, :]
    return jnp.asarray(ovt, BF16), jnp.asarray(expand_t, BF16)


def _arrange_w_in(w):
    cuts = np.cumsum([0, 512, 256, 64, 1024, 256, 256, 256, 256, 256, 256, 24])
    c_q, c_kv, k_r, q_n, k_c, v_c, k_s, v_s, k_w, v_w, gates = [w[:, cuts[i]:cuts[i + 1]] for i in range(11)]
    z = lambda n: jnp.zeros((w.shape[0], n), w.dtype)
    w1 = jnp.concatenate([c_q, c_kv, k_r, z(64), q_n, k_c, k_s, k_w, v_c, gates, z(104)], axis=1).astype(BF16)
    return w1, jnp.concatenate([v_s, v_w], axis=1).T.astype(BF16)


def _arrange_w_uq(w):
    w = w.reshape(MLA_Q_RANK, MLA_HEADS, MLA_NOPE + MLA_ROPE)
    w = jnp.pad(w, ((0, 0), (0, 0), (0, MLA_QK_PAD - MLA_NOPE - MLA_ROPE)))
    return w.reshape(MLA_Q_RANK, MLA_HEADS * MLA_QK_PAD).astype(BF16)


def _split_w_ukv(w):
    w = w.reshape(MLA_KV_RANK, MLA_HEADS, MLA_NOPE + MLA_V)
    return (w[:, :, :MLA_NOPE].reshape(MLA_KV_RANK, -1).astype(BF16),
            w[:, :, MLA_NOPE:].reshape(MLA_KV_RANK, -1).T.astype(BF16))


def _router_weights(w_grp, b_grp, w_exp, b_exp):
    pad = 128 - N_GROUPS - N_EXPERTS
    w = jnp.concatenate([w_grp, w_exp, jnp.zeros((w_grp.shape[0], pad), F32)], axis=1)
    hi = w.astype(BF16)
    lo = (w - hi.astype(F32)).astype(BF16)
    b = jnp.concatenate([b_grp, b_exp, jnp.zeros((pad,), F32)])[None, :]
    return hi, lo, b


def kernel(x, mem, positions, g_mem, g_final, g_mix, w_in, g_cq, w_uq, g_ckv, w_ukv, pe_k, pe_v, w_ck1, w_ck2,
           w_cv1, w_cv2, g_ya, g_yb, w_out, g_x, w_xq, w_xkv, w_xo, g_ffn, w_grp, b_grp, w_exp, b_exp,
           w_e1, w_e3, w_e2):
    b, s, d = x.shape
    depth = w_in.shape[0]
    t = b * s
    ncp = s // CMP_STRIDE
    tab = _rope_tables(positions)
    ovt, expand_t = _selection_constants(s)
    r2 = lambda v: v.reshape(1, -1)

    w_kv_all = jnp.concatenate([w_xkv[l] for l in range(depth)], axis=1).astype(BF16)
    memkv = _memkv(mem.reshape(-1, d), r2(g_mem), w_kv_all).reshape(b, mem.shape[1], -1)

    h = x.reshape(t, d)
    for l in range(depth):
        wkn, wv_t = _split_w_ukv(w_ukv[l])
        w1, wvt = _arrange_w_in(w_in[l])
        qa, ka, va_t, qn, kc, vc, ks, kw, vt, gt = _in_proj(
            h, r2(g_mix[l]), w1, wvt, r2(g_cq[l]), r2(g_ckv[l]), _arrange_w_uq(w_uq[l]), wkn, wv_t, tab)
        kcmp, vcmp_t = _compress(
            kc.reshape(NSA_G, b, ncp, CMP_STRIDE * NSA_D), vc.reshape(NSA_G, b, ncp, CMP_STRIDE * NSA_D),
            pe_k[l].reshape(1, -1), pe_v[l].reshape(1, -1), w_ck1[l].astype(BF16), w_ck2[l].astype(BF16),
            w_cv1[l].astype(BF16), w_cv2[l].astype(BF16))
        r3 = lambda a: a.reshape(b, s, -1)
        ya = _mla_attn(r3(qa), r3(ka), va_t)
        yb = _nsa_attn(r3(qn), kcmp, vcmp_t, r3(ks), r3(kw), vt, gt.reshape(NSA_G, b, s, 128), ovt, expand_t)
        wrh, wrl, br = _router_weights(w_grp[l], b_grp[l], w_exp[l], b_exp[l])
        h2, xn, eid, gate, counts = _post_mixer(
            ya, yb, r3(h), r2(g_ya[l]), r2(g_yb[l]), w_out[l].astype(BF16), r2(g_x[l]), w_xq[l].astype(BF16),
            memkv, l, w_xo[l].astype(BF16), r2(g_ffn[l]), wrh, wrl, br)
        eid = eid.reshape(t, 128)
        dest, last_block, blocks, n_slots = _dispatch(
            eid[:, :TOP_K], eid[:, TOP_K:2 * TOP_K], counts[0, :N_EXPERTS])
        xs = _scatter_rows(dest, last_block, blocks[-1], xn, n_slots)
        ys = _experts(*blocks, xs, w_e1, w_e3, w_e2, l)
        h = _combine(dest, ys, h2.reshape(t, d), gate.reshape(t, 128), r2(g_final), final_norm=(l == depth - 1))
    return h.reshape(b, s, d)
```
